```python
import math
import jax, jax.numpy as jnp
from jax import lax
import numpy as np

D_MODEL = 2048
BATCH = 1
SEQ = 8192
DEPTH = 1

PLE_DIM = 256
MLA_HEADS = 8
Q_LORA = 512
KV_LORA = 512
QK_NOPE = 128
QK_ROPE = 64
V_HEAD = 128
ROPE_THETA = 10000.0
SWA_HEADS = 16
SWA_KV_HEADS = 2
SWA_HEAD_DIM = 64
WINDOW = 128
BLOCK = 128
REL_BUCKETS = 32
REL_MAX_DIST = 128
D_FF = 5632
ALPHA = (2.0 * DEPTH) ** 0.25
BETA = (8.0 * DEPTH) ** -0.25
EPS = 1e-5

MLA_WIDTH = MLA_HEADS * V_HEAD
SWA_WIDTH = SWA_HEADS * SWA_HEAD_DIM
MIX_WIDTH = MLA_WIDTH + SWA_WIDTH
SWA_KV_WIDTH = SWA_KV_HEADS * SWA_HEAD_DIM
IN_SIZES = (Q_LORA, KV_LORA, QK_ROPE, SWA_WIDTH, SWA_KV_WIDTH, SWA_KV_WIDTH)
D_IN = sum(IN_SIZES)

kernel_name = "hybrid_mla_swa_macaron_deepnorm"


def layer_norm(x, g, b):
    xf = x.astype(jnp.float32)
    mu = jnp.mean(xf, axis=-1, keepdims=True)
    xc = xf - mu
    var = jnp.mean(xc * xc, axis=-1, keepdims=True)
    return (xc * lax.rsqrt(var + EPS) * g.astype(jnp.float32) + b.astype(jnp.float32)).astype(x.dtype)


def rms_norm(x, g):
    xf = x.astype(jnp.float32)
    ms = jnp.mean(xf * xf, axis=-1, keepdims=True)
    return (xf * lax.rsqrt(ms + EPS) * g.astype(jnp.float32)).astype(x.dtype)


def swiglu(x, w1, w3, w2):
    return (jax.nn.silu(x @ w1) * (x @ w3)) @ w2


def rope_tables(positions, dim):
    freqs = ROPE_THETA ** (-jnp.arange(0, dim, 2, dtype=jnp.float32) / dim)
    ang = positions.astype(jnp.float32)[..., None] * freqs
    return jnp.cos(ang), jnp.sin(ang)


def apply_rope(x, cos, sin):
    half = x.shape[-1] // 2
    x1, x2 = x[..., :half], x[..., half:]
    c, s = cos.astype(x.dtype), sin.astype(x.dtype)
    return jnp.concatenate([x1 * c - x2 * s, x1 * s + x2 * c], axis=-1)


def t5_bucket(dist):
    n = jnp.maximum(dist, 0)
    max_exact = REL_BUCKETS // 2
    large = max_exact + (jnp.log(jnp.maximum(n, 1).astype(jnp.float32) / max_exact)
                         / math.log(REL_MAX_DIST / max_exact)
                         * (REL_BUCKETS - max_exact)).astype(jnp.int32)
    large = jnp.minimum(large, REL_BUCKETS - 1)
    return jnp.where(n < max_exact, n, large)


def mla_attention(c_q, c_kv, k_rope_raw, cos, sin, q_norm_g, w_uq, kv_norm_g, w_ukv):
    B, S, _ = c_q.shape
    H = MLA_HEADS
    q = (rms_norm(c_q, q_norm_g) @ w_uq).reshape(B, S, H, QK_NOPE + QK_ROPE)
    q_nope = q[..., :QK_NOPE]
    q_rope = apply_rope(q[..., QK_NOPE:], cos[:, :, None, :], sin[:, :, None, :])
    kv = (rms_norm(c_kv, kv_norm_g) @ w_ukv).reshape(B, S, H, QK_NOPE + V_HEAD)
    k_nope, v = kv[..., :QK_NOPE], kv[..., QK_NOPE:]
    k_rope = apply_rope(k_rope_raw, cos, sin)
    scale = (QK_NOPE + QK_ROPE) ** -0.5
    nb = S // BLOCK
    qn_blk = q_nope.reshape(B, nb, BLOCK, H, QK_NOPE).transpose(1, 0, 2, 3, 4)
    qr_blk = q_rope.reshape(B, nb, BLOCK, H, QK_ROPE).transpose(1, 0, 2, 3, 4)
    key_pos = jnp.arange(S)
    neg = jnp.finfo(jnp.float32).min

    def one_block(args):
        qn, qr, blk = args
        s = (jnp.einsum('bqhd,bkhd->bhqk', qn, k_nope)
             + jnp.einsum('bqhd,bkd->bhqk', qr, k_rope)).astype(jnp.float32) * scale
        q_pos = blk * BLOCK + jnp.arange(BLOCK)
        s = jnp.where(key_pos[None, :] <= q_pos[:, None], s, neg)
        pr = jax.nn.softmax(s, axis=-1)
        return jnp.einsum('bhqk,bkhd->bqhd', pr.astype(v.dtype), v)

    out = lax.map(one_block, (qn_blk, qr_blk, jnp.arange(nb)))
    return out.transpose(1, 0, 2, 3, 4).reshape(B, S, MLA_WIDTH)


def swa_attention(q, k, v, sinks, rel_bias):
    B, S, _ = q.shape
    nb = S // BLOCK
    KV, G, Dh = SWA_KV_HEADS, SWA_HEADS // SWA_KV_HEADS, SWA_HEAD_DIM
    q = q.reshape(B, nb, BLOCK, KV, G, Dh)
    k = k.reshape(B, nb, BLOCK, KV, Dh)
    v = v.reshape(B, nb, BLOCK, KV, Dh)
    k_prev = jnp.concatenate([jnp.zeros_like(k[:, :1]), k[:, :-1]], axis=1)
    v_prev = jnp.concatenate([jnp.zeros_like(v[:, :1]), v[:, :-1]], axis=1)
    k_band = jnp.concatenate([k_prev, k], axis=2)
    v_band = jnp.concatenate([v_prev, v], axis=2)
    s = jnp.einsum('bnqkgd,bnjkd->bkgnqj', q, k_band).astype(jnp.float32) * (Dh ** -0.5)
    qi = jnp.arange(BLOCK)[:, None] + BLOCK
    kj = jnp.arange(2 * BLOCK)[None, :]
    dist = qi - kj
    bias = rel_bias[t5_bucket(dist)].astype(jnp.float32)
    s = s + bias.transpose(2, 0, 1).reshape(KV, G, 1, BLOCK, 2 * BLOCK)
    valid = (dist >= 0) & (dist < WINDOW)
    blk = jnp.arange(nb)[:, None, None]
    valid = valid[None] & ((blk > 0) | (kj[None] >= BLOCK))
    s = jnp.where(valid, s, jnp.finfo(jnp.float32).min)
    sink = sinks.astype(jnp.float32).reshape(KV, G, 1, 1, 1)
    m = jnp.maximum(jnp.max(s, axis=-1, keepdims=True), sink)
    e = jnp.exp(s - m)
    pr = e / (jnp.sum(e, axis=-1, keepdims=True) + jnp.exp(sink - m))
    out = jnp.einsum('bkgnqj,bnjkd->bnqkgd', pr.astype(v.dtype), v_band)
    return out.reshape(B, S, SWA_WIDTH)


def setup_inputs(seed: int = 0) -> dict:
    key = jax.random.key(seed)
    ks = jax.random.split(key, 32)
    f32 = jnp.float32
    L, D = DEPTH, D_MODEL

    def nrm(k, shape, scale):
        return jax.random.normal(k, shape, f32) * scale

    def gain(k, shape):
        return 1.0 + 0.02 * jax.random.normal(k, shape, f32)

    return {
        "x": nrm(ks[0], (BATCH, SEQ, D), 1.0),
        "p": nrm(ks[1], (DEPTH, BATCH, SEQ, PLE_DIM), 1.0),
        "positions": jnp.broadcast_to(jnp.arange(SEQ, dtype=jnp.int32), (BATCH, SEQ)),
        "rel_bias": nrm(ks[2], (REL_BUCKETS, SWA_HEADS), 0.5),
        "ln1_g": gain(ks[3], (L, D)),
        "ln1_b": nrm(ks[4], (L, D), 0.02),
        "ffn1_w1": nrm(ks[5], (L, D, D_FF), D ** -0.5),
        "ffn1_w3": nrm(ks[6], (L, D, D_FF), D ** -0.5),
        "ffn1_w2": nrm(ks[7], (L, D_FF, D), D_FF ** -0.5 * BETA),
        "w_in": nrm(ks[8], (L, D, D_IN), D ** -0.5),
        "q_norm_g": gain(ks[9], (L, Q_LORA)),
        "w_uq": nrm(ks[10], (L, Q_LORA, MLA_HEADS * (QK_NOPE + QK_ROPE)), Q_LORA ** -0.5),
        "kv_norm_g": gain(ks[11], (L, KV_LORA)),
        "w_ukv": nrm(ks[12], (L, KV_LORA, MLA_HEADS * (QK_NOPE + V_HEAD)), KV_LORA ** -0.5),
        "swa_sinks": nrm(ks[13], (L, SWA_HEADS), 0.5),
        "mla_out_g": gain(ks[14], (L, MLA_WIDTH)),
        "swa_out_g": gain(ks[15], (L, SWA_WIDTH)),
        "w_out": nrm(ks[16], (L, MIX_WIDTH, D), MIX_WIDTH ** -0.5 * BETA),
        "ln2_g": gain(ks[17], (L, D)),
        "ln2_b": nrm(ks[18], (L, D), 0.02),
        "ffn2_w1": nrm(ks[19], (L, D, D_FF), D ** -0.5),
        "ffn2_w3": nrm(ks[20], (L, D, D_FF), D ** -0.5),
        "ffn2_w2": nrm(ks[21], (L, D_FF, D), D_FF ** -0.5 * BETA),
        "ln3_g": gain(ks[22], (L, D)),
        "ln3_b": nrm(ks[23], (L, D), 0.02),
        "ple_w_gate": nrm(ks[24], (L, D, D), D ** -0.5),
        "ple_w_proj": nrm(ks[25], (L, PLE_DIM, D), PLE_DIM ** -0.5 * BETA),
    }


def reference(x, p, positions, rel_bias, ln1_g, ln1_b, ffn1_w1, ffn1_w3, ffn1_w2,
              w_in, q_norm_g, w_uq, kv_norm_g, w_ukv, swa_sinks, mla_out_g, swa_out_g,
              w_out, ln2_g, ln2_b, ffn2_w1, ffn2_w3, ffn2_w2, ln3_g, ln3_b,
              ple_w_gate, ple_w_proj):
    cos, sin = rope_tables(positions, QK_ROPE)
    bounds = list(np.cumsum(IN_SIZES)[:-1])
    h = x
    for i in range(DEPTH):
        h = layer_norm(ALPHA * h + 0.5 * swiglu(h, ffn1_w1[i], ffn1_w3[i], ffn1_w2[i]), ln1_g[i], ln1_b[i])
        z = h @ w_in[i]
        c_q, c_kv, k_rope_raw, q_s, k_s, v_s = jnp.split(z, bounds, axis=-1)
        a_mla = mla_attention(c_q, c_kv, k_rope_raw, cos, sin,
                              q_norm_g[i], w_uq[i], kv_norm_g[i], w_ukv[i])
        a_swa = swa_attention(q_s, k_s, v_s, swa_sinks[i], rel_bias)
        mixed = jnp.concatenate([rms_norm(a_mla, mla_out_g[i]), rms_norm(a_swa, swa_out_g[i])], axis=-1)
        h = layer_norm(ALPHA * h + mixed @ w_out[i], ln2_g[i], ln2_b[i])
        h = layer_norm(ALPHA * h + 0.5 * swiglu(h, ffn2_w1[i], ffn2_w3[i], ffn2_w2[i]), ln3_g[i], ln3_b[i])
        h = h + jax.nn.sigmoid(h @ ple_w_gate[i]) * (p[i] @ ple_w_proj[i])
    return h
```

```python
import functools
import math

import numpy as np
import jax
import jax.numpy as jnp
from jax import lax
from jax.experimental import pallas as pl
from jax.experimental.pallas import tpu as pltpu

F32 = jnp.float32
BF16 = jnp.bfloat16

D_MODEL = 2048
SEQ = 8192
DEPTH = 1
PLE_DIM = 256
MLA_HEADS = 8
Q_LORA = 512
KV_LORA = 512
QK_NOPE = 128
QK_ROPE = 64
V_HEAD = 128
ROPE_THETA = 10000.0
SWA_HEADS = 16
SWA_KV_HEADS = 2
SWA_GROUP = SWA_HEADS // SWA_KV_HEADS
SWA_HEAD_DIM = 64
WINDOW = 128
BLOCK = 128
REL_BUCKETS = 32
REL_MAX_DIST = 128
D_FF = 5632
ALPHA = (2.0 * DEPTH) ** 0.25
EPS = 1e-5
MLA_WIDTH = MLA_HEADS * V_HEAD
SWA_WIDTH = SWA_HEADS * SWA_HEAD_DIM
SWA_KV_WIDTH = SWA_KV_HEADS * SWA_HEAD_DIM

V7X_LANES = 128
V7X_VMEM_BYTES = 64 * 1024 * 1024
MIB = 1024 * 1024

MLA_QK_PAD = 2 * V7X_LANES
NEG_BIG = float(np.finfo(np.float32).min)


def _vmem_limit(estimate_bytes):
    return int(min(estimate_bytes + 8 * MIB, V7X_VMEM_BYTES - 4 * MIB))


def _layer_norm(y, g, b):
    mu = jnp.mean(y, axis=-1, keepdims=True)
    yc = y - mu
    var = jnp.mean(yc * yc, axis=-1, keepdims=True)
    return yc * lax.rsqrt(var + EPS) * g + b


def _rms_norm(x, g):
    ms = jnp.mean(x * x, axis=-1, keepdims=True)
    return x * lax.rsqrt(ms + EPS) * g


def _dot(a, b):
    return jnp.dot(a, b, preferred_element_type=F32)


def _dot_nt(a, b):
    return lax.dot_general(a, b, (((1,), (1,)), ((), ())), preferred_element_type=F32)


FFN_TM = 512
FFN_TF = 512


def _ffn_ln_kernel(x_ref, w1_ref, w3_ref, w2_ref, g_ref, b_ref, o_ref, xb_ref, acc_ref):
    f = pl.program_id(1)

    @pl.when(f == 0)
    def _():
        xb_ref[...] = x_ref[...].astype(BF16)
        acc_ref[...] = jnp.zeros_like(acc_ref)

    xb = xb_ref[...]
    gate = _dot(xb, w1_ref[...])
    up = _dot(xb, w3_ref[...])
    hidden = gate * (1.0 / (1.0 + jnp.exp(-gate))) * up
    acc_ref[...] += _dot(hidden.astype(BF16), w2_ref[...])

    @pl.when(f == pl.num_programs(1) - 1)
    def _():
        y = ALPHA * x_ref[...] + 0.5 * acc_ref[...]
        o_ref[...] = _layer_norm(y, g_ref[...], b_ref[...])


def _ffn_ln(x, w1, w3, w2, g, b):
    s, d = x.shape
    dff = w1.shape[1]
    tm, tf = FFN_TM, FFN_TF
    est = (2 * tm * d * 4 + tm * d * 2 + tm * d * 4 + 2 * 3 * d * tf * 2 + 2 * tm * d * 4
           + 4 * tm * tf * 4)
    return pl.pallas_call(
        _ffn_ln_kernel,
        grid=(s // tm, dff // tf),
        in_specs=[
            pl.BlockSpec((tm, d), lambda i, f: (i, 0)),
            pl.BlockSpec((d, tf), lambda i, f: (0, f)),
            pl.BlockSpec((d, tf), lambda i, f: (0, f)),
            pl.BlockSpec((tf, d), lambda i, f: (f, 0)),
            pl.BlockSpec((1, d), lambda i, f: (0, 0)),
            pl.BlockSpec((1, d), lambda i, f: (0, 0)),
        ],
        out_specs=pl.BlockSpec((tm, d), lambda i, f: (i, 0)),
        out_shape=jax.ShapeDtypeStruct((s, d), F32),
        scratch_shapes=[pltpu.VMEM((tm, d), BF16), pltpu.VMEM((tm, d), F32)],
        compiler_params=pltpu.CompilerParams(
            dimension_semantics=("parallel", "arbitrary"),
            vmem_limit_bytes=_vmem_limit(est)),
        name="ffn_ln",
    )(x, w1, w3, w2, g, b)


INPROJ_TM = 512
Z_CQ = 0
Z_CKV = Z_CQ + Q_LORA
Z_KR = Z_CKV + KV_LORA
Z_QS = Z_KR + V7X_LANES
Z_KS = Z_QS + SWA_WIDTH
Z_VS = Z_KS + SWA_KV_WIDTH
Z_WIDTH = Z_VS + SWA_KV_WIDTH


def _rope_tables(pos_col, freq_row):
    ang = pos_col * freq_row
    lane = lax.broadcasted_iota(jnp.int32, ang.shape, 1)
    half = QK_ROPE // 2
    cos = jnp.cos(ang)
    sin = jnp.sin(ang)
    sin_lo = jnp.where(lane < half, -sin, 0.0)
    sin_hi = jnp.where((lane >= half) & (lane < 2 * half), sin, 0.0)
    return cos, sin_lo, sin_hi


def _apply_rope(x, tables):
    cos, sin_lo, sin_hi = tables
    half = QK_ROPE // 2
    x_up = pltpu.roll(x, V7X_LANES - half, axis=1)
    x_dn = pltpu.roll(x, half, axis=1)
    return x * cos + x_up * sin_lo + x_dn * sin_hi


def _in_proj_kernel(h_ref, win_ref, qg_ref, kvg_ref, wuq_ref, wkn_ref, wv_ref, pos_ref, freq_ref,
                    q_ref, k_ref, v_ref, qs_ref, ks_ref, vs_ref):
    hb = h_ref[...].astype(BF16)
    z = _dot(hb, win_ref[...])
    tables = _rope_tables(pos_ref[...], freq_ref[...])

    cq = _rms_norm(z[:, Z_CQ:Z_CQ + Q_LORA], qg_ref[...]).astype(BF16)
    ckv = _rms_norm(z[:, Z_CKV:Z_CKV + KV_LORA], kvg_ref[...]).astype(BF16)
    q = _dot(cq, wuq_ref[...])
    kn = _dot(ckv, wkn_ref[...])
    v_ref[...] = _dot(ckv, wv_ref[...]).astype(BF16)
    k_rope = _apply_rope(z[:, Z_KR:Z_KR + V7X_LANES], tables).astype(BF16)

    for h in range(MLA_HEADS):
        base = h * MLA_QK_PAD
        q_ref[:, base:base + QK_NOPE] = q[:, base:base + QK_NOPE].astype(BF16)
        q_ref[:, base + QK_NOPE:base + MLA_QK_PAD] = _apply_rope(
            q[:, base + QK_NOPE:base + MLA_QK_PAD], tables).astype(BF16)
        k_ref[:, base:base + QK_NOPE] = kn[:, h * QK_NOPE:(h + 1) * QK_NOPE].astype(BF16)
        k_ref[:, base + QK_NOPE:base + MLA_QK_PAD] = k_rope

    for h in range(SWA_HEADS):
        lo = Z_QS + h * SWA_HEAD_DIM
        qs_ref[h] = z[:, lo:lo + SWA_HEAD_DIM].astype(BF16)
    for h in range(SWA_KV_HEADS):
        lo = Z_KS + h * SWA_HEAD_DIM
        ks_ref[h] = z[:, lo:lo + SWA_HEAD_DIM].astype(BF16)
        lo = Z_VS + h * SWA_HEAD_DIM
        vs_ref[h] = z[:, lo:lo + SWA_HEAD_DIM].astype(BF16)


def _in_proj(h, win_p, q_g, kv_g, wuq_p, wkn, wv, pos_col, freq_row):
    s, d = h.shape
    tm = INPROJ_TM
    est = (2 * tm * d * 4 + 2 * d * Z_WIDTH * 2 + 2 * Q_LORA * (MLA_HEADS * MLA_QK_PAD + 2 * MLA_WIDTH) * 2
           + 2 * tm * (2 * MLA_HEADS * MLA_QK_PAD + MLA_WIDTH) * 2
           + 2 * (SWA_HEADS + 2 * SWA_KV_HEADS) * tm * V7X_LANES * 2
           + 4 * tm * Z_WIDTH * 4)
    const = lambda i: (0, 0)
    return pl.pallas_call(
        _in_proj_kernel,
        grid=(s // tm,),
        in_specs=[
            pl.BlockSpec((tm, d), lambda i: (i, 0)),
            pl.BlockSpec((d, Z_WIDTH), const),
            pl.BlockSpec((1, Q_LORA), const),
            pl.BlockSpec((1, KV_LORA), const),
            pl.BlockSpec((Q_LORA, MLA_HEADS * MLA_QK_PAD), const),
            pl.BlockSpec((KV_LORA, MLA_WIDTH), const),
            pl.BlockSpec((KV_LORA, MLA_WIDTH), const),
            pl.BlockSpec((tm, 1), lambda i: (i, 0)),
            pl.BlockSpec((1, V7X_LANES), const),
        ],
        out_specs=[
            pl.BlockSpec((tm, MLA_HEADS * MLA_QK_PAD), lambda i: (i, 0)),
            pl.BlockSpec((tm, MLA_HEADS * MLA_QK_PAD), lambda i: (i, 0)),
            pl.BlockSpec((tm, MLA_WIDTH), lambda i: (i, 0)),
            pl.BlockSpec((SWA_HEADS, tm, SWA_HEAD_DIM), lambda i: (0, i, 0)),
            pl.BlockSpec((SWA_KV_HEADS, tm, SWA_HEAD_DIM), lambda i: (0, i, 0)),
            pl.BlockSpec((SWA_KV_HEADS, tm, SWA_HEAD_DIM), lambda i: (0, i, 0)),
        ],
        out_shape=[
            jax.ShapeDtypeStruct((s, MLA_HEADS * MLA_QK_PAD), BF16),
            jax.ShapeDtypeStruct((s, MLA_HEADS * MLA_QK_PAD), BF16),
            jax.ShapeDtypeStruct((s, MLA_WIDTH), BF16),
            jax.ShapeDtypeStruct((SWA_HEADS, s, SWA_HEAD_DIM), BF16),
            jax.ShapeDtypeStruct((SWA_KV_HEADS, s, SWA_HEAD_DIM), BF16),
            jax.ShapeDtypeStruct((SWA_KV_HEADS, s, SWA_HEAD_DIM), BF16),
        ],
        compiler_params=pltpu.CompilerParams(
            dimension_semantics=("parallel",),
            vmem_limit_bytes=_vmem_limit(est)),
        name="in_proj",
    )(h, win_p, q_g, kv_g, wuq_p, wkn, wv, pos_col, freq_row)


MLA_TQ = 512
MLA_TK = 512
MLA_SCALE = (QK_NOPE + QK_ROPE) ** -0.5


def _mla_flash_kernel(q_ref, k_ref, v_ref, o_ref, s_ref, m_ref, l_ref, acc_ref):
    i = pl.program_id(1)
    tk = MLA_TK
    q = q_ref[...]

    m_ref[...] = jnp.full_like(m_ref, NEG_BIG)
    l_ref[...] = jnp.zeros_like(l_ref)
    acc_ref[...] = jnp.zeros_like(acc_ref)

    def scores(j):
        start = pl.multiple_of(j * tk, tk)
        return _dot_nt(q, k_ref[pl.ds(start, tk), :])

    def softmax_pv(s, j, masked):
        start = pl.multiple_of(j * tk, tk)
        v = v_ref[pl.ds(start, tk), :]
        s = s * MLA_SCALE
        if masked:
            row = lax.broadcasted_iota(jnp.int32, s.shape, 0)
            col = lax.broadcasted_iota(jnp.int32, s.shape, 1)
            s = jnp.where(col <= row, s, NEG_BIG)
        m_prev = m_ref[...]
        m_new = jnp.maximum(m_prev, jnp.max(s, axis=1, keepdims=True))
        p = jnp.exp(s - m_new)
        corr = jnp.exp(m_prev - m_new)
        l_ref[...] = corr * l_ref[...] + jnp.sum(p, axis=1, keepdims=True)
        acc_ref[...] = corr * acc_ref[...] + _dot(p.astype(BF16), v)
        m_ref[...] = m_new

    s_ref[...] = scores(0)

    def body(j, carry):
        s = s_ref[...]
        s_ref[...] = scores(j + 1)
        softmax_pv(s, j, masked=False)
        return carry

    lax.fori_loop(0, i, body, 0)
    softmax_pv(s_ref[...], i, masked=True)
    o_ref[...] = acc_ref[...] / l_ref[...]


def _mla_flash(q, k, v):
    s = q.shape[0]
    tq = MLA_TQ
    assert MLA_TQ == MLA_TK
    est = (2 * tq * MLA_QK_PAD * 2 + 2 * s * MLA_QK_PAD * 2 + 2 * s * V_HEAD * 2 + 2 * tq * V_HEAD * 4
           + 2 * tq * V7X_LANES * 4 + tq * V_HEAD * 4 + 4 * tq * MLA_TK * 4)
    return pl.pallas_call(
        _mla_flash_kernel,
        grid=(MLA_HEADS, s // tq),
        in_specs=[
            pl.BlockSpec((tq, MLA_QK_PAD), lambda h, i: (i, h)),
            pl.BlockSpec((s, MLA_QK_PAD), lambda h, i: (0, h)),
            pl.BlockSpec((s, V_HEAD), lambda h, i: (0, h)),
        ],
        out_specs=pl.BlockSpec((tq, V_HEAD), lambda h, i: (i, h)),
        out_shape=jax.ShapeDtypeStruct((s, MLA_WIDTH), F32),
        scratch_shapes=[pltpu.VMEM((tq, MLA_TK), F32), pltpu.VMEM((tq, 1), F32),
                        pltpu.VMEM((tq, 1), F32), pltpu.VMEM((tq, V_HEAD), F32)],
        compiler_params=pltpu.CompilerParams(
            dimension_semantics=("parallel", "arbitrary"),
            vmem_limit_bytes=_vmem_limit(est)),
        name="mla_flash",
    )(q, k, v)


SWA_SCALE = SWA_HEAD_DIM ** -0.5
SWA_ROWS = SWA_GROUP * BLOCK


def _t5_bucket_np(dist):
    n = np.maximum(dist, 0)
    max_exact = REL_BUCKETS // 2
    large = max_exact + (np.log(np.maximum(n, 1).astype(np.float32) / max_exact)
                         / math.log(REL_MAX_DIST / max_exact)
                         * (REL_BUCKETS - max_exact)).astype(np.int32)
    large = np.minimum(large, REL_BUCKETS - 1)
    return np.where(n < max_exact, n, large).astype(np.int32)


def _swa_bucket_table():
    i = np.arange(BLOCK)[:, None]
    j = np.arange(BLOCK)[None, :]
    dist = np.where(j <= i, i - j, BLOCK + i - j)
    return _t5_bucket_np(dist)


def _swa_kernel(rb_ref, bkt_ref, q_ref, k_ref, v_ref, sink_ref, o_ref, bias_ref):
    kv = pl.program_id(0)
    n = pl.program_id(1)

    @pl.when((kv == 0) & (n == 0))
    def _():
        bkt = bkt_ref[...]
        for h in range(SWA_HEADS):
            acc = jnp.zeros((BLOCK, BLOCK), F32)
            for b in range(REL_BUCKETS):
                acc = jnp.where(bkt == b, rb_ref[b, h], acc)
            bias_ref[h] = acc

    q = q_ref[...].reshape(SWA_ROWS, SWA_HEAD_DIM)
    cur = pl.multiple_of(n * BLOCK, BLOCK)
    prev = pl.multiple_of(jnp.maximum(n - 1, 0) * BLOCK, BLOCK)
    k_cur = k_ref[pl.ds(cur, BLOCK), :]
    k_prev = k_ref[pl.ds(prev, BLOCK), :]
    v_cur = v_ref[pl.ds(cur, BLOCK), :]
    v_prev = v_ref[pl.ds(prev, BLOCK), :]

    s_cur = _dot_nt(q, k_cur)
    s_prev = _dot_nt(q, k_prev)
    row = lax.broadcasted_iota(jnp.int32, (SWA_GROUP, BLOCK, BLOCK), 1).reshape(SWA_ROWS, BLOCK)
    col = lax.broadcasted_iota(jnp.int32, (SWA_ROWS, BLOCK), 1)
    lower = col <= row
    bias = bias_ref[pl.ds(kv * SWA_GROUP, SWA_GROUP)].reshape(SWA_ROWS, BLOCK)
    s = jnp.where(lower, s_cur, s_prev) * SWA_SCALE + bias
    s = jnp.where(lower | (n > 0), s, NEG_BIG)

    sink = sink_ref[...]
    m = jnp.maximum(jnp.max(s, axis=1, keepdims=True), sink)
    e = jnp.exp(s - m)
    denom = jnp.sum(e, axis=1, keepdims=True) + jnp.exp(sink - m)
    p = e / denom
    p_cur = jnp.where(lower, p, 0.0).astype(BF16)
    p_prev = jnp.where(lower, 0.0, p).astype(BF16)
    o = _dot(p_cur, v_cur) + _dot(p_prev, v_prev)
    for g in range(SWA_GROUP):
        o_ref[:, g * SWA_HEAD_DIM:(g + 1) * SWA_HEAD_DIM] = o[g * BLOCK:(g + 1) * BLOCK, :]


def _swa_attn(rel_bias, bucket_tbl, qs, ks, vs, sink_rows):
    s = qs.shape[1]
    nb = s // BLOCK
    est = (2 * SWA_ROWS * V7X_LANES * 2 + 4 * s * V7X_LANES * 2 + 2 * SWA_ROWS * V7X_LANES * 4
           + 2 * BLOCK * SWA_GROUP * SWA_HEAD_DIM * 4 + SWA_HEADS * BLOCK * BLOCK * 4
           + 8 * SWA_ROWS * BLOCK * 4)
    return pl.pallas_call(
        _swa_kernel,
        grid=(SWA_KV_HEADS, nb),
        in_specs=[
            pl.BlockSpec(memory_space=pltpu.SMEM),
            pl.BlockSpec((BLOCK, BLOCK), lambda kv, n: (0, 0)),
            pl.BlockSpec((SWA_GROUP, BLOCK, SWA_HEAD_DIM), lambda kv, n: (kv, n, 0)),
            pl.BlockSpec((None, s, SWA_HEAD_DIM), lambda kv, n: (kv, 0, 0)),
            pl.BlockSpec((None, s, SWA_HEAD_DIM), lambda kv, n: (kv, 0, 0)),
            pl.BlockSpec((None, SWA_ROWS, 1), lambda kv, n: (kv, 0, 0)),
        ],
        out_specs=pl.BlockSpec((BLOCK, SWA_GROUP * SWA_HEAD_DIM), lambda kv, n: (n, kv)),
        out_shape=jax.ShapeDtypeStruct((s, SWA_WIDTH), F32),
        scratch_shapes=[pltpu.VMEM((SWA_HEADS, BLOCK, BLOCK), F32)],
        compiler_params=pltpu.CompilerParams(
            dimension_semantics=("arbitrary", "arbitrary"),
            vmem_limit_bytes=_vmem_limit(est)),
        name="swa_attn",
    )(rel_bias, bucket_tbl, qs, ks, vs, sink_rows)


OUTPROJ_TM = 512


def _out_proj_kernel(am_ref, as_ref, h_ref, mg_ref, sg_ref, wm_ref, ws_ref, g_ref, b_ref, o_ref):
    nm = _rms_norm(am_ref[...], mg_ref[...]).astype(BF16)
    ns = _rms_norm(as_ref[...], sg_ref[...]).astype(BF16)
    mixed = _dot(nm, wm_ref[...]) + _dot(ns, ws_ref[...])
    o_ref[...] = _layer_norm(ALPHA * h_ref[...] + mixed, g_ref[...], b_ref[...])


def _out_proj(a_mla, a_swa, h, mla_g, swa_g, w_mla, w_swa, g, b):
    s, d = h.shape
    tm = OUTPROJ_TM
    est = (2 * tm * (MLA_WIDTH + SWA_WIDTH) * 4 + 4 * tm * d * 4 + 2 * (MLA_WIDTH + SWA_WIDTH) * d * 2
           + 4 * tm * d * 4)
    const = lambda i: (0, 0)
    return pl.pallas_call(
        _out_proj_kernel,
        grid=(s // tm,),
        in_specs=[
            pl.BlockSpec((tm, MLA_WIDTH), lambda i: (i, 0)),
            pl.BlockSpec((tm, SWA_WIDTH), lambda i: (i, 0)),
            pl.BlockSpec((tm, d), lambda i: (i, 0)),
            pl.BlockSpec((1, MLA_WIDTH), const),
            pl.BlockSpec((1, SWA_WIDTH), const),
            pl.BlockSpec((MLA_WIDTH, d), const),
            pl.BlockSpec((SWA_WIDTH, d), const),
            pl.BlockSpec((1, d), const),
            pl.BlockSpec((1, d), const),
        ],
        out_specs=pl.BlockSpec((tm, d), lambda i: (i, 0)),
        out_shape=jax.ShapeDtypeStruct((s, d), F32),
        compiler_params=pltpu.CompilerParams(
            dimension_semantics=("parallel",),
            vmem_limit_bytes=_vmem_limit(est)),
        name="out_proj",
    )(a_mla, a_swa, h, mla_g, swa_g, w_mla, w_swa, g, b)


PLE_TM = 512


def _ple_kernel(h_ref, p_ref, wg_ref, wp_ref, o_ref):
    h = h_ref[...]
    gate = _dot(h.astype(BF16), wg_ref[...])
    proj = _dot(p_ref[...].astype(BF16), wp_ref[...])
    o_ref[...] = h + (1.0 / (1.0 + jnp.exp(-gate))) * proj


def _ple(h, p, w_gate, w_proj):
    s, d = h.shape
    tm = PLE_TM
    est = 4 * tm * d * 4 + 2 * tm * PLE_DIM * 4 + 2 * (d + PLE_DIM) * d * 2 + 4 * tm * d * 4
    const = lambda i: (0, 0)
    return pl.pallas_call(
        _ple_kernel,
        grid=(s // tm,),
        in_specs=[
            pl.BlockSpec((tm, d), lambda i: (i, 0)),
            pl.BlockSpec((tm, PLE_DIM), lambda i: (i, 0)),
            pl.BlockSpec((d, d), const),
            pl.BlockSpec((PLE_DIM, d), const),
        ],
        out_specs=pl.BlockSpec((tm, d), lambda i: (i, 0)),
        out_shape=jax.ShapeDtypeStruct((s, d), F32),
        compiler_params=pltpu.CompilerParams(
            dimension_semantics=("parallel",),
            vmem_limit_bytes=_vmem_limit(est)),
        name="ple",
    )(h, p, w_gate, w_proj)


def _pack_w_in(w_in):
    d = w_in.shape[0]
    b0 = Q_LORA + KV_LORA
    b1 = b0 + QK_ROPE
    pad = jnp.zeros((d, V7X_LANES - QK_ROPE), w_in.dtype)
    return jnp.concatenate([w_in[:, :b1], pad, w_in[:, b1:]], axis=1)


def _pack_w_uq(w_uq):
    r = w_uq.shape[0]
    w = w_uq.reshape(r, MLA_HEADS, QK_NOPE + QK_ROPE)
    w = jnp.pad(w, ((0, 0), (0, 0), (0, MLA_QK_PAD - QK_NOPE - QK_ROPE)))
    return w.reshape(r, MLA_HEADS * MLA_QK_PAD)


def _rope_freq_row():
    half = QK_ROPE // 2
    freqs = ROPE_THETA ** (-jnp.arange(0, QK_ROPE, 2, dtype=F32) / QK_ROPE)
    return jnp.concatenate([freqs, freqs, jnp.zeros((V7X_LANES - 2 * half,), F32)]).reshape(1, V7X_LANES)


def kernel(x, p, positions, rel_bias, ln1_g, ln1_b, ffn1_w1, ffn1_w3, ffn1_w2, w_in, q_norm_g, w_uq,
           kv_norm_g, w_ukv, swa_sinks, mla_out_g, swa_out_g, w_out, ln2_g, ln2_b, ffn2_w1, ffn2_w3,
           ffn2_w2, ln3_g, ln3_b, ple_w_gate, ple_w_proj):
    assert x.shape == (1, SEQ, D_MODEL) and DEPTH == 1
    row = lambda a: a.reshape(1, -1)
    h = x[0]
    pos_col = positions[0].astype(F32).reshape(SEQ, 1)
    freq_row = _rope_freq_row()
    bucket_tbl = jnp.asarray(_swa_bucket_table())

    i = 0
    h = _ffn_ln(h, ffn1_w1[i].astype(BF16), ffn1_w3[i].astype(BF16), ffn1_w2[i].astype(BF16),
                row(ln1_g[i]), row(ln1_b[i]))

    w_ukv_h = w_ukv[i].reshape(KV_LORA, MLA_HEADS, QK_NOPE + V_HEAD)
    wkn = w_ukv_h[:, :, :QK_NOPE].reshape(KV_LORA, MLA_WIDTH).astype(BF16)
    wv = w_ukv_h[:, :, QK_NOPE:].reshape(KV_LORA, MLA_WIDTH).astype(BF16)
    q, k, v, qs, ks, vs = _in_proj(
        h, _pack_w_in(w_in[i]).astype(BF16), row(q_norm_g[i]), row(kv_norm_g[i]),
        _pack_w_uq(w_uq[i]).astype(BF16), wkn, wv, pos_col, freq_row)

    a_mla = _mla_flash(q, k, v)
    sink_rows = jnp.repeat(swa_sinks[i], BLOCK).reshape(SWA_KV_HEADS, SWA_ROWS, 1)
    a_swa = _swa_attn(rel_bias, bucket_tbl, qs, ks, vs, sink_rows)

    w_out_b = w_out[i].astype(BF16)
    h = _out_proj(a_mla, a_swa, h, row(mla_out_g[i]), row(swa_out_g[i]),
                  w_out_b[:MLA_WIDTH], w_out_b[MLA_WIDTH:], row(ln2_g[i]), row(ln2_b[i]))

    h = _ffn_ln(h, ffn2_w1[i].astype(BF16), ffn2_w3[i].astype(BF16), ffn2_w2[i].astype(BF16),
                row(ln3_g[i]), row(ln3_b[i]))
    h = _ple(h, p[i, 0], ple_w_gate[i].astype(BF16), ple_w_proj[i].astype(BF16))
    return h[None]
```

```python
import math

import numpy as np
import jax
import jax.numpy as jnp
from jax import lax
from jax.experimental import pallas as pl
from jax.experimental.pallas import tpu as pltpu

F32 = jnp.float32
BF16 = jnp.bfloat16

D_MODEL = 2048
SEQ = 8192
DEPTH = 1
PLE_DIM = 256
MLA_HEADS = 8
Q_LORA = 512
KV_LORA = 512
QK_NOPE = 128
QK_ROPE = 64
V_HEAD = 128
ROPE_THETA = 10000.0
SWA_HEADS = 16
SWA_KV_HEADS = 2
SWA_GROUP = SWA_HEADS // SWA_KV_HEADS
SWA_HEAD_DIM = 64
WINDOW = 128
BLOCK = 128
REL_BUCKETS = 32
REL_MAX_DIST = 128
D_FF = 5632
ALPHA = (2.0 * DEPTH) ** 0.25
EPS = 1e-5
MLA_WIDTH = MLA_HEADS * V_HEAD
SWA_WIDTH = SWA_HEADS * SWA_HEAD_DIM
SWA_KV_WIDTH = SWA_KV_HEADS * SWA_HEAD_DIM

V7X_LANES = 128
V7X_VMEM_BYTES = 64 * 1024 * 1024
MIB = 1024 * 1024

MLA_QK_PAD = 2 * V7X_LANES
NEG_BIG = float(np.finfo(np.float32).min)


def _vmem_limit(estimate_bytes):
    return int(min(estimate_bytes + 8 * MIB, V7X_VMEM_BYTES - 4 * MIB))


def _layer_norm(y, g, b):
    mu = jnp.mean(y, axis=-1, keepdims=True)
    yc = y - mu
    var = jnp.mean(yc * yc, axis=-1, keepdims=True)
    return yc * lax.rsqrt(var + EPS) * g + b


def _rms_norm(x, g):
    ms = jnp.mean(x * x, axis=-1, keepdims=True)
    return x * lax.rsqrt(ms + EPS) * g


def _dot(a, b):
    return jnp.dot(a, b, preferred_element_type=F32)


def _dot_nt(a, b):
    return lax.dot_general(a, b, (((1,), (1,)), ((), ())), preferred_element_type=F32)


FFN_TM = 512
FFN_TF = 512


def _ffn_ln_kernel(x_ref, w1_ref, w3_ref, w2_ref, g_ref, b_ref, o_ref, xb_ref, acc_ref):
    f = pl.program_id(1)

    @pl.when(f == 0)
    def _():
        xb_ref[...] = x_ref[...].astype(BF16)
        acc_ref[...] = jnp.zeros_like(acc_ref)

    xb = xb_ref[...]
    gate = _dot(xb, w1_ref[...])
    up = _dot(xb, w3_ref[...])
    hidden = gate * (1.0 / (1.0 + jnp.exp(-gate))) * up
    acc_ref[...] += _dot(hidden.astype(BF16), w2_ref[...])

    @pl.when(f == pl.num_programs(1) - 1)
    def _():
        y = ALPHA * x_ref[...] + 0.5 * acc_ref[...]
        o_ref[...] = _layer_norm(y, g_ref[...], b_ref[...])


def _ffn_ln(x, w1, w3, w2, g, b):
    s, d = x.shape
    dff = w1.shape[1]
    tm, tf = FFN_TM, FFN_TF
    est = (2 * tm * d * 4 + tm * d * 2 + tm * d * 4 + 2 * 3 * d * tf * 2 + 2 * tm * d * 4
           + 4 * tm * tf * 4)
    return pl.pallas_call(
        _ffn_ln_kernel,
        grid=(s // tm, dff // tf),
        in_specs=[
            pl.BlockSpec((tm, d), lambda i, f: (i, 0)),
            pl.BlockSpec((d, tf), lambda i, f: (0, f)),
            pl.BlockSpec((d, tf), lambda i, f: (0, f)),
            pl.BlockSpec((tf, d), lambda i, f: (f, 0)),
            pl.BlockSpec((1, d), lambda i, f: (0, 0)),
            pl.BlockSpec((1, d), lambda i, f: (0, 0)),
        ],
        out_specs=pl.BlockSpec((tm, d), lambda i, f: (i, 0)),
        out_shape=jax.ShapeDtypeStruct((s, d), F32),
        scratch_shapes=[pltpu.VMEM((tm, d), BF16), pltpu.VMEM((tm, d), F32)],
        compiler_params=pltpu.CompilerParams(
            dimension_semantics=("parallel", "arbitrary"),
            vmem_limit_bytes=_vmem_limit(est)),
        name="ffn_ln",
    )(x, w1, w3, w2, g, b)


INPROJ_TM = 512
Z_CQ = 0
Z_CKV = Z_CQ + Q_LORA
Z_KS = Z_CKV + KV_LORA
Z_WIDTH = Z_KS + SWA_KV_WIDTH
ZT_QS = 0
ZT_VS = ZT_QS + SWA_WIDTH
ZT_KR = ZT_VS + SWA_KV_WIDTH
ZT_ROWS = ZT_KR + V7X_LANES
HALF_ROPE = QK_ROPE // 2


def _rope_rows(x1, x2, cos, sin):
    return x1 * cos - x2 * sin, x1 * sin + x2 * cos


def _in_proj_kernel(h_ref, wz_ref, wt_ref, qg_ref, kvg_ref, wuqt_ref, wkn_ref, wvt_ref, pos_ref,
                    freq_ref, qt_ref, k_ref, vt_ref, qst_ref, ks_ref, vst_ref):
    tm = h_ref.shape[0]
    hb = h_ref[...].astype(BF16)
    z = _dot(hb, wz_ref[...])
    zt = _dot_nt(wt_ref[...], hb)

    ang = freq_ref[...] * pos_ref[...]
    cos = jnp.cos(ang)
    sin = jnp.sin(ang)

    cq = _rms_norm(z[:, Z_CQ:Z_CQ + Q_LORA], qg_ref[...]).astype(BF16)
    ckv = _rms_norm(z[:, Z_CKV:Z_CKV + KV_LORA], kvg_ref[...]).astype(BF16)
    qt = _dot_nt(wuqt_ref[...], cq)
    kn = _dot(ckv, wkn_ref[...])
    vt = _dot_nt(wvt_ref[...], ckv)

    kr1, kr2 = _rope_rows(zt[ZT_KR:ZT_KR + HALF_ROPE], zt[ZT_KR + HALF_ROPE:ZT_KR + QK_ROPE], cos, sin)
    k_rope = jnp.concatenate([kr1, kr2, zt[ZT_KR + QK_ROPE:ZT_KR + V7X_LANES]], axis=0).T.astype(BF16)

    for h in range(MLA_HEADS):
        base = h * MLA_QK_PAD
        r0 = base + QK_NOPE
        qt_ref[base:r0, :] = qt[base:r0].astype(BF16)
        q1, q2 = _rope_rows(qt[r0:r0 + HALF_ROPE], qt[r0 + HALF_ROPE:r0 + QK_ROPE], cos, sin)
        qt_ref[r0:r0 + HALF_ROPE, :] = q1.astype(BF16)
        qt_ref[r0 + HALF_ROPE:r0 + QK_ROPE, :] = q2.astype(BF16)
        qt_ref[r0 + QK_ROPE:base + MLA_QK_PAD, :] = qt[r0 + QK_ROPE:base + MLA_QK_PAD].astype(BF16)
        k_ref[:, base:r0] = kn[:, h * QK_NOPE:(h + 1) * QK_NOPE].astype(BF16)
        k_ref[:, r0:base + MLA_QK_PAD] = k_rope
        vt_ref[h, 0] = vt[h * V_HEAD:(h + 1) * V_HEAD].astype(BF16)

    qst_ref[...] = zt[ZT_QS:ZT_QS + SWA_WIDTH].reshape(SWA_HEADS, SWA_HEAD_DIM, tm).astype(BF16)
    for kv in range(SWA_KV_HEADS):
        lo = Z_KS + kv * SWA_HEAD_DIM
        ks_ref[kv] = z[:, lo:lo + SWA_HEAD_DIM].astype(BF16)
        lo = ZT_VS + kv * SWA_HEAD_DIM
        for c in range(tm // BLOCK):
            vst_ref[kv, c] = zt[lo:lo + SWA_HEAD_DIM, c * BLOCK:(c + 1) * BLOCK].astype(BF16)


def _in_proj(h, wz, wt, q_g, kv_g, wuq_t, wkn, wv_t, pos_row, freq_col):
    s, d = h.shape
    tm = INPROJ_TM
    assert tm == MLA_TK
    est = (2 * tm * d * 4 + 2 * d * (Z_WIDTH + ZT_ROWS) * 2
           + 2 * Q_LORA * (MLA_HEADS * MLA_QK_PAD + 2 * MLA_WIDTH) * 2
           + 2 * tm * (2 * MLA_HEADS * MLA_QK_PAD + MLA_WIDTH + SWA_WIDTH + 4 * V7X_LANES) * 2
           + 4 * tm * (Z_WIDTH + ZT_ROWS + 2 * MLA_HEADS * MLA_QK_PAD) * 4)
    const = lambda i: (0, 0)
    return pl.pallas_call(
        _in_proj_kernel,
        grid=(s // tm,),
        in_specs=[
            pl.BlockSpec((tm, d), lambda i: (i, 0)),
            pl.BlockSpec((d, Z_WIDTH), const),
            pl.BlockSpec((ZT_ROWS, d), const),
            pl.BlockSpec((1, Q_LORA), const),
            pl.BlockSpec((1, KV_LORA), const),
            pl.BlockSpec((MLA_HEADS * MLA_QK_PAD, Q_LORA), const),
            pl.BlockSpec((KV_LORA, MLA_WIDTH), const),
            pl.BlockSpec((MLA_WIDTH, KV_LORA), const),
            pl.BlockSpec((1, tm), lambda i: (0, i)),
            pl.BlockSpec((HALF_ROPE, 1), const),
        ],
        out_specs=[
            pl.BlockSpec((MLA_HEADS * MLA_QK_PAD, tm), lambda i: (0, i)),
            pl.BlockSpec((tm, MLA_HEADS * MLA_QK_PAD), lambda i: (i, 0)),
            pl.BlockSpec((MLA_HEADS, 1, V_HEAD, tm), lambda i: (0, i, 0, 0)),
            pl.BlockSpec((SWA_HEADS, SWA_HEAD_DIM, tm), lambda i: (0, 0, i)),
            pl.BlockSpec((SWA_KV_HEADS, tm, SWA_HEAD_DIM), lambda i: (0, i, 0)),
            pl.BlockSpec((SWA_KV_HEADS, tm // BLOCK, SWA_HEAD_DIM, BLOCK), lambda i: (0, i, 0, 0)),
        ],
        out_shape=[
            jax.ShapeDtypeStruct((MLA_HEADS * MLA_QK_PAD, s), BF16),
            jax.ShapeDtypeStruct((s, MLA_HEADS * MLA_QK_PAD), BF16),
            jax.ShapeDtypeStruct((MLA_HEADS, s // tm, V_HEAD, tm), BF16),
            jax.ShapeDtypeStruct((SWA_HEADS, SWA_HEAD_DIM, s), BF16),
            jax.ShapeDtypeStruct((SWA_KV_HEADS, s, SWA_HEAD_DIM), BF16),
            jax.ShapeDtypeStruct((SWA_KV_HEADS, s // BLOCK, SWA_HEAD_DIM, BLOCK), BF16),
        ],
        compiler_params=pltpu.CompilerParams(
            dimension_semantics=("parallel",),
            vmem_limit_bytes=_vmem_limit(est)),
        name="in_proj",
    )(h, wz, wt, q_g, kv_g, wuq_t, wkn, wv_t, pos_row, freq_col)


MLA_TQ = 512
MLA_TK = 512
MLA_SCALE = (QK_NOPE + QK_ROPE) ** -0.5


def _mla_flash_kernel(qt_ref, k_ref, vt_ref, o_ref, s_ref, m_ref, l_ref, acc_ref):
    i = pl.program_id(1)
    tk = MLA_TK
    qt = qt_ref[...]

    m_ref[...] = jnp.full_like(m_ref, NEG_BIG)
    l_ref[...] = jnp.zeros_like(l_ref)
    acc_ref[...] = jnp.zeros_like(acc_ref)

    def scores(j):
        start = pl.multiple_of(j * tk, tk)
        return _dot(k_ref[pl.ds(start, tk), :], qt)

    def softmax_pv(s, j, masked):
        s = s * MLA_SCALE
        if masked:
            key = lax.broadcasted_iota(jnp.int32, s.shape, 0)
            qry = lax.broadcasted_iota(jnp.int32, s.shape, 1)
            s = jnp.where(key <= qry, s, NEG_BIG)
        m_prev = m_ref[...]
        m_new = jnp.maximum(m_prev, jnp.max(s, axis=0, keepdims=True))
        p = jnp.exp(s - m_new)
        corr = jnp.exp(m_prev - m_new)
        l_ref[...] = corr * l_ref[...] + jnp.sum(p, axis=0, keepdims=True)
        acc_ref[...] = corr * acc_ref[...] + _dot(vt_ref[j], p.astype(BF16))
        m_ref[...] = m_new

    s_ref[...] = scores(0)

    def body(j, carry):
        s = s_ref[...]
        s_ref[...] = scores(j + 1)
        softmax_pv(s, j, masked=False)
        return carry

    lax.fori_loop(0, i, body, 0)
    softmax_pv(s_ref[...], i, masked=True)
    o_ref[...] = (acc_ref[...] / l_ref[...]).T


def _mla_flash(qt, k, vt):
    s = k.shape[0]
    tq, tk = MLA_TQ, MLA_TK
    assert tq == tk
    est = (2 * tq * MLA_QK_PAD * 2 + 2 * s * MLA_QK_PAD * 2 + 2 * s * V_HEAD * 2 + 2 * tq * V_HEAD * 4
           + tk * tq * 4 + tq * V_HEAD * 4 + 4 * tq * tk * 4)
    return pl.pallas_call(
        _mla_flash_kernel,
        grid=(MLA_HEADS, s // tq),
        in_specs=[
            pl.BlockSpec((MLA_QK_PAD, tq), lambda h, i: (h, i)),
            pl.BlockSpec((s, MLA_QK_PAD), lambda h, i: (0, h)),
            pl.BlockSpec((None, s // tk, V_HEAD, tk), lambda h, i: (h, 0, 0, 0)),
        ],
        out_specs=pl.BlockSpec((tq, V_HEAD), lambda h, i: (i, h)),
        out_shape=jax.ShapeDtypeStruct((s, MLA_WIDTH), F32),
        scratch_shapes=[pltpu.VMEM((tk, tq), F32), pltpu.VMEM((1, tq), F32),
                        pltpu.VMEM((1, tq), F32), pltpu.VMEM((V_HEAD, tq), F32)],
        compiler_params=pltpu.CompilerParams(
            dimension_semantics=("parallel", "arbitrary"),
            vmem_limit_bytes=_vmem_limit(est)),
        name="mla_flash",
    )(qt, k, vt)


SWA_SCALE = SWA_HEAD_DIM ** -0.5
SWA_COLS = SWA_GROUP * BLOCK
SWA_NB = 4


def _t5_bucket_np(dist):
    n = np.maximum(dist, 0)
    max_exact = REL_BUCKETS // 2
    large = max_exact + (np.log(np.maximum(n, 1).astype(np.float32) / max_exact)
                         / math.log(REL_MAX_DIST / max_exact)
                         * (REL_BUCKETS - max_exact)).astype(np.int32)
    large = np.minimum(large, REL_BUCKETS - 1)
    return np.where(n < max_exact, n, large).astype(np.int32)


def _swa_bucket_table():
    j = np.arange(BLOCK)[:, None]
    i = np.arange(BLOCK)[None, :]
    dist = np.where(j <= i, i - j, BLOCK + i - j)
    return _t5_bucket_np(dist)


def _swa_kernel(rb_ref, bkt_ref, qt_ref, k_ref, vt_ref, sink_ref, o_ref, bias_ref):
    kv = pl.program_id(0)
    n = pl.program_id(1)

    @pl.when((kv == 0) & (n == 0))
    def _():
        bkt = bkt_ref[...]
        for h in range(SWA_HEADS):
            acc = jnp.zeros((BLOCK, BLOCK), F32)
            for b in range(REL_BUCKETS):
                acc = jnp.where(bkt == b, rb_ref[b, h], acc)
            g = h % SWA_GROUP
            bias_ref[h // SWA_GROUP, :, g * BLOCK:(g + 1) * BLOCK] = acc

    key = lax.broadcasted_iota(jnp.int32, (BLOCK, SWA_COLS), 0)
    qry = lax.broadcasted_iota(jnp.int32, (BLOCK, SWA_COLS), 1) & (BLOCK - 1)
    lower = key <= qry
    bias = bias_ref[kv]
    sink = sink_ref[...]

    for b in range(SWA_NB):
        blk = n * SWA_NB + b
        qt = jnp.concatenate([qt_ref[g, :, b * BLOCK:(b + 1) * BLOCK] for g in range(SWA_GROUP)],
                             axis=1)
        cur = pl.multiple_of(blk * BLOCK, BLOCK)
        prev_blk = jnp.maximum(blk - 1, 0)
        prev = pl.multiple_of(prev_blk * BLOCK, BLOCK)
        k_band = jnp.concatenate([k_ref[pl.ds(prev, BLOCK), :], k_ref[pl.ds(cur, BLOCK), :]], axis=0)
        s_band = _dot(k_band, qt)
        s = jnp.where(lower, s_band[BLOCK:], s_band[:BLOCK]) * SWA_SCALE + bias
        if b == 0:
            s = jnp.where(lower | (blk > 0), s, NEG_BIG)

        m = jnp.maximum(jnp.max(s, axis=0, keepdims=True), sink)
        e = jnp.exp(s - m)
        denom = jnp.sum(e, axis=0, keepdims=True) + jnp.exp(sink - m)
        p = e * (1.0 / denom)
        p_cur = jnp.where(lower, p, 0.0).astype(BF16)
        p_prev = jnp.where(lower, 0.0, p).astype(BF16)
        ot = _dot(vt_ref[blk], p_cur) + _dot(vt_ref[prev_blk], p_prev)
        o_ref[b * BLOCK:(b + 1) * BLOCK, :] = jnp.concatenate(
            [ot[:, g * BLOCK:(g + 1) * BLOCK] for g in range(SWA_GROUP)], axis=0).T


def _swa_attn(rel_bias, bucket_tbl, qst, ks, vst, sink_rows):
    s = ks.shape[1]
    rows = SWA_NB * BLOCK
    est = (2 * SWA_GROUP * SWA_HEAD_DIM * rows * 2 + 2 * s * V7X_LANES * 2 + 2 * s * SWA_HEAD_DIM * 2
           + 2 * rows * SWA_GROUP * SWA_HEAD_DIM * 4 + SWA_KV_HEADS * BLOCK * SWA_COLS * 4
           + 10 * SWA_NB * BLOCK * SWA_COLS * 4)
    return pl.pallas_call(
        _swa_kernel,
        grid=(SWA_KV_HEADS, s // rows),
        in_specs=[
            pl.BlockSpec(memory_space=pltpu.SMEM),
            pl.BlockSpec((BLOCK, BLOCK), lambda kv, n: (0, 0)),
            pl.BlockSpec((SWA_GROUP, SWA_HEAD_DIM, rows), lambda kv, n: (kv, 0, n)),
            pl.BlockSpec((None, s, SWA_HEAD_DIM), lambda kv, n: (kv, 0, 0)),
            pl.BlockSpec((None, s // BLOCK, SWA_HEAD_DIM, BLOCK), lambda kv, n: (kv, 0, 0, 0)),
            pl.BlockSpec((None, 1, SWA_COLS), lambda kv, n: (kv, 0, 0)),
        ],
        out_specs=pl.BlockSpec((rows, SWA_GROUP * SWA_HEAD_DIM), lambda kv, n: (n, kv)),
        out_shape=jax.ShapeDtypeStruct((s, SWA_WIDTH), F32),
        scratch_shapes=[pltpu.VMEM((SWA_KV_HEADS, BLOCK, SWA_COLS), F32)],
        compiler_params=pltpu.CompilerParams(
            dimension_semantics=("arbitrary", "arbitrary"),
            vmem_limit_bytes=_vmem_limit(est)),
        name="swa_attn",
    )(rel_bias, bucket_tbl, qst, ks, vst, sink_rows)


OUTPROJ_TM = 512


def _out_proj_kernel(am_ref, as_ref, h_ref, mg_ref, sg_ref, wm_ref, ws_ref, g_ref, b_ref, o_ref):
    nm = _rms_norm(am_ref[...], mg_ref[...]).astype(BF16)
    ns = _rms_norm(as_ref[...], sg_ref[...]).astype(BF16)
    mixed = _dot(nm, wm_ref[...]) + _dot(ns, ws_ref[...])
    o_ref[...] = _layer_norm(ALPHA * h_ref[...] + mixed, g_ref[...], b_ref[...])


def _out_proj(a_mla, a_swa, h, mla_g, swa_g, w_mla, w_swa, g, b):
    s, d = h.shape
    tm = OUTPROJ_TM
    est = (2 * tm * (MLA_WIDTH + SWA_WIDTH) * 4 + 4 * tm * d * 4 + 2 * (MLA_WIDTH + SWA_WIDTH) * d * 2
           + 4 * tm * d * 4)
    const = lambda i: (0, 0)
    return pl.pallas_call(
        _out_proj_kernel,
        grid=(s // tm,),
        in_specs=[
            pl.BlockSpec((tm, MLA_WIDTH), lambda i: (i, 0)),
            pl.BlockSpec((tm, SWA_WIDTH), lambda i: (i, 0)),
            pl.BlockSpec((tm, d), lambda i: (i, 0)),
            pl.BlockSpec((1, MLA_WIDTH), const),
            pl.BlockSpec((1, SWA_WIDTH), const),
            pl.BlockSpec((MLA_WIDTH, d), const),
            pl.BlockSpec((SWA_WIDTH, d), const),
            pl.BlockSpec((1, d), const),
            pl.BlockSpec((1, d), const),
        ],
        out_specs=pl.BlockSpec((tm, d), lambda i: (i, 0)),
        out_shape=jax.ShapeDtypeStruct((s, d), F32),
        compiler_params=pltpu.CompilerParams(
            dimension_semantics=("parallel",),
            vmem_limit_bytes=_vmem_limit(est)),
        name="out_proj",
    )(a_mla, a_swa, h, mla_g, swa_g, w_mla, w_swa, g, b)


PLE_TM = 512


def _ple_kernel(h_ref, p_ref, wg_ref, wp_ref, o_ref):
    h = h_ref[...]
    gate = _dot(h.astype(BF16), wg_ref[...])
    proj = _dot(p_ref[...].astype(BF16), wp_ref[...])
    o_ref[...] = h + (1.0 / (1.0 + jnp.exp(-gate))) * proj


def _ple(h, p, w_gate, w_proj):
    s, d = h.shape
    tm = PLE_TM
    est = 4 * tm * d * 4 + 2 * tm * PLE_DIM * 4 + 2 * (d + PLE_DIM) * d * 2 + 4 * tm * d * 4
    const = lambda i: (0, 0)
    return pl.pallas_call(
        _ple_kernel,
        grid=(s // tm,),
        in_specs=[
            pl.BlockSpec((tm, d), lambda i: (i, 0)),
            pl.BlockSpec((tm, PLE_DIM), lambda i: (i, 0)),
            pl.BlockSpec((d, d), const),
            pl.BlockSpec((PLE_DIM, d), const),
        ],
        out_specs=pl.BlockSpec((tm, d), lambda i: (i, 0)),
        out_shape=jax.ShapeDtypeStruct((s, d), F32),
        compiler_params=pltpu.CompilerParams(
            dimension_semantics=("parallel",),
            vmem_limit_bytes=_vmem_limit(est)),
        name="ple",
    )(h, p, w_gate, w_proj)


def _split_w_in(w_in):
    d = w_in.shape[0]
    b_kr = Q_LORA + KV_LORA
    b_qs = b_kr + QK_ROPE
    b_ks = b_qs + SWA_WIDTH
    b_vs = b_ks + SWA_KV_WIDTH
    wz = jnp.concatenate([w_in[:, :b_kr], w_in[:, b_ks:b_vs]], axis=1)
    pad = jnp.zeros((d, V7X_LANES - QK_ROPE), w_in.dtype)
    wt = jnp.concatenate([w_in[:, b_qs:b_ks], w_in[:, b_vs:], w_in[:, b_kr:b_qs], pad], axis=1).T
    return wz, wt


def _pack_w_uq_t(w_uq):
    r = w_uq.shape[0]
    w = w_uq.reshape(r, MLA_HEADS, QK_NOPE + QK_ROPE)
    w = jnp.pad(w, ((0, 0), (0, 0), (0, MLA_QK_PAD - QK_NOPE - QK_ROPE)))
    return w.reshape(r, MLA_HEADS * MLA_QK_PAD).T


def _rope_freq_col():
    freqs = ROPE_THETA ** (-jnp.arange(0, QK_ROPE, 2, dtype=F32) / QK_ROPE)
    return freqs.reshape(HALF_ROPE, 1)


def kernel(x, p, positions, rel_bias, ln1_g, ln1_b, ffn1_w1, ffn1_w3, ffn1_w2, w_in, q_norm_g, w_uq,
           kv_norm_g, w_ukv, swa_sinks, mla_out_g, swa_out_g, w_out, ln2_g, ln2_b, ffn2_w1, ffn2_w3,
           ffn2_w2, ln3_g, ln3_b, ple_w_gate, ple_w_proj):
    assert x.shape == (1, SEQ, D_MODEL) and DEPTH == 1
    row = lambda a: a.reshape(1, -1)
    h = x[0]
    pos_row = positions[0].astype(F32).reshape(1, SEQ)
    freq_col = _rope_freq_col()
    bucket_tbl = jnp.asarray(_swa_bucket_table())

    i = 0
    h = _ffn_ln(h, ffn1_w1[i].astype(BF16), ffn1_w3[i].astype(BF16), ffn1_w2[i].astype(BF16),
                row(ln1_g[i]), row(ln1_b[i]))

    w_ukv_h = w_ukv[i].reshape(KV_LORA, MLA_HEADS, QK_NOPE + V_HEAD)
    wkn = w_ukv_h[:, :, :QK_NOPE].reshape(KV_LORA, MLA_WIDTH).astype(BF16)
    wv_t = w_ukv_h[:, :, QK_NOPE:].reshape(KV_LORA, MLA_WIDTH).T.astype(BF16)
    wz, wt = _split_w_in(w_in[i])
    qt, k, vt, qst, ks, vst = _in_proj(
        h, wz.astype(BF16), wt.astype(BF16), row(q_norm_g[i]), row(kv_norm_g[i]),
        _pack_w_uq_t(w_uq[i]).astype(BF16), wkn, wv_t, pos_row, freq_col)

    a_mla = _mla_flash(qt, k, vt)
    sink_rows = jnp.repeat(swa_sinks[i], BLOCK).reshape(SWA_KV_HEADS, 1, SWA_COLS)
    a_swa = _swa_attn(rel_bias, bucket_tbl, qst, ks, vst, sink_rows)

    w_out_b = w_out[i].astype(BF16)
    h = _out_proj(a_mla, a_swa, h, row(mla_out_g[i]), row(swa_out_g[i]),
                  w_out_b[:MLA_WIDTH], w_out_b[MLA_WIDTH:], row(ln2_g[i]), row(ln2_b[i]))

    h = _ffn_ln(h, ffn2_w1[i].astype(BF16), ffn2_w3[i].astype(BF16), ffn2_w2[i].astype(BF16),
                row(ln3_g[i]), row(ln3_b[i]))
    h = _ple(h, p[i, 0], ple_w_gate[i].astype(BF16), ple_w_proj[i].astype(BF16))
    return h[None]
```

```python
import math

import numpy as np
import jax
import jax.numpy as jnp
from jax import lax
from jax.experimental import pallas as pl
from jax.experimental.pallas import tpu as pltpu

F32 = jnp.float32
BF16 = jnp.bfloat16

D_MODEL = 2048
SEQ = 8192
DEPTH = 1
PLE_DIM = 256
MLA_HEADS = 8
Q_LORA = 512
KV_LORA = 512
QK_NOPE = 128
QK_ROPE = 64
V_HEAD = 128
ROPE_THETA = 10000.0
SWA_HEADS = 16
SWA_KV_HEADS = 2
SWA_GROUP = SWA_HEADS // SWA_KV_HEADS
SWA_HEAD_DIM = 64
WINDOW = 128
BLOCK = 128
REL_BUCKETS = 32
REL_MAX_DIST = 128
D_FF = 5632
ALPHA = (2.0 * DEPTH) ** 0.25
EPS = 1e-5
MLA_WIDTH = MLA_HEADS * V_HEAD
SWA_WIDTH = SWA_HEADS * SWA_HEAD_DIM
SWA_KV_WIDTH = SWA_KV_HEADS * SWA_HEAD_DIM

V7X_LANES = 128
V7X_VMEM_BYTES = 64 * 1024 * 1024
MIB = 1024 * 1024

V7X_BF16_SUBLANES = 16

MLA_QK_PAD = 2 * V7X_LANES
MLA_V_ROWS = V_HEAD + V7X_BF16_SUBLANES
NEG_BIG = float(np.finfo(np.float32).min)


def _vmem_limit(estimate_bytes):
    return int(min(estimate_bytes + 8 * MIB, V7X_VMEM_BYTES - 4 * MIB))


def _layer_norm(y, g, b):
    mu = jnp.mean(y, axis=-1, keepdims=True)
    yc = y - mu
    var = jnp.mean(yc * yc, axis=-1, keepdims=True)
    return yc * lax.rsqrt(var + EPS) * g + b


def _rms_norm(x, g):
    ms = jnp.mean(x * x, axis=-1, keepdims=True)
    return x * lax.rsqrt(ms + EPS) * g


def _dot(a, b):
    return jnp.dot(a, b, preferred_element_type=F32)


def _dot_nt(a, b):
    return lax.dot_general(a, b, (((1,), (1,)), ((), ())), preferred_element_type=F32)


FFN_TM = 512
FFN_TF = 512


def _ffn_ln_kernel(x_ref, w1_ref, w3_ref, w2_ref, g_ref, b_ref, o_ref, xb_ref, acc_ref):
    f = pl.program_id(1)

    @pl.when(f == 0)
    def _():
        xb_ref[...] = x_ref[...].astype(BF16)
        acc_ref[...] = jnp.zeros_like(acc_ref)

    xb = xb_ref[...]
    gate = _dot(xb, w1_ref[...])
    up = _dot(xb, w3_ref[...])
    hidden = gate * (1.0 / (1.0 + jnp.exp(-gate))) * up
    acc_ref[...] += _dot(hidden.astype(BF16), w2_ref[...])

    @pl.when(f == pl.num_programs(1) - 1)
    def _():
        y = ALPHA * x_ref[...] + 0.5 * acc_ref[...]
        o_ref[...] = _layer_norm(y, g_ref[...], b_ref[...])


def _ffn_ln(x, w1, w3, w2, g, b):
    s, d = x.shape
    dff = w1.shape[1]
    tm, tf = FFN_TM, FFN_TF
    est = (2 * tm * d * 4 + tm * d * 2 + tm * d * 4 + 2 * 3 * d * tf * 2 + 2 * tm * d * 4
           + 4 * tm * tf * 4)
    return pl.pallas_call(
        _ffn_ln_kernel,
        grid=(s // tm, dff // tf),
        in_specs=[
            pl.BlockSpec((tm, d), lambda i, f: (i, 0)),
            pl.BlockSpec((d, tf), lambda i, f: (0, f)),
            pl.BlockSpec((d, tf), lambda i, f: (0, f)),
            pl.BlockSpec((tf, d), lambda i, f: (f, 0)),
            pl.BlockSpec((1, d), lambda i, f: (0, 0)),
            pl.BlockSpec((1, d), lambda i, f: (0, 0)),
        ],
        out_specs=pl.BlockSpec((tm, d), lambda i, f: (i, 0)),
        out_shape=jax.ShapeDtypeStruct((s, d), F32),
        scratch_shapes=[pltpu.VMEM((tm, d), BF16), pltpu.VMEM((tm, d), F32)],
        compiler_params=pltpu.CompilerParams(
            dimension_semantics=("parallel", "arbitrary"),
            vmem_limit_bytes=_vmem_limit(est)),
        name="ffn_ln",
    )(x, w1, w3, w2, g, b)


INPROJ_TM = 512
Z_CQ = 0
Z_CKV = Z_CQ + Q_LORA
Z_KS = Z_CKV + KV_LORA
Z_WIDTH = Z_KS + SWA_KV_WIDTH
ZT_QS = 0
ZT_VS = ZT_QS + SWA_WIDTH
ZT_KR = ZT_VS + SWA_KV_WIDTH
ZT_ROWS = ZT_KR + V7X_LANES
HALF_ROPE = QK_ROPE // 2


def _rope_rows(x1, x2, cos, sin):
    return x1 * cos - x2 * sin, x1 * sin + x2 * cos


def _in_proj_kernel(h_ref, wz_ref, wt_ref, qg_ref, kvg_ref, wuqt_ref, wkn_ref, wvt_ref, pos_ref,
                    freq_ref, qt_ref, k_ref, vt_ref, qst_ref, ks_ref, vst_ref):
    tm = h_ref.shape[0]
    hb = h_ref[...].astype(BF16)
    z = _dot(hb, wz_ref[...])
    zt = _dot_nt(wt_ref[...], hb)

    ang = freq_ref[...] * pos_ref[...]
    cos = jnp.cos(ang)
    sin = jnp.sin(ang)

    cq = _rms_norm(z[:, Z_CQ:Z_CQ + Q_LORA], qg_ref[...]).astype(BF16)
    ckv = _rms_norm(z[:, Z_CKV:Z_CKV + KV_LORA], kvg_ref[...]).astype(BF16)
    qt = _dot_nt(wuqt_ref[...], cq)
    kn = _dot(ckv, wkn_ref[...])
    vt = _dot_nt(wvt_ref[...], ckv)

    kr1, kr2 = _rope_rows(zt[ZT_KR:ZT_KR + HALF_ROPE], zt[ZT_KR + HALF_ROPE:ZT_KR + QK_ROPE], cos, sin)
    k_rope = jnp.concatenate([kr1, kr2, zt[ZT_KR + QK_ROPE:ZT_KR + V7X_LANES]], axis=0).T.astype(BF16)

    for h in range(MLA_HEADS):
        base = h * MLA_QK_PAD
        r0 = base + QK_NOPE
        qt_ref[base:r0, :] = qt[base:r0].astype(BF16)
        q1, q2 = _rope_rows(qt[r0:r0 + HALF_ROPE], qt[r0 + HALF_ROPE:r0 + QK_ROPE], cos, sin)
        qt_ref[r0:r0 + HALF_ROPE, :] = q1.astype(BF16)
        qt_ref[r0 + HALF_ROPE:r0 + QK_ROPE, :] = q2.astype(BF16)
        qt_ref[r0 + QK_ROPE:base + MLA_QK_PAD, :] = qt[r0 + QK_ROPE:base + MLA_QK_PAD].astype(BF16)
        k_ref[:, base:r0] = kn[:, h * QK_NOPE:(h + 1) * QK_NOPE].astype(BF16)
        k_ref[:, r0:base + MLA_QK_PAD] = k_rope
        vt_ref[h, 0, 0:V_HEAD, :] = vt[h * V_HEAD:(h + 1) * V_HEAD].astype(BF16)
        vt_ref[h, 0, V_HEAD:MLA_V_ROWS, :] = jnp.ones((MLA_V_ROWS - V_HEAD, tm), BF16)

    qst_ref[...] = zt[ZT_QS:ZT_QS + SWA_WIDTH].reshape(SWA_HEADS, SWA_HEAD_DIM, tm).astype(BF16)
    for kv in range(SWA_KV_HEADS):
        lo = Z_KS + kv * SWA_HEAD_DIM
        ks_ref[kv] = z[:, lo:lo + SWA_HEAD_DIM].astype(BF16)
        lo = ZT_VS + kv * SWA_HEAD_DIM
        for c in range(tm // BLOCK):
            vst_ref[kv, c] = zt[lo:lo + SWA_HEAD_DIM, c * BLOCK:(c + 1) * BLOCK].astype(BF16)


def _in_proj(h, wz, wt, q_g, kv_g, wuq_t, wkn, wv_t, pos_row, freq_col):
    s, d = h.shape
    tm = INPROJ_TM
    assert tm == MLA_TK
    est = (2 * tm * d * 4 + 2 * d * (Z_WIDTH + ZT_ROWS) * 2
           + 2 * Q_LORA * (MLA_HEADS * MLA_QK_PAD + 2 * MLA_WIDTH) * 2
           + 2 * tm * (2 * MLA_HEADS * MLA_QK_PAD + MLA_WIDTH + SWA_WIDTH + 4 * V7X_LANES) * 2
           + 4 * tm * (Z_WIDTH + ZT_ROWS + 2 * MLA_HEADS * MLA_QK_PAD) * 4)
    const = lambda i: (0, 0)
    return pl.pallas_call(
        _in_proj_kernel,
        grid=(s // tm,),
        in_specs=[
            pl.BlockSpec((tm, d), lambda i: (i, 0)),
            pl.BlockSpec((d, Z_WIDTH), const),
            pl.BlockSpec((ZT_ROWS, d), const),
            pl.BlockSpec((1, Q_LORA), const),
            pl.BlockSpec((1, KV_LORA), const),
            pl.BlockSpec((MLA_HEADS * MLA_QK_PAD, Q_LORA), const),
            pl.BlockSpec((KV_LORA, MLA_WIDTH), const),
            pl.BlockSpec((MLA_WIDTH, KV_LORA), const),
            pl.BlockSpec((1, tm), lambda i: (0, i)),
            pl.BlockSpec((HALF_ROPE, 1), const),
        ],
        out_specs=[
            pl.BlockSpec((MLA_HEADS * MLA_QK_PAD, tm), lambda i: (0, i)),
            pl.BlockSpec((tm, MLA_HEADS * MLA_QK_PAD), lambda i: (i, 0)),
            pl.BlockSpec((MLA_HEADS, 1, MLA_V_ROWS, tm), lambda i: (0, i, 0, 0)),
            pl.BlockSpec((SWA_HEADS, SWA_HEAD_DIM, tm), lambda i: (0, 0, i)),
            pl.BlockSpec((SWA_KV_HEADS, tm, SWA_HEAD_DIM), lambda i: (0, i, 0)),
            pl.BlockSpec((SWA_KV_HEADS, tm // BLOCK, SWA_HEAD_DIM, BLOCK), lambda i: (0, i, 0, 0)),
        ],
        out_shape=[
            jax.ShapeDtypeStruct((MLA_HEADS * MLA_QK_PAD, s), BF16),
            jax.ShapeDtypeStruct((s, MLA_HEADS * MLA_QK_PAD), BF16),
            jax.ShapeDtypeStruct((MLA_HEADS, s // tm, MLA_V_ROWS, tm), BF16),
            jax.ShapeDtypeStruct((SWA_HEADS, SWA_HEAD_DIM, s), BF16),
            jax.ShapeDtypeStruct((SWA_KV_HEADS, s, SWA_HEAD_DIM), BF16),
            jax.ShapeDtypeStruct((SWA_KV_HEADS, s // BLOCK, SWA_HEAD_DIM, BLOCK), BF16),
        ],
        compiler_params=pltpu.CompilerParams(
            dimension_semantics=("parallel",),
            vmem_limit_bytes=_vmem_limit(est)),
        name="in_proj",
    )(h, wz, wt, q_g, kv_g, wuq_t, wkn, wv_t, pos_row, freq_col)


MLA_TQ = 512
MLA_TK = 512
MLA_HB = 2
MLA_SCALE = (QK_NOPE + QK_ROPE) ** -0.5
MLA_SCALE_LOG2E = MLA_SCALE * math.log2(math.e)


def _mla_flash_kernel(qt_ref, k_ref, vt_ref, o_ref, s_ref, p_ref, m_ref, corr_ref, acc_ref):
    i = pl.program_id(1)
    tk = MLA_TK
    heads = range(MLA_HB)

    def qk(t):
        start = pl.multiple_of(t * tk, tk)
        for h in heads:
            cols = slice(h * MLA_QK_PAD, (h + 1) * MLA_QK_PAD)
            s_ref[h] = _dot(k_ref[pl.ds(start, tk), cols], qt_ref[cols, :])

    def softmax(s_all, masked):
        for h in heads:
            s = s_all[h] * MLA_SCALE_LOG2E
            if masked:
                key = lax.broadcasted_iota(jnp.int32, s.shape, 0)
                qry = lax.broadcasted_iota(jnp.int32, s.shape, 1)
                s = jnp.where(key <= qry, s, NEG_BIG)
            m_prev = m_ref[h]
            m_new = jnp.maximum(m_prev, jnp.max(s, axis=0, keepdims=True))
            p_ref[h] = jnp.exp2(s - m_new).astype(BF16)
            corr_ref[h] = jnp.exp2(m_prev - m_new)
            m_ref[h] = m_new

    def pv(t):
        for h in heads:
            acc_ref[h] = corr_ref[h] * acc_ref[h] + _dot(vt_ref[h, t], p_ref[h])

    def load_scores():
        return [s_ref[h] for h in heads]

    m_ref[...] = jnp.full_like(m_ref, NEG_BIG)
    acc_ref[...] = jnp.zeros_like(acc_ref)

    @pl.when(i == 0)
    def _():
        p_ref[...] = jnp.zeros_like(p_ref)
        corr_ref[...] = jnp.zeros_like(corr_ref)

    qk(0)

    @pl.when(i >= 1)
    def _():
        s_all = load_scores()
        qk(1)
        softmax(s_all, masked=False)

    def body(t, carry):
        pv(t)
        s_all = load_scores()
        qk(t + 2)
        softmax(s_all, masked=False)
        return carry

    lax.fori_loop(0, i - 1, body, 0)
    pv(jnp.maximum(i - 1, 0))
    softmax(load_scores(), masked=True)
    pv(i)
    for h in heads:
        o_ref[:, h * V_HEAD:(h + 1) * V_HEAD] = (
            acc_ref[h, 0:V_HEAD, :] / acc_ref[h, V_HEAD:V_HEAD + 1, :]).T


def _mla_flash(qt, k, vt):
    s = k.shape[0]
    tq, tk, hb = MLA_TQ, MLA_TK, MLA_HB
    assert tq == tk
    est = (2 * hb * tq * MLA_QK_PAD * 2 + 2 * hb * s * MLA_QK_PAD * 2 + 2 * hb * s * MLA_V_ROWS * 2
           + 2 * hb * tq * V_HEAD * 4 + hb * tk * tq * (4 + 2) + hb * tq * MLA_V_ROWS * 4
           + 4 * hb * tq * tk * 4)
    return pl.pallas_call(
        _mla_flash_kernel,
        grid=(MLA_HEADS // hb, s // tq),
        in_specs=[
            pl.BlockSpec((hb * MLA_QK_PAD, tq), lambda g, i: (g, i)),
            pl.BlockSpec((s, hb * MLA_QK_PAD), lambda g, i: (0, g)),
            pl.BlockSpec((hb, s // tk, MLA_V_ROWS, tk), lambda g, i: (g, 0, 0, 0)),
        ],
        out_specs=pl.BlockSpec((tq, hb * V_HEAD), lambda g, i: (i, g)),
        out_shape=jax.ShapeDtypeStruct((s, MLA_WIDTH), F32),
        scratch_shapes=[pltpu.VMEM((hb, tk, tq), F32), pltpu.VMEM((hb, tk, tq), BF16),
                        pltpu.VMEM((hb, 1, tq), F32), pltpu.VMEM((hb, 1, tq), F32),
                        pltpu.VMEM((hb, MLA_V_ROWS, tq), F32)],
        compiler_params=pltpu.CompilerParams(
            dimension_semantics=("parallel", "arbitrary"),
            vmem_limit_bytes=_vmem_limit(est)),
        name="mla_flash",
    )(qt, k, vt)


SWA_SCALE = SWA_HEAD_DIM ** -0.5
SWA_COLS = SWA_GROUP * BLOCK
SWA_NB = 4


def _t5_bucket_np(dist):
    n = np.maximum(dist, 0)
    max_exact = REL_BUCKETS // 2
    large = max_exact + (np.log(np.maximum(n, 1).astype(np.float32) / max_exact)
                         / math.log(REL_MAX_DIST / max_exact)
                         * (REL_BUCKETS - max_exact)).astype(np.int32)
    large = np.minimum(large, REL_BUCKETS - 1)
    return np.where(n < max_exact, n, large).astype(np.int32)


def _swa_bucket_table():
    j = np.arange(BLOCK)[:, None]
    i = np.arange(BLOCK)[None, :]
    dist = np.where(j <= i, i - j, BLOCK + i - j)
    return _t5_bucket_np(dist)


def _swa_kernel(rb_ref, bkt_ref, qt_ref, k_ref, vt_ref, sink_ref, o_ref, bias_ref):
    kv = pl.program_id(0)
    n = pl.program_id(1)

    @pl.when((kv == 0) & (n == 0))
    def _():
        bkt = bkt_ref[...]
        for h in range(SWA_HEADS):
            acc = jnp.zeros((BLOCK, BLOCK), F32)
            for b in range(REL_BUCKETS):
                acc = jnp.where(bkt == b, rb_ref[b, h], acc)
            g = h % SWA_GROUP
            bias_ref[h // SWA_GROUP, :, g * BLOCK:(g + 1) * BLOCK] = acc

    key = lax.broadcasted_iota(jnp.int32, (BLOCK, SWA_COLS), 0)
    qry = lax.broadcasted_iota(jnp.int32, (BLOCK, SWA_COLS), 1) & (BLOCK - 1)
    lower = key <= qry
    bias = bias_ref[kv]
    sink = sink_ref[...]

    for b in range(SWA_NB):
        blk = n * SWA_NB + b
        qt = jnp.concatenate([qt_ref[g, :, b * BLOCK:(b + 1) * BLOCK] for g in range(SWA_GROUP)],
                             axis=1)
        cur = pl.multiple_of(blk * BLOCK, BLOCK)
        prev_blk = jnp.maximum(blk - 1, 0)
        prev = pl.multiple_of(prev_blk * BLOCK, BLOCK)
        k_band = jnp.concatenate([k_ref[pl.ds(prev, BLOCK), :], k_ref[pl.ds(cur, BLOCK), :]], axis=0)
        s_band = _dot(k_band, qt)
        s = jnp.where(lower, s_band[BLOCK:], s_band[:BLOCK]) * SWA_SCALE + bias
        if b == 0:
            s = jnp.where(lower | (blk > 0), s, NEG_BIG)

        m = jnp.maximum(jnp.max(s, axis=0, keepdims=True), sink)
        e = jnp.exp(s - m)
        denom = jnp.sum(e, axis=0, keepdims=True) + jnp.exp(sink - m)
        p = e * (1.0 / denom)
        p_cur = jnp.where(lower, p, 0.0).astype(BF16)
        p_prev = jnp.where(lower, 0.0, p).astype(BF16)
        ot = _dot(vt_ref[blk], p_cur) + _dot(vt_ref[prev_blk], p_prev)
        o_ref[b * BLOCK:(b + 1) * BLOCK, :] = jnp.concatenate(
            [ot[:, g * BLOCK:(g + 1) * BLOCK] for g in range(SWA_GROUP)], axis=0).T


def _swa_attn(rel_bias, bucket_tbl, qst, ks, vst, sink_rows):
    s = ks.shape[1]
    rows = SWA_NB * BLOCK
    est = (2 * SWA_GROUP * SWA_HEAD_DIM * rows * 2 + 2 * s * V7X_LANES * 2 + 2 * s * SWA_HEAD_DIM * 2
           + 2 * rows * SWA_GROUP * SWA_HEAD_DIM * 4 + SWA_KV_HEADS * BLOCK * SWA_COLS * 4
           + 10 * SWA_NB * BLOCK * SWA_COLS * 4)
    return pl.pallas_call(
        _swa_kernel,
        grid=(SWA_KV_HEADS, s // rows),
        in_specs=[
            pl.BlockSpec(memory_space=pltpu.SMEM),
            pl.BlockSpec((BLOCK, BLOCK), lambda kv, n: (0, 0)),
            pl.BlockSpec((SWA_GROUP, SWA_HEAD_DIM, rows), lambda kv, n: (kv, 0, n)),
            pl.BlockSpec((None, s, SWA_HEAD_DIM), lambda kv, n: (kv, 0, 0)),
            pl.BlockSpec((None, s // BLOCK, SWA_HEAD_DIM, BLOCK), lambda kv, n: (kv, 0, 0, 0)),
            pl.BlockSpec((None, 1, SWA_COLS), lambda kv, n: (kv, 0, 0)),
        ],
        out_specs=pl.BlockSpec((rows, SWA_GROUP * SWA_HEAD_DIM), lambda kv, n: (n, kv)),
        out_shape=jax.ShapeDtypeStruct((s, SWA_WIDTH), F32),
        scratch_shapes=[pltpu.VMEM((SWA_KV_HEADS, BLOCK, SWA_COLS), F32)],
        compiler_params=pltpu.CompilerParams(
            dimension_semantics=("arbitrary", "arbitrary"),
            vmem_limit_bytes=_vmem_limit(est)),
        name="swa_attn",
    )(rel_bias, bucket_tbl, qst, ks, vst, sink_rows)


OUTPROJ_TM = 512


def _out_proj_kernel(am_ref, as_ref, h_ref, mg_ref, sg_ref, wm_ref, ws_ref, g_ref, b_ref, o_ref):
    nm = _rms_norm(am_ref[...], mg_ref[...]).astype(BF16)
    ns = _rms_norm(as_ref[...], sg_ref[...]).astype(BF16)
    mixed = _dot(nm, wm_ref[...]) + _dot(ns, ws_ref[...])
    o_ref[...] = _layer_norm(ALPHA * h_ref[...] + mixed, g_ref[...], b_ref[...])


def _out_proj(a_mla, a_swa, h, mla_g, swa_g, w_mla, w_swa, g, b):
    s, d = h.shape
    tm = OUTPROJ_TM
    est = (2 * tm * (MLA_WIDTH + SWA_WIDTH) * 4 + 4 * tm * d * 4 + 2 * (MLA_WIDTH + SWA_WIDTH) * d * 2
           + 4 * tm * d * 4)
    const = lambda i: (0, 0)
    return pl.pallas_call(
        _out_proj_kernel,
        grid=(s // tm,),
        in_specs=[
            pl.BlockSpec((tm, MLA_WIDTH), lambda i: (i, 0)),
            pl.BlockSpec((tm, SWA_WIDTH), lambda i: (i, 0)),
            pl.BlockSpec((tm, d), lambda i: (i, 0)),
            pl.BlockSpec((1, MLA_WIDTH), const),
            pl.BlockSpec((1, SWA_WIDTH), const),
            pl.BlockSpec((MLA_WIDTH, d), const),
            pl.BlockSpec((SWA_WIDTH, d), const),
            pl.BlockSpec((1, d), const),
            pl.BlockSpec((1, d), const),
        ],
        out_specs=pl.BlockSpec((tm, d), lambda i: (i, 0)),
        out_shape=jax.ShapeDtypeStruct((s, d), F32),
        compiler_params=pltpu.CompilerParams(
            dimension_semantics=("parallel",),
            vmem_limit_bytes=_vmem_limit(est)),
        name="out_proj",
    )(a_mla, a_swa, h, mla_g, swa_g, w_mla, w_swa, g, b)


PLE_TM = 512


def _ple_kernel(h_ref, p_ref, wg_ref, wp_ref, o_ref):
    h = h_ref[...]
    gate = _dot(h.astype(BF16), wg_ref[...])
    proj = _dot(p_ref[...].astype(BF16), wp_ref[...])
    o_ref[...] = h + (1.0 / (1.0 + jnp.exp(-gate))) * proj


def _ple(h, p, w_gate, w_proj):
    s, d = h.shape
    tm = PLE_TM
    est = 4 * tm * d * 4 + 2 * tm * PLE_DIM * 4 + 2 * (d + PLE_DIM) * d * 2 + 4 * tm * d * 4
    const = lambda i: (0, 0)
    return pl.pallas_call(
        _ple_kernel,
        grid=(s // tm,),
        in_specs=[
            pl.BlockSpec((tm, d), lambda i: (i, 0)),
            pl.BlockSpec((tm, PLE_DIM), lambda i: (i, 0)),
            pl.BlockSpec((d, d), const),
            pl.BlockSpec((PLE_DIM, d), const),
        ],
        out_specs=pl.BlockSpec((tm, d), lambda i: (i, 0)),
        out_shape=jax.ShapeDtypeStruct((s, d), F32),
        compiler_params=pltpu.CompilerParams(
            dimension_semantics=("parallel",),
            vmem_limit_bytes=_vmem_limit(est)),
        name="ple",
    )(h, p, w_gate, w_proj)


def _split_w_in(w_in):
    d = w_in.shape[0]
    b_kr = Q_LORA + KV_LORA
    b_qs = b_kr + QK_ROPE
    b_ks = b_qs + SWA_WIDTH
    b_vs = b_ks + SWA_KV_WIDTH
    wz = jnp.concatenate([w_in[:, :b_kr], w_in[:, b_ks:b_vs]], axis=1)
    pad = jnp.zeros((d, V7X_LANES - QK_ROPE), w_in.dtype)
    wt = jnp.concatenate([w_in[:, b_qs:b_ks], w_in[:, b_vs:], w_in[:, b_kr:b_qs], pad], axis=1).T
    return wz, wt


def _pack_w_uq_t(w_uq):
    r = w_uq.shape[0]
    w = w_uq.reshape(r, MLA_HEADS, QK_NOPE + QK_ROPE)
    w = jnp.pad(w, ((0, 0), (0, 0), (0, MLA_QK_PAD - QK_NOPE - QK_ROPE)))
    return w.reshape(r, MLA_HEADS * MLA_QK_PAD).T


def _rope_freq_col():
    freqs = ROPE_THETA ** (-jnp.arange(0, QK_ROPE, 2, dtype=F32) / QK_ROPE)
    return freqs.reshape(HALF_ROPE, 1)


def kernel(x, p, positions, rel_bias, ln1_g, ln1_b, ffn1_w1, ffn1_w3, ffn1_w2, w_in, q_norm_g, w_uq,
           kv_norm_g, w_ukv, swa_sinks, mla_out_g, swa_out_g, w_out, ln2_g, ln2_b, ffn2_w1, ffn2_w3,
           ffn2_w2, ln3_g, ln3_b, ple_w_gate, ple_w_proj):
    assert x.shape == (1, SEQ, D_MODEL) and DEPTH == 1
    row = lambda a: a.reshape(1, -1)
    h = x[0]
    pos_row = positions[0].astype(F32).reshape(1, SEQ)
    freq_col = _rope_freq_col()
    bucket_tbl = jnp.asarray(_swa_bucket_table())

    i = 0
    h = _ffn_ln(h, ffn1_w1[i].astype(BF16), ffn1_w3[i].astype(BF16), ffn1_w2[i].astype(BF16),
                row(ln1_g[i]), row(ln1_b[i]))

    w_ukv_h = w_ukv[i].reshape(KV_LORA, MLA_HEADS, QK_NOPE + V_HEAD)
    wkn = w_ukv_h[:, :, :QK_NOPE].reshape(KV_LORA, MLA_WIDTH).astype(BF16)
    wv_t = w_ukv_h[:, :, QK_NOPE:].reshape(KV_LORA, MLA_WIDTH).T.astype(BF16)
    wz, wt = _split_w_in(w_in[i])
    qt, k, vt, qst, ks, vst = _in_proj(
        h, wz.astype(BF16), wt.astype(BF16), row(q_norm_g[i]), row(kv_norm_g[i]),
        _pack_w_uq_t(w_uq[i]).astype(BF16), wkn, wv_t, pos_row, freq_col)

    a_mla = _mla_flash(qt, k, vt)
    sink_rows = jnp.repeat(swa_sinks[i], BLOCK).reshape(SWA_KV_HEADS, 1, SWA_COLS)
    a_swa = _swa_attn(rel_bias, bucket_tbl, qst, ks, vst, sink_rows)

    w_out_b = w_out[i].astype(BF16)
    h = _out_proj(a_mla, a_swa, h, row(mla_out_g[i]), row(swa_out_g[i]),
                  w_out_b[:MLA_WIDTH], w_out_b[MLA_WIDTH:], row(ln2_g[i]), row(ln2_b[i]))

    h = _ffn_ln(h, ffn2_w1[i].astype(BF16), ffn2_w3[i].astype(BF16), ffn2_w2[i].astype(BF16),
                row(ln3_g[i]), row(ln3_b[i]))
    h = _ple(h, p[i, 0], ple_w_gate[i].astype(BF16), ple_w_proj[i].astype(BF16))
    return h[None]
```

```python
import functools
import math

import numpy as np
import jax
import jax.numpy as jnp
from jax import lax
from jax.experimental import pallas as pl
from jax.experimental.pallas import tpu as pltpu

F32 = jnp.float32
BF16 = jnp.bfloat16

D_MODEL = 2048
SEQ = 8192
DEPTH = 1
PLE_DIM = 256
MLA_HEADS = 8
Q_LORA = 512
KV_LORA = 512
QK_NOPE = 128
QK_ROPE = 64
V_HEAD = 128
ROPE_THETA = 10000.0
SWA_HEADS = 16
SWA_KV_HEADS = 2
SWA_GROUP = SWA_HEADS // SWA_KV_HEADS
SWA_HEAD_DIM = 64
WINDOW = 128
BLOCK = 128
REL_BUCKETS = 32
REL_MAX_DIST = 128
D_FF = 5632
ALPHA = (2.0 * DEPTH) ** 0.25
EPS = 1e-5
MLA_WIDTH = MLA_HEADS * V_HEAD
SWA_WIDTH = SWA_HEADS * SWA_HEAD_DIM
SWA_KV_WIDTH = SWA_KV_HEADS * SWA_HEAD_DIM

V7X_LANES = 128
V7X_VMEM_BYTES = 64 * 1024 * 1024
MIB = 1024 * 1024

V7X_BF16_SUBLANES = 16

MLA_QK_PAD = 2 * V7X_LANES
MLA_V_ROWS = V_HEAD + V7X_BF16_SUBLANES
NEG_BIG = float(np.finfo(np.float32).min)


def _vmem_limit(estimate_bytes):
    return int(min(estimate_bytes + 8 * MIB, V7X_VMEM_BYTES - 4 * MIB))


def _layer_norm(y, g, b):
    mu = jnp.mean(y, axis=-1, keepdims=True)
    yc = y - mu
    var = jnp.mean(yc * yc, axis=-1, keepdims=True)
    return yc * lax.rsqrt(var + EPS) * g + b


def _rms_norm(x, g):
    ms = jnp.mean(x * x, axis=-1, keepdims=True)
    return x * lax.rsqrt(ms + EPS) * g


def _dot(a, b):
    return jnp.dot(a, b, preferred_element_type=F32)


def _dot_nt(a, b):
    return lax.dot_general(a, b, (((1,), (1,)), ((), ())), preferred_element_type=F32)


FFN_TM = 512
FFN_TF = 512


def _ffn_ln_kernel(x_ref, w1_ref, w3_ref, w2_ref, g_ref, b_ref, *rest, n_step_casts, n_row_casts):
    n_casts = n_step_casts + n_row_casts
    srcs, o_ref, dsts = rest[:n_casts], rest[n_casts], rest[n_casts + 1:2 * n_casts + 1]
    xb_ref, acc_ref = rest[2 * n_casts + 1:]
    f = pl.program_id(1)

    @pl.when(f == 0)
    def _():
        xb_ref[...] = x_ref[...].astype(BF16)
        acc_ref[...] = jnp.zeros_like(acc_ref)
        for src, dst in zip(srcs[n_step_casts:], dsts[n_step_casts:]):
            dst[...] = src[...].astype(BF16)

    for src, dst in zip(srcs[:n_step_casts], dsts[:n_step_casts]):
        dst[...] = src[...].astype(BF16)

    xb = xb_ref[...]
    gate = _dot(xb, w1_ref[...])
    up = _dot(xb, w3_ref[...])
    hidden = gate * (1.0 / (1.0 + jnp.exp(-gate))) * up
    acc_ref[...] += _dot(hidden.astype(BF16), w2_ref[...])

    @pl.when(f == pl.num_programs(1) - 1)
    def _():
        y = ALPHA * x_ref[...] + 0.5 * acc_ref[...]
        o_ref[...] = _layer_norm(y, g_ref[...], b_ref[...])


def _ffn_ln(x, w1, w3, w2, g, b, up_casts=(), down_casts=(), row_casts=()):
    s, d = x.shape
    dff = w1.shape[1]
    tm, tf = FFN_TM, FFN_TF
    nm, nf = s // tm, dff // tf
    dc = d // nm
    step_specs = ([pl.BlockSpec((dc, tf), lambda i, f: (i, f))] * len(up_casts)
                  + [pl.BlockSpec((tf, dc), lambda i, f: (f, i))] * len(down_casts))
    row_specs = [pl.BlockSpec((a.shape[0] // nm, a.shape[1]), lambda i, f: (i, 0)) for a in row_casts]
    casts = list(up_casts) + list(down_casts) + list(row_casts)
    cast_bytes = sum(2 * (4 + 2) * math.prod(spec.block_shape) for spec in step_specs + row_specs)
    est = (2 * tm * d * 4 + tm * d * 2 + tm * d * 4 + 2 * 3 * d * tf * 2 + 2 * tm * d * 4
           + 4 * tm * tf * 4 + cast_bytes)
    out = pl.pallas_call(
        functools.partial(_ffn_ln_kernel, n_step_casts=len(step_specs), n_row_casts=len(row_specs)),
        grid=(nm, nf),
        in_specs=[
            pl.BlockSpec((tm, d), lambda i, f: (i, 0)),
            pl.BlockSpec((d, tf), lambda i, f: (0, f)),
            pl.BlockSpec((d, tf), lambda i, f: (0, f)),
            pl.BlockSpec((tf, d), lambda i, f: (f, 0)),
            pl.BlockSpec((1, d), lambda i, f: (0, 0)),
            pl.BlockSpec((1, d), lambda i, f: (0, 0)),
        ] + step_specs + row_specs,
        out_specs=[pl.BlockSpec((tm, d), lambda i, f: (i, 0))] + step_specs + row_specs,
        out_shape=[jax.ShapeDtypeStruct((s, d), F32)] + [jax.ShapeDtypeStruct(a.shape, BF16) for a in casts],
        scratch_shapes=[pltpu.VMEM((tm, d), BF16), pltpu.VMEM((tm, d), F32)],
        compiler_params=pltpu.CompilerParams(
            dimension_semantics=("arbitrary", "arbitrary"),
            vmem_limit_bytes=_vmem_limit(est)),
        name="ffn_ln",
    )(x, w1, w3, w2, g, b, *casts)
    return out[0], out[1:]


INPROJ_TM = 512
Z_CQ = 0
Z_CKV = Z_CQ + Q_LORA
Z_KS = Z_CKV + KV_LORA
Z_WIDTH = Z_KS + SWA_KV_WIDTH
ZT_QS = 0
ZT_VS = ZT_QS + SWA_WIDTH
ZT_KR = ZT_VS + SWA_KV_WIDTH
ZT_ROWS = ZT_KR + V7X_LANES
HALF_ROPE = QK_ROPE // 2


def _rope_rows(x1, x2, cos, sin):
    return x1 * cos - x2 * sin, x1 * sin + x2 * cos


def _in_proj_kernel(h_ref, wz_ref, wt_ref, qg_ref, kvg_ref, wuqt_ref, wkn_ref, wvt_ref, pos_ref,
                    freq_ref, qt_ref, k_ref, vt_ref, qst_ref, ks_ref, vst_ref):
    tm = h_ref.shape[0]
    hb = h_ref[...].astype(BF16)
    z = _dot(hb, wz_ref[...])
    zt = _dot_nt(wt_ref[...], hb)

    ang = freq_ref[...] * pos_ref[...]
    cos = jnp.cos(ang)
    sin = jnp.sin(ang)

    cq = _rms_norm(z[:, Z_CQ:Z_CQ + Q_LORA], qg_ref[...]).astype(BF16)
    ckv = _rms_norm(z[:, Z_CKV:Z_CKV + KV_LORA], kvg_ref[...]).astype(BF16)
    qt = _dot_nt(wuqt_ref[...], cq)
    kn = _dot(ckv, wkn_ref[...])
    vt = _dot_nt(wvt_ref[...], ckv)

    kr1, kr2 = _rope_rows(zt[ZT_KR:ZT_KR + HALF_ROPE], zt[ZT_KR + HALF_ROPE:ZT_KR + QK_ROPE], cos, sin)
    k_rope = jnp.concatenate([kr1, kr2, zt[ZT_KR + QK_ROPE:ZT_KR + V7X_LANES]], axis=0).T.astype(BF16)

    for h in range(MLA_HEADS):
        base = h * MLA_QK_PAD
        r0 = base + QK_NOPE
        qt_ref[base:r0, :] = qt[base:r0].astype(BF16)
        q1, q2 = _rope_rows(qt[r0:r0 + HALF_ROPE], qt[r0 + HALF_ROPE:r0 + QK_ROPE], cos, sin)
        qt_ref[r0:r0 + HALF_ROPE, :] = q1.astype(BF16)
        qt_ref[r0 + HALF_ROPE:r0 + QK_ROPE, :] = q2.astype(BF16)
        qt_ref[r0 + QK_ROPE:base + MLA_QK_PAD, :] = qt[r0 + QK_ROPE:base + MLA_QK_PAD].astype(BF16)
        k_ref[:, base:r0] = kn[:, h * QK_NOPE:(h + 1) * QK_NOPE].astype(BF16)
        k_ref[:, r0:base + MLA_QK_PAD] = k_rope
        vt_ref[h, 0, 0:V_HEAD, :] = vt[h * V_HEAD:(h + 1) * V_HEAD].astype(BF16)
        vt_ref[h, 0, V_HEAD:MLA_V_ROWS, :] = jnp.ones((MLA_V_ROWS - V_HEAD, tm), BF16)

    qst_ref[...] = zt[ZT_QS:ZT_QS + SWA_WIDTH].reshape(SWA_HEADS, SWA_HEAD_DIM, tm).astype(BF16)
    for kv in range(SWA_KV_HEADS):
        lo = Z_KS + kv * SWA_HEAD_DIM
        ks_ref[kv] = z[:, lo:lo + SWA_HEAD_DIM].astype(BF16)
        lo = ZT_VS + kv * SWA_HEAD_DIM
        for c in range(tm // BLOCK):
            vst_ref[kv, c] = zt[lo:lo + SWA_HEAD_DIM, c * BLOCK:(c + 1) * BLOCK].astype(BF16)


def _in_proj(h, wz, wt, q_g, kv_g, wuq_t, wkn, wv_t, pos_row, freq_col):
    s, d = h.shape
    tm = INPROJ_TM
    assert tm == MLA_TK
    est = (2 * tm * d * 4 + 2 * d * (Z_WIDTH + ZT_ROWS) * 2
           + 2 * Q_LORA * (MLA_HEADS * MLA_QK_PAD + 2 * MLA_WIDTH) * 2
           + 2 * tm * (2 * MLA_HEADS * MLA_QK_PAD + MLA_WIDTH + SWA_WIDTH + 4 * V7X_LANES) * 2
           + 4 * tm * (Z_WIDTH + ZT_ROWS + 2 * MLA_HEADS * MLA_QK_PAD) * 4)
    const = lambda i: (0, 0)
    return pl.pallas_call(
        _in_proj_kernel,
        grid=(s // tm,),
        in_specs=[
            pl.BlockSpec((tm, d), lambda i: (i, 0)),
            pl.BlockSpec((d, Z_WIDTH), const),
            pl.BlockSpec((ZT_ROWS, d), const),
            pl.BlockSpec((1, Q_LORA), const),
            pl.BlockSpec((1, KV_LORA), const),
            pl.BlockSpec((MLA_HEADS * MLA_QK_PAD, Q_LORA), const),
            pl.BlockSpec((KV_LORA, MLA_WIDTH), const),
            pl.BlockSpec((MLA_WIDTH, KV_LORA), const),
            pl.BlockSpec((1, tm), lambda i: (0, i)),
            pl.BlockSpec((HALF_ROPE, 1), const),
        ],
        out_specs=[
            pl.BlockSpec((MLA_HEADS * MLA_QK_PAD, tm), lambda i: (0, i)),
            pl.BlockSpec((tm, MLA_HEADS * MLA_QK_PAD), lambda i: (i, 0)),
            pl.BlockSpec((MLA_HEADS, 1, MLA_V_ROWS, tm), lambda i: (0, i, 0, 0)),
            pl.BlockSpec((SWA_HEADS, SWA_HEAD_DIM, tm), lambda i: (0, 0, i)),
            pl.BlockSpec((SWA_KV_HEADS, tm, SWA_HEAD_DIM), lambda i: (0, i, 0)),
            pl.BlockSpec((SWA_KV_HEADS, tm // BLOCK, SWA_HEAD_DIM, BLOCK), lambda i: (0, i, 0, 0)),
        ],
        out_shape=[
            jax.ShapeDtypeStruct((MLA_HEADS * MLA_QK_PAD, s), BF16),
            jax.ShapeDtypeStruct((s, MLA_HEADS * MLA_QK_PAD), BF16),
            jax.ShapeDtypeStruct((MLA_HEADS, s // tm, MLA_V_ROWS, tm), BF16),
            jax.ShapeDtypeStruct((SWA_HEADS, SWA_HEAD_DIM, s), BF16),
            jax.ShapeDtypeStruct((SWA_KV_HEADS, s, SWA_HEAD_DIM), BF16),
            jax.ShapeDtypeStruct((SWA_KV_HEADS, s // BLOCK, SWA_HEAD_DIM, BLOCK), BF16),
        ],
        compiler_params=pltpu.CompilerParams(
            dimension_semantics=("parallel",),
            vmem_limit_bytes=_vmem_limit(est)),
        name="in_proj",
    )(h, wz, wt, q_g, kv_g, wuq_t, wkn, wv_t, pos_row, freq_col)


MLA_TQ = 512
MLA_TK = 512
MLA_HB = 2
MLA_SCALE = (QK_NOPE + QK_ROPE) ** -0.5
MLA_SCALE_LOG2E = MLA_SCALE * math.log2(math.e)


def _mla_flash_kernel(qt_ref, k_ref, vt_ref, o_ref, s_ref, p_ref, m_ref, corr_ref, acc_ref):
    i = pl.program_id(1)
    tk = MLA_TK
    heads = range(MLA_HB)

    def qk(t):
        start = pl.multiple_of(t * tk, tk)
        for h in heads:
            cols = slice(h * MLA_QK_PAD, (h + 1) * MLA_QK_PAD)
            s_ref[h] = _dot(k_ref[pl.ds(start, tk), cols], qt_ref[cols, :])

    def softmax(s_all, masked):
        for h in heads:
            s = s_all[h] * MLA_SCALE_LOG2E
            if masked:
                key = lax.broadcasted_iota(jnp.int32, s.shape, 0)
                qry = lax.broadcasted_iota(jnp.int32, s.shape, 1)
                s = jnp.where(key <= qry, s, NEG_BIG)
            m_prev = m_ref[h]
            m_new = jnp.maximum(m_prev, jnp.max(s, axis=0, keepdims=True))
            p_ref[h] = jnp.exp2(s - m_new).astype(BF16)
            corr_ref[h] = jnp.exp2(m_prev - m_new)
            m_ref[h] = m_new

    def pv(t):
        for h in heads:
            acc_ref[h] = corr_ref[h] * acc_ref[h] + _dot(vt_ref[h, t], p_ref[h])

    def load_scores():
        return [s_ref[h] for h in heads]

    m_ref[...] = jnp.full_like(m_ref, NEG_BIG)
    acc_ref[...] = jnp.zeros_like(acc_ref)

    @pl.when(i == 0)
    def _():
        p_ref[...] = jnp.zeros_like(p_ref)
        corr_ref[...] = jnp.zeros_like(corr_ref)

    qk(0)

    @pl.when(i >= 1)
    def _():
        s_all = load_scores()
        qk(1)
        softmax(s_all, masked=False)

    def body(t, carry):
        pv(t)
        s_all = load_scores()
        qk(t + 2)
        softmax(s_all, masked=False)
        return carry

    lax.fori_loop(0, i - 1, body, 0)
    pv(jnp.maximum(i - 1, 0))
    softmax(load_scores(), masked=True)
    pv(i)
    for h in heads:
        o_ref[:, h * V_HEAD:(h + 1) * V_HEAD] = (
            acc_ref[h, 0:V_HEAD, :] / acc_ref[h, V_HEAD:V_HEAD + 1, :]).T


def _mla_flash(qt, k, vt):
    s = k.shape[0]
    tq, tk, hb = MLA_TQ, MLA_TK, MLA_HB
    assert tq == tk
    est = (2 * hb * tq * MLA_QK_PAD * 2 + 2 * hb * s * MLA_QK_PAD * 2 + 2 * hb * s * MLA_V_ROWS * 2
           + 2 * hb * tq * V_HEAD * 4 + hb * tk * tq * (4 + 2) + hb * tq * MLA_V_ROWS * 4
           + 4 * hb * tq * tk * 4)
    return pl.pallas_call(
        _mla_flash_kernel,
        grid=(MLA_HEADS // hb, s // tq),
        in_specs=[
            pl.BlockSpec((hb * MLA_QK_PAD, tq), lambda g, i: (g, i)),
            pl.BlockSpec((s, hb * MLA_QK_PAD), lambda g, i: (0, g)),
            pl.BlockSpec((hb, s // tk, MLA_V_ROWS, tk), lambda g, i: (g, 0, 0, 0)),
        ],
        out_specs=pl.BlockSpec((tq, hb * V_HEAD), lambda g, i: (i, g)),
        out_shape=jax.ShapeDtypeStruct((s, MLA_WIDTH), F32),
        scratch_shapes=[pltpu.VMEM((hb, tk, tq), F32), pltpu.VMEM((hb, tk, tq), BF16),
                        pltpu.VMEM((hb, 1, tq), F32), pltpu.VMEM((hb, 1, tq), F32),
                        pltpu.VMEM((hb, MLA_V_ROWS, tq), F32)],
        compiler_params=pltpu.CompilerParams(
            dimension_semantics=("parallel", "arbitrary"),
            vmem_limit_bytes=_vmem_limit(est)),
        name="mla_flash",
    )(qt, k, vt)


SWA_SCALE = SWA_HEAD_DIM ** -0.5
SWA_COLS = SWA_GROUP * BLOCK
SWA_NB = 4


def _t5_bucket_np(dist):
    n = np.maximum(dist, 0)
    max_exact = REL_BUCKETS // 2
    large = max_exact + (np.log(np.maximum(n, 1).astype(np.float32) / max_exact)
                         / math.log(REL_MAX_DIST / max_exact)
                         * (REL_BUCKETS - max_exact)).astype(np.int32)
    large = np.minimum(large, REL_BUCKETS - 1)
    return np.where(n < max_exact, n, large).astype(np.int32)


def _swa_bucket_table():
    j = np.arange(BLOCK)[:, None]
    i = np.arange(BLOCK)[None, :]
    dist = np.where(j <= i, i - j, BLOCK + i - j)
    return _t5_bucket_np(dist)


def _swa_kernel(rb_ref, bkt_ref, qt_ref, k_ref, vt_ref, sink_ref, o_ref, bias_ref):
    kv = pl.program_id(0)
    n = pl.program_id(1)

    @pl.when((kv == 0) & (n == 0))
    def _():
        bkt = bkt_ref[...]
        for h in range(SWA_HEADS):
            acc = jnp.zeros((BLOCK, BLOCK), F32)
            for b in range(REL_BUCKETS):
                acc = jnp.where(bkt == b, rb_ref[b, h], acc)
            g = h % SWA_GROUP
            bias_ref[h // SWA_GROUP, :, g * BLOCK:(g + 1) * BLOCK] = acc

    key = lax.broadcasted_iota(jnp.int32, (BLOCK, SWA_COLS), 0)
    qry = lax.broadcasted_iota(jnp.int32, (BLOCK, SWA_COLS), 1) & (BLOCK - 1)
    lower = key <= qry
    bias = bias_ref[kv]
    sink = sink_ref[...]

    for b in range(SWA_NB):
        blk = n * SWA_NB + b
        qt = jnp.concatenate([qt_ref[g, :, b * BLOCK:(b + 1) * BLOCK] for g in range(SWA_GROUP)],
                             axis=1)
        cur = pl.multiple_of(blk * BLOCK, BLOCK)
        prev_blk = jnp.maximum(blk - 1, 0)
        prev = pl.multiple_of(prev_blk * BLOCK, BLOCK)
        k_band = jnp.concatenate([k_ref[pl.ds(prev, BLOCK), :], k_ref[pl.ds(cur, BLOCK), :]], axis=0)
        s_band = _dot(k_band, qt)
        s = jnp.where(lower, s_band[BLOCK:], s_band[:BLOCK]) * SWA_SCALE + bias
        if b == 0:
            s = jnp.where(lower | (blk > 0), s, NEG_BIG)

        m = jnp.maximum(jnp.max(s, axis=0, keepdims=True), sink)
        e = jnp.exp(s - m)
        denom = jnp.sum(e, axis=0, keepdims=True) + jnp.exp(sink - m)
        p = e * (1.0 / denom)
        p_cur = jnp.where(lower, p, 0.0).astype(BF16)
        p_prev = jnp.where(lower, 0.0, p).astype(BF16)
        ot = _dot(vt_ref[blk], p_cur) + _dot(vt_ref[prev_blk], p_prev)
        o_ref[b * BLOCK:(b + 1) * BLOCK, :] = jnp.concatenate(
            [ot[:, g * BLOCK:(g + 1) * BLOCK] for g in range(SWA_GROUP)], axis=0).T


def _swa_attn(rel_bias, bucket_tbl, qst, ks, vst, sink_rows):
    s = ks.shape[1]
    rows = SWA_NB * BLOCK
    est = (2 * SWA_GROUP * SWA_HEAD_DIM * rows * 2 + 2 * s * V7X_LANES * 2 + 2 * s * SWA_HEAD_DIM * 2
           + 2 * rows * SWA_GROUP * SWA_HEAD_DIM * 4 + SWA_KV_HEADS * BLOCK * SWA_COLS * 4
           + 10 * SWA_NB * BLOCK * SWA_COLS * 4)
    return pl.pallas_call(
        _swa_kernel,
        grid=(SWA_KV_HEADS, s // rows),
        in_specs=[
            pl.BlockSpec(memory_space=pltpu.SMEM),
            pl.BlockSpec((BLOCK, BLOCK), lambda kv, n: (0, 0)),
            pl.BlockSpec((SWA_GROUP, SWA_HEAD_DIM, rows), lambda kv, n: (kv, 0, n)),
            pl.BlockSpec((None, s, SWA_HEAD_DIM), lambda kv, n: (kv, 0, 0)),
            pl.BlockSpec((None, s // BLOCK, SWA_HEAD_DIM, BLOCK), lambda kv, n: (kv, 0, 0, 0)),
            pl.BlockSpec((None, 1, SWA_COLS), lambda kv, n: (kv, 0, 0)),
        ],
        out_specs=pl.BlockSpec((rows, SWA_GROUP * SWA_HEAD_DIM), lambda kv, n: (n, kv)),
        out_shape=jax.ShapeDtypeStruct((s, SWA_WIDTH), F32),
        scratch_shapes=[pltpu.VMEM((SWA_KV_HEADS, BLOCK, SWA_COLS), F32)],
        compiler_params=pltpu.CompilerParams(
            dimension_semantics=("arbitrary", "arbitrary"),
            vmem_limit_bytes=_vmem_limit(est)),
        name="swa_attn",
    )(rel_bias, bucket_tbl, qst, ks, vst, sink_rows)


OUTPROJ_TM = 512
OUTPROJ_CHUNKS = 2


def _out_proj_kernel(am_ref, as_ref, h_ref, mg_ref, sg_ref, wm_ref, ws_ref, g_ref, b_ref, o_ref):
    chunk = o_ref.shape[0] // OUTPROJ_CHUNKS
    for c in range(OUTPROJ_CHUNKS):
        rows = slice(c * chunk, (c + 1) * chunk)
        nm = _rms_norm(am_ref[rows, :], mg_ref[...]).astype(BF16)
        ns = _rms_norm(as_ref[rows, :], sg_ref[...]).astype(BF16)
        mixed = _dot(nm, wm_ref[...]) + _dot(ns, ws_ref[...])
        o_ref[rows, :] = _layer_norm(ALPHA * h_ref[rows, :] + mixed, g_ref[...], b_ref[...])


def _out_proj(a_mla, a_swa, h, mla_g, swa_g, w_out, g, b):
    s, d = h.shape
    tm = OUTPROJ_TM
    est = (2 * tm * (MLA_WIDTH + SWA_WIDTH) * 4 + 4 * tm * d * 4 + 2 * (MLA_WIDTH + SWA_WIDTH) * d * 2
           + 4 * tm * d * 4)
    const = lambda i: (0, 0)
    return pl.pallas_call(
        _out_proj_kernel,
        grid=(s // tm,),
        in_specs=[
            pl.BlockSpec((tm, MLA_WIDTH), lambda i: (i, 0)),
            pl.BlockSpec((tm, SWA_WIDTH), lambda i: (i, 0)),
            pl.BlockSpec((tm, d), lambda i: (i, 0)),
            pl.BlockSpec((1, MLA_WIDTH), const),
            pl.BlockSpec((1, SWA_WIDTH), const),
            pl.BlockSpec((MLA_WIDTH, d), const),
            pl.BlockSpec((SWA_WIDTH, d), lambda i: (MLA_WIDTH // SWA_WIDTH, 0)),
            pl.BlockSpec((1, d), const),
            pl.BlockSpec((1, d), const),
        ],
        out_specs=pl.BlockSpec((tm, d), lambda i: (i, 0)),
        out_shape=jax.ShapeDtypeStruct((s, d), F32),
        compiler_params=pltpu.CompilerParams(
            dimension_semantics=("parallel",),
            vmem_limit_bytes=_vmem_limit(est)),
        name="out_proj",
    )(a_mla, a_swa, h, mla_g, swa_g, w_out, w_out, g, b)


PLE_TM = 512


def _ple_kernel(h_ref, p_ref, wg_ref, wp_ref, o_ref):
    h = h_ref[...]
    gate = _dot(h.astype(BF16), wg_ref[...])
    proj = _dot(p_ref[...].astype(BF16), wp_ref[...])
    o_ref[...] = h + (1.0 / (1.0 + jnp.exp(-gate))) * proj


def _ple(h, p, w_gate, w_proj):
    s, d = h.shape
    tm = PLE_TM
    est = 4 * tm * d * 4 + 2 * tm * PLE_DIM * 4 + 2 * (d + PLE_DIM) * d * 2 + 4 * tm * d * 4
    const = lambda i: (0, 0)
    return pl.pallas_call(
        _ple_kernel,
        grid=(s // tm,),
        in_specs=[
            pl.BlockSpec((tm, d), lambda i: (i, 0)),
            pl.BlockSpec((tm, PLE_DIM), lambda i: (i, 0)),
            pl.BlockSpec((d, d), const),
            pl.BlockSpec((PLE_DIM, d), const),
        ],
        out_specs=pl.BlockSpec((tm, d), lambda i: (i, 0)),
        out_shape=jax.ShapeDtypeStruct((s, d), F32),
        compiler_params=pltpu.CompilerParams(
            dimension_semantics=("parallel",),
            vmem_limit_bytes=_vmem_limit(est)),
        name="ple",
    )(h, p, w_gate, w_proj)


def _split_w_in(w_in):
    d = w_in.shape[0]
    b_kr = Q_LORA + KV_LORA
    b_qs = b_kr + QK_ROPE
    b_ks = b_qs + SWA_WIDTH
    b_vs = b_ks + SWA_KV_WIDTH
    wz = jnp.concatenate([w_in[:, :b_kr], w_in[:, b_ks:b_vs]], axis=1)
    pad = jnp.zeros((d, V7X_LANES - QK_ROPE), w_in.dtype)
    wt = jnp.concatenate([w_in[:, b_qs:b_ks], w_in[:, b_vs:], w_in[:, b_kr:b_qs], pad], axis=1).T
    return wz, wt


def _pack_w_uq_t(w_uq):
    r = w_uq.shape[0]
    w = w_uq.reshape(r, MLA_HEADS, QK_NOPE + QK_ROPE)
    w = jnp.pad(w, ((0, 0), (0, 0), (0, MLA_QK_PAD - QK_NOPE - QK_ROPE)))
    return w.reshape(r, MLA_HEADS * MLA_QK_PAD).T


def _rope_freq_col():
    freqs = ROPE_THETA ** (-jnp.arange(0, QK_ROPE, 2, dtype=F32) / QK_ROPE)
    return freqs.reshape(HALF_ROPE, 1)


def kernel(x, p, positions, rel_bias, ln1_g, ln1_b, ffn1_w1, ffn1_w3, ffn1_w2, w_in, q_norm_g, w_uq,
           kv_norm_g, w_ukv, swa_sinks, mla_out_g, swa_out_g, w_out, ln2_g, ln2_b, ffn2_w1, ffn2_w3,
           ffn2_w2, ln3_g, ln3_b, ple_w_gate, ple_w_proj):
    assert x.shape == (1, SEQ, D_MODEL) and DEPTH == 1
    row = lambda a: a.reshape(1, -1)
    h = x[0]
    pos_row = positions[0].astype(F32).reshape(1, SEQ)
    freq_col = _rope_freq_col()
    bucket_tbl = jnp.asarray(_swa_bucket_table())

    i = 0
    h, (w1b, w3b, w2b, w_out_b, w_gate_b, w_proj_b) = _ffn_ln(
        h, ffn1_w1[i].astype(BF16), ffn1_w3[i].astype(BF16), ffn1_w2[i].astype(BF16),
        row(ln1_g[i]), row(ln1_b[i]),
        up_casts=(ffn2_w1[i], ffn2_w3[i]), down_casts=(ffn2_w2[i],),
        row_casts=(w_out[i], ple_w_gate[i], ple_w_proj[i]))

    w_ukv_h = w_ukv[i].reshape(KV_LORA, MLA_HEADS, QK_NOPE + V_HEAD)
    wkn = w_ukv_h[:, :, :QK_NOPE].reshape(KV_LORA, MLA_WIDTH).astype(BF16)
    wv_t = w_ukv_h[:, :, QK_NOPE:].reshape(KV_LORA, MLA_WIDTH).T.astype(BF16)
    wz, wt = _split_w_in(w_in[i])
    qt, k, vt, qst, ks, vst = _in_proj(
        h, wz.astype(BF16), wt.astype(BF16), row(q_norm_g[i]), row(kv_norm_g[i]),
        _pack_w_uq_t(w_uq[i]).astype(BF16), wkn, wv_t, pos_row, freq_col)

    a_mla = _mla_flash(qt, k, vt)
    sink_rows = jnp.repeat(swa_sinks[i], BLOCK).reshape(SWA_KV_HEADS, 1, SWA_COLS)
    a_swa = _swa_attn(rel_bias, bucket_tbl, qst, ks, vst, sink_rows)

    h = _out_proj(a_mla, a_swa, h, row(mla_out_g[i]), row(swa_out_g[i]), w_out_b,
                  row(ln2_g[i]), row(ln2_b[i]))

    h, _ = _ffn_ln(h, w1b, w3b, w2b, row(ln3_g[i]), row(ln3_b[i]))
    h = _ple(h, p[i, 0], w_gate_b, w_proj_b)
    return h[None]
```

```python
import functools
import math

import numpy as np
import jax
import jax.numpy as jnp
from jax import lax
from jax.experimental import pallas as pl
from jax.experimental.pallas import tpu as pltpu

F32 = jnp.float32
BF16 = jnp.bfloat16

D_MODEL = 2048
SEQ = 8192
DEPTH = 1
PLE_DIM = 256
MLA_HEADS = 8
Q_LORA = 512
KV_LORA = 512
QK_NOPE = 128
QK_ROPE = 64
V_HEAD = 128
ROPE_THETA = 10000.0
SWA_HEADS = 16
SWA_KV_HEADS = 2
SWA_GROUP = SWA_HEADS // SWA_KV_HEADS
SWA_HEAD_DIM = 64
WINDOW = 128
BLOCK = 128
REL_BUCKETS = 32
REL_MAX_DIST = 128
D_FF = 5632
ALPHA = (2.0 * DEPTH) ** 0.25
EPS = 1e-5
MLA_WIDTH = MLA_HEADS * V_HEAD
SWA_WIDTH = SWA_HEADS * SWA_HEAD_DIM
SWA_KV_WIDTH = SWA_KV_HEADS * SWA_HEAD_DIM

V7X_LANES = 128
V7X_VMEM_BYTES = 64 * 1024 * 1024
MIB = 1024 * 1024

V7X_BF16_SUBLANES = 16

MLA_QK_PAD = 2 * V7X_LANES
MLA_V_ROWS = V_HEAD + V7X_BF16_SUBLANES
NEG_BIG = float(np.finfo(np.float32).min)


def _vmem_limit(estimate_bytes):
    return int(min(estimate_bytes + 8 * MIB, V7X_VMEM_BYTES - 4 * MIB))


def _layer_norm(y, g, b):
    mu = jnp.mean(y, axis=-1, keepdims=True)
    yc = y - mu
    var = jnp.mean(yc * yc, axis=-1, keepdims=True)
    return yc * lax.rsqrt(var + EPS) * g + b


def _rms_norm(x, g):
    ms = jnp.mean(x * x, axis=-1, keepdims=True)
    return x * lax.rsqrt(ms + EPS) * g


def _dot(a, b):
    return jnp.dot(a, b, preferred_element_type=F32)


def _dot_nt(a, b):
    return lax.dot_general(a, b, (((1,), (1,)), ((), ())), preferred_element_type=F32)


FFN_TM_CANDIDATES = (1024, 512)
FFN_TF = 512
FFN_VMEM_BUDGET = 52 * MIB
FFN_ROW_CHUNK = 512
FFN_LN_ROWS = 256
FFN_CAST_SPLIT = 8


def _ffn_ln_kernel(x_ref, w1_ref, w3_ref, w2_ref, g_ref, b_ref, *rest, n_casts):
    srcs, o_ref, dsts = rest[:n_casts], rest[n_casts], rest[n_casts + 1:2 * n_casts + 1]
    (xb_ref,) = rest[2 * n_casts + 1:]
    f = pl.program_id(1)

    @pl.when(f == 0)
    def _():
        xb_ref[...] = x_ref[...].astype(BF16)
        o_ref[...] = jnp.zeros_like(o_ref)

    for src, dst in zip(srcs, dsts):
        dst[...] = src[...].astype(BF16)

    for c in range(o_ref.shape[0] // FFN_ROW_CHUNK):
        rows = slice(c * FFN_ROW_CHUNK, (c + 1) * FFN_ROW_CHUNK)
        xb = xb_ref[rows, :]
        gate = _dot(xb, w1_ref[...])
        up = _dot(xb, w3_ref[...])
        hidden = gate * (1.0 / (1.0 + jnp.exp(-gate))) * up
        o_ref[rows, :] += _dot(hidden.astype(BF16), w2_ref[...])

    @pl.when(f == pl.num_programs(1) - 1)
    def _():
        for c in range(o_ref.shape[0] // FFN_LN_ROWS):
            rows = slice(c * FFN_LN_ROWS, (c + 1) * FFN_LN_ROWS)
            y = ALPHA * x_ref[rows, :] + 0.5 * o_ref[rows, :]
            o_ref[rows, :] = _layer_norm(y, g_ref[...], b_ref[...])


def _row_cast_spec(a, nm, nf):
    r, c = a.shape
    split = FFN_CAST_SPLIT if r % (nm * FFN_CAST_SPLIT * V7X_BF16_SUBLANES) == 0 else 1
    assert split <= nf and r % (nm * split * V7X_BF16_SUBLANES) == 0
    return pl.BlockSpec((r // (nm * split), c), lambda i, f: (i * split + jnp.minimum(f, split - 1), 0))


def _ffn_plan(s, d, dff, up_casts, down_casts, row_casts):
    tf = FFN_TF
    for tm in FFN_TM_CANDIDATES:
        nm, nf = s // tm, dff // tf
        dc = d // nm
        cast_specs = ([pl.BlockSpec((dc, tf), lambda i, f: (i, f))] * len(up_casts)
                      + [pl.BlockSpec((tf, dc), lambda i, f: (f, i))] * len(down_casts)
                      + [_row_cast_spec(a, nm, nf) for a in row_casts])
        cast_bytes = sum(2 * (4 + 2) * math.prod(spec.block_shape) for spec in cast_specs)
        est = (2 * tm * d * 4 + tm * d * 2 + 2 * 3 * d * tf * 2 + 2 * tm * d * 4
               + 3 * FFN_ROW_CHUNK * tf * 4 + cast_bytes)
        if est <= FFN_VMEM_BUDGET:
            break
    return tm, tf, cast_specs, est


def _ffn_ln(x, w1, w3, w2, g, b, up_casts=(), down_casts=(), row_casts=()):
    s, d = x.shape
    dff = w1.shape[1]
    tm, tf, cast_specs, est = _ffn_plan(s, d, dff, up_casts, down_casts, row_casts)
    nm, nf = s // tm, dff // tf
    casts = list(up_casts) + list(down_casts) + list(row_casts)
    out = pl.pallas_call(
        functools.partial(_ffn_ln_kernel, n_casts=len(casts)),
        grid=(nm, nf),
        in_specs=[
            pl.BlockSpec((tm, d), lambda i, f: (i, 0)),
            pl.BlockSpec((d, tf), lambda i, f: (0, f)),
            pl.BlockSpec((d, tf), lambda i, f: (0, f)),
            pl.BlockSpec((tf, d), lambda i, f: (f, 0)),
            pl.BlockSpec((1, d), lambda i, f: (0, 0)),
            pl.BlockSpec((1, d), lambda i, f: (0, 0)),
        ] + cast_specs,
        out_specs=[pl.BlockSpec((tm, d), lambda i, f: (i, 0))] + cast_specs,
        out_shape=[jax.ShapeDtypeStruct((s, d), F32)] + [jax.ShapeDtypeStruct(a.shape, BF16) for a in casts],
        scratch_shapes=[pltpu.VMEM((tm, d), BF16)],
        compiler_params=pltpu.CompilerParams(
            dimension_semantics=("arbitrary", "arbitrary"),
            vmem_limit_bytes=_vmem_limit(est)),
        name="ffn_ln",
    )(x, w1, w3, w2, g, b, *casts)
    return out[0], out[1:]


INPROJ_TM = 512
Z_CQ = 0
Z_CKV = Z_CQ + Q_LORA
Z_KS = Z_CKV + KV_LORA
Z_WIDTH = Z_KS + SWA_KV_WIDTH
ZT_QS = 0
ZT_VS = ZT_QS + SWA_WIDTH
ZT_KR = ZT_VS + SWA_KV_WIDTH
ZT_ROWS = ZT_KR + V7X_LANES
HALF_ROPE = QK_ROPE // 2


def _rope_rows(x1, x2, cos, sin):
    return x1 * cos - x2 * sin, x1 * sin + x2 * cos


def _in_proj_kernel(h_ref, wz_ref, wt_ref, qg_ref, kvg_ref, wuqt_ref, wkn_ref, wvt_ref, pos_ref,
                    freq_ref, qt_ref, k_ref, vt_ref, qst_ref, ks_ref, vst_ref):
    tm = h_ref.shape[0]
    hb = h_ref[...].astype(BF16)
    z = _dot(hb, wz_ref[...])
    zt = _dot_nt(wt_ref[...], hb)

    ang = freq_ref[...] * pos_ref[...]
    cos = jnp.cos(ang)
    sin = jnp.sin(ang)

    cq = _rms_norm(z[:, Z_CQ:Z_CQ + Q_LORA], qg_ref[...]).astype(BF16)
    ckv = _rms_norm(z[:, Z_CKV:Z_CKV + KV_LORA], kvg_ref[...]).astype(BF16)
    qt = _dot_nt(wuqt_ref[...], cq)
    kn = _dot(ckv, wkn_ref[...])
    vt = _dot_nt(wvt_ref[...], ckv)

    kr1, kr2 = _rope_rows(zt[ZT_KR:ZT_KR + HALF_ROPE], zt[ZT_KR + HALF_ROPE:ZT_KR + QK_ROPE], cos, sin)
    k_rope = jnp.concatenate([kr1, kr2, zt[ZT_KR + QK_ROPE:ZT_KR + V7X_LANES]], axis=0).T.astype(BF16)

    for h in range(MLA_HEADS):
        base = h * MLA_QK_PAD
        r0 = base + QK_NOPE
        qt_ref[base:r0, :] = qt[base:r0].astype(BF16)
        q1, q2 = _rope_rows(qt[r0:r0 + HALF_ROPE], qt[r0 + HALF_ROPE:r0 + QK_ROPE], cos, sin)
        qt_ref[r0:r0 + HALF_ROPE, :] = q1.astype(BF16)
        qt_ref[r0 + HALF_ROPE:r0 + QK_ROPE, :] = q2.astype(BF16)
        qt_ref[r0 + QK_ROPE:base + MLA_QK_PAD, :] = qt[r0 + QK_ROPE:base + MLA_QK_PAD].astype(BF16)
        k_ref[:, base:r0] = kn[:, h * QK_NOPE:(h + 1) * QK_NOPE].astype(BF16)
        k_ref[:, r0:base + MLA_QK_PAD] = k_rope
        vt_ref[h, 0, 0:V_HEAD, :] = vt[h * V_HEAD:(h + 1) * V_HEAD].astype(BF16)
        vt_ref[h, 0, V_HEAD:MLA_V_ROWS, :] = jnp.ones((MLA_V_ROWS - V_HEAD, tm), BF16)

    qst_ref[...] = zt[ZT_QS:ZT_QS + SWA_WIDTH].reshape(SWA_HEADS, SWA_HEAD_DIM, tm).astype(BF16)
    for kv in range(SWA_KV_HEADS):
        lo = Z_KS + kv * SWA_HEAD_DIM
        ks_ref[kv] = z[:, lo:lo + SWA_HEAD_DIM].astype(BF16)
        lo = ZT_VS + kv * SWA_HEAD_DIM
        for c in range(tm // BLOCK):
            vst_ref[kv, c] = zt[lo:lo + SWA_HEAD_DIM, c * BLOCK:(c + 1) * BLOCK].astype(BF16)


def _in_proj(h, wz, wt, q_g, kv_g, wuq_t, wkn, wv_t, pos_row, freq_col):
    s, d = h.shape
    tm = INPROJ_TM
    assert tm == MLA_TK
    est = (2 * tm * d * 4 + 2 * d * (Z_WIDTH + ZT_ROWS) * 2
           + 2 * Q_LORA * (MLA_HEADS * MLA_QK_PAD + 2 * MLA_WIDTH) * 2
           + 2 * tm * (2 * MLA_HEADS * MLA_QK_PAD + MLA_WIDTH + SWA_WIDTH + 4 * V7X_LANES) * 2
           + 4 * tm * (Z_WIDTH + ZT_ROWS + 2 * MLA_HEADS * MLA_QK_PAD) * 4)
    const = lambda i: (0, 0)
    return pl.pallas_call(
        _in_proj_kernel,
        grid=(s // tm,),
        in_specs=[
            pl.BlockSpec((tm, d), lambda i: (i, 0)),
            pl.BlockSpec((d, Z_WIDTH), const),
            pl.BlockSpec((ZT_ROWS, d), const),
            pl.BlockSpec((1, Q_LORA), const),
            pl.BlockSpec((1, KV_LORA), const),
            pl.BlockSpec((MLA_HEADS * MLA_QK_PAD, Q_LORA), const),
            pl.BlockSpec((KV_LORA, MLA_WIDTH), const),
            pl.BlockSpec((MLA_WIDTH, KV_LORA), const),
            pl.BlockSpec((1, tm), lambda i: (0, i)),
            pl.BlockSpec((HALF_ROPE, 1), const),
        ],
        out_specs=[
            pl.BlockSpec((MLA_HEADS * MLA_QK_PAD, tm), lambda i: (0, i)),
            pl.BlockSpec((tm, MLA_HEADS * MLA_QK_PAD), lambda i: (i, 0)),
            pl.BlockSpec((MLA_HEADS, 1, MLA_V_ROWS, tm), lambda i: (0, i, 0, 0)),
            pl.BlockSpec((SWA_HEADS, SWA_HEAD_DIM, tm), lambda i: (0, 0, i)),
            pl.BlockSpec((SWA_KV_HEADS, tm, SWA_HEAD_DIM), lambda i: (0, i, 0)),
            pl.BlockSpec((SWA_KV_HEADS, tm // BLOCK, SWA_HEAD_DIM, BLOCK), lambda i: (0, i, 0, 0)),
        ],
        out_shape=[
            jax.ShapeDtypeStruct((MLA_HEADS * MLA_QK_PAD, s), BF16),
            jax.ShapeDtypeStruct((s, MLA_HEADS * MLA_QK_PAD), BF16),
            jax.ShapeDtypeStruct((MLA_HEADS, s // tm, MLA_V_ROWS, tm), BF16),
            jax.ShapeDtypeStruct((SWA_HEADS, SWA_HEAD_DIM, s), BF16),
            jax.ShapeDtypeStruct((SWA_KV_HEADS, s, SWA_HEAD_DIM), BF16),
            jax.ShapeDtypeStruct((SWA_KV_HEADS, s // BLOCK, SWA_HEAD_DIM, BLOCK), BF16),
        ],
        compiler_params=pltpu.CompilerParams(
            dimension_semantics=("parallel",),
            vmem_limit_bytes=_vmem_limit(est)),
        name="in_proj",
    )(h, wz, wt, q_g, kv_g, wuq_t, wkn, wv_t, pos_row, freq_col)


MLA_TQ = 512
MLA_TK = 512
MLA_HB = 2
MLA_SCALE = (QK_NOPE + QK_ROPE) ** -0.5
MLA_SCALE_LOG2E = MLA_SCALE * math.log2(math.e)


def _mla_flash_kernel(qt_ref, k_ref, vt_ref, o_ref, s_ref, p_ref, m_ref, corr_ref, acc_ref):
    i = pl.program_id(1)
    tk = MLA_TK
    heads = range(MLA_HB)

    def qk(t):
        start = pl.multiple_of(t * tk, tk)
        for h in heads:
            cols = slice(h * MLA_QK_PAD, (h + 1) * MLA_QK_PAD)
            s_ref[h] = _dot(k_ref[pl.ds(start, tk), cols], qt_ref[cols, :])

    def softmax(s_all, masked):
        for h in heads:
            s = s_all[h] * MLA_SCALE_LOG2E
            if masked:
                key = lax.broadcasted_iota(jnp.int32, s.shape, 0)
                qry = lax.broadcasted_iota(jnp.int32, s.shape, 1)
                s = jnp.where(key <= qry, s, NEG_BIG)
            m_prev = m_ref[h]
            m_new = jnp.maximum(m_prev, jnp.max(s, axis=0, keepdims=True))
            p_ref[h] = jnp.exp2(s - m_new).astype(BF16)
            corr_ref[h] = jnp.exp2(m_prev - m_new)
            m_ref[h] = m_new

    def pv(t):
        for h in heads:
            acc_ref[h] = corr_ref[h] * acc_ref[h] + _dot(vt_ref[h, t], p_ref[h])

    def load_scores():
        return [s_ref[h] for h in heads]

    m_ref[...] = jnp.full_like(m_ref, NEG_BIG)
    acc_ref[...] = jnp.zeros_like(acc_ref)

    @pl.when(i == 0)
    def _():
        p_ref[...] = jnp.zeros_like(p_ref)
        corr_ref[...] = jnp.zeros_like(corr_ref)

    qk(0)

    @pl.when(i >= 1)
    def _():
        s_all = load_scores()
        qk(1)
        softmax(s_all, masked=False)

    def body(t, carry):
        pv(t)
        s_all = load_scores()
        qk(t + 2)
        softmax(s_all, masked=False)
        return carry

    lax.fori_loop(0, i - 1, body, 0)
    pv(jnp.maximum(i - 1, 0))
    softmax(load_scores(), masked=True)
    pv(i)
    for h in heads:
        o_ref[:, h * V_HEAD:(h + 1) * V_HEAD] = (
            acc_ref[h, 0:V_HEAD, :] / acc_ref[h, V_HEAD:V_HEAD + 1, :]).T


def _mla_flash(qt, k, vt):
    s = k.shape[0]
    tq, tk, hb = MLA_TQ, MLA_TK, MLA_HB
    assert tq == tk
    est = (2 * hb * tq * MLA_QK_PAD * 2 + 2 * hb * s * MLA_QK_PAD * 2 + 2 * hb * s * MLA_V_ROWS * 2
           + 2 * hb * tq * V_HEAD * 4 + hb * tk * tq * (4 + 2) + hb * tq * MLA_V_ROWS * 4
           + 4 * hb * tq * tk * 4)
    return pl.pallas_call(
        _mla_flash_kernel,
        grid=(MLA_HEADS // hb, s // tq),
        in_specs=[
            pl.BlockSpec((hb * MLA_QK_PAD, tq), lambda g, i: (g, i)),
            pl.BlockSpec((s, hb * MLA_QK_PAD), lambda g, i: (0, g)),
            pl.BlockSpec((hb, s // tk, MLA_V_ROWS, tk), lambda g, i: (g, 0, 0, 0)),
        ],
        out_specs=pl.BlockSpec((tq, hb * V_HEAD), lambda g, i: (i, g)),
        out_shape=jax.ShapeDtypeStruct((s, MLA_WIDTH), F32),
        scratch_shapes=[pltpu.VMEM((hb, tk, tq), F32), pltpu.VMEM((hb, tk, tq), BF16),
                        pltpu.VMEM((hb, 1, tq), F32), pltpu.VMEM((hb, 1, tq), F32),
                        pltpu.VMEM((hb, MLA_V_ROWS, tq), F32)],
        compiler_params=pltpu.CompilerParams(
            dimension_semantics=("parallel", "arbitrary"),
            vmem_limit_bytes=_vmem_limit(est)),
        name="mla_flash",
    )(qt, k, vt)


SWA_SCALE = SWA_HEAD_DIM ** -0.5
SWA_COLS = SWA_GROUP * BLOCK
SWA_NB = 4


def _t5_bucket_np(dist):
    n = np.maximum(dist, 0)
    max_exact = REL_BUCKETS // 2
    large = max_exact + (np.log(np.maximum(n, 1).astype(np.float32) / max_exact)
                         / math.log(REL_MAX_DIST / max_exact)
                         * (REL_BUCKETS - max_exact)).astype(np.int32)
    large = np.minimum(large, REL_BUCKETS - 1)
    return np.where(n < max_exact, n, large).astype(np.int32)


def _swa_bucket_table():
    j = np.arange(BLOCK)[:, None]
    i = np.arange(BLOCK)[None, :]
    dist = np.where(j <= i, i - j, BLOCK + i - j)
    return _t5_bucket_np(dist)


def _swa_kernel(rb_ref, bkt_ref, qt_ref, k_ref, vt_ref, sink_ref, o_ref, bias_ref):
    kv = pl.program_id(0)
    n = pl.program_id(1)

    @pl.when((kv == 0) & (n == 0))
    def _():
        bkt = bkt_ref[...]
        for h in range(SWA_HEADS):
            acc = jnp.zeros((BLOCK, BLOCK), F32)
            for b in range(REL_BUCKETS):
                acc = jnp.where(bkt == b, rb_ref[b, h], acc)
            g = h % SWA_GROUP
            bias_ref[h // SWA_GROUP, :, g * BLOCK:(g + 1) * BLOCK] = acc

    key = lax.broadcasted_iota(jnp.int32, (BLOCK, SWA_COLS), 0)
    qry = lax.broadcasted_iota(jnp.int32, (BLOCK, SWA_COLS), 1) & (BLOCK - 1)
    lower = key <= qry
    bias = bias_ref[kv]
    sink = sink_ref[...]

    for b in range(SWA_NB):
        blk = n * SWA_NB + b
        qt = jnp.concatenate([qt_ref[g, :, b * BLOCK:(b + 1) * BLOCK] for g in range(SWA_GROUP)],
                             axis=1)
        cur = pl.multiple_of(blk * BLOCK, BLOCK)
        prev_blk = jnp.maximum(blk - 1, 0)
        prev = pl.multiple_of(prev_blk * BLOCK, BLOCK)
        k_band = jnp.concatenate([k_ref[pl.ds(prev, BLOCK), :], k_ref[pl.ds(cur, BLOCK), :]], axis=0)
        s_band = _dot(k_band, qt)
        s = jnp.where(lower, s_band[BLOCK:], s_band[:BLOCK]) * SWA_SCALE + bias
        if b == 0:
            s = jnp.where(lower | (blk > 0), s, NEG_BIG)

        m = jnp.maximum(jnp.max(s, axis=0, keepdims=True), sink)
        e = jnp.exp(s - m)
        denom = jnp.sum(e, axis=0, keepdims=True) + jnp.exp(sink - m)
        p = e * (1.0 / denom)
        p_cur = jnp.where(lower, p, 0.0).astype(BF16)
        p_prev = jnp.where(lower, 0.0, p).astype(BF16)
        ot = _dot(vt_ref[blk], p_cur) + _dot(vt_ref[prev_blk], p_prev)
        o_ref[b * BLOCK:(b + 1) * BLOCK, :] = jnp.concatenate(
            [ot[:, g * BLOCK:(g + 1) * BLOCK] for g in range(SWA_GROUP)], axis=0).T


def _swa_attn(rel_bias, bucket_tbl, qst, ks, vst, sink_rows):
    s = ks.shape[1]
    rows = SWA_NB * BLOCK
    est = (2 * SWA_GROUP * SWA_HEAD_DIM * rows * 2 + 2 * s * V7X_LANES * 2 + 2 * s * SWA_HEAD_DIM * 2
           + 2 * rows * SWA_GROUP * SWA_HEAD_DIM * 4 + SWA_KV_HEADS * BLOCK * SWA_COLS * 4
           + 10 * SWA_NB * BLOCK * SWA_COLS * 4)
    return pl.pallas_call(
        _swa_kernel,
        grid=(SWA_KV_HEADS, s // rows),
        in_specs=[
            pl.BlockSpec(memory_space=pltpu.SMEM),
            pl.BlockSpec((BLOCK, BLOCK), lambda kv, n: (0, 0)),
            pl.BlockSpec((SWA_GROUP, SWA_HEAD_DIM, rows), lambda kv, n: (kv, 0, n)),
            pl.BlockSpec((None, s, SWA_HEAD_DIM), lambda kv, n: (kv, 0, 0)),
            pl.BlockSpec((None, s // BLOCK, SWA_HEAD_DIM, BLOCK), lambda kv, n: (kv, 0, 0, 0)),
            pl.BlockSpec((None, 1, SWA_COLS), lambda kv, n: (kv, 0, 0)),
        ],
        out_specs=pl.BlockSpec((rows, SWA_GROUP * SWA_HEAD_DIM), lambda kv, n: (n, kv)),
        out_shape=jax.ShapeDtypeStruct((s, SWA_WIDTH), F32),
        scratch_shapes=[pltpu.VMEM((SWA_KV_HEADS, BLOCK, SWA_COLS), F32)],
        compiler_params=pltpu.CompilerParams(
            dimension_semantics=("arbitrary", "arbitrary"),
            vmem_limit_bytes=_vmem_limit(est)),
        name="swa_attn",
    )(rel_bias, bucket_tbl, qst, ks, vst, sink_rows)


OUTPROJ_TM = 512
OUTPROJ_CHUNKS = 2


def _out_proj_kernel(am_ref, as_ref, h_ref, mg_ref, sg_ref, wm_ref, ws_ref, g_ref, b_ref, o_ref):
    chunk = o_ref.shape[0] // OUTPROJ_CHUNKS
    for c in range(OUTPROJ_CHUNKS):
        rows = slice(c * chunk, (c + 1) * chunk)
        nm = _rms_norm(am_ref[rows, :], mg_ref[...]).astype(BF16)
        ns = _rms_norm(as_ref[rows, :], sg_ref[...]).astype(BF16)
        mixed = _dot(nm, wm_ref[...]) + _dot(ns, ws_ref[...])
        o_ref[rows, :] = _layer_norm(ALPHA * h_ref[rows, :] + mixed, g_ref[...], b_ref[...])


def _out_proj(a_mla, a_swa, h, mla_g, swa_g, w_out, g, b):
    s, d = h.shape
    tm = OUTPROJ_TM
    est = (2 * tm * (MLA_WIDTH + SWA_WIDTH) * 4 + 4 * tm * d * 4 + 2 * (MLA_WIDTH + SWA_WIDTH) * d * 2
           + 4 * tm * d * 4)
    const = lambda i: (0, 0)
    return pl.pallas_call(
        _out_proj_kernel,
        grid=(s // tm,),
        in_specs=[
            pl.BlockSpec((tm, MLA_WIDTH), lambda i: (i, 0)),
            pl.BlockSpec((tm, SWA_WIDTH), lambda i: (i, 0)),
            pl.BlockSpec((tm, d), lambda i: (i, 0)),
            pl.BlockSpec((1, MLA_WIDTH), const),
            pl.BlockSpec((1, SWA_WIDTH), const),
            pl.BlockSpec((MLA_WIDTH, d), const),
            pl.BlockSpec((SWA_WIDTH, d), lambda i: (MLA_WIDTH // SWA_WIDTH, 0)),
            pl.BlockSpec((1, d), const),
            pl.BlockSpec((1, d), const),
        ],
        out_specs=pl.BlockSpec((tm, d), lambda i: (i, 0)),
        out_shape=jax.ShapeDtypeStruct((s, d), F32),
        compiler_params=pltpu.CompilerParams(
            dimension_semantics=("parallel",),
            vmem_limit_bytes=_vmem_limit(est)),
        name="out_proj",
    )(a_mla, a_swa, h, mla_g, swa_g, w_out, w_out, g, b)


PLE_TM = 512


def _ple_kernel(h_ref, p_ref, wg_ref, wp_ref, o_ref):
    h = h_ref[...]
    gate = _dot(h.astype(BF16), wg_ref[...])
    proj = _dot(p_ref[...].astype(BF16), wp_ref[...])
    o_ref[...] = h + (1.0 / (1.0 + jnp.exp(-gate))) * proj


def _ple(h, p, w_gate, w_proj):
    s, d = h.shape
    tm = PLE_TM
    est = 4 * tm * d * 4 + 2 * tm * PLE_DIM * 4 + 2 * (d + PLE_DIM) * d * 2 + 4 * tm * d * 4
    const = lambda i: (0, 0)
    return pl.pallas_call(
        _ple_kernel,
        grid=(s // tm,),
        in_specs=[
            pl.BlockSpec((tm, d), lambda i: (i, 0)),
            pl.BlockSpec((tm, PLE_DIM), lambda i: (i, 0)),
            pl.BlockSpec((d, d), const),
            pl.BlockSpec((PLE_DIM, d), const),
        ],
        out_specs=pl.BlockSpec((tm, d), lambda i: (i, 0)),
        out_shape=jax.ShapeDtypeStruct((s, d), F32),
        compiler_params=pltpu.CompilerParams(
            dimension_semantics=("parallel",),
            vmem_limit_bytes=_vmem_limit(est)),
        name="ple",
    )(h, p, w_gate, w_proj)


def _split_w_in(w_in):
    d = w_in.shape[0]
    b_kr = Q_LORA + KV_LORA
    b_qs = b_kr + QK_ROPE
    b_ks = b_qs + SWA_WIDTH
    b_vs = b_ks + SWA_KV_WIDTH
    wz = jnp.concatenate([w_in[:, :b_kr], w_in[:, b_ks:b_vs]], axis=1)
    pad = jnp.zeros((d, V7X_LANES - QK_ROPE), w_in.dtype)
    wt = jnp.concatenate([w_in[:, b_qs:b_ks], w_in[:, b_vs:], w_in[:, b_kr:b_qs], pad], axis=1).T
    return wz, wt


def _pack_w_uq_t(w_uq):
    r = w_uq.shape[0]
    w = w_uq.reshape(r, MLA_HEADS, QK_NOPE + QK_ROPE)
    w = jnp.pad(w, ((0, 0), (0, 0), (0, MLA_QK_PAD - QK_NOPE - QK_ROPE)))
    return w.reshape(r, MLA_HEADS * MLA_QK_PAD).T


def _rope_freq_col():
    freqs = ROPE_THETA ** (-jnp.arange(0, QK_ROPE, 2, dtype=F32) / QK_ROPE)
    return freqs.reshape(HALF_ROPE, 1)


def kernel(x, p, positions, rel_bias, ln1_g, ln1_b, ffn1_w1, ffn1_w3, ffn1_w2, w_in, q_norm_g, w_uq,
           kv_norm_g, w_ukv, swa_sinks, mla_out_g, swa_out_g, w_out, ln2_g, ln2_b, ffn2_w1, ffn2_w3,
           ffn2_w2, ln3_g, ln3_b, ple_w_gate, ple_w_proj):
    assert x.shape == (1, SEQ, D_MODEL) and DEPTH == 1
    row = lambda a: a.reshape(1, -1)
    h = x[0]
    pos_row = positions[0].astype(F32).reshape(1, SEQ)
    freq_col = _rope_freq_col()
    bucket_tbl = jnp.asarray(_swa_bucket_table())

    i = 0
    h, (w1b, w3b, w2b, w_out_b, w_gate_b, w_proj_b) = _ffn_ln(
        h, ffn1_w1[i].astype(BF16), ffn1_w3[i].astype(BF16), ffn1_w2[i].astype(BF16),
        row(ln1_g[i]), row(ln1_b[i]),
        up_casts=(ffn2_w1[i], ffn2_w3[i]), down_casts=(ffn2_w2[i],),
        row_casts=(w_out[i], ple_w_gate[i], ple_w_proj[i]))

    w_ukv_h = w_ukv[i].reshape(KV_LORA, MLA_HEADS, QK_NOPE + V_HEAD)
    wkn = w_ukv_h[:, :, :QK_NOPE].reshape(KV_LORA, MLA_WIDTH).astype(BF16)
    wv_t = w_ukv_h[:, :, QK_NOPE:].reshape(KV_LORA, MLA_WIDTH).T.astype(BF16)
    wz, wt = _split_w_in(w_in[i])
    qt, k, vt, qst, ks, vst = _in_proj(
        h, wz.astype(BF16), wt.astype(BF16), row(q_norm_g[i]), row(kv_norm_g[i]),
        _pack_w_uq_t(w_uq[i]).astype(BF16), wkn, wv_t, pos_row, freq_col)

    a_mla = _mla_flash(qt, k, vt)
    sink_rows = jnp.repeat(swa_sinks[i], BLOCK).reshape(SWA_KV_HEADS, 1, SWA_COLS)
    a_swa = _swa_attn(rel_bias, bucket_tbl, qst, ks, vst, sink_rows)

    h = _out_proj(a_mla, a_swa, h, row(mla_out_g[i]), row(swa_out_g[i]), w_out_b,
                  row(ln2_g[i]), row(ln2_b[i]))

    h, _ = _ffn_ln(h, w1b, w3b, w2b, row(ln3_g[i]), row(ln3_b[i]))
    h = _ple(h, p[i, 0], w_gate_b, w_proj_b)
    return h[None]
```

```python
import functools
import math

import numpy as np
import jax
import jax.numpy as jnp
from jax import lax
from jax.experimental import pallas as pl
from jax.experimental.pallas import tpu as pltpu

F32 = jnp.float32
BF16 = jnp.bfloat16

D_MODEL = 2048
SEQ = 8192
DEPTH = 1
PLE_DIM = 256
MLA_HEADS = 8
Q_LORA = 512
KV_LORA = 512
QK_NOPE = 128
QK_ROPE = 64
V_HEAD = 128
ROPE_THETA = 10000.0
SWA_HEADS = 16
SWA_KV_HEADS = 2
SWA_GROUP = SWA_HEADS // SWA_KV_HEADS
SWA_HEAD_DIM = 64
WINDOW = 128
BLOCK = 128
REL_BUCKETS = 32
REL_MAX_DIST = 128
D_FF = 5632
ALPHA = (2.0 * DEPTH) ** 0.25
EPS = 1e-5
MLA_WIDTH = MLA_HEADS * V_HEAD
SWA_WIDTH = SWA_HEADS * SWA_HEAD_DIM
SWA_KV_WIDTH = SWA_KV_HEADS * SWA_HEAD_DIM

V7X_LANES = 128
V7X_VMEM_BYTES = 64 * 1024 * 1024
MIB = 1024 * 1024

V7X_BF16_SUBLANES = 16

MLA_QK_PAD = 2 * V7X_LANES
MLA_V_ROWS = V_HEAD + V7X_BF16_SUBLANES
NEG_BIG = float(np.finfo(np.float32).min)


def _vmem_limit(estimate_bytes):
    return int(min(estimate_bytes + 8 * MIB, V7X_VMEM_BYTES - 4 * MIB))


def _layer_norm(y, g, b):
    mu = jnp.mean(y, axis=-1, keepdims=True)
    yc = y - mu
    var = jnp.mean(yc * yc, axis=-1, keepdims=True)
    return yc * lax.rsqrt(var + EPS) * g + b


def _rms_norm(x, g):
    ms = jnp.mean(x * x, axis=-1, keepdims=True)
    return x * lax.rsqrt(ms + EPS) * g


def _dot(a, b):
    return jnp.dot(a, b, preferred_element_type=F32)


def _dot_nt(a, b):
    return lax.dot_general(a, b, (((1,), (1,)), ((), ())), preferred_element_type=F32)


FFN_TM = 1024
FFN_TF = 256
FFN_ROW_CHUNK = 512
FFN_LN_ROWS = 256
FFN_CAST_SPLIT = 8


def _ffn_ln_kernel(x_ref, w1_ref, w3_ref, w2_ref, g_ref, b_ref, *rest, n_casts):
    srcs, o_ref, dsts = rest[:n_casts], rest[n_casts], rest[n_casts + 1:2 * n_casts + 1]
    xb_ref, w1b_ref, w3b_ref, w2b_ref = rest[2 * n_casts + 1:]
    f = pl.program_id(1)

    @pl.when(f == 0)
    def _():
        xb_ref[...] = x_ref[...].astype(BF16)
        o_ref[...] = jnp.zeros_like(o_ref)

    w1b_ref[...] = w1_ref[...].astype(BF16)
    w3b_ref[...] = w3_ref[...].astype(BF16)
    w2b_ref[...] = w2_ref[...].astype(BF16)
    for src, dst in zip(srcs, dsts):
        dst[...] = src[...].astype(BF16)

    for c in range(o_ref.shape[0] // FFN_ROW_CHUNK):
        rows = slice(c * FFN_ROW_CHUNK, (c + 1) * FFN_ROW_CHUNK)
        xb = xb_ref[rows, :]
        gate = _dot(xb, w1b_ref[...])
        up = _dot(xb, w3b_ref[...])
        hidden = gate * (1.0 / (1.0 + jnp.exp(-gate))) * up
        o_ref[rows, :] += _dot(hidden.astype(BF16), w2b_ref[...])

    @pl.when(f == pl.num_programs(1) - 1)
    def _():
        for c in range(o_ref.shape[0] // FFN_LN_ROWS):
            rows = slice(c * FFN_LN_ROWS, (c + 1) * FFN_LN_ROWS)
            y = ALPHA * x_ref[rows, :] + 0.5 * o_ref[rows, :]
            o_ref[rows, :] = _layer_norm(y, g_ref[...], b_ref[...])


def _row_cast_spec(a, nm, nf):
    r, c = a.shape
    split = FFN_CAST_SPLIT if r % (nm * FFN_CAST_SPLIT * V7X_BF16_SUBLANES) == 0 else 1
    assert split <= nf and r % (nm * split * V7X_BF16_SUBLANES) == 0
    return pl.BlockSpec((r // (nm * split), c), lambda i, f: (i * split + jnp.minimum(f, split - 1), 0))


def _ffn_ln(x, w1, w3, w2, g, b, row_casts=()):
    s, d = x.shape
    dff = w1.shape[1]
    tm, tf = FFN_TM, FFN_TF
    nm, nf = s // tm, dff // tf
    cast_specs = [_row_cast_spec(a, nm, nf) for a in row_casts]
    cast_bytes = sum(2 * (4 + 2) * math.prod(spec.block_shape) for spec in cast_specs)
    est = (2 * tm * d * 4 + tm * d * 2 + 2 * 3 * d * tf * 4 + 3 * d * tf * 2 + 2 * tm * d * 4
           + 3 * FFN_ROW_CHUNK * tf * 4 + cast_bytes)
    out = pl.pallas_call(
        functools.partial(_ffn_ln_kernel, n_casts=len(row_casts)),
        grid=(nm, nf),
        in_specs=[
            pl.BlockSpec((tm, d), lambda i, f: (i, 0)),
            pl.BlockSpec((d, tf), lambda i, f: (0, f)),
            pl.BlockSpec((d, tf), lambda i, f: (0, f)),
            pl.BlockSpec((tf, d), lambda i, f: (f, 0)),
            pl.BlockSpec((1, d), lambda i, f: (0, 0)),
            pl.BlockSpec((1, d), lambda i, f: (0, 0)),
        ] + cast_specs,
        out_specs=[pl.BlockSpec((tm, d), lambda i, f: (i, 0))] + cast_specs,
        out_shape=[jax.ShapeDtypeStruct((s, d), F32)] + [jax.ShapeDtypeStruct(a.shape, BF16) for a in row_casts],
        scratch_shapes=[pltpu.VMEM((tm, d), BF16), pltpu.VMEM((d, tf), BF16), pltpu.VMEM((d, tf), BF16),
                        pltpu.VMEM((tf, d), BF16)],
        compiler_params=pltpu.CompilerParams(
            dimension_semantics=("arbitrary", "arbitrary"),
            vmem_limit_bytes=_vmem_limit(est)),
        name="ffn_ln",
    )(x, w1, w3, w2, g, b, *row_casts)
    return out[0], out[1:]


INPROJ_TM = 512
Z_CQ = 0
Z_CKV = Z_CQ + Q_LORA
Z_KS = Z_CKV + KV_LORA
Z_WIDTH = Z_KS + SWA_KV_WIDTH
ZT_QS = 0
ZT_VS = ZT_QS + SWA_WIDTH
ZT_KR = ZT_VS + SWA_KV_WIDTH
ZT_ROWS = ZT_KR + V7X_LANES
HALF_ROPE = QK_ROPE // 2


def _rope_rows(x1, x2, cos, sin):
    return x1 * cos - x2 * sin, x1 * sin + x2 * cos


def _in_proj_kernel(h_ref, wz_ref, wt_ref, qg_ref, kvg_ref, wuqt_ref, wkn_ref, wvt_ref, pos_ref,
                    freq_ref, qt_ref, k_ref, vt_ref, qst_ref, ks_ref, vst_ref):
    tm = h_ref.shape[0]
    hb = h_ref[...].astype(BF16)
    z = _dot(hb, wz_ref[...])
    zt = _dot_nt(wt_ref[...], hb)

    ang = freq_ref[...] * pos_ref[...]
    cos = jnp.cos(ang)
    sin = jnp.sin(ang)

    cq = _rms_norm(z[:, Z_CQ:Z_CQ + Q_LORA], qg_ref[...]).astype(BF16)
    ckv = _rms_norm(z[:, Z_CKV:Z_CKV + KV_LORA], kvg_ref[...]).astype(BF16)
    qt = _dot_nt(wuqt_ref[...], cq)
    kn = _dot(ckv, wkn_ref[...])
    vt = _dot_nt(wvt_ref[...], ckv)

    kr1, kr2 = _rope_rows(zt[ZT_KR:ZT_KR + HALF_ROPE], zt[ZT_KR + HALF_ROPE:ZT_KR + QK_ROPE], cos, sin)
    k_rope = jnp.concatenate([kr1, kr2, zt[ZT_KR + QK_ROPE:ZT_KR + V7X_LANES]], axis=0).T.astype(BF16)

    for h in range(MLA_HEADS):
        base = h * MLA_QK_PAD
        r0 = base + QK_NOPE
        qt_ref[base:r0, :] = qt[base:r0].astype(BF16)
        q1, q2 = _rope_rows(qt[r0:r0 + HALF_ROPE], qt[r0 + HALF_ROPE:r0 + QK_ROPE], cos, sin)
        qt_ref[r0:r0 + HALF_ROPE, :] = q1.astype(BF16)
        qt_ref[r0 + HALF_ROPE:r0 + QK_ROPE, :] = q2.astype(BF16)
        qt_ref[r0 + QK_ROPE:base + MLA_QK_PAD, :] = qt[r0 + QK_ROPE:base + MLA_QK_PAD].astype(BF16)
        k_ref[:, base:r0] = kn[:, h * QK_NOPE:(h + 1) * QK_NOPE].astype(BF16)
        k_ref[:, r0:base + MLA_QK_PAD] = k_rope
        vt_ref[h, 0, 0:V_HEAD, :] = vt[h * V_HEAD:(h + 1) * V_HEAD].astype(BF16)
        vt_ref[h, 0, V_HEAD:MLA_V_ROWS, :] = jnp.ones((MLA_V_ROWS - V_HEAD, tm), BF16)

    qst_ref[...] = zt[ZT_QS:ZT_QS + SWA_WIDTH].reshape(SWA_HEADS, SWA_HEAD_DIM, tm).astype(BF16)
    for kv in range(SWA_KV_HEADS):
        lo = Z_KS + kv * SWA_HEAD_DIM
        ks_ref[kv] = z[:, lo:lo + SWA_HEAD_DIM].astype(BF16)
        lo = ZT_VS + kv * SWA_HEAD_DIM
        for c in range(tm // BLOCK):
            vst_ref[kv, c] = zt[lo:lo + SWA_HEAD_DIM, c * BLOCK:(c + 1) * BLOCK].astype(BF16)


def _in_proj(h, wz, wt, q_g, kv_g, wuq_t, wkn, wv_t, pos_row, freq_col):
    s, d = h.shape
    tm = INPROJ_TM
    assert tm == MLA_TK
    est = (2 * tm * d * 4 + 2 * d * (Z_WIDTH + ZT_ROWS) * 2
           + 2 * Q_LORA * (MLA_HEADS * MLA_QK_PAD + 2 * MLA_WIDTH) * 2
           + 2 * tm * (2 * MLA_HEADS * MLA_QK_PAD + MLA_WIDTH + SWA_WIDTH + 4 * V7X_LANES) * 2
           + 4 * tm * (Z_WIDTH + ZT_ROWS + 2 * MLA_HEADS * MLA_QK_PAD) * 4)
    const = lambda i: (0, 0)
    return pl.pallas_call(
        _in_proj_kernel,
        grid=(s // tm,),
        in_specs=[
            pl.BlockSpec((tm, d), lambda i: (i, 0)),
            pl.BlockSpec((d, Z_WIDTH), const),
            pl.BlockSpec((ZT_ROWS, d), const),
            pl.BlockSpec((1, Q_LORA), const),
            pl.BlockSpec((1, KV_LORA), const),
            pl.BlockSpec((MLA_HEADS * MLA_QK_PAD, Q_LORA), const),
            pl.BlockSpec((KV_LORA, MLA_WIDTH), const),
            pl.BlockSpec((MLA_WIDTH, KV_LORA), const),
            pl.BlockSpec((1, tm), lambda i: (0, i)),
            pl.BlockSpec((HALF_ROPE, 1), const),
        ],
        out_specs=[
            pl.BlockSpec((MLA_HEADS * MLA_QK_PAD, tm), lambda i: (0, i)),
            pl.BlockSpec((tm, MLA_HEADS * MLA_QK_PAD), lambda i: (i, 0)),
            pl.BlockSpec((MLA_HEADS, 1, MLA_V_ROWS, tm), lambda i: (0, i, 0, 0)),
            pl.BlockSpec((SWA_HEADS, SWA_HEAD_DIM, tm), lambda i: (0, 0, i)),
            pl.BlockSpec((SWA_KV_HEADS, tm, SWA_HEAD_DIM), lambda i: (0, i, 0)),
            pl.BlockSpec((SWA_KV_HEADS, tm // BLOCK, SWA_HEAD_DIM, BLOCK), lambda i: (0, i, 0, 0)),
        ],
        out_shape=[
            jax.ShapeDtypeStruct((MLA_HEADS * MLA_QK_PAD, s), BF16),
            jax.ShapeDtypeStruct((s, MLA_HEADS * MLA_QK_PAD), BF16),
            jax.ShapeDtypeStruct((MLA_HEADS, s // tm, MLA_V_ROWS, tm), BF16),
            jax.ShapeDtypeStruct((SWA_HEADS, SWA_HEAD_DIM, s), BF16),
            jax.ShapeDtypeStruct((SWA_KV_HEADS, s, SWA_HEAD_DIM), BF16),
            jax.ShapeDtypeStruct((SWA_KV_HEADS, s // BLOCK, SWA_HEAD_DIM, BLOCK), BF16),
        ],
        compiler_params=pltpu.CompilerParams(
            dimension_semantics=("parallel",),
            vmem_limit_bytes=_vmem_limit(est)),
        name="in_proj",
    )(h, wz, wt, q_g, kv_g, wuq_t, wkn, wv_t, pos_row, freq_col)


MLA_TQ = 512
MLA_TK = 512
MLA_HB = 2
MLA_SCALE = (QK_NOPE + QK_ROPE) ** -0.5
MLA_SCALE_LOG2E = MLA_SCALE * math.log2(math.e)


def _mla_flash_kernel(qt_ref, k_ref, vt_ref, o_ref, s_ref, p_ref, m_ref, corr_ref, acc_ref):
    i = pl.program_id(1)
    tk = MLA_TK
    heads = range(MLA_HB)

    def qk(t):
        start = pl.multiple_of(t * tk, tk)
        for h in heads:
            cols = slice(h * MLA_QK_PAD, (h + 1) * MLA_QK_PAD)
            s_ref[h] = _dot(k_ref[pl.ds(start, tk), cols], qt_ref[cols, :])

    def softmax(s_all, masked):
        for h in heads:
            s = s_all[h] * MLA_SCALE_LOG2E
            if masked:
                key = lax.broadcasted_iota(jnp.int32, s.shape, 0)
                qry = lax.broadcasted_iota(jnp.int32, s.shape, 1)
                s = jnp.where(key <= qry, s, NEG_BIG)
            m_prev = m_ref[h]
            m_new = jnp.maximum(m_prev, jnp.max(s, axis=0, keepdims=True))
            p_ref[h] = jnp.exp2(s - m_new).astype(BF16)
            corr_ref[h] = jnp.exp2(m_prev - m_new)
            m_ref[h] = m_new

    def pv(t):
        for h in heads:
            acc_ref[h] = corr_ref[h] * acc_ref[h] + _dot(vt_ref[h, t], p_ref[h])

    def load_scores():
        return [s_ref[h] for h in heads]

    m_ref[...] = jnp.full_like(m_ref, NEG_BIG)
    acc_ref[...] = jnp.zeros_like(acc_ref)

    @pl.when(i == 0)
    def _():
        p_ref[...] = jnp.zeros_like(p_ref)
        corr_ref[...] = jnp.zeros_like(corr_ref)

    qk(0)

    @pl.when(i >= 1)
    def _():
        s_all = load_scores()
        qk(1)
        softmax(s_all, masked=False)

    def body(t, carry):
        pv(t)
        s_all = load_scores()
        qk(t + 2)
        softmax(s_all, masked=False)
        return carry

    lax.fori_loop(0, i - 1, body, 0)
    pv(jnp.maximum(i - 1, 0))
    softmax(load_scores(), masked=True)
    pv(i)
    for h in heads:
        o_ref[:, h * V_HEAD:(h + 1) * V_HEAD] = (
            acc_ref[h, 0:V_HEAD, :] / acc_ref[h, V_HEAD:V_HEAD + 1, :]).T


def _mla_flash(qt, k, vt):
    s = k.shape[0]
    tq, tk, hb = MLA_TQ, MLA_TK, MLA_HB
    assert tq == tk
    est = (2 * hb * tq * MLA_QK_PAD * 2 + 2 * hb * s * MLA_QK_PAD * 2 + 2 * hb * s * MLA_V_ROWS * 2
           + 2 * hb * tq * V_HEAD * 4 + hb * tk * tq * (4 + 2) + hb * tq * MLA_V_ROWS * 4
           + 4 * hb * tq * tk * 4)
    return pl.pallas_call(
        _mla_flash_kernel,
        grid=(MLA_HEADS // hb, s // tq),
        in_specs=[
            pl.BlockSpec((hb * MLA_QK_PAD, tq), lambda g, i: (g, i)),
            pl.BlockSpec((s, hb * MLA_QK_PAD), lambda g, i: (0, g)),
            pl.BlockSpec((hb, s // tk, MLA_V_ROWS, tk), lambda g, i: (g, 0, 0, 0)),
        ],
        out_specs=pl.BlockSpec((tq, hb * V_HEAD), lambda g, i: (i, g)),
        out_shape=jax.ShapeDtypeStruct((s, MLA_WIDTH), F32),
        scratch_shapes=[pltpu.VMEM((hb, tk, tq), F32), pltpu.VMEM((hb, tk, tq), BF16),
                        pltpu.VMEM((hb, 1, tq), F32), pltpu.VMEM((hb, 1, tq), F32),
                        pltpu.VMEM((hb, MLA_V_ROWS, tq), F32)],
        compiler_params=pltpu.CompilerParams(
            dimension_semantics=("parallel", "arbitrary"),
            vmem_limit_bytes=_vmem_limit(est)),
        name="mla_flash",
    )(qt, k, vt)


SWA_SCALE = SWA_HEAD_DIM ** -0.5
SWA_COLS = SWA_GROUP * BLOCK
SWA_NB = 4


def _t5_bucket_np(dist):
    n = np.maximum(dist, 0)
    max_exact = REL_BUCKETS // 2
    large = max_exact + (np.log(np.maximum(n, 1).astype(np.float32) / max_exact)
                         / math.log(REL_MAX_DIST / max_exact)
                         * (REL_BUCKETS - max_exact)).astype(np.int32)
    large = np.minimum(large, REL_BUCKETS - 1)
    return np.where(n < max_exact, n, large).astype(np.int32)


def _swa_bucket_table():
    j = np.arange(BLOCK)[:, None]
    i = np.arange(BLOCK)[None, :]
    dist = np.where(j <= i, i - j, BLOCK + i - j)
    return _t5_bucket_np(dist)


def _swa_kernel(rb_ref, bkt_ref, qt_ref, k_ref, vt_ref, sink_ref, o_ref, bias_ref):
    kv = pl.program_id(0)
    n = pl.program_id(1)

    @pl.when((kv == 0) & (n == 0))
    def _():
        bkt = bkt_ref[...]
        for h in range(SWA_HEADS):
            acc = jnp.zeros((BLOCK, BLOCK), F32)
            for b in range(REL_BUCKETS):
                acc = jnp.where(bkt == b, rb_ref[b, h], acc)
            g = h % SWA_GROUP
            bias_ref[h // SWA_GROUP, :, g * BLOCK:(g + 1) * BLOCK] = acc

    key = lax.broadcasted_iota(jnp.int32, (BLOCK, SWA_COLS), 0)
    qry = lax.broadcasted_iota(jnp.int32, (BLOCK, SWA_COLS), 1) & (BLOCK - 1)
    lower = key <= qry
    bias = bias_ref[kv]
    sink = sink_ref[...]

    for b in range(SWA_NB):
        blk = n * SWA_NB + b
        qt = jnp.concatenate([qt_ref[g, :, b * BLOCK:(b + 1) * BLOCK] for g in range(SWA_GROUP)],
                             axis=1)
        cur = pl.multiple_of(blk * BLOCK, BLOCK)
        prev_blk = jnp.maximum(blk - 1, 0)
        prev = pl.multiple_of(prev_blk * BLOCK, BLOCK)
        k_band = jnp.concatenate([k_ref[pl.ds(prev, BLOCK), :], k_ref[pl.ds(cur, BLOCK), :]], axis=0)
        s_band = _dot(k_band, qt)
        s = jnp.where(lower, s_band[BLOCK:], s_band[:BLOCK]) * SWA_SCALE + bias
        if b == 0:
            s = jnp.where(lower | (blk > 0), s, NEG_BIG)

        m = jnp.maximum(jnp.max(s, axis=0, keepdims=True), sink)
        e = jnp.exp(s - m)
        denom = jnp.sum(e, axis=0, keepdims=True) + jnp.exp(sink - m)
        p = e * (1.0 / denom)
        p_cur = jnp.where(lower, p, 0.0).astype(BF16)
        p_prev = jnp.where(lower, 0.0, p).astype(BF16)
        ot = _dot(vt_ref[blk], p_cur) + _dot(vt_ref[prev_blk], p_prev)
        o_ref[b * BLOCK:(b + 1) * BLOCK, :] = jnp.concatenate(
            [ot[:, g * BLOCK:(g + 1) * BLOCK] for g in range(SWA_GROUP)], axis=0).T


def _swa_attn(rel_bias, bucket_tbl, qst, ks, vst, sink_rows):
    s = ks.shape[1]
    rows = SWA_NB * BLOCK
    est = (2 * SWA_GROUP * SWA_HEAD_DIM * rows * 2 + 2 * s * V7X_LANES * 2 + 2 * s * SWA_HEAD_DIM * 2
           + 2 * rows * SWA_GROUP * SWA_HEAD_DIM * 4 + SWA_KV_HEADS * BLOCK * SWA_COLS * 4
           + 10 * SWA_NB * BLOCK * SWA_COLS * 4)
    return pl.pallas_call(
        _swa_kernel,
        grid=(SWA_KV_HEADS, s // rows),
        in_specs=[
            pl.BlockSpec(memory_space=pltpu.SMEM),
            pl.BlockSpec((BLOCK, BLOCK), lambda kv, n: (0, 0)),
            pl.BlockSpec((SWA_GROUP, SWA_HEAD_DIM, rows), lambda kv, n: (kv, 0, n)),
            pl.BlockSpec((None, s, SWA_HEAD_DIM), lambda kv, n: (kv, 0, 0)),
            pl.BlockSpec((None, s // BLOCK, SWA_HEAD_DIM, BLOCK), lambda kv, n: (kv, 0, 0, 0)),
            pl.BlockSpec((None, 1, SWA_COLS), lambda kv, n: (kv, 0, 0)),
        ],
        out_specs=pl.BlockSpec((rows, SWA_GROUP * SWA_HEAD_DIM), lambda kv, n: (n, kv)),
        out_shape=jax.ShapeDtypeStruct((s, SWA_WIDTH), F32),
        scratch_shapes=[pltpu.VMEM((SWA_KV_HEADS, BLOCK, SWA_COLS), F32)],
        compiler_params=pltpu.CompilerParams(
            dimension_semantics=("arbitrary", "arbitrary"),
            vmem_limit_bytes=_vmem_limit(est)),
        name="swa_attn",
    )(rel_bias, bucket_tbl, qst, ks, vst, sink_rows)


OUTPROJ_TM = 512
OUTPROJ_CHUNKS = 2


def _out_proj_kernel(am_ref, as_ref, h_ref, mg_ref, sg_ref, wm_ref, ws_ref, g_ref, b_ref, o_ref):
    chunk = o_ref.shape[0] // OUTPROJ_CHUNKS
    for c in range(OUTPROJ_CHUNKS):
        rows = slice(c * chunk, (c + 1) * chunk)
        nm = _rms_norm(am_ref[rows, :], mg_ref[...]).astype(BF16)
        ns = _rms_norm(as_ref[rows, :], sg_ref[...]).astype(BF16)
        mixed = _dot(nm, wm_ref[...]) + _dot(ns, ws_ref[...])
        o_ref[rows, :] = _layer_norm(ALPHA * h_ref[rows, :] + mixed, g_ref[...], b_ref[...])


def _out_proj(a_mla, a_swa, h, mla_g, swa_g, w_out, g, b):
    s, d = h.shape
    tm = OUTPROJ_TM
    est = (2 * tm * (MLA_WIDTH + SWA_WIDTH) * 4 + 4 * tm * d * 4 + 2 * (MLA_WIDTH + SWA_WIDTH) * d * 2
           + 4 * tm * d * 4)
    const = lambda i: (0, 0)
    return pl.pallas_call(
        _out_proj_kernel,
        grid=(s // tm,),
        in_specs=[
            pl.BlockSpec((tm, MLA_WIDTH), lambda i: (i, 0)),
            pl.BlockSpec((tm, SWA_WIDTH), lambda i: (i, 0)),
            pl.BlockSpec((tm, d), lambda i: (i, 0)),
            pl.BlockSpec((1, MLA_WIDTH), const),
            pl.BlockSpec((1, SWA_WIDTH), const),
            pl.BlockSpec((MLA_WIDTH, d), const),
            pl.BlockSpec((SWA_WIDTH, d), lambda i: (MLA_WIDTH // SWA_WIDTH, 0)),
            pl.BlockSpec((1, d), const),
            pl.BlockSpec((1, d), const),
        ],
        out_specs=pl.BlockSpec((tm, d), lambda i: (i, 0)),
        out_shape=jax.ShapeDtypeStruct((s, d), F32),
        compiler_params=pltpu.CompilerParams(
            dimension_semantics=("parallel",),
            vmem_limit_bytes=_vmem_limit(est)),
        name="out_proj",
    )(a_mla, a_swa, h, mla_g, swa_g, w_out, w_out, g, b)


PLE_TM = 512


def _ple_kernel(h_ref, p_ref, wg_ref, wp_ref, o_ref):
    h = h_ref[...]
    gate = _dot(h.astype(BF16), wg_ref[...])
    proj = _dot(p_ref[...].astype(BF16), wp_ref[...])
    o_ref[...] = h + (1.0 / (1.0 + jnp.exp(-gate))) * proj


def _ple(h, p, w_gate, w_proj):
    s, d = h.shape
    tm = PLE_TM
    est = 4 * tm * d * 4 + 2 * tm * PLE_DIM * 4 + 2 * (d + PLE_DIM) * d * 2 + 4 * tm * d * 4
    const = lambda i: (0, 0)
    return pl.pallas_call(
        _ple_kernel,
        grid=(s // tm,),
        in_specs=[
            pl.BlockSpec((tm, d), lambda i: (i, 0)),
            pl.BlockSpec((tm, PLE_DIM), lambda i: (i, 0)),
            pl.BlockSpec((d, d), const),
            pl.BlockSpec((PLE_DIM, d), const),
        ],
        out_specs=pl.BlockSpec((tm, d), lambda i: (i, 0)),
        out_shape=jax.ShapeDtypeStruct((s, d), F32),
        compiler_params=pltpu.CompilerParams(
            dimension_semantics=("parallel",),
            vmem_limit_bytes=_vmem_limit(est)),
        name="ple",
    )(h, p, w_gate, w_proj)


def _split_w_in(w_in):
    d = w_in.shape[0]
    b_kr = Q_LORA + KV_LORA
    b_qs = b_kr + QK_ROPE
    b_ks = b_qs + SWA_WIDTH
    b_vs = b_ks + SWA_KV_WIDTH
    wz = jnp.concatenate([w_in[:, :b_kr], w_in[:, b_ks:b_vs]], axis=1)
    pad = jnp.zeros((d, V7X_LANES - QK_ROPE), w_in.dtype)
    wt = jnp.concatenate([w_in[:, b_qs:b_ks], w_in[:, b_vs:], w_in[:, b_kr:b_qs], pad], axis=1).T
    return wz, wt


def _pack_w_uq_t(w_uq):
    r = w_uq.shape[0]
    w = w_uq.reshape(r, MLA_HEADS, QK_NOPE + QK_ROPE)
    w = jnp.pad(w, ((0, 0), (0, 0), (0, MLA_QK_PAD - QK_NOPE - QK_ROPE)))
    return w.reshape(r, MLA_HEADS * MLA_QK_PAD).T


def _rope_freq_col():
    freqs = ROPE_THETA ** (-jnp.arange(0, QK_ROPE, 2, dtype=F32) / QK_ROPE)
    return freqs.reshape(HALF_ROPE, 1)


def kernel(x, p, positions, rel_bias, ln1_g, ln1_b, ffn1_w1, ffn1_w3, ffn1_w2, w_in, q_norm_g, w_uq,
           kv_norm_g, w_ukv, swa_sinks, mla_out_g, swa_out_g, w_out, ln2_g, ln2_b, ffn2_w1, ffn2_w3,
           ffn2_w2, ln3_g, ln3_b, ple_w_gate, ple_w_proj):
    assert x.shape == (1, SEQ, D_MODEL) and DEPTH == 1
    row = lambda a: a.reshape(1, -1)
    h = x[0]
    pos_row = positions[0].astype(F32).reshape(1, SEQ)
    freq_col = _rope_freq_col()
    bucket_tbl = jnp.asarray(_swa_bucket_table())

    i = 0
    h, (w_out_b, w_gate_b, w_proj_b) = _ffn_ln(
        h, ffn1_w1[i], ffn1_w3[i], ffn1_w2[i], row(ln1_g[i]), row(ln1_b[i]),
        row_casts=(w_out[i], ple_w_gate[i], ple_w_proj[i]))

    w_ukv_h = w_ukv[i].reshape(KV_LORA, MLA_HEADS, QK_NOPE + V_HEAD)
    wkn = w_ukv_h[:, :, :QK_NOPE].reshape(KV_LORA, MLA_WIDTH).astype(BF16)
    wv_t = w_ukv_h[:, :, QK_NOPE:].reshape(KV_LORA, MLA_WIDTH).T.astype(BF16)
    wz, wt = _split_w_in(w_in[i])
    qt, k, vt, qst, ks, vst = _in_proj(
        h, wz.astype(BF16), wt.astype(BF16), row(q_norm_g[i]), row(kv_norm_g[i]),
        _pack_w_uq_t(w_uq[i]).astype(BF16), wkn, wv_t, pos_row, freq_col)

    a_mla = _mla_flash(qt, k, vt)
    sink_rows = jnp.repeat(swa_sinks[i], BLOCK).reshape(SWA_KV_HEADS, 1, SWA_COLS)
    a_swa = _swa_attn(rel_bias, bucket_tbl, qst, ks, vst, sink_rows)

    h = _out_proj(a_mla, a_swa, h, row(mla_out_g[i]), row(swa_out_g[i]), w_out_b,
                  row(ln2_g[i]), row(ln2_b[i]))

    h, _ = _ffn_ln(h, ffn2_w1[i], ffn2_w3[i], ffn2_w2[i], row(ln3_g[i]), row(ln3_b[i]))
    h = _ple(h, p[i, 0], w_gate_b, w_proj_b)
    return h[None]
```

```python
import functools
import math

import numpy as np
import jax
import jax.numpy as jnp
from jax import lax
from jax.experimental import pallas as pl
from jax.experimental.pallas import tpu as pltpu

F32 = jnp.float32
BF16 = jnp.bfloat16

D_MODEL = 2048
SEQ = 8192
DEPTH = 1
PLE_DIM = 256
MLA_HEADS = 8
Q_LORA = 512
KV_LORA = 512
QK_NOPE = 128
QK_ROPE = 64
V_HEAD = 128
ROPE_THETA = 10000.0
SWA_HEADS = 16
SWA_KV_HEADS = 2
SWA_GROUP = SWA_HEADS // SWA_KV_HEADS
SWA_HEAD_DIM = 64
WINDOW = 128
BLOCK = 128
REL_BUCKETS = 32
REL_MAX_DIST = 128
D_FF = 5632
ALPHA = (2.0 * DEPTH) ** 0.25
EPS = 1e-5
MLA_WIDTH = MLA_HEADS * V_HEAD
SWA_WIDTH = SWA_HEADS * SWA_HEAD_DIM
SWA_KV_WIDTH = SWA_KV_HEADS * SWA_HEAD_DIM

V7X_LANES = 128
V7X_VMEM_BYTES = 64 * 1024 * 1024
MIB = 1024 * 1024

V7X_BF16_SUBLANES = 16

MLA_QK_PAD = 2 * V7X_LANES
MLA_V_ROWS = V_HEAD + V7X_BF16_SUBLANES
NEG_BIG = float(np.finfo(np.float32).min)


def _vmem_limit(estimate_bytes):
    return int(min(estimate_bytes + 8 * MIB, V7X_VMEM_BYTES - 4 * MIB))


def _layer_norm(y, g, b):
    mu = jnp.mean(y, axis=-1, keepdims=True)
    yc = y - mu
    var = jnp.mean(yc * yc, axis=-1, keepdims=True)
    return yc * lax.rsqrt(var + EPS) * g + b


def _rms_norm(x, g):
    ms = jnp.mean(x * x, axis=-1, keepdims=True)
    return x * lax.rsqrt(ms + EPS) * g


def _dot(a, b):
    return jnp.dot(a, b, preferred_element_type=F32)


def _dot_nt(a, b):
    return lax.dot_general(a, b, (((1,), (1,)), ((), ())), preferred_element_type=F32)


FFN_TM = 1024
FFN_TF = 256
FFN_ROW_CHUNK = 512
FFN_LN_ROWS = 256
FFN_CAST_SPLIT = 8


def _ffn_ln_kernel(x_ref, w1_ref, w3_ref, w2_ref, g_ref, b_ref, *rest, n_casts):
    srcs, o_ref, dsts = rest[:n_casts], rest[n_casts], rest[n_casts + 1:2 * n_casts + 1]
    xb_ref, w1b_ref, w3b_ref, w2b_ref = rest[2 * n_casts + 1:]
    f = pl.program_id(1)

    @pl.when(f == 0)
    def _():
        xb_ref[...] = x_ref[...].astype(BF16)
        o_ref[...] = jnp.zeros_like(o_ref)

    w1b_ref[...] = w1_ref[...].astype(BF16)
    w3b_ref[...] = w3_ref[...].astype(BF16)
    w2b_ref[...] = w2_ref[...].astype(BF16)
    for src, dst in zip(srcs, dsts):
        dst[...] = src[...].astype(BF16)

    for c in range(o_ref.shape[0] // FFN_ROW_CHUNK):
        rows = slice(c * FFN_ROW_CHUNK, (c + 1) * FFN_ROW_CHUNK)
        xb = xb_ref[rows, :]
        gate = _dot(xb, w1b_ref[...])
        up = _dot(xb, w3b_ref[...])
        hidden = gate * (1.0 / (1.0 + jnp.exp(-gate))) * up
        o_ref[rows, :] += _dot(hidden.astype(BF16), w2b_ref[...])

    @pl.when(f == pl.num_programs(1) - 1)
    def _():
        for c in range(o_ref.shape[0] // FFN_LN_ROWS):
            rows = slice(c * FFN_LN_ROWS, (c + 1) * FFN_LN_ROWS)
            y = ALPHA * x_ref[rows, :] + 0.5 * o_ref[rows, :]
            o_ref[rows, :] = _layer_norm(y, g_ref[...], b_ref[...])


def _row_cast_spec(a, nm, nf):
    r, c = a.shape
    split = FFN_CAST_SPLIT if r % (nm * FFN_CAST_SPLIT * V7X_BF16_SUBLANES) == 0 else 1
    assert split <= nf and r % (nm * split * V7X_BF16_SUBLANES) == 0
    return pl.BlockSpec((r // (nm * split), c), lambda i, f: (i * split + jnp.minimum(f, split - 1), 0))


def _ffn_ln(x, w1, w3, w2, g, b, row_casts=()):
    s, d = x.shape
    dff = w1.shape[1]
    tm, tf = FFN_TM, FFN_TF
    nm, nf = s // tm, dff // tf
    cast_specs = [_row_cast_spec(a, nm, nf) for a in row_casts]
    cast_bytes = sum(2 * (4 + 2) * math.prod(spec.block_shape) for spec in cast_specs)
    est = (2 * tm * d * 4 + tm * d * 2 + 2 * 3 * d * tf * 4 + 3 * d * tf * 2 + 2 * tm * d * 4
           + 3 * FFN_ROW_CHUNK * tf * 4 + cast_bytes)
    out = pl.pallas_call(
        functools.partial(_ffn_ln_kernel, n_casts=len(row_casts)),
        grid=(nm, nf),
        in_specs=[
            pl.BlockSpec((tm, d), lambda i, f: (i, 0)),
            pl.BlockSpec((d, tf), lambda i, f: (0, f)),
            pl.BlockSpec((d, tf), lambda i, f: (0, f)),
            pl.BlockSpec((tf, d), lambda i, f: (f, 0)),
            pl.BlockSpec((1, d), lambda i, f: (0, 0)),
            pl.BlockSpec((1, d), lambda i, f: (0, 0)),
        ] + cast_specs,
        out_specs=[pl.BlockSpec((tm, d), lambda i, f: (i, 0))] + cast_specs,
        out_shape=[jax.ShapeDtypeStruct((s, d), F32)] + [jax.ShapeDtypeStruct(a.shape, BF16) for a in row_casts],
        scratch_shapes=[pltpu.VMEM((tm, d), BF16), pltpu.VMEM((d, tf), BF16), pltpu.VMEM((d, tf), BF16),
                        pltpu.VMEM((tf, d), BF16)],
        compiler_params=pltpu.CompilerParams(
            dimension_semantics=("arbitrary", "arbitrary"),
            vmem_limit_bytes=_vmem_limit(est)),
        name="ffn_ln",
    )(x, w1, w3, w2, g, b, *row_casts)
    return out[0], out[1:]


INPROJ_TM = 512
Z_CQ = 0
Z_CKV = Z_CQ + Q_LORA
Z_KS = Z_CKV + KV_LORA
Z_WIDTH = Z_KS + SWA_KV_WIDTH
ZT_QS = 0
ZT_VS = ZT_QS + SWA_WIDTH
ZT_KR = ZT_VS + SWA_KV_WIDTH
ZT_ROWS = ZT_KR + V7X_LANES
HALF_ROPE = QK_ROPE // 2


def _rope_rows(x1, x2, cos, sin):
    return x1 * cos - x2 * sin, x1 * sin + x2 * cos


def _in_proj_kernel(h_ref, wz_ref, wt_ref, qg_ref, kvg_ref, wuqt_ref, wkn_ref, wvt_ref, pos_ref,
                    freq_ref, qt_ref, k_ref, vt_ref, qst_ref, ks_ref, vst_ref):
    tm = h_ref.shape[0]
    hb = h_ref[...].astype(BF16)
    z = _dot(hb, wz_ref[...])
    zt = _dot_nt(wt_ref[...], hb)

    ang = freq_ref[...] * pos_ref[...]
    cos = jnp.cos(ang)
    sin = jnp.sin(ang)

    cq = _rms_norm(z[:, Z_CQ:Z_CQ + Q_LORA], qg_ref[...]).astype(BF16)
    ckv = _rms_norm(z[:, Z_CKV:Z_CKV + KV_LORA], kvg_ref[...]).astype(BF16)
    qt = _dot_nt(wuqt_ref[...], cq)
    kn = _dot(ckv, wkn_ref[...])
    vt = _dot_nt(wvt_ref[...], ckv)

    kr1, kr2 = _rope_rows(zt[ZT_KR:ZT_KR + HALF_ROPE], zt[ZT_KR + HALF_ROPE:ZT_KR + QK_ROPE], cos, sin)
    k_rope = jnp.concatenate([kr1, kr2, zt[ZT_KR + QK_ROPE:ZT_KR + V7X_LANES]], axis=0).T.astype(BF16)

    for h in range(MLA_HEADS):
        base = h * MLA_QK_PAD
        r0 = base + QK_NOPE
        qt_ref[base:r0, :] = qt[base:r0].astype(BF16)
        q1, q2 = _rope_rows(qt[r0:r0 + HALF_ROPE], qt[r0 + HALF_ROPE:r0 + QK_ROPE], cos, sin)
        qt_ref[r0:r0 + HALF_ROPE, :] = q1.astype(BF16)
        qt_ref[r0 + HALF_ROPE:r0 + QK_ROPE, :] = q2.astype(BF16)
        qt_ref[r0 + QK_ROPE:base + MLA_QK_PAD, :] = qt[r0 + QK_ROPE:base + MLA_QK_PAD].astype(BF16)
        k_ref[:, base:r0] = kn[:, h * QK_NOPE:(h + 1) * QK_NOPE].astype(BF16)
        k_ref[:, r0:base + MLA_QK_PAD] = k_rope
        vt_ref[h, 0, 0:V_HEAD, :] = vt[h * V_HEAD:(h + 1) * V_HEAD].astype(BF16)
        vt_ref[h, 0, V_HEAD:MLA_V_ROWS, :] = jnp.ones((MLA_V_ROWS - V_HEAD, tm), BF16)

    qst_ref[...] = zt[ZT_QS:ZT_QS + SWA_WIDTH].reshape(SWA_HEADS, SWA_HEAD_DIM, tm).astype(BF16)
    for kv in range(SWA_KV_HEADS):
        lo = Z_KS + kv * SWA_HEAD_DIM
        ks_ref[kv] = z[:, lo:lo + SWA_HEAD_DIM].astype(BF16)
        lo = ZT_VS + kv * SWA_HEAD_DIM
        for c in range(tm // BLOCK):
            vst_ref[kv, c] = zt[lo:lo + SWA_HEAD_DIM, c * BLOCK:(c + 1) * BLOCK].astype(BF16)


def _in_proj(h, wz, wt, q_g, kv_g, wuq_t, wkn, wv_t, pos_row, freq_col):
    s, d = h.shape
    tm = INPROJ_TM
    assert tm == MLA_TK
    est = (2 * tm * d * 4 + 2 * d * (Z_WIDTH + ZT_ROWS) * 2
           + 2 * Q_LORA * (MLA_HEADS * MLA_QK_PAD + 2 * MLA_WIDTH) * 2
           + 2 * tm * (2 * MLA_HEADS * MLA_QK_PAD + MLA_WIDTH + SWA_WIDTH + 4 * V7X_LANES) * 2
           + 4 * tm * (Z_WIDTH + ZT_ROWS + 2 * MLA_HEADS * MLA_QK_PAD) * 4)
    const = lambda i: (0, 0)
    return pl.pallas_call(
        _in_proj_kernel,
        grid=(s // tm,),
        in_specs=[
            pl.BlockSpec((tm, d), lambda i: (i, 0)),
            pl.BlockSpec((d, Z_WIDTH), const),
            pl.BlockSpec((ZT_ROWS, d), const),
            pl.BlockSpec((1, Q_LORA), const),
            pl.BlockSpec((1, KV_LORA), const),
            pl.BlockSpec((MLA_HEADS * MLA_QK_PAD, Q_LORA), const),
            pl.BlockSpec((KV_LORA, MLA_WIDTH), const),
            pl.BlockSpec((MLA_WIDTH, KV_LORA), const),
            pl.BlockSpec((1, tm), lambda i: (0, i)),
            pl.BlockSpec((HALF_ROPE, 1), const),
        ],
        out_specs=[
            pl.BlockSpec((MLA_HEADS * MLA_QK_PAD, tm), lambda i: (0, i)),
            pl.BlockSpec((tm, MLA_HEADS * MLA_QK_PAD), lambda i: (i, 0)),
            pl.BlockSpec((MLA_HEADS, 1, MLA_V_ROWS, tm), lambda i: (0, i, 0, 0)),
            pl.BlockSpec((SWA_HEADS, SWA_HEAD_DIM, tm), lambda i: (0, 0, i)),
            pl.BlockSpec((SWA_KV_HEADS, tm, SWA_HEAD_DIM), lambda i: (0, i, 0)),
            pl.BlockSpec((SWA_KV_HEADS, tm // BLOCK, SWA_HEAD_DIM, BLOCK), lambda i: (0, i, 0, 0)),
        ],
        out_shape=[
            jax.ShapeDtypeStruct((MLA_HEADS * MLA_QK_PAD, s), BF16),
            jax.ShapeDtypeStruct((s, MLA_HEADS * MLA_QK_PAD), BF16),
            jax.ShapeDtypeStruct((MLA_HEADS, s // tm, MLA_V_ROWS, tm), BF16),
            jax.ShapeDtypeStruct((SWA_HEADS, SWA_HEAD_DIM, s), BF16),
            jax.ShapeDtypeStruct((SWA_KV_HEADS, s, SWA_HEAD_DIM), BF16),
            jax.ShapeDtypeStruct((SWA_KV_HEADS, s // BLOCK, SWA_HEAD_DIM, BLOCK), BF16),
        ],
        compiler_params=pltpu.CompilerParams(
            dimension_semantics=("parallel",),
            vmem_limit_bytes=_vmem_limit(est)),
        name="in_proj",
    )(h, wz, wt, q_g, kv_g, wuq_t, wkn, wv_t, pos_row, freq_col)


MLA_TQ = 512
MLA_TK = 512
MLA_HB = 2
MLA_SCALE = (QK_NOPE + QK_ROPE) ** -0.5
MLA_SCALE_LOG2E = MLA_SCALE * math.log2(math.e)


def _mla_flash_kernel(qt_ref, k_ref, vt_ref, o_ref, s_ref, smax_ref, p_ref, m_ref, corr_ref, acc_ref):
    i = pl.program_id(1)
    tk = MLA_TK
    heads = range(MLA_HB)

    def qk(t):
        start = pl.multiple_of(t * tk, tk)
        for h in heads:
            cols = slice(h * MLA_QK_PAD, (h + 1) * MLA_QK_PAD)
            s = _dot(k_ref[pl.ds(start, tk), cols], qt_ref[cols, :])
            s_ref[h] = s
            smax_ref[h] = jnp.max(s, axis=0, keepdims=True)

    def softmax(scores, masked):
        for h in heads:
            s, smax = scores[h]
            s = s * MLA_SCALE_LOG2E
            if masked:
                key = lax.broadcasted_iota(jnp.int32, s.shape, 0)
                qry = lax.broadcasted_iota(jnp.int32, s.shape, 1)
                s = jnp.where(key <= qry, s, NEG_BIG)
                tile_max = jnp.max(s, axis=0, keepdims=True)
            else:
                tile_max = smax * MLA_SCALE_LOG2E
            m_prev = m_ref[h]
            m_new = jnp.maximum(m_prev, tile_max)
            p_ref[h] = jnp.exp2(s - m_new).astype(BF16)
            corr_ref[h] = jnp.exp2(m_prev - m_new)
            m_ref[h] = m_new

    def pv(t):
        for h in heads:
            acc_ref[h] = corr_ref[h] * acc_ref[h] + _dot(vt_ref[h, t], p_ref[h])

    def load_scores():
        return [(s_ref[h], smax_ref[h]) for h in heads]

    m_ref[...] = jnp.full_like(m_ref, NEG_BIG)
    acc_ref[...] = jnp.zeros_like(acc_ref)

    @pl.when(i == 0)
    def _():
        p_ref[...] = jnp.zeros_like(p_ref)
        corr_ref[...] = jnp.zeros_like(corr_ref)

    qk(0)

    @pl.when(i >= 1)
    def _():
        scores = load_scores()
        qk(1)
        softmax(scores, masked=False)

    def body(t, carry):
        pv(t)
        scores = load_scores()
        qk(t + 2)
        softmax(scores, masked=False)
        return carry

    lax.fori_loop(0, i - 1, body, 0)
    pv(jnp.maximum(i - 1, 0))
    softmax(load_scores(), masked=True)
    pv(i)
    for h in heads:
        o_ref[:, h * V_HEAD:(h + 1) * V_HEAD] = (
            acc_ref[h, 0:V_HEAD, :] / acc_ref[h, V_HEAD:V_HEAD + 1, :]).T


def _mla_flash(qt, k, vt):
    s = k.shape[0]
    tq, tk, hb = MLA_TQ, MLA_TK, MLA_HB
    assert tq == tk
    est = (2 * hb * tq * MLA_QK_PAD * 2 + 2 * hb * s * MLA_QK_PAD * 2 + 2 * hb * s * MLA_V_ROWS * 2
           + 2 * hb * tq * V_HEAD * 4 + hb * tk * tq * (4 + 2) + hb * tq * MLA_V_ROWS * 4
           + 4 * hb * tq * tk * 4)
    return pl.pallas_call(
        _mla_flash_kernel,
        grid=(MLA_HEADS // hb, s // tq),
        in_specs=[
            pl.BlockSpec((hb * MLA_QK_PAD, tq), lambda g, i: (g, i)),
            pl.BlockSpec((s, hb * MLA_QK_PAD), lambda g, i: (0, g)),
            pl.BlockSpec((hb, s // tk, MLA_V_ROWS, tk), lambda g, i: (g, 0, 0, 0)),
        ],
        out_specs=pl.BlockSpec((tq, hb * V_HEAD), lambda g, i: (i, g)),
        out_shape=jax.ShapeDtypeStruct((s, MLA_WIDTH), F32),
        scratch_shapes=[pltpu.VMEM((hb, tk, tq), F32), pltpu.VMEM((hb, 1, tq), F32),
                        pltpu.VMEM((hb, tk, tq), BF16),
                        pltpu.VMEM((hb, 1, tq), F32), pltpu.VMEM((hb, 1, tq), F32),
                        pltpu.VMEM((hb, MLA_V_ROWS, tq), F32)],
        compiler_params=pltpu.CompilerParams(
            dimension_semantics=("parallel", "arbitrary"),
            vmem_limit_bytes=_vmem_limit(est)),
        name="mla_flash",
    )(qt, k, vt)


SWA_SCALE = SWA_HEAD_DIM ** -0.5
SWA_COLS = SWA_GROUP * BLOCK
SWA_NB = 4


def _t5_bucket_np(dist):
    n = np.maximum(dist, 0)
    max_exact = REL_BUCKETS // 2
    large = max_exact + (np.log(np.maximum(n, 1).astype(np.float32) / max_exact)
                         / math.log(REL_MAX_DIST / max_exact)
                         * (REL_BUCKETS - max_exact)).astype(np.int32)
    large = np.minimum(large, REL_BUCKETS - 1)
    return np.where(n < max_exact, n, large).astype(np.int32)


def _swa_bucket_table():
    j = np.arange(BLOCK)[:, None]
    i = np.arange(BLOCK)[None, :]
    dist = np.where(j <= i, i - j, BLOCK + i - j)
    return _t5_bucket_np(dist)


def _swa_kernel(rb_ref, bkt_ref, qt_ref, k_ref, vt_ref, sink_ref, o_ref, bias_ref):
    kv = pl.program_id(0)
    n = pl.program_id(1)

    @pl.when((kv == 0) & (n == 0))
    def _():
        bkt = bkt_ref[...]
        for h in range(SWA_HEADS):
            acc = jnp.zeros((BLOCK, BLOCK), F32)
            for b in range(REL_BUCKETS):
                acc = jnp.where(bkt == b, rb_ref[b, h], acc)
            g = h % SWA_GROUP
            bias_ref[h // SWA_GROUP, :, g * BLOCK:(g + 1) * BLOCK] = acc

    key = lax.broadcasted_iota(jnp.int32, (BLOCK, SWA_COLS), 0)
    qry = lax.broadcasted_iota(jnp.int32, (BLOCK, SWA_COLS), 1) & (BLOCK - 1)
    lower = key <= qry
    bias = bias_ref[kv]
    sink = sink_ref[...]

    for b in range(SWA_NB):
        blk = n * SWA_NB + b
        qt = jnp.concatenate([qt_ref[g, :, b * BLOCK:(b + 1) * BLOCK] for g in range(SWA_GROUP)],
                             axis=1)
        cur = pl.multiple_of(blk * BLOCK, BLOCK)
        prev_blk = jnp.maximum(blk - 1, 0)
        prev = pl.multiple_of(prev_blk * BLOCK, BLOCK)
        k_band = jnp.concatenate([k_ref[pl.ds(prev, BLOCK), :], k_ref[pl.ds(cur, BLOCK), :]], axis=0)
        s_band = _dot(k_band, qt)
        s = jnp.where(lower, s_band[BLOCK:], s_band[:BLOCK]) * SWA_SCALE + bias
        if b == 0:
            s = jnp.where(lower | (blk > 0), s, NEG_BIG)

        m = jnp.maximum(jnp.max(s, axis=0, keepdims=True), sink)
        e = jnp.exp(s - m)
        denom = jnp.sum(e, axis=0, keepdims=True) + jnp.exp(sink - m)
        p = e * (1.0 / denom)
        p_cur = jnp.where(lower, p, 0.0).astype(BF16)
        p_prev = jnp.where(lower, 0.0, p).astype(BF16)
        ot = _dot(vt_ref[blk], p_cur) + _dot(vt_ref[prev_blk], p_prev)
        o_ref[b * BLOCK:(b + 1) * BLOCK, :] = jnp.concatenate(
            [ot[:, g * BLOCK:(g + 1) * BLOCK] for g in range(SWA_GROUP)], axis=0).T


def _swa_attn(rel_bias, bucket_tbl, qst, ks, vst, sink_rows):
    s = ks.shape[1]
    rows = SWA_NB * BLOCK
    est = (2 * SWA_GROUP * SWA_HEAD_DIM * rows * 2 + 2 * s * V7X_LANES * 2 + 2 * s * SWA_HEAD_DIM * 2
           + 2 * rows * SWA_GROUP * SWA_HEAD_DIM * 4 + SWA_KV_HEADS * BLOCK * SWA_COLS * 4
           + 10 * SWA_NB * BLOCK * SWA_COLS * 4)
    return pl.pallas_call(
        _swa_kernel,
        grid=(SWA_KV_HEADS, s // rows),
        in_specs=[
            pl.BlockSpec(memory_space=pltpu.SMEM),
            pl.BlockSpec((BLOCK, BLOCK), lambda kv, n: (0, 0)),
            pl.BlockSpec((SWA_GROUP, SWA_HEAD_DIM, rows), lambda kv, n: (kv, 0, n)),
            pl.BlockSpec((None, s, SWA_HEAD_DIM), lambda kv, n: (kv, 0, 0)),
            pl.BlockSpec((None, s // BLOCK, SWA_HEAD_DIM, BLOCK), lambda kv, n: (kv, 0, 0, 0)),
            pl.BlockSpec((None, 1, SWA_COLS), lambda kv, n: (kv, 0, 0)),
        ],
        out_specs=pl.BlockSpec((rows, SWA_GROUP * SWA_HEAD_DIM), lambda kv, n: (n, kv)),
        out_shape=jax.ShapeDtypeStruct((s, SWA_WIDTH), F32),
        scratch_shapes=[pltpu.VMEM((SWA_KV_HEADS, BLOCK, SWA_COLS), F32)],
        compiler_params=pltpu.CompilerParams(
            dimension_semantics=("arbitrary", "arbitrary"),
            vmem_limit_bytes=_vmem_limit(est)),
        name="swa_attn",
    )(rel_bias, bucket_tbl, qst, ks, vst, sink_rows)


OUTPROJ_TM = 512
OUTPROJ_CHUNKS = 2


def _out_proj_kernel(am_ref, as_ref, h_ref, mg_ref, sg_ref, wm_ref, ws_ref, g_ref, b_ref, o_ref):
    chunk = o_ref.shape[0] // OUTPROJ_CHUNKS
    for c in range(OUTPROJ_CHUNKS):
        rows = slice(c * chunk, (c + 1) * chunk)
        nm = _rms_norm(am_ref[rows, :], mg_ref[...]).astype(BF16)
        ns = _rms_norm(as_ref[rows, :], sg_ref[...]).astype(BF16)
        mixed = _dot(nm, wm_ref[...]) + _dot(ns, ws_ref[...])
        o_ref[rows, :] = _layer_norm(ALPHA * h_ref[rows, :] + mixed, g_ref[...], b_ref[...])


def _out_proj(a_mla, a_swa, h, mla_g, swa_g, w_out, g, b):
    s, d = h.shape
    tm = OUTPROJ_TM
    est = (2 * tm * (MLA_WIDTH + SWA_WIDTH) * 4 + 4 * tm * d * 4 + 2 * (MLA_WIDTH + SWA_WIDTH) * d * 2
           + 4 * tm * d * 4)
    const = lambda i: (0, 0)
    return pl.pallas_call(
        _out_proj_kernel,
        grid=(s // tm,),
        in_specs=[
            pl.BlockSpec((tm, MLA_WIDTH), lambda i: (i, 0)),
            pl.BlockSpec((tm, SWA_WIDTH), lambda i: (i, 0)),
            pl.BlockSpec((tm, d), lambda i: (i, 0)),
            pl.BlockSpec((1, MLA_WIDTH), const),
            pl.BlockSpec((1, SWA_WIDTH), const),
            pl.BlockSpec((MLA_WIDTH, d), const),
            pl.BlockSpec((SWA_WIDTH, d), lambda i: (MLA_WIDTH // SWA_WIDTH, 0)),
            pl.BlockSpec((1, d), const),
            pl.BlockSpec((1, d), const),
        ],
        out_specs=pl.BlockSpec((tm, d), lambda i: (i, 0)),
        out_shape=jax.ShapeDtypeStruct((s, d), F32),
        compiler_params=pltpu.CompilerParams(
            dimension_semantics=("parallel",),
            vmem_limit_bytes=_vmem_limit(est)),
        name="out_proj",
    )(a_mla, a_swa, h, mla_g, swa_g, w_out, w_out, g, b)


PLE_TM = 512


def _ple_kernel(h_ref, p_ref, wg_ref, wp_ref, o_ref):
    h = h_ref[...]
    gate = _dot(h.astype(BF16), wg_ref[...])
    proj = _dot(p_ref[...].astype(BF16), wp_ref[...])
    o_ref[...] = h + (1.0 / (1.0 + jnp.exp(-gate))) * proj


def _ple(h, p, w_gate, w_proj):
    s, d = h.shape
    tm = PLE_TM
    est = 4 * tm * d * 4 + 2 * tm * PLE_DIM * 4 + 2 * (d + PLE_DIM) * d * 2 + 4 * tm * d * 4
    const = lambda i: (0, 0)
    return pl.pallas_call(
        _ple_kernel,
        grid=(s // tm,),
        in_specs=[
            pl.BlockSpec((tm, d), lambda i: (i, 0)),
            pl.BlockSpec((tm, PLE_DIM), lambda i: (i, 0)),
            pl.BlockSpec((d, d), const),
            pl.BlockSpec((PLE_DIM, d), const),
        ],
        out_specs=pl.BlockSpec((tm, d), lambda i: (i, 0)),
        out_shape=jax.ShapeDtypeStruct((s, d), F32),
        compiler_params=pltpu.CompilerParams(
            dimension_semantics=("parallel",),
            vmem_limit_bytes=_vmem_limit(est)),
        name="ple",
    )(h, p, w_gate, w_proj)


def _split_w_in(w_in):
    d = w_in.shape[0]
    b_kr = Q_LORA + KV_LORA
    b_qs = b_kr + QK_ROPE
    b_ks = b_qs + SWA_WIDTH
    b_vs = b_ks + SWA_KV_WIDTH
    wz = jnp.concatenate([w_in[:, :b_kr], w_in[:, b_ks:b_vs]], axis=1)
    pad = jnp.zeros((d, V7X_LANES - QK_ROPE), w_in.dtype)
    wt = jnp.concatenate([w_in[:, b_qs:b_ks], w_in[:, b_vs:], w_in[:, b_kr:b_qs], pad], axis=1).T
    return wz, wt


def _pack_w_uq_t(w_uq):
    r = w_uq.shape[0]
    w = w_uq.reshape(r, MLA_HEADS, QK_NOPE + QK_ROPE)
    w = jnp.pad(w, ((0, 0), (0, 0), (0, MLA_QK_PAD - QK_NOPE - QK_ROPE)))
    return w.reshape(r, MLA_HEADS * MLA_QK_PAD).T


def _rope_freq_col():
    freqs = ROPE_THETA ** (-jnp.arange(0, QK_ROPE, 2, dtype=F32) / QK_ROPE)
    return freqs.reshape(HALF_ROPE, 1)


def kernel(x, p, positions, rel_bias, ln1_g, ln1_b, ffn1_w1, ffn1_w3, ffn1_w2, w_in, q_norm_g, w_uq,
           kv_norm_g, w_ukv, swa_sinks, mla_out_g, swa_out_g, w_out, ln2_g, ln2_b, ffn2_w1, ffn2_w3,
           ffn2_w2, ln3_g, ln3_b, ple_w_gate, ple_w_proj):
    assert x.shape == (1, SEQ, D_MODEL) and DEPTH == 1
    row = lambda a: a.reshape(1, -1)
    h = x[0]
    pos_row = positions[0].astype(F32).reshape(1, SEQ)
    freq_col = _rope_freq_col()
    bucket_tbl = jnp.asarray(_swa_bucket_table())

    i = 0
    h, (w_out_b, w_gate_b, w_proj_b) = _ffn_ln(
        h, ffn1_w1[i], ffn1_w3[i], ffn1_w2[i], row(ln1_g[i]), row(ln1_b[i]),
        row_casts=(w_out[i], ple_w_gate[i], ple_w_proj[i]))

    w_ukv_h = w_ukv[i].reshape(KV_LORA, MLA_HEADS, QK_NOPE + V_HEAD)
    wkn = w_ukv_h[:, :, :QK_NOPE].reshape(KV_LORA, MLA_WIDTH).astype(BF16)
    wv_t = w_ukv_h[:, :, QK_NOPE:].reshape(KV_LORA, MLA_WIDTH).T.astype(BF16)
    wz, wt = _split_w_in(w_in[i])
    qt, k, vt, qst, ks, vst = _in_proj(
        h, wz.astype(BF16), wt.astype(BF16), row(q_norm_g[i]), row(kv_norm_g[i]),
        _pack_w_uq_t(w_uq[i]).astype(BF16), wkn, wv_t, pos_row, freq_col)

    a_mla = _mla_flash(qt, k, vt)
    sink_rows = jnp.repeat(swa_sinks[i], BLOCK).reshape(SWA_KV_HEADS, 1, SWA_COLS)
    a_swa = _swa_attn(rel_bias, bucket_tbl, qst, ks, vst, sink_rows)

    h = _out_proj(a_mla, a_swa, h, row(mla_out_g[i]), row(swa_out_g[i]), w_out_b,
                  row(ln2_g[i]), row(ln2_b[i]))

    h, _ = _ffn_ln(h, ffn2_w1[i], ffn2_w3[i], ffn2_w2[i], row(ln3_g[i]), row(ln3_b[i]))
    h = _ple(h, p[i, 0], w_gate_b, w_proj_b)
    return h[None]
```

```python
import functools
import math

import numpy as np
import jax
import jax.numpy as jnp
from jax import lax
from jax.experimental import pallas as pl
from jax.experimental.pallas import tpu as pltpu

F32 = jnp.float32
BF16 = jnp.bfloat16

D_MODEL = 2048
SEQ = 8192
DEPTH = 1
PLE_DIM = 256
MLA_HEADS = 8
Q_LORA = 512
KV_LORA = 512
QK_NOPE = 128
QK_ROPE = 64
V_HEAD = 128
ROPE_THETA = 10000.0
SWA_HEADS = 16
SWA_KV_HEADS = 2
SWA_GROUP = SWA_HEADS // SWA_KV_HEADS
SWA_HEAD_DIM = 64
WINDOW = 128
BLOCK = 128
REL_BUCKETS = 32
REL_MAX_DIST = 128
D_FF = 5632
ALPHA = (2.0 * DEPTH) ** 0.25
EPS = 1e-5
MLA_WIDTH = MLA_HEADS * V_HEAD
SWA_WIDTH = SWA_HEADS * SWA_HEAD_DIM
SWA_KV_WIDTH = SWA_KV_HEADS * SWA_HEAD_DIM

V7X_LANES = 128
V7X_VMEM_BYTES = 64 * 1024 * 1024
MIB = 1024 * 1024

V7X_BF16_SUBLANES = 16

MLA_QK_PAD = 2 * V7X_LANES
MLA_V_ROWS = V_HEAD + V7X_BF16_SUBLANES
NEG_BIG = float(np.finfo(np.float32).min)


def _vmem_limit(estimate_bytes):
    return int(min(estimate_bytes + 8 * MIB, V7X_VMEM_BYTES - 4 * MIB))


def _layer_norm(y, g, b):
    mu = jnp.mean(y, axis=-1, keepdims=True)
    yc = y - mu
    var = jnp.mean(yc * yc, axis=-1, keepdims=True)
    return yc * lax.rsqrt(var + EPS) * g + b


def _rms_norm(x, g):
    ms = jnp.mean(x * x, axis=-1, keepdims=True)
    return x * lax.rsqrt(ms + EPS) * g


def _dot(a, b):
    return jnp.dot(a, b, preferred_element_type=F32)


def _dot_nt(a, b):
    return lax.dot_general(a, b, (((1,), (1,)), ((), ())), preferred_element_type=F32)


FFN_TM = 1024
FFN_TF = 256
FFN_ROW_CHUNK = 512
FFN_LN_ROWS = 256
FFN_CAST_SPLIT = 8


def _ffn_ln_kernel(x_ref, w1_ref, w3_ref, w2_ref, g_ref, b_ref, *rest, n_casts):
    srcs, o_ref, dsts = rest[:n_casts], rest[n_casts], rest[n_casts + 1:2 * n_casts + 1]
    xb_ref, w1b_ref, w3b_ref, w2b_ref = rest[2 * n_casts + 1:]
    f = pl.program_id(1)

    last = pl.num_programs(1) - 1

    def chains(first, final, chunk):
        w1b_ref[...] = w1_ref[...].astype(BF16)
        w3b_ref[...] = w3_ref[...].astype(BF16)
        w2b_ref[...] = w2_ref[...].astype(BF16)
        for src, dst in zip(srcs, dsts):
            dst[...] = src[...].astype(BF16)
        for c in range(o_ref.shape[0] // chunk):
            rows = slice(c * chunk, (c + 1) * chunk)
            if first:
                xb = x_ref[rows, :].astype(BF16)
                xb_ref[rows, :] = xb
            else:
                xb = xb_ref[rows, :]
            gate = _dot(xb, w1b_ref[...])
            up = _dot(xb, w3b_ref[...])
            hidden = gate * (1.0 / (1.0 + jnp.exp(-gate))) * up
            down = _dot(hidden.astype(BF16), w2b_ref[...])
            acc = down if first else o_ref[rows, :] + down
            if final:
                acc = _layer_norm(ALPHA * x_ref[rows, :] + 0.5 * acc, g_ref[...], b_ref[...])
            o_ref[rows, :] = acc

    @pl.when(f == 0)
    def _():
        chains(first=True, final=False, chunk=FFN_ROW_CHUNK)

    @pl.when((f > 0) & (f < last))
    def _():
        chains(first=False, final=False, chunk=FFN_ROW_CHUNK)

    @pl.when(f == last)
    def _():
        chains(first=False, final=True, chunk=FFN_LN_ROWS)


def _row_cast_spec(a, nm, nf):
    r, c = a.shape
    split = FFN_CAST_SPLIT if r % (nm * FFN_CAST_SPLIT * V7X_BF16_SUBLANES) == 0 else 1
    assert split <= nf and r % (nm * split * V7X_BF16_SUBLANES) == 0
    return pl.BlockSpec((r // (nm * split), c), lambda i, f: (i * split + jnp.minimum(f, split - 1), 0))


def _ffn_ln(x, w1, w3, w2, g, b, row_casts=()):
    s, d = x.shape
    dff = w1.shape[1]
    tm, tf = FFN_TM, FFN_TF
    nm, nf = s // tm, dff // tf
    cast_specs = [_row_cast_spec(a, nm, nf) for a in row_casts]
    cast_bytes = sum(2 * (4 + 2) * math.prod(spec.block_shape) for spec in cast_specs)
    est = (2 * tm * d * 4 + tm * d * 2 + 2 * 3 * d * tf * 4 + 3 * d * tf * 2 + 2 * tm * d * 4
           + 3 * FFN_ROW_CHUNK * tf * 4 + cast_bytes)
    out = pl.pallas_call(
        functools.partial(_ffn_ln_kernel, n_casts=len(row_casts)),
        grid=(nm, nf),
        in_specs=[
            pl.BlockSpec((tm, d), lambda i, f: (i, 0)),
            pl.BlockSpec((d, tf), lambda i, f: (0, f)),
            pl.BlockSpec((d, tf), lambda i, f: (0, f)),
            pl.BlockSpec((tf, d), lambda i, f: (f, 0)),
            pl.BlockSpec((1, d), lambda i, f: (0, 0)),
            pl.BlockSpec((1, d), lambda i, f: (0, 0)),
        ] + cast_specs,
        out_specs=[pl.BlockSpec((tm, d), lambda i, f: (i, 0))] + cast_specs,
        out_shape=[jax.ShapeDtypeStruct((s, d), F32)] + [jax.ShapeDtypeStruct(a.shape, BF16) for a in row_casts],
        scratch_shapes=[pltpu.VMEM((tm, d), BF16), pltpu.VMEM((d, tf), BF16), pltpu.VMEM((d, tf), BF16),
                        pltpu.VMEM((tf, d), BF16)],
        compiler_params=pltpu.CompilerParams(
            dimension_semantics=("arbitrary", "arbitrary"),
            vmem_limit_bytes=_vmem_limit(est)),
        name="ffn_ln",
    )(x, w1, w3, w2, g, b, *row_casts)
    return out[0], out[1:]


INPROJ_TM = 512
Z_CQ = 0
Z_CKV = Z_CQ + Q_LORA
Z_KS = Z_CKV + KV_LORA
Z_WIDTH = Z_KS + SWA_KV_WIDTH
ZT_QS = 0
ZT_VS = ZT_QS + SWA_WIDTH
ZT_KR = ZT_VS + SWA_KV_WIDTH
ZT_ROWS = ZT_KR + V7X_LANES
HALF_ROPE = QK_ROPE // 2


def _rope_rows(x1, x2, cos, sin):
    return x1 * cos - x2 * sin, x1 * sin + x2 * cos


def _in_proj_kernel(h_ref, wz_ref, wt_ref, qg_ref, kvg_ref, wuqt_ref, wkn_ref, wvt_ref, pos_ref,
                    freq_ref, qt_ref, k_ref, vt_ref, qst_ref, ks_ref, vst_ref):
    tm = h_ref.shape[0]
    hb = h_ref[...].astype(BF16)
    z = _dot(hb, wz_ref[...])
    zt = _dot_nt(wt_ref[...], hb)

    ang = freq_ref[...] * pos_ref[...]
    cos = jnp.cos(ang)
    sin = jnp.sin(ang)

    cq = _rms_norm(z[:, Z_CQ:Z_CQ + Q_LORA], qg_ref[...]).astype(BF16)
    ckv = _rms_norm(z[:, Z_CKV:Z_CKV + KV_LORA], kvg_ref[...]).astype(BF16)
    qt = _dot_nt(wuqt_ref[...], cq)
    kn = _dot(ckv, wkn_ref[...])
    vt = _dot_nt(wvt_ref[...], ckv)

    kr1, kr2 = _rope_rows(zt[ZT_KR:ZT_KR + HALF_ROPE], zt[ZT_KR + HALF_ROPE:ZT_KR + QK_ROPE], cos, sin)
    k_rope = jnp.concatenate([kr1, kr2, zt[ZT_KR + QK_ROPE:ZT_KR + V7X_LANES]], axis=0).T.astype(BF16)

    for h in range(MLA_HEADS):
        base = h * MLA_QK_PAD
        r0 = base + QK_NOPE
        qt_ref[base:r0, :] = qt[base:r0].astype(BF16)
        q1, q2 = _rope_rows(qt[r0:r0 + HALF_ROPE], qt[r0 + HALF_ROPE:r0 + QK_ROPE], cos, sin)
        qt_ref[r0:r0 + HALF_ROPE, :] = q1.astype(BF16)
        qt_ref[r0 + HALF_ROPE:r0 + QK_ROPE, :] = q2.astype(BF16)
        qt_ref[r0 + QK_ROPE:base + MLA_QK_PAD, :] = qt[r0 + QK_ROPE:base + MLA_QK_PAD].astype(BF16)
        k_ref[:, base:r0] = kn[:, h * QK_NOPE:(h + 1) * QK_NOPE].astype(BF16)
        k_ref[:, r0:base + MLA_QK_PAD] = k_rope
        vt_ref[h, 0, 0:V_HEAD, :] = vt[h * V_HEAD:(h + 1) * V_HEAD].astype(BF16)
        vt_ref[h, 0, V_HEAD:MLA_V_ROWS, :] = jnp.ones((MLA_V_ROWS - V_HEAD, tm), BF16)

    qst_ref[...] = zt[ZT_QS:ZT_QS + SWA_WIDTH].reshape(SWA_HEADS, SWA_HEAD_DIM, tm).astype(BF16)
    for kv in range(SWA_KV_HEADS):
        lo = Z_KS + kv * SWA_HEAD_DIM
        ks_ref[kv] = z[:, lo:lo + SWA_HEAD_DIM].astype(BF16)
        lo = ZT_VS + kv * SWA_HEAD_DIM
        for c in range(tm // BLOCK):
            vst_ref[kv, c] = zt[lo:lo + SWA_HEAD_DIM, c * BLOCK:(c + 1) * BLOCK].astype(BF16)


def _in_proj(h, wz, wt, q_g, kv_g, wuq_t, wkn, wv_t, pos_row, freq_col):
    s, d = h.shape
    tm = INPROJ_TM
    assert tm == MLA_TK
    est = (2 * tm * d * 4 + 2 * d * (Z_WIDTH + ZT_ROWS) * 2
           + 2 * Q_LORA * (MLA_HEADS * MLA_QK_PAD + 2 * MLA_WIDTH) * 2
           + 2 * tm * (2 * MLA_HEADS * MLA_QK_PAD + MLA_WIDTH + SWA_WIDTH + 4 * V7X_LANES) * 2
           + 4 * tm * (Z_WIDTH + ZT_ROWS + 2 * MLA_HEADS * MLA_QK_PAD) * 4)
    const = lambda i: (0, 0)
    return pl.pallas_call(
        _in_proj_kernel,
        grid=(s // tm,),
        in_specs=[
            pl.BlockSpec((tm, d), lambda i: (i, 0)),
            pl.BlockSpec((d, Z_WIDTH), const),
            pl.BlockSpec((ZT_ROWS, d), const),
            pl.BlockSpec((1, Q_LORA), const),
            pl.BlockSpec((1, KV_LORA), const),
            pl.BlockSpec((MLA_HEADS * MLA_QK_PAD, Q_LORA), const),
            pl.BlockSpec((KV_LORA, MLA_WIDTH), const),
            pl.BlockSpec((MLA_WIDTH, KV_LORA), const),
            pl.BlockSpec((1, tm), lambda i: (0, i)),
            pl.BlockSpec((HALF_ROPE, 1), const),
        ],
        out_specs=[
            pl.BlockSpec((MLA_HEADS * MLA_QK_PAD, tm), lambda i: (0, i)),
            pl.BlockSpec((tm, MLA_HEADS * MLA_QK_PAD), lambda i: (i, 0)),
            pl.BlockSpec((MLA_HEADS, 1, MLA_V_ROWS, tm), lambda i: (0, i, 0, 0)),
            pl.BlockSpec((SWA_HEADS, SWA_HEAD_DIM, tm), lambda i: (0, 0, i)),
            pl.BlockSpec((SWA_KV_HEADS, tm, SWA_HEAD_DIM), lambda i: (0, i, 0)),
            pl.BlockSpec((SWA_KV_HEADS, tm // BLOCK, SWA_HEAD_DIM, BLOCK), lambda i: (0, i, 0, 0)),
        ],
        out_shape=[
            jax.ShapeDtypeStruct((MLA_HEADS * MLA_QK_PAD, s), BF16),
            jax.ShapeDtypeStruct((s, MLA_HEADS * MLA_QK_PAD), BF16),
            jax.ShapeDtypeStruct((MLA_HEADS, s // tm, MLA_V_ROWS, tm), BF16),
            jax.ShapeDtypeStruct((SWA_HEADS, SWA_HEAD_DIM, s), BF16),
            jax.ShapeDtypeStruct((SWA_KV_HEADS, s, SWA_HEAD_DIM), BF16),
            jax.ShapeDtypeStruct((SWA_KV_HEADS, s // BLOCK, SWA_HEAD_DIM, BLOCK), BF16),
        ],
        compiler_params=pltpu.CompilerParams(
            dimension_semantics=("parallel",),
            vmem_limit_bytes=_vmem_limit(est)),
        name="in_proj",
    )(h, wz, wt, q_g, kv_g, wuq_t, wkn, wv_t, pos_row, freq_col)


MLA_TQ = 512
MLA_TK = 512
MLA_HB = 2
MLA_SCALE = (QK_NOPE + QK_ROPE) ** -0.5
MLA_SCALE_LOG2E = MLA_SCALE * math.log2(math.e)


def _mla_flash_kernel(qt_ref, k_ref, vt_ref, o_ref, s_ref, smax_ref, p_ref, m_ref, corr_ref, acc_ref):
    i = pl.program_id(1)
    tk = MLA_TK
    heads = range(MLA_HB)

    def qk(t):
        start = pl.multiple_of(t * tk, tk)
        for h in heads:
            cols = slice(h * MLA_QK_PAD, (h + 1) * MLA_QK_PAD)
            s = _dot(k_ref[pl.ds(start, tk), cols], qt_ref[cols, :])
            s_ref[h] = s
            smax_ref[h] = jnp.max(s, axis=0, keepdims=True)

    def softmax(scores, masked):
        for h in heads:
            s, smax = scores[h]
            s = s * MLA_SCALE_LOG2E
            if masked:
                key = lax.broadcasted_iota(jnp.int32, s.shape, 0)
                qry = lax.broadcasted_iota(jnp.int32, s.shape, 1)
                s = jnp.where(key <= qry, s, NEG_BIG)
                tile_max = jnp.max(s, axis=0, keepdims=True)
            else:
                tile_max = smax * MLA_SCALE_LOG2E
            m_prev = m_ref[h]
            m_new = jnp.maximum(m_prev, tile_max)
            p_ref[h] = jnp.exp2(s - m_new).astype(BF16)
            corr_ref[h] = jnp.exp2(m_prev - m_new)
            m_ref[h] = m_new

    def pv(t):
        for h in heads:
            acc_ref[h] = corr_ref[h] * acc_ref[h] + _dot(vt_ref[h, t], p_ref[h])

    def load_scores():
        return [(s_ref[h], smax_ref[h]) for h in heads]

    m_ref[...] = jnp.full_like(m_ref, NEG_BIG)
    acc_ref[...] = jnp.zeros_like(acc_ref)

    @pl.when(i == 0)
    def _():
        p_ref[...] = jnp.zeros_like(p_ref)
        corr_ref[...] = jnp.zeros_like(corr_ref)

    qk(0)

    @pl.when(i >= 1)
    def _():
        scores = load_scores()
        qk(1)
        softmax(scores, masked=False)

    def body(t, carry):
        pv(t)
        scores = load_scores()
        qk(t + 2)
        softmax(scores, masked=False)
        return carry

    lax.fori_loop(0, i - 1, body, 0)
    pv(jnp.maximum(i - 1, 0))
    softmax(load_scores(), masked=True)
    pv(i)
    for h in heads:
        o_ref[:, h * V_HEAD:(h + 1) * V_HEAD] = (
            acc_ref[h, 0:V_HEAD, :] / acc_ref[h, V_HEAD:V_HEAD + 1, :]).T


def _mla_flash(qt, k, vt):
    s = k.shape[0]
    tq, tk, hb = MLA_TQ, MLA_TK, MLA_HB
    assert tq == tk
    est = (2 * hb * tq * MLA_QK_PAD * 2 + 2 * hb * s * MLA_QK_PAD * 2 + 2 * hb * s * MLA_V_ROWS * 2
           + 2 * hb * tq * V_HEAD * 4 + hb * tk * tq * (4 + 2) + hb * tq * MLA_V_ROWS * 4
           + 4 * hb * tq * tk * 4)
    return pl.pallas_call(
        _mla_flash_kernel,
        grid=(MLA_HEADS // hb, s // tq),
        in_specs=[
            pl.BlockSpec((hb * MLA_QK_PAD, tq), lambda g, i: (g, i)),
            pl.BlockSpec((s, hb * MLA_QK_PAD), lambda g, i: (0, g)),
            pl.BlockSpec((hb, s // tk, MLA_V_ROWS, tk), lambda g, i: (g, 0, 0, 0)),
        ],
        out_specs=pl.BlockSpec((tq, hb * V_HEAD), lambda g, i: (i, g)),
        out_shape=jax.ShapeDtypeStruct((s, MLA_WIDTH), F32),
        scratch_shapes=[pltpu.VMEM((hb, tk, tq), F32), pltpu.VMEM((hb, 1, tq), F32),
                        pltpu.VMEM((hb, tk, tq), BF16),
                        pltpu.VMEM((hb, 1, tq), F32), pltpu.VMEM((hb, 1, tq), F32),
                        pltpu.VMEM((hb, MLA_V_ROWS, tq), F32)],
        compiler_params=pltpu.CompilerParams(
            dimension_semantics=("parallel", "arbitrary"),
            vmem_limit_bytes=_vmem_limit(est)),
        name="mla_flash",
    )(qt, k, vt)


SWA_SCALE = SWA_HEAD_DIM ** -0.5
SWA_COLS = SWA_GROUP * BLOCK
SWA_NB = 4


def _t5_bucket_np(dist):
    n = np.maximum(dist, 0)
    max_exact = REL_BUCKETS // 2
    large = max_exact + (np.log(np.maximum(n, 1).astype(np.float32) / max_exact)
                         / math.log(REL_MAX_DIST / max_exact)
                         * (REL_BUCKETS - max_exact)).astype(np.int32)
    large = np.minimum(large, REL_BUCKETS - 1)
    return np.where(n < max_exact, n, large).astype(np.int32)


def _swa_bucket_table():
    j = np.arange(BLOCK)[:, None]
    i = np.arange(BLOCK)[None, :]
    dist = np.where(j <= i, i - j, BLOCK + i - j)
    return _t5_bucket_np(dist)


def _swa_kernel(rb_ref, bkt_ref, qt_ref, k_ref, vt_ref, sink_ref, o_ref, bias_ref):
    kv = pl.program_id(0)
    n = pl.program_id(1)

    @pl.when((kv == 0) & (n == 0))
    def _():
        bkt = bkt_ref[...]
        for h in range(SWA_HEADS):
            acc = jnp.zeros((BLOCK, BLOCK), F32)
            for b in range(REL_BUCKETS):
                acc = jnp.where(bkt == b, rb_ref[b, h], acc)
            g = h % SWA_GROUP
            bias_ref[h // SWA_GROUP, :, g * BLOCK:(g + 1) * BLOCK] = acc

    key = lax.broadcasted_iota(jnp.int32, (BLOCK, SWA_COLS), 0)
    qry = lax.broadcasted_iota(jnp.int32, (BLOCK, SWA_COLS), 1) & (BLOCK - 1)
    lower = key <= qry
    bias = bias_ref[kv]
    sink = sink_ref[...]

    for b in range(SWA_NB):
        blk = n * SWA_NB + b
        qt = jnp.concatenate([qt_ref[g, :, b * BLOCK:(b + 1) * BLOCK] for g in range(SWA_GROUP)],
                             axis=1)
        cur = pl.multiple_of(blk * BLOCK, BLOCK)
        prev_blk = jnp.maximum(blk - 1, 0)
        prev = pl.multiple_of(prev_blk * BLOCK, BLOCK)
        k_band = jnp.concatenate([k_ref[pl.ds(prev, BLOCK), :], k_ref[pl.ds(cur, BLOCK), :]], axis=0)
        s_band = _dot(k_band, qt)
        s = jnp.where(lower, s_band[BLOCK:], s_band[:BLOCK]) * SWA_SCALE + bias
        if b == 0:
            s = jnp.where(lower | (blk > 0), s, NEG_BIG)

        m = jnp.maximum(jnp.max(s, axis=0, keepdims=True), sink)
        e = jnp.exp(s - m)
        denom = jnp.sum(e, axis=0, keepdims=True) + jnp.exp(sink - m)
        p = e * (1.0 / denom)
        p_cur = jnp.where(lower, p, 0.0).astype(BF16)
        p_prev = jnp.where(lower, 0.0, p).astype(BF16)
        ot = _dot(vt_ref[blk], p_cur) + _dot(vt_ref[prev_blk], p_prev)
        o_ref[b * BLOCK:(b + 1) * BLOCK, :] = jnp.concatenate(
            [ot[:, g * BLOCK:(g + 1) * BLOCK] for g in range(SWA_GROUP)], axis=0).T


def _swa_attn(rel_bias, bucket_tbl, qst, ks, vst, sink_rows):
    s = ks.shape[1]
    rows = SWA_NB * BLOCK
    est = (2 * SWA_GROUP * SWA_HEAD_DIM * rows * 2 + 2 * s * V7X_LANES * 2 + 2 * s * SWA_HEAD_DIM * 2
           + 2 * rows * SWA_GROUP * SWA_HEAD_DIM * 4 + SWA_KV_HEADS * BLOCK * SWA_COLS * 4
           + 10 * SWA_NB * BLOCK * SWA_COLS * 4)
    return pl.pallas_call(
        _swa_kernel,
        grid=(SWA_KV_HEADS, s // rows),
        in_specs=[
            pl.BlockSpec(memory_space=pltpu.SMEM),
            pl.BlockSpec((BLOCK, BLOCK), lambda kv, n: (0, 0)),
            pl.BlockSpec((SWA_GROUP, SWA_HEAD_DIM, rows), lambda kv, n: (kv, 0, n)),
            pl.BlockSpec((None, s, SWA_HEAD_DIM), lambda kv, n: (kv, 0, 0)),
            pl.BlockSpec((None, s // BLOCK, SWA_HEAD_DIM, BLOCK), lambda kv, n: (kv, 0, 0, 0)),
            pl.BlockSpec((None, 1, SWA_COLS), lambda kv, n: (kv, 0, 0)),
        ],
        out_specs=pl.BlockSpec((rows, SWA_GROUP * SWA_HEAD_DIM), lambda kv, n: (n, kv)),
        out_shape=jax.ShapeDtypeStruct((s, SWA_WIDTH), F32),
        scratch_shapes=[pltpu.VMEM((SWA_KV_HEADS, BLOCK, SWA_COLS), F32)],
        compiler_params=pltpu.CompilerParams(
            dimension_semantics=("arbitrary", "arbitrary"),
            vmem_limit_bytes=_vmem_limit(est)),
        name="swa_attn",
    )(rel_bias, bucket_tbl, qst, ks, vst, sink_rows)


OUTPROJ_TM = 512
OUTPROJ_CHUNKS = 2


def _out_proj_kernel(am_ref, as_ref, h_ref, mg_ref, sg_ref, wm_ref, ws_ref, g_ref, b_ref, o_ref):
    chunk = o_ref.shape[0] // OUTPROJ_CHUNKS
    for c in range(OUTPROJ_CHUNKS):
        rows = slice(c * chunk, (c + 1) * chunk)
        nm = _rms_norm(am_ref[rows, :], mg_ref[...]).astype(BF16)
        ns = _rms_norm(as_ref[rows, :], sg_ref[...]).astype(BF16)
        mixed = _dot(nm, wm_ref[...]) + _dot(ns, ws_ref[...])
        o_ref[rows, :] = _layer_norm(ALPHA * h_ref[rows, :] + mixed, g_ref[...], b_ref[...])


def _out_proj(a_mla, a_swa, h, mla_g, swa_g, w_out, g, b):
    s, d = h.shape
    tm = OUTPROJ_TM
    est = (2 * tm * (MLA_WIDTH + SWA_WIDTH) * 4 + 4 * tm * d * 4 + 2 * (MLA_WIDTH + SWA_WIDTH) * d * 2
           + 4 * tm * d * 4)
    const = lambda i: (0, 0)
    return pl.pallas_call(
        _out_proj_kernel,
        grid=(s // tm,),
        in_specs=[
            pl.BlockSpec((tm, MLA_WIDTH), lambda i: (i, 0)),
            pl.BlockSpec((tm, SWA_WIDTH), lambda i: (i, 0)),
            pl.BlockSpec((tm, d), lambda i: (i, 0)),
            pl.BlockSpec((1, MLA_WIDTH), const),
            pl.BlockSpec((1, SWA_WIDTH), const),
            pl.BlockSpec((MLA_WIDTH, d), const),
            pl.BlockSpec((SWA_WIDTH, d), lambda i: (MLA_WIDTH // SWA_WIDTH, 0)),
            pl.BlockSpec((1, d), const),
            pl.BlockSpec((1, d), const),
        ],
        out_specs=pl.BlockSpec((tm, d), lambda i: (i, 0)),
        out_shape=jax.ShapeDtypeStruct((s, d), F32),
        compiler_params=pltpu.CompilerParams(
            dimension_semantics=("parallel",),
            vmem_limit_bytes=_vmem_limit(est)),
        name="out_proj",
    )(a_mla, a_swa, h, mla_g, swa_g, w_out, w_out, g, b)


PLE_TM = 512


def _ple_kernel(h_ref, p_ref, wg_ref, wp_ref, o_ref):
    h = h_ref[...]
    gate = _dot(h.astype(BF16), wg_ref[...])
    proj = _dot(p_ref[...].astype(BF16), wp_ref[...])
    o_ref[...] = h + (1.0 / (1.0 + jnp.exp(-gate))) * proj


def _ple(h, p, w_gate, w_proj):
    s, d = h.shape
    tm = PLE_TM
    est = 4 * tm * d * 4 + 2 * tm * PLE_DIM * 4 + 2 * (d + PLE_DIM) * d * 2 + 4 * tm * d * 4
    const = lambda i: (0, 0)
    return pl.pallas_call(
        _ple_kernel,
        grid=(s // tm,),
        in_specs=[
            pl.BlockSpec((tm, d), lambda i: (i, 0)),
            pl.BlockSpec((tm, PLE_DIM), lambda i: (i, 0)),
            pl.BlockSpec((d, d), const),
            pl.BlockSpec((PLE_DIM, d), const),
        ],
        out_specs=pl.BlockSpec((tm, d), lambda i: (i, 0)),
        out_shape=jax.ShapeDtypeStruct((s, d), F32),
        compiler_params=pltpu.CompilerParams(
            dimension_semantics=("parallel",),
            vmem_limit_bytes=_vmem_limit(est)),
        name="ple",
    )(h, p, w_gate, w_proj)


def _split_w_in(w_in):
    d = w_in.shape[0]
    b_kr = Q_LORA + KV_LORA
    b_qs = b_kr + QK_ROPE
    b_ks = b_qs + SWA_WIDTH
    b_vs = b_ks + SWA_KV_WIDTH
    wz = jnp.concatenate([w_in[:, :b_kr], w_in[:, b_ks:b_vs]], axis=1)
    pad = jnp.zeros((d, V7X_LANES - QK_ROPE), w_in.dtype)
    wt = jnp.concatenate([w_in[:, b_qs:b_ks], w_in[:, b_vs:], w_in[:, b_kr:b_qs], pad], axis=1).T
    return wz, wt


def _pack_w_uq_t(w_uq):
    r = w_uq.shape[0]
    w = w_uq.reshape(r, MLA_HEADS, QK_NOPE + QK_ROPE)
    w = jnp.pad(w, ((0, 0), (0, 0), (0, MLA_QK_PAD - QK_NOPE - QK_ROPE)))
    return w.reshape(r, MLA_HEADS * MLA_QK_PAD).T


def _rope_freq_col():
    freqs = ROPE_THETA ** (-jnp.arange(0, QK_ROPE, 2, dtype=F32) / QK_ROPE)
    return freqs.reshape(HALF_ROPE, 1)


def kernel(x, p, positions, rel_bias, ln1_g, ln1_b, ffn1_w1, ffn1_w3, ffn1_w2, w_in, q_norm_g, w_uq,
           kv_norm_g, w_ukv, swa_sinks, mla_out_g, swa_out_g, w_out, ln2_g, ln2_b, ffn2_w1, ffn2_w3,
           ffn2_w2, ln3_g, ln3_b, ple_w_gate, ple_w_proj):
    assert x.shape == (1, SEQ, D_MODEL) and DEPTH == 1
    row = lambda a: a.reshape(1, -1)
    h = x[0]
    pos_row = positions[0].astype(F32).reshape(1, SEQ)
    freq_col = _rope_freq_col()
    bucket_tbl = jnp.asarray(_swa_bucket_table())

    i = 0
    h, (w_out_b, w_gate_b, w_proj_b) = _ffn_ln(
        h, ffn1_w1[i], ffn1_w3[i], ffn1_w2[i], row(ln1_g[i]), row(ln1_b[i]),
        row_casts=(w_out[i], ple_w_gate[i], ple_w_proj[i]))

    w_ukv_h = w_ukv[i].reshape(KV_LORA, MLA_HEADS, QK_NOPE + V_HEAD)
    wkn = w_ukv_h[:, :, :QK_NOPE].reshape(KV_LORA, MLA_WIDTH).astype(BF16)
    wv_t = w_ukv_h[:, :, QK_NOPE:].reshape(KV_LORA, MLA_WIDTH).T.astype(BF16)
    wz, wt = _split_w_in(w_in[i])
    qt, k, vt, qst, ks, vst = _in_proj(
        h, wz.astype(BF16), wt.astype(BF16), row(q_norm_g[i]), row(kv_norm_g[i]),
        _pack_w_uq_t(w_uq[i]).astype(BF16), wkn, wv_t, pos_row, freq_col)

    a_mla = _mla_flash(qt, k, vt)
    sink_rows = jnp.repeat(swa_sinks[i], BLOCK).reshape(SWA_KV_HEADS, 1, SWA_COLS)
    a_swa = _swa_attn(rel_bias, bucket_tbl, qst, ks, vst, sink_rows)

    h = _out_proj(a_mla, a_swa, h, row(mla_out_g[i]), row(swa_out_g[i]), w_out_b,
                  row(ln2_g[i]), row(ln2_b[i]))

    h, _ = _ffn_ln(h, ffn2_w1[i], ffn2_w3[i], ffn2_w2[i], row(ln3_g[i]), row(ln3_b[i]))
    h = _ple(h, p[i, 0], w_gate_b, w_proj_b)
    return h[None]
```

```python
import functools
import math

import numpy as np
import jax
import jax.numpy as jnp
from jax import lax
from jax.experimental import pallas as pl
from jax.experimental.pallas import tpu as pltpu

F32 = jnp.float32
BF16 = jnp.bfloat16

D_MODEL = 2048
SEQ = 8192
DEPTH = 1
PLE_DIM = 256
MLA_HEADS = 8
Q_LORA = 512
KV_LORA = 512
QK_NOPE = 128
QK_ROPE = 64
V_HEAD = 128
ROPE_THETA = 10000.0
SWA_HEADS = 16
SWA_KV_HEADS = 2
SWA_GROUP = SWA_HEADS // SWA_KV_HEADS
SWA_HEAD_DIM = 64
WINDOW = 128
BLOCK = 128
REL_BUCKETS = 32
REL_MAX_DIST = 128
D_FF = 5632
ALPHA = (2.0 * DEPTH) ** 0.25
EPS = 1e-5
MLA_WIDTH = MLA_HEADS * V_HEAD
SWA_WIDTH = SWA_HEADS * SWA_HEAD_DIM
SWA_KV_WIDTH = SWA_KV_HEADS * SWA_HEAD_DIM

V7X_LANES = 128
V7X_VMEM_BYTES = 64 * 1024 * 1024
MIB = 1024 * 1024

V7X_BF16_SUBLANES = 16

MLA_QK_PAD = 2 * V7X_LANES
MLA_V_ROWS = V_HEAD + V7X_BF16_SUBLANES
NEG_BIG = float(np.finfo(np.float32).min)
LOG2E = math.log2(math.e)


def _vmem_limit(estimate_bytes):
    return int(min(estimate_bytes + 8 * MIB, V7X_VMEM_BYTES - 4 * MIB))


def _layer_norm(y, g, b):
    mu = jnp.mean(y, axis=-1, keepdims=True)
    yc = y - mu
    var = jnp.mean(yc * yc, axis=-1, keepdims=True)
    return yc * lax.rsqrt(var + EPS) * g + b


def _rms_norm(x, g):
    ms = jnp.mean(x * x, axis=-1, keepdims=True)
    return x * lax.rsqrt(ms + EPS) * g


def _dot(a, b):
    return jnp.dot(a, b, preferred_element_type=F32)


def _dot_nt(a, b):
    return lax.dot_general(a, b, (((1,), (1,)), ((), ())), preferred_element_type=F32)


FFN_TM = 1024
FFN_TF = 256
FFN_ROW_CHUNK = 512
FFN_LN_ROWS = 256
FFN_CAST_SPLIT = 8


def _ffn_ln_kernel(x_ref, w1_ref, w3_ref, w2_ref, g_ref, b_ref, *rest, n_casts):
    srcs, o_ref, dsts = rest[:n_casts], rest[n_casts], rest[n_casts + 1:2 * n_casts + 1]
    xb_ref, w1b_ref, w3b_ref, w2b_ref = rest[2 * n_casts + 1:]
    f = pl.program_id(1)

    @pl.when(f == 0)
    def _():
        xb_ref[...] = x_ref[...].astype(BF16)
        o_ref[...] = jnp.zeros_like(o_ref)

    w1b_ref[...] = w1_ref[...].astype(BF16)
    w3b_ref[...] = w3_ref[...].astype(BF16)
    w2b_ref[...] = w2_ref[...].astype(BF16)
    for src, dst in zip(srcs, dsts):
        dst[...] = src[...].astype(BF16)

    for c in range(o_ref.shape[0] // FFN_ROW_CHUNK):
        rows = slice(c * FFN_ROW_CHUNK, (c + 1) * FFN_ROW_CHUNK)
        xb = xb_ref[rows, :]
        gate = _dot(xb, w1b_ref[...])
        up = _dot(xb, w3b_ref[...])
        hidden = gate * (1.0 / (1.0 + jnp.exp(-gate))) * up
        o_ref[rows, :] += _dot(hidden.astype(BF16), w2b_ref[...])

    @pl.when(f == pl.num_programs(1) - 1)
    def _():
        for c in range(o_ref.shape[0] // FFN_LN_ROWS):
            rows = slice(c * FFN_LN_ROWS, (c + 1) * FFN_LN_ROWS)
            y = ALPHA * x_ref[rows, :] + 0.5 * o_ref[rows, :]
            o_ref[rows, :] = _layer_norm(y, g_ref[...], b_ref[...])


def _row_cast_spec(a, nm, nf):
    r, c = a.shape
    split = FFN_CAST_SPLIT if r % (nm * FFN_CAST_SPLIT * V7X_BF16_SUBLANES) == 0 else 1
    assert split <= nf and r % (nm * split * V7X_BF16_SUBLANES) == 0
    return pl.BlockSpec((r // (nm * split), c), lambda i, f: (i * split + jnp.minimum(f, split - 1), 0))


def _ffn_ln(x, w1, w3, w2, g, b, row_casts=()):
    s, d = x.shape
    dff = w1.shape[1]
    tm, tf = FFN_TM, FFN_TF
    nm, nf = s // tm, dff // tf
    cast_specs = [_row_cast_spec(a, nm, nf) for a in row_casts]
    cast_bytes = sum(2 * (4 + 2) * math.prod(spec.block_shape) for spec in cast_specs)
    est = (2 * tm * d * 4 + tm * d * 2 + 2 * 3 * d * tf * 4 + 3 * d * tf * 2 + 2 * tm * d * 4
           + 3 * FFN_ROW_CHUNK * tf * 4 + cast_bytes)
    out = pl.pallas_call(
        functools.partial(_ffn_ln_kernel, n_casts=len(row_casts)),
        grid=(nm, nf),
        in_specs=[
            pl.BlockSpec((tm, d), lambda i, f: (i, 0)),
            pl.BlockSpec((d, tf), lambda i, f: (0, f)),
            pl.BlockSpec((d, tf), lambda i, f: (0, f)),
            pl.BlockSpec((tf, d), lambda i, f: (f, 0)),
            pl.BlockSpec((1, d), lambda i, f: (0, 0)),
            pl.BlockSpec((1, d), lambda i, f: (0, 0)),
        ] + cast_specs,
        out_specs=[pl.BlockSpec((tm, d), lambda i, f: (i, 0))] + cast_specs,
        out_shape=[jax.ShapeDtypeStruct((s, d), F32)] + [jax.ShapeDtypeStruct(a.shape, BF16) for a in row_casts],
        scratch_shapes=[pltpu.VMEM((tm, d), BF16), pltpu.VMEM((d, tf), BF16), pltpu.VMEM((d, tf), BF16),
                        pltpu.VMEM((tf, d), BF16)],
        compiler_params=pltpu.CompilerParams(
            dimension_semantics=("arbitrary", "arbitrary"),
            vmem_limit_bytes=_vmem_limit(est)),
        name="ffn_ln",
    )(x, w1, w3, w2, g, b, *row_casts)
    return out[0], out[1:]


INPROJ_TM = 512
Z_CQ = 0
Z_CKV = Z_CQ + Q_LORA
Z_KS = Z_CKV + KV_LORA
Z_WIDTH = Z_KS + SWA_KV_WIDTH
ZT_QS = 0
ZT_VS = ZT_QS + SWA_WIDTH
ZT_KR = ZT_VS + SWA_KV_WIDTH
ZT_ROWS = ZT_KR + V7X_LANES
HALF_ROPE = QK_ROPE // 2


def _rope_rows(x1, x2, cos, sin):
    return x1 * cos - x2 * sin, x1 * sin + x2 * cos


def _in_proj_kernel(h_ref, wz_ref, wt_ref, qg_ref, kvg_ref, wuqt_ref, wkn_ref, wvt_ref, pos_ref,
                    freq_ref, qt_ref, k_ref, vt_ref, qst_ref, ks_ref, vst_ref):
    tm = h_ref.shape[0]
    hb = h_ref[...].astype(BF16)
    z = _dot(hb, wz_ref[...])
    zt = _dot_nt(wt_ref[...], hb)

    ang = freq_ref[...] * pos_ref[...]
    cos = jnp.cos(ang)
    sin = jnp.sin(ang)

    cq = _rms_norm(z[:, Z_CQ:Z_CQ + Q_LORA], qg_ref[...]).astype(BF16)
    ckv = _rms_norm(z[:, Z_CKV:Z_CKV + KV_LORA], kvg_ref[...]).astype(BF16)
    qt = _dot_nt(wuqt_ref[...], cq)
    kn = _dot(ckv, wkn_ref[...])
    vt = _dot_nt(wvt_ref[...], ckv)

    kr1, kr2 = _rope_rows(zt[ZT_KR:ZT_KR + HALF_ROPE], zt[ZT_KR + HALF_ROPE:ZT_KR + QK_ROPE], cos, sin)
    k_rope = jnp.concatenate([kr1, kr2, zt[ZT_KR + QK_ROPE:ZT_KR + V7X_LANES]], axis=0).T.astype(BF16)

    for h in range(MLA_HEADS):
        base = h * MLA_QK_PAD
        r0 = base + QK_NOPE
        qt_ref[base:r0, :] = qt[base:r0].astype(BF16)
        q1, q2 = _rope_rows(qt[r0:r0 + HALF_ROPE], qt[r0 + HALF_ROPE:r0 + QK_ROPE], cos, sin)
        qt_ref[r0:r0 + HALF_ROPE, :] = q1.astype(BF16)
        qt_ref[r0 + HALF_ROPE:r0 + QK_ROPE, :] = q2.astype(BF16)
        qt_ref[r0 + QK_ROPE:base + MLA_QK_PAD, :] = qt[r0 + QK_ROPE:base + MLA_QK_PAD].astype(BF16)
        k_ref[:, base:r0] = kn[:, h * QK_NOPE:(h + 1) * QK_NOPE].astype(BF16)
        k_ref[:, r0:base + MLA_QK_PAD] = k_rope
        vt_ref[h, 0, 0:V_HEAD, :] = vt[h * V_HEAD:(h + 1) * V_HEAD].astype(BF16)
        vt_ref[h, 0, V_HEAD:MLA_V_ROWS, :] = jnp.ones((MLA_V_ROWS - V_HEAD, tm), BF16)

    qst_ref[...] = zt[ZT_QS:ZT_QS + SWA_WIDTH].reshape(SWA_HEADS, SWA_HEAD_DIM, tm).astype(BF16)
    for kv in range(SWA_KV_HEADS):
        lo = Z_KS + kv * SWA_HEAD_DIM
        ks_ref[kv] = z[:, lo:lo + SWA_HEAD_DIM].astype(BF16)
        lo = ZT_VS + kv * SWA_HEAD_DIM
        for c in range(tm // BLOCK):
            vst_ref[kv, c] = zt[lo:lo + SWA_HEAD_DIM, c * BLOCK:(c + 1) * BLOCK].astype(BF16)


def _in_proj(h, wz, wt, q_g, kv_g, wuq_t, wkn, wv_t, pos_row, freq_col):
    s, d = h.shape
    tm = INPROJ_TM
    assert tm == MLA_TK
    est = (2 * tm * d * 4 + 2 * d * (Z_WIDTH + ZT_ROWS) * 2
           + 2 * Q_LORA * (MLA_HEADS * MLA_QK_PAD + 2 * MLA_WIDTH) * 2
           + 2 * tm * (2 * MLA_HEADS * MLA_QK_PAD + MLA_WIDTH + SWA_WIDTH + 4 * V7X_LANES) * 2
           + 4 * tm * (Z_WIDTH + ZT_ROWS + 2 * MLA_HEADS * MLA_QK_PAD) * 4)
    const = lambda i: (0, 0)
    return pl.pallas_call(
        _in_proj_kernel,
        grid=(s // tm,),
        in_specs=[
            pl.BlockSpec((tm, d), lambda i: (i, 0)),
            pl.BlockSpec((d, Z_WIDTH), const),
            pl.BlockSpec((ZT_ROWS, d), const),
            pl.BlockSpec((1, Q_LORA), const),
            pl.BlockSpec((1, KV_LORA), const),
            pl.BlockSpec((MLA_HEADS * MLA_QK_PAD, Q_LORA), const),
            pl.BlockSpec((KV_LORA, MLA_WIDTH), const),
            pl.BlockSpec((MLA_WIDTH, KV_LORA), const),
            pl.BlockSpec((1, tm), lambda i: (0, i)),
            pl.BlockSpec((HALF_ROPE, 1), const),
        ],
        out_specs=[
            pl.BlockSpec((MLA_HEADS * MLA_QK_PAD, tm), lambda i: (0, i)),
            pl.BlockSpec((tm, MLA_HEADS * MLA_QK_PAD), lambda i: (i, 0)),
            pl.BlockSpec((MLA_HEADS, 1, MLA_V_ROWS, tm), lambda i: (0, i, 0, 0)),
            pl.BlockSpec((SWA_HEADS, SWA_HEAD_DIM, tm), lambda i: (0, 0, i)),
            pl.BlockSpec((SWA_KV_HEADS, tm, SWA_HEAD_DIM), lambda i: (0, i, 0)),
            pl.BlockSpec((SWA_KV_HEADS, tm // BLOCK, SWA_HEAD_DIM, BLOCK), lambda i: (0, i, 0, 0)),
        ],
        out_shape=[
            jax.ShapeDtypeStruct((MLA_HEADS * MLA_QK_PAD, s), BF16),
            jax.ShapeDtypeStruct((s, MLA_HEADS * MLA_QK_PAD), BF16),
            jax.ShapeDtypeStruct((MLA_HEADS, s // tm, MLA_V_ROWS, tm), BF16),
            jax.ShapeDtypeStruct((SWA_HEADS, SWA_HEAD_DIM, s), BF16),
            jax.ShapeDtypeStruct((SWA_KV_HEADS, s, SWA_HEAD_DIM), BF16),
            jax.ShapeDtypeStruct((SWA_KV_HEADS, s // BLOCK, SWA_HEAD_DIM, BLOCK), BF16),
        ],
        compiler_params=pltpu.CompilerParams(
            dimension_semantics=("parallel",),
            vmem_limit_bytes=_vmem_limit(est)),
        name="in_proj",
    )(h, wz, wt, q_g, kv_g, wuq_t, wkn, wv_t, pos_row, freq_col)


MLA_TQ = 512
MLA_TK = 512
MLA_HB = 2
MLA_SCALE = (QK_NOPE + QK_ROPE) ** -0.5
MLA_SCALE_LOG2E = MLA_SCALE * LOG2E


def _mla_flash_kernel(qt_ref, k_ref, vt_ref, o_ref, s_ref, smax_ref, p_ref, m_ref, corr_ref, acc_ref):
    i = pl.program_id(1)
    tk = MLA_TK
    heads = range(MLA_HB)

    def qk(t):
        start = pl.multiple_of(t * tk, tk)
        for h in heads:
            cols = slice(h * MLA_QK_PAD, (h + 1) * MLA_QK_PAD)
            s = _dot(k_ref[pl.ds(start, tk), cols], qt_ref[cols, :])
            s_ref[h] = s
            smax_ref[h] = jnp.max(s, axis=0, keepdims=True)

    def softmax(scores, masked):
        for h in heads:
            s, smax = scores[h]
            s = s * MLA_SCALE_LOG2E
            if masked:
                key = lax.broadcasted_iota(jnp.int32, s.shape, 0)
                qry = lax.broadcasted_iota(jnp.int32, s.shape, 1)
                s = jnp.where(key <= qry, s, NEG_BIG)
                tile_max = jnp.max(s, axis=0, keepdims=True)
            else:
                tile_max = smax * MLA_SCALE_LOG2E
            m_prev = m_ref[h]
            m_new = jnp.maximum(m_prev, tile_max)
            p_ref[h] = jnp.exp2(s - m_new).astype(BF16)
            corr_ref[h] = jnp.exp2(m_prev - m_new)
            m_ref[h] = m_new

    def pv(t):
        for h in heads:
            acc_ref[h] = corr_ref[h] * acc_ref[h] + _dot(vt_ref[h, t], p_ref[h])

    def load_scores():
        return [(s_ref[h], smax_ref[h]) for h in heads]

    m_ref[...] = jnp.full_like(m_ref, NEG_BIG)
    acc_ref[...] = jnp.zeros_like(acc_ref)

    @pl.when(i == 0)
    def _():
        p_ref[...] = jnp.zeros_like(p_ref)
        corr_ref[...] = jnp.zeros_like(corr_ref)

    qk(0)

    @pl.when(i >= 1)
    def _():
        scores = load_scores()
        qk(1)
        softmax(scores, masked=False)

    def body(t, carry):
        pv(t)
        scores = load_scores()
        qk(t + 2)
        softmax(scores, masked=False)
        return carry

    lax.fori_loop(0, i - 1, body, 0)
    pv(jnp.maximum(i - 1, 0))
    softmax(load_scores(), masked=True)
    pv(i)
    for h in heads:
        o_ref[:, h * V_HEAD:(h + 1) * V_HEAD] = (
            acc_ref[h, 0:V_HEAD, :] / acc_ref[h, V_HEAD:V_HEAD + 1, :]).T


def _mla_flash(qt, k, vt):
    s = k.shape[0]
    tq, tk, hb = MLA_TQ, MLA_TK, MLA_HB
    assert tq == tk
    est = (2 * hb * tq * MLA_QK_PAD * 2 + 2 * hb * s * MLA_QK_PAD * 2 + 2 * hb * s * MLA_V_ROWS * 2
           + 2 * hb * tq * V_HEAD * 4 + hb * tk * tq * (4 + 2) + hb * tq * MLA_V_ROWS * 4
           + 4 * hb * tq * tk * 4)
    return pl.pallas_call(
        _mla_flash_kernel,
        grid=(MLA_HEADS // hb, s // tq),
        in_specs=[
            pl.BlockSpec((hb * MLA_QK_PAD, tq), lambda g, i: (g, i)),
            pl.BlockSpec((s, hb * MLA_QK_PAD), lambda g, i: (0, g)),
            pl.BlockSpec((hb, s // tk, MLA_V_ROWS, tk), lambda g, i: (g, 0, 0, 0)),
        ],
        out_specs=pl.BlockSpec((tq, hb * V_HEAD), lambda g, i: (i, g)),
        out_shape=jax.ShapeDtypeStruct((s, MLA_WIDTH), F32),
        scratch_shapes=[pltpu.VMEM((hb, tk, tq), F32), pltpu.VMEM((hb, 1, tq), F32),
                        pltpu.VMEM((hb, tk, tq), BF16),
                        pltpu.VMEM((hb, 1, tq), F32), pltpu.VMEM((hb, 1, tq), F32),
                        pltpu.VMEM((hb, MLA_V_ROWS, tq), F32)],
        compiler_params=pltpu.CompilerParams(
            dimension_semantics=("parallel", "arbitrary"),
            vmem_limit_bytes=_vmem_limit(est)),
        name="mla_flash",
    )(qt, k, vt)


SWA_SCALE = SWA_HEAD_DIM ** -0.5
SWA_COLS = SWA_GROUP * BLOCK
SWA_NB = 4


def _t5_bucket_np(dist):
    n = np.maximum(dist, 0)
    max_exact = REL_BUCKETS // 2
    large = max_exact + (np.log(np.maximum(n, 1).astype(np.float32) / max_exact)
                         / math.log(REL_MAX_DIST / max_exact)
                         * (REL_BUCKETS - max_exact)).astype(np.int32)
    large = np.minimum(large, REL_BUCKETS - 1)
    return np.where(n < max_exact, n, large).astype(np.int32)


def _swa_bucket_table():
    j = np.arange(BLOCK)[:, None]
    i = np.arange(BLOCK)[None, :]
    dist = np.where(j <= i, i - j, BLOCK + i - j)
    return _t5_bucket_np(dist)


def _swa_kernel(rb_ref, bkt_ref, qt_ref, qt_next_ref, k_ref, vt_ref, sink_ref, o_ref, bias_ref, s_ref):
    kv = pl.program_id(0)
    n = pl.program_id(1)

    @pl.when((kv == 0) & (n == 0))
    def _():
        bkt = bkt_ref[...]
        for h in range(SWA_HEADS):
            acc = jnp.zeros((BLOCK, BLOCK), F32)
            for b in range(REL_BUCKETS):
                acc = jnp.where(bkt == b, rb_ref[b, h], acc)
            g = h % SWA_GROUP
            bias_ref[h // SWA_GROUP, :, g * BLOCK:(g + 1) * BLOCK] = acc * LOG2E

    key = lax.broadcasted_iota(jnp.int32, (BLOCK, SWA_COLS), 0)
    qry = lax.broadcasted_iota(jnp.int32, (BLOCK, SWA_COLS), 1) & (BLOCK - 1)
    lower = key <= qry
    lower_bf = jnp.where(lower, 1.0, 0.0).astype(BF16)
    bias = bias_ref[kv]
    sink = sink_ref[...] * LOG2E

    def band_scores(q_ref, step):
        for b in range(SWA_NB):
            blk = step * SWA_NB + b
            qt = jnp.concatenate([q_ref[g, :, b * BLOCK:(b + 1) * BLOCK] for g in range(SWA_GROUP)],
                                 axis=1)
            cur = pl.multiple_of(blk * BLOCK, BLOCK)
            prev = pl.multiple_of(jnp.maximum(blk - 1, 0) * BLOCK, BLOCK)
            k_band = jnp.concatenate([k_ref[pl.ds(prev, BLOCK), :], k_ref[pl.ds(cur, BLOCK), :]], axis=0)
            s_ref[b] = _dot(k_band, qt)

    @pl.when(n == 0)
    def _():
        band_scores(qt_ref, 0)

    bands = [s_ref[b] for b in range(SWA_NB)]
    band_scores(qt_next_ref, jnp.minimum(n + 1, pl.num_programs(1) - 1))

    for b in range(SWA_NB):
        blk = n * SWA_NB + b
        prev_blk = jnp.maximum(blk - 1, 0)
        s_band = bands[b]
        s = jnp.where(lower, s_band[BLOCK:], s_band[:BLOCK]) * (SWA_SCALE * LOG2E) + bias
        if b == 0:
            s = jnp.where(lower | (blk > 0), s, NEG_BIG)

        m = jnp.maximum(jnp.max(s, axis=0, keepdims=True), sink)
        e = jnp.exp2(s - m)
        denom = jnp.sum(e, axis=0, keepdims=True) + jnp.exp2(sink - m)
        p = (e * (1.0 / denom)).astype(BF16)
        p_cur = p * lower_bf
        p_prev = p - p_cur
        ot = _dot(vt_ref[blk], p_cur) + _dot(vt_ref[prev_blk], p_prev)
        o_ref[b * BLOCK:(b + 1) * BLOCK, :] = jnp.concatenate(
            [ot[:, g * BLOCK:(g + 1) * BLOCK] for g in range(SWA_GROUP)], axis=0).T


def _swa_attn(rel_bias, bucket_tbl, qst, ks, vst, sink_rows):
    s = ks.shape[1]
    rows = SWA_NB * BLOCK
    n_steps = s // rows
    est = (4 * SWA_GROUP * SWA_HEAD_DIM * rows * 2 + 2 * s * V7X_LANES * 2 + 2 * s * SWA_HEAD_DIM * 2
           + 2 * rows * SWA_GROUP * SWA_HEAD_DIM * 4 + SWA_KV_HEADS * BLOCK * SWA_COLS * 4
           + SWA_NB * 2 * BLOCK * SWA_COLS * 4 + 10 * SWA_NB * BLOCK * SWA_COLS * 4)
    return pl.pallas_call(
        _swa_kernel,
        grid=(SWA_KV_HEADS, n_steps),
        in_specs=[
            pl.BlockSpec(memory_space=pltpu.SMEM),
            pl.BlockSpec((BLOCK, BLOCK), lambda kv, n: (0, 0)),
            pl.BlockSpec((SWA_GROUP, SWA_HEAD_DIM, rows), lambda kv, n: (kv, 0, n)),
            pl.BlockSpec((SWA_GROUP, SWA_HEAD_DIM, rows),
                         lambda kv, n: (kv, 0, jnp.minimum(n + 1, n_steps - 1))),
            pl.BlockSpec((None, s, SWA_HEAD_DIM), lambda kv, n: (kv, 0, 0)),
            pl.BlockSpec((None, s // BLOCK, SWA_HEAD_DIM, BLOCK), lambda kv, n: (kv, 0, 0, 0)),
            pl.BlockSpec((None, 1, SWA_COLS), lambda kv, n: (kv, 0, 0)),
        ],
        out_specs=pl.BlockSpec((rows, SWA_GROUP * SWA_HEAD_DIM), lambda kv, n: (n, kv)),
        out_shape=jax.ShapeDtypeStruct((s, SWA_WIDTH), F32),
        scratch_shapes=[pltpu.VMEM((SWA_KV_HEADS, BLOCK, SWA_COLS), F32),
                        pltpu.VMEM((SWA_NB, 2 * BLOCK, SWA_COLS), F32)],
        compiler_params=pltpu.CompilerParams(
            dimension_semantics=("arbitrary", "arbitrary"),
            vmem_limit_bytes=_vmem_limit(est)),
        name="swa_attn",
    )(rel_bias, bucket_tbl, qst, qst, ks, vst, sink_rows)


OUTPROJ_TM = 512
OUTPROJ_CHUNKS = 2


def _out_proj_kernel(am_ref, as_ref, h_ref, mg_ref, sg_ref, wm_ref, ws_ref, g_ref, b_ref, o_ref):
    chunk = o_ref.shape[0] // OUTPROJ_CHUNKS
    for c in range(OUTPROJ_CHUNKS):
        rows = slice(c * chunk, (c + 1) * chunk)
        nm = _rms_norm(am_ref[rows, :], mg_ref[...]).astype(BF16)
        ns = _rms_norm(as_ref[rows, :], sg_ref[...]).astype(BF16)
        mixed = _dot(nm, wm_ref[...]) + _dot(ns, ws_ref[...])
        o_ref[rows, :] = _layer_norm(ALPHA * h_ref[rows, :] + mixed, g_ref[...], b_ref[...])


def _out_proj(a_mla, a_swa, h, mla_g, swa_g, w_out, g, b):
    s, d = h.shape
    tm = OUTPROJ_TM
    est = (2 * tm * (MLA_WIDTH + SWA_WIDTH) * 4 + 4 * tm * d * 4 + 2 * (MLA_WIDTH + SWA_WIDTH) * d * 2
           + 4 * tm * d * 4)
    const = lambda i: (0, 0)
    return pl.pallas_call(
        _out_proj_kernel,
        grid=(s // tm,),
        in_specs=[
            pl.BlockSpec((tm, MLA_WIDTH), lambda i: (i, 0)),
            pl.BlockSpec((tm, SWA_WIDTH), lambda i: (i, 0)),
            pl.BlockSpec((tm, d), lambda i: (i, 0)),
            pl.BlockSpec((1, MLA_WIDTH), const),
            pl.BlockSpec((1, SWA_WIDTH), const),
            pl.BlockSpec((MLA_WIDTH, d), const),
            pl.BlockSpec((SWA_WIDTH, d), lambda i: (MLA_WIDTH // SWA_WIDTH, 0)),
            pl.BlockSpec((1, d), const),
            pl.BlockSpec((1, d), const),
        ],
        out_specs=pl.BlockSpec((tm, d), lambda i: (i, 0)),
        out_shape=jax.ShapeDtypeStruct((s, d), F32),
        compiler_params=pltpu.CompilerParams(
            dimension_semantics=("parallel",),
            vmem_limit_bytes=_vmem_limit(est)),
        name="out_proj",
    )(a_mla, a_swa, h, mla_g, swa_g, w_out, w_out, g, b)


PLE_TM = 512


def _ple_kernel(h_ref, p_ref, wg_ref, wp_ref, o_ref):
    h = h_ref[...]
    gate = _dot(h.astype(BF16), wg_ref[...])
    proj = _dot(p_ref[...].astype(BF16), wp_ref[...])
    o_ref[...] = h + (1.0 / (1.0 + jnp.exp(-gate))) * proj


def _ple(h, p, w_gate, w_proj):
    s, d = h.shape
    tm = PLE_TM
    est = 4 * tm * d * 4 + 2 * tm * PLE_DIM * 4 + 2 * (d + PLE_DIM) * d * 2 + 4 * tm * d * 4
    const = lambda i: (0, 0)
    return pl.pallas_call(
        _ple_kernel,
        grid=(s // tm,),
        in_specs=[
            pl.BlockSpec((tm, d), lambda i: (i, 0)),
            pl.BlockSpec((tm, PLE_DIM), lambda i: (i, 0)),
            pl.BlockSpec((d, d), const),
            pl.BlockSpec((PLE_DIM, d), const),
        ],
        out_specs=pl.BlockSpec((tm, d), lambda i: (i, 0)),
        out_shape=jax.ShapeDtypeStruct((s, d), F32),
        compiler_params=pltpu.CompilerParams(
            dimension_semantics=("parallel",),
            vmem_limit_bytes=_vmem_limit(est)),
        name="ple",
    )(h, p, w_gate, w_proj)


def _split_w_in(w_in):
    d = w_in.shape[0]
    b_kr = Q_LORA + KV_LORA
    b_qs = b_kr + QK_ROPE
    b_ks = b_qs + SWA_WIDTH
    b_vs = b_ks + SWA_KV_WIDTH
    wz = jnp.concatenate([w_in[:, :b_kr], w_in[:, b_ks:b_vs]], axis=1)
    pad = jnp.zeros((d, V7X_LANES - QK_ROPE), w_in.dtype)
    wt = jnp.concatenate([w_in[:, b_qs:b_ks], w_in[:, b_vs:], w_in[:, b_kr:b_qs], pad], axis=1).T
    return wz, wt


def _pack_w_uq_t(w_uq):
    r = w_uq.shape[0]
    w = w_uq.reshape(r, MLA_HEADS, QK_NOPE + QK_ROPE)
    w = jnp.pad(w, ((0, 0), (0, 0), (0, MLA_QK_PAD - QK_NOPE - QK_ROPE)))
    return w.reshape(r, MLA_HEADS * MLA_QK_PAD).T


def _rope_freq_col():
    freqs = ROPE_THETA ** (-jnp.arange(0, QK_ROPE, 2, dtype=F32) / QK_ROPE)
    return freqs.reshape(HALF_ROPE, 1)


def kernel(x, p, positions, rel_bias, ln1_g, ln1_b, ffn1_w1, ffn1_w3, ffn1_w2, w_in, q_norm_g, w_uq,
           kv_norm_g, w_ukv, swa_sinks, mla_out_g, swa_out_g, w_out, ln2_g, ln2_b, ffn2_w1, ffn2_w3,
           ffn2_w2, ln3_g, ln3_b, ple_w_gate, ple_w_proj):
    assert x.shape == (1, SEQ, D_MODEL) and DEPTH == 1
    row = lambda a: a.reshape(1, -1)
    h = x[0]
    pos_row = positions[0].astype(F32).reshape(1, SEQ)
    freq_col = _rope_freq_col()
    bucket_tbl = jnp.asarray(_swa_bucket_table())

    i = 0
    h, (w_out_b, w_gate_b, w_proj_b) = _ffn_ln(
        h, ffn1_w1[i], ffn1_w3[i], ffn1_w2[i], row(ln1_g[i]), row(ln1_b[i]),
        row_casts=(w_out[i], ple_w_gate[i], ple_w_proj[i]))

    w_ukv_h = w_ukv[i].reshape(KV_LORA, MLA_HEADS, QK_NOPE + V_HEAD)
    wkn = w_ukv_h[:, :, :QK_NOPE].reshape(KV_LORA, MLA_WIDTH).astype(BF16)
    wv_t = w_ukv_h[:, :, QK_NOPE:].reshape(KV_LORA, MLA_WIDTH).T.astype(BF16)
    wz, wt = _split_w_in(w_in[i])
    qt, k, vt, qst, ks, vst = _in_proj(
        h, wz.astype(BF16), wt.astype(BF16), row(q_norm_g[i]), row(kv_norm_g[i]),
        _pack_w_uq_t(w_uq[i]).astype(BF16), wkn, wv_t, pos_row, freq_col)

    a_mla = _mla_flash(qt, k, vt)
    sink_rows = jnp.repeat(swa_sinks[i], BLOCK).reshape(SWA_KV_HEADS, 1, SWA_COLS)
    a_swa = _swa_attn(rel_bias, bucket_tbl, qst, ks, vst, sink_rows)

    h = _out_proj(a_mla, a_swa, h, row(mla_out_g[i]), row(swa_out_g[i]), w_out_b,
                  row(ln2_g[i]), row(ln2_b[i]))

    h, _ = _ffn_ln(h, ffn2_w1[i], ffn2_w3[i], ffn2_w2[i], row(ln3_g[i]), row(ln3_b[i]))
    h = _ple(h, p[i, 0], w_gate_b, w_proj_b)
    return h[None]
```

```python
import math

import numpy as np
import jax
import jax.numpy as jnp
from jax import lax
from jax.experimental import pallas as pl
from jax.experimental.pallas import tpu as pltpu

F32 = jnp.float32
BF16 = jnp.bfloat16

D_MODEL = 2048
SEQ = 8192
DEPTH = 1
PLE_DIM = 256
MLA_HEADS = 8
Q_LORA = 512
KV_LORA = 512
QK_NOPE = 128
QK_ROPE = 64
V_HEAD = 128
ROPE_THETA = 10000.0
SWA_HEADS = 16
SWA_KV_HEADS = 2
SWA_GROUP = SWA_HEADS // SWA_KV_HEADS
SWA_HEAD_DIM = 64
WINDOW = 128
BLOCK = 128
REL_BUCKETS = 32
REL_MAX_DIST = 128
D_FF = 5632
ALPHA = (2.0 * DEPTH) ** 0.25
EPS = 1e-5
MLA_WIDTH = MLA_HEADS * V_HEAD
SWA_WIDTH = SWA_HEADS * SWA_HEAD_DIM
SWA_KV_WIDTH = SWA_KV_HEADS * SWA_HEAD_DIM

V7X_LANES = 128
V7X_VMEM_BYTES = 64 * 1024 * 1024
MIB = 1024 * 1024

V7X_BF16_SUBLANES = 16

MLA_QK_PAD = 2 * V7X_LANES
MLA_V_ROWS = V_HEAD + V7X_BF16_SUBLANES
NEG_BIG = float(np.finfo(np.float32).min)
LOG2E = math.log2(math.e)


def _vmem_limit(estimate_bytes):
    return int(min(estimate_bytes + 8 * MIB, V7X_VMEM_BYTES - 4 * MIB))


def _layer_norm(y, g, b):
    mu = jnp.mean(y, axis=-1, keepdims=True)
    yc = y - mu
    var = jnp.mean(yc * yc, axis=-1, keepdims=True)
    return yc * lax.rsqrt(var + EPS) * g + b


def _rms_norm(x, g):
    ms = jnp.mean(x * x, axis=-1, keepdims=True)
    return x * lax.rsqrt(ms + EPS) * g


def _dot(a, b):
    return jnp.dot(a, b, preferred_element_type=F32)


def _dot_nt(a, b):
    return lax.dot_general(a, b, (((1,), (1,)), ((), ())), preferred_element_type=F32)


FFN_TM = 1024
FFN_TF = 256
FFN_ROW_CHUNK = 512
FFN_LN_ROWS = 256


def _ffn_ln_kernel(x_ref, w1_ref, w3_ref, w2_ref, g_ref, b_ref, o_ref, xb_ref, w1b_ref, w3b_ref, w2b_ref):
    f = pl.program_id(1)

    @pl.when(f == 0)
    def _():
        xb_ref[...] = x_ref[...].astype(BF16)
        o_ref[...] = jnp.zeros_like(o_ref)

    w1b_ref[...] = w1_ref[...].astype(BF16)
    w3b_ref[...] = w3_ref[...].astype(BF16)
    w2b_ref[...] = w2_ref[...].astype(BF16)

    for c in range(o_ref.shape[0] // FFN_ROW_CHUNK):
        rows = slice(c * FFN_ROW_CHUNK, (c + 1) * FFN_ROW_CHUNK)
        xb = xb_ref[rows, :]
        gate = _dot(xb, w1b_ref[...])
        up = _dot(xb, w3b_ref[...])
        hidden = gate * (1.0 / (1.0 + jnp.exp(-gate))) * up
        o_ref[rows, :] += _dot(hidden.astype(BF16), w2b_ref[...])

    @pl.when(f == pl.num_programs(1) - 1)
    def _():
        for c in range(o_ref.shape[0] // FFN_LN_ROWS):
            rows = slice(c * FFN_LN_ROWS, (c + 1) * FFN_LN_ROWS)
            y = ALPHA * x_ref[rows, :] + 0.5 * o_ref[rows, :]
            o_ref[rows, :] = _layer_norm(y, g_ref[...], b_ref[...])


def _ffn_ln(x, w1, w3, w2, g, b):
    s, d = x.shape
    dff = w1.shape[1]
    tm, tf = FFN_TM, FFN_TF
    est = (2 * tm * d * 4 + tm * d * 2 + 2 * 3 * d * tf * 4 + 3 * d * tf * 2 + 2 * tm * d * 4
           + 3 * FFN_ROW_CHUNK * tf * 4)
    return pl.pallas_call(
        _ffn_ln_kernel,
        grid=(s // tm, dff // tf),
        in_specs=[
            pl.BlockSpec((tm, d), lambda i, f: (i, 0)),
            pl.BlockSpec((d, tf), lambda i, f: (0, f)),
            pl.BlockSpec((d, tf), lambda i, f: (0, f)),
            pl.BlockSpec((tf, d), lambda i, f: (f, 0)),
            pl.BlockSpec((1, d), lambda i, f: (0, 0)),
            pl.BlockSpec((1, d), lambda i, f: (0, 0)),
        ],
        out_specs=pl.BlockSpec((tm, d), lambda i, f: (i, 0)),
        out_shape=jax.ShapeDtypeStruct((s, d), F32),
        scratch_shapes=[pltpu.VMEM((tm, d), BF16), pltpu.VMEM((d, tf), BF16), pltpu.VMEM((d, tf), BF16),
                        pltpu.VMEM((tf, d), BF16)],
        compiler_params=pltpu.CompilerParams(
            dimension_semantics=("parallel", "arbitrary"),
            vmem_limit_bytes=_vmem_limit(est)),
        name="ffn_ln",
    )(x, w1, w3, w2, g, b)


INPROJ_TM = 512
Z_CQ = 0
Z_CKV = Z_CQ + Q_LORA
Z_KS = Z_CKV + KV_LORA
Z_WIDTH = Z_KS + SWA_KV_WIDTH
ZT_QS = 0
ZT_VS = ZT_QS + SWA_WIDTH
ZT_KR = ZT_VS + SWA_KV_WIDTH
ZT_ROWS = ZT_KR + V7X_LANES
HALF_ROPE = QK_ROPE // 2


def _rope_rows(x1, x2, cos, sin):
    return x1 * cos - x2 * sin, x1 * sin + x2 * cos


def _in_proj_kernel(h_ref, wz_ref, wt_ref, qg_ref, kvg_ref, wuqt_ref, wkn_ref, wvt_ref, pos_ref,
                    freq_ref, qt_ref, k_ref, vt_ref, qst_ref, ks_ref, vst_ref):
    tm = h_ref.shape[0]
    hb = h_ref[...].astype(BF16)
    z = _dot(hb, wz_ref[...])
    zt = _dot_nt(wt_ref[...], hb)

    ang = freq_ref[...] * pos_ref[...]
    cos = jnp.cos(ang)
    sin = jnp.sin(ang)

    cq = _rms_norm(z[:, Z_CQ:Z_CQ + Q_LORA], qg_ref[...]).astype(BF16)
    ckv = _rms_norm(z[:, Z_CKV:Z_CKV + KV_LORA], kvg_ref[...]).astype(BF16)
    qt = _dot_nt(wuqt_ref[...], cq)
    kn = _dot(ckv, wkn_ref[...])
    vt = _dot_nt(wvt_ref[...], ckv)

    kr1, kr2 = _rope_rows(zt[ZT_KR:ZT_KR + HALF_ROPE], zt[ZT_KR + HALF_ROPE:ZT_KR + QK_ROPE], cos, sin)
    k_rope = jnp.concatenate([kr1, kr2, zt[ZT_KR + QK_ROPE:ZT_KR + V7X_LANES]], axis=0).T.astype(BF16)

    for h in range(MLA_HEADS):
        base = h * MLA_QK_PAD
        r0 = base + QK_NOPE
        qt_ref[base:r0, :] = qt[base:r0].astype(BF16)
        q1, q2 = _rope_rows(qt[r0:r0 + HALF_ROPE], qt[r0 + HALF_ROPE:r0 + QK_ROPE], cos, sin)
        qt_ref[r0:r0 + HALF_ROPE, :] = q1.astype(BF16)
        qt_ref[r0 + HALF_ROPE:r0 + QK_ROPE, :] = q2.astype(BF16)
        qt_ref[r0 + QK_ROPE:base + MLA_QK_PAD, :] = qt[r0 + QK_ROPE:base + MLA_QK_PAD].astype(BF16)
        k_ref[:, base:r0] = kn[:, h * QK_NOPE:(h + 1) * QK_NOPE].astype(BF16)
        k_ref[:, r0:base + MLA_QK_PAD] = k_rope
        vt_ref[h, 0, 0:V_HEAD, :] = vt[h * V_HEAD:(h + 1) * V_HEAD].astype(BF16)
        vt_ref[h, 0, V_HEAD:MLA_V_ROWS, :] = jnp.ones((MLA_V_ROWS - V_HEAD, tm), BF16)

    qst_ref[...] = zt[ZT_QS:ZT_QS + SWA_WIDTH].reshape(SWA_HEADS, SWA_HEAD_DIM, tm).astype(BF16)
    for kv in range(SWA_KV_HEADS):
        lo = Z_KS + kv * SWA_HEAD_DIM
        ks_ref[kv] = z[:, lo:lo + SWA_HEAD_DIM].astype(BF16)
        lo = ZT_VS + kv * SWA_HEAD_DIM
        for c in range(tm // BLOCK):
            vst_ref[kv, c] = zt[lo:lo + SWA_HEAD_DIM, c * BLOCK:(c + 1) * BLOCK].astype(BF16)


def _in_proj(h, wz, wt, q_g, kv_g, wuq_t, wkn, wv_t, pos_row, freq_col):
    s, d = h.shape
    tm = INPROJ_TM
    assert tm == MLA_TK
    est = (2 * tm * d * 4 + 2 * d * (Z_WIDTH + ZT_ROWS) * 2
           + 2 * Q_LORA * (MLA_HEADS * MLA_QK_PAD + 2 * MLA_WIDTH) * 2
           + 2 * tm * (2 * MLA_HEADS * MLA_QK_PAD + MLA_WIDTH + SWA_WIDTH + 4 * V7X_LANES) * 2
           + 4 * tm * (Z_WIDTH + ZT_ROWS + 2 * MLA_HEADS * MLA_QK_PAD) * 4)
    const = lambda i: (0, 0)
    return pl.pallas_call(
        _in_proj_kernel,
        grid=(s // tm,),
        in_specs=[
            pl.BlockSpec((tm, d), lambda i: (i, 0)),
            pl.BlockSpec((d, Z_WIDTH), const),
            pl.BlockSpec((ZT_ROWS, d), const),
            pl.BlockSpec((1, Q_LORA), const),
            pl.BlockSpec((1, KV_LORA), const),
            pl.BlockSpec((MLA_HEADS * MLA_QK_PAD, Q_LORA), const),
            pl.BlockSpec((KV_LORA, MLA_WIDTH), const),
            pl.BlockSpec((MLA_WIDTH, KV_LORA), const),
            pl.BlockSpec((1, tm), lambda i: (0, i)),
            pl.BlockSpec((HALF_ROPE, 1), const),
        ],
        out_specs=[
            pl.BlockSpec((MLA_HEADS * MLA_QK_PAD, tm), lambda i: (0, i)),
            pl.BlockSpec((tm, MLA_HEADS * MLA_QK_PAD), lambda i: (i, 0)),
            pl.BlockSpec((MLA_HEADS, 1, MLA_V_ROWS, tm), lambda i: (0, i, 0, 0)),
            pl.BlockSpec((SWA_HEADS, SWA_HEAD_DIM, tm), lambda i: (0, 0, i)),
            pl.BlockSpec((SWA_KV_HEADS, tm, SWA_HEAD_DIM), lambda i: (0, i, 0)),
            pl.BlockSpec((SWA_KV_HEADS, tm // BLOCK, SWA_HEAD_DIM, BLOCK), lambda i: (0, i, 0, 0)),
        ],
        out_shape=[
            jax.ShapeDtypeStruct((MLA_HEADS * MLA_QK_PAD, s), BF16),
            jax.ShapeDtypeStruct((s, MLA_HEADS * MLA_QK_PAD), BF16),
            jax.ShapeDtypeStruct((MLA_HEADS, s // tm, MLA_V_ROWS, tm), BF16),
            jax.ShapeDtypeStruct((SWA_HEADS, SWA_HEAD_DIM, s), BF16),
            jax.ShapeDtypeStruct((SWA_KV_HEADS, s, SWA_HEAD_DIM), BF16),
            jax.ShapeDtypeStruct((SWA_KV_HEADS, s // BLOCK, SWA_HEAD_DIM, BLOCK), BF16),
        ],
        compiler_params=pltpu.CompilerParams(
            dimension_semantics=("parallel",),
            vmem_limit_bytes=_vmem_limit(est)),
        name="in_proj",
    )(h, wz, wt, q_g, kv_g, wuq_t, wkn, wv_t, pos_row, freq_col)


MLA_TQ = 512
MLA_TK = 512
MLA_HB = 2
MLA_SCALE = (QK_NOPE + QK_ROPE) ** -0.5
MLA_SCALE_LOG2E = MLA_SCALE * LOG2E


def _mla_flash_kernel(qt_ref, k_ref, vt_ref, o_ref, s_ref, smax_ref, p_ref, m_ref, corr_ref, acc_ref):
    i = pl.program_id(1)
    tk = MLA_TK
    heads = range(MLA_HB)

    def qk(t):
        start = pl.multiple_of(t * tk, tk)
        for h in heads:
            cols = slice(h * MLA_QK_PAD, (h + 1) * MLA_QK_PAD)
            s = _dot(k_ref[pl.ds(start, tk), cols], qt_ref[cols, :])
            s_ref[h] = s
            smax_ref[h] = jnp.max(s, axis=0, keepdims=True)

    def softmax(scores, masked):
        for h in heads:
            s, smax = scores[h]
            s = s * MLA_SCALE_LOG2E
            if masked:
                key = lax.broadcasted_iota(jnp.int32, s.shape, 0)
                qry = lax.broadcasted_iota(jnp.int32, s.shape, 1)
                s = jnp.where(key <= qry, s, NEG_BIG)
                tile_max = jnp.max(s, axis=0, keepdims=True)
            else:
                tile_max = smax * MLA_SCALE_LOG2E
            m_prev = m_ref[h]
            m_new = jnp.maximum(m_prev, tile_max)
            p_ref[h] = jnp.exp2(s - m_new).astype(BF16)
            corr_ref[h] = jnp.exp2(m_prev - m_new)
            m_ref[h] = m_new

    def pv(t):
        for h in heads:
            acc_ref[h] = corr_ref[h] * acc_ref[h] + _dot(vt_ref[h, t], p_ref[h])

    def load_scores():
        return [(s_ref[h], smax_ref[h]) for h in heads]

    def init():
        m_ref[...] = jnp.full_like(m_ref, NEG_BIG)
        acc_ref[...] = jnp.zeros_like(acc_ref)

    def finish():
        softmax(load_scores(), masked=True)
        pv(i)
        for h in heads:
            o_ref[:, h * V_HEAD:(h + 1) * V_HEAD] = (
                acc_ref[h, 0:V_HEAD, :] / acc_ref[h, V_HEAD:V_HEAD + 1, :]).T

    @pl.when(i == 0)
    def _():
        init()
        qk(0)
        finish()

    @pl.when(i >= 1)
    def _():
        init()
        qk(0)
        scores = load_scores()
        qk(1)
        softmax(scores, masked=False)

        def body(t, carry):
            pv(t)
            scores = load_scores()
            qk(t + 2)
            softmax(scores, masked=False)
            return carry

        lax.fori_loop(0, i - 1, body, 0)
        pv(i - 1)
        finish()


def _mla_flash(qt, k, vt):
    s = k.shape[0]
    tq, tk, hb = MLA_TQ, MLA_TK, MLA_HB
    assert tq == tk
    est = (2 * hb * tq * MLA_QK_PAD * 2 + 2 * hb * s * MLA_QK_PAD * 2 + 2 * hb * s * MLA_V_ROWS * 2
           + 2 * hb * tq * V_HEAD * 4 + hb * tk * tq * (4 + 2) + hb * tq * MLA_V_ROWS * 4
           + 4 * hb * tq * tk * 4)
    return pl.pallas_call(
        _mla_flash_kernel,
        grid=(MLA_HEADS // hb, s // tq),
        in_specs=[
            pl.BlockSpec((hb * MLA_QK_PAD, tq), lambda g, i: (g, i)),
            pl.BlockSpec((s, hb * MLA_QK_PAD), lambda g, i: (0, g)),
            pl.BlockSpec((hb, s // tk, MLA_V_ROWS, tk), lambda g, i: (g, 0, 0, 0)),
        ],
        out_specs=pl.BlockSpec((tq, hb * V_HEAD), lambda g, i: (i, g)),
        out_shape=jax.ShapeDtypeStruct((s, MLA_WIDTH), F32),
        scratch_shapes=[pltpu.VMEM((hb, tk, tq), F32), pltpu.VMEM((hb, 1, tq), F32),
                        pltpu.VMEM((hb, tk, tq), BF16),
                        pltpu.VMEM((hb, 1, tq), F32), pltpu.VMEM((hb, 1, tq), F32),
                        pltpu.VMEM((hb, MLA_V_ROWS, tq), F32)],
        compiler_params=pltpu.CompilerParams(
            dimension_semantics=("parallel", "arbitrary"),
            vmem_limit_bytes=_vmem_limit(est)),
        name="mla_flash",
    )(qt, k, vt)


SWA_SCALE = SWA_HEAD_DIM ** -0.5
SWA_COLS = SWA_GROUP * BLOCK
SWA_NB = 4


def _t5_bucket_np(dist):
    n = np.maximum(dist, 0)
    max_exact = REL_BUCKETS // 2
    large = max_exact + (np.log(np.maximum(n, 1).astype(np.float32) / max_exact)
                         / math.log(REL_MAX_DIST / max_exact)
                         * (REL_BUCKETS - max_exact)).astype(np.int32)
    large = np.minimum(large, REL_BUCKETS - 1)
    return np.where(n < max_exact, n, large).astype(np.int32)


def _swa_bucket_table():
    j = np.arange(BLOCK)[:, None]
    i = np.arange(BLOCK)[None, :]
    dist = np.where(j <= i, i - j, BLOCK + i - j)
    return _t5_bucket_np(dist)


def _swa_kernel(rb_ref, bkt_ref, qt_ref, qt_next_ref, k_ref, vt_ref, sink_ref, o_ref, bias_ref, s_ref):
    kv = pl.program_id(0)
    n = pl.program_id(1)

    @pl.when((kv == 0) & (n == 0))
    def _():
        bkt = bkt_ref[...]
        for h in range(SWA_HEADS):
            acc = jnp.zeros((BLOCK, BLOCK), F32)
            for b in range(REL_BUCKETS):
                acc = jnp.where(bkt == b, rb_ref[b, h], acc)
            g = h % SWA_GROUP
            bias_ref[h // SWA_GROUP, :, g * BLOCK:(g + 1) * BLOCK] = acc * LOG2E

    key = lax.broadcasted_iota(jnp.int32, (BLOCK, SWA_COLS), 0)
    qry = lax.broadcasted_iota(jnp.int32, (BLOCK, SWA_COLS), 1) & (BLOCK - 1)
    lower = key <= qry
    lower_bf = jnp.where(lower, 1.0, 0.0).astype(BF16)
    bias = bias_ref[kv]
    sink = sink_ref[...] * LOG2E

    def band_scores(q_ref, step):
        for b in range(SWA_NB):
            blk = step * SWA_NB + b
            qt = jnp.concatenate([q_ref[g, :, b * BLOCK:(b + 1) * BLOCK] for g in range(SWA_GROUP)],
                                 axis=1)
            cur = pl.multiple_of(blk * BLOCK, BLOCK)
            prev = pl.multiple_of(jnp.maximum(blk - 1, 0) * BLOCK, BLOCK)
            k_band = jnp.concatenate([k_ref[pl.ds(prev, BLOCK), :], k_ref[pl.ds(cur, BLOCK), :]], axis=0)
            s_ref[b] = _dot(k_band, qt)

    @pl.when(n == 0)
    def _():
        band_scores(qt_ref, 0)

    bands = [s_ref[b] for b in range(SWA_NB)]
    band_scores(qt_next_ref, jnp.minimum(n + 1, pl.num_programs(1) - 1))

    for b in range(SWA_NB):
        blk = n * SWA_NB + b
        prev_blk = jnp.maximum(blk - 1, 0)
        s_band = bands[b]
        s = jnp.where(lower, s_band[BLOCK:], s_band[:BLOCK]) * (SWA_SCALE * LOG2E) + bias
        if b == 0:
            s = jnp.where(lower | (blk > 0), s, NEG_BIG)

        m = jnp.maximum(jnp.max(s, axis=0, keepdims=True), sink)
        e = jnp.exp2(s - m)
        denom = jnp.sum(e, axis=0, keepdims=True) + jnp.exp2(sink - m)
        p = (e * (1.0 / denom)).astype(BF16)
        p_cur = p * lower_bf
        p_prev = p - p_cur
        ot = _dot(vt_ref[blk], p_cur) + _dot(vt_ref[prev_blk], p_prev)
        o_ref[b * BLOCK:(b + 1) * BLOCK, :] = jnp.concatenate(
            [ot[:, g * BLOCK:(g + 1) * BLOCK] for g in range(SWA_GROUP)], axis=0).T


def _swa_attn(rel_bias, bucket_tbl, qst, ks, vst, sink_rows):
    s = ks.shape[1]
    rows = SWA_NB * BLOCK
    n_steps = s // rows
    est = (4 * SWA_GROUP * SWA_HEAD_DIM * rows * 2 + 2 * s * V7X_LANES * 2 + 2 * s * SWA_HEAD_DIM * 2
           + 2 * rows * SWA_GROUP * SWA_HEAD_DIM * 4 + SWA_KV_HEADS * BLOCK * SWA_COLS * 4
           + SWA_NB * 2 * BLOCK * SWA_COLS * 4 + 10 * SWA_NB * BLOCK * SWA_COLS * 4)
    return pl.pallas_call(
        _swa_kernel,
        grid=(SWA_KV_HEADS, n_steps),
        in_specs=[
            pl.BlockSpec(memory_space=pltpu.SMEM),
            pl.BlockSpec((BLOCK, BLOCK), lambda kv, n: (0, 0)),
            pl.BlockSpec((SWA_GROUP, SWA_HEAD_DIM, rows), lambda kv, n: (kv, 0, n)),
            pl.BlockSpec((SWA_GROUP, SWA_HEAD_DIM, rows),
                         lambda kv, n: (kv, 0, jnp.minimum(n + 1, n_steps - 1))),
            pl.BlockSpec((None, s, SWA_HEAD_DIM), lambda kv, n: (kv, 0, 0)),
            pl.BlockSpec((None, s // BLOCK, SWA_HEAD_DIM, BLOCK), lambda kv, n: (kv, 0, 0, 0)),
            pl.BlockSpec((None, 1, SWA_COLS), lambda kv, n: (kv, 0, 0)),
        ],
        out_specs=pl.BlockSpec((rows, SWA_GROUP * SWA_HEAD_DIM), lambda kv, n: (n, kv)),
        out_shape=jax.ShapeDtypeStruct((s, SWA_WIDTH), F32),
        scratch_shapes=[pltpu.VMEM((SWA_KV_HEADS, BLOCK, SWA_COLS), F32),
                        pltpu.VMEM((SWA_NB, 2 * BLOCK, SWA_COLS), F32)],
        compiler_params=pltpu.CompilerParams(
            dimension_semantics=("arbitrary", "arbitrary"),
            vmem_limit_bytes=_vmem_limit(est)),
        name="swa_attn",
    )(rel_bias, bucket_tbl, qst, qst, ks, vst, sink_rows)


OUTPROJ_TM = 512
OUTPROJ_CHUNKS = 2


def _out_proj_kernel(am_ref, as_ref, h_ref, mg_ref, sg_ref, w_ref, g_ref, b_ref, o_ref, wb_ref):
    @pl.when(pl.program_id(0) == 0)
    def _():
        wb_ref[...] = w_ref[...].astype(BF16)

    chunk = o_ref.shape[0] // OUTPROJ_CHUNKS
    for c in range(OUTPROJ_CHUNKS):
        rows = slice(c * chunk, (c + 1) * chunk)
        nm = _rms_norm(am_ref[rows, :], mg_ref[...]).astype(BF16)
        ns = _rms_norm(as_ref[rows, :], sg_ref[...]).astype(BF16)
        mixed = _dot(nm, wb_ref[0:MLA_WIDTH, :]) + _dot(ns, wb_ref[MLA_WIDTH:MLA_WIDTH + SWA_WIDTH, :])
        o_ref[rows, :] = _layer_norm(ALPHA * h_ref[rows, :] + mixed, g_ref[...], b_ref[...])


def _out_proj(a_mla, a_swa, h, mla_g, swa_g, w_out, g, b):
    s, d = h.shape
    tm = OUTPROJ_TM
    est = (2 * tm * (MLA_WIDTH + SWA_WIDTH) * 4 + 4 * tm * d * 4 + (MLA_WIDTH + SWA_WIDTH) * d * (4 + 2)
           + 4 * tm * d * 4)
    const = lambda i: (0, 0)
    return pl.pallas_call(
        _out_proj_kernel,
        grid=(s // tm,),
        in_specs=[
            pl.BlockSpec((tm, MLA_WIDTH), lambda i: (i, 0)),
            pl.BlockSpec((tm, SWA_WIDTH), lambda i: (i, 0)),
            pl.BlockSpec((tm, d), lambda i: (i, 0)),
            pl.BlockSpec((1, MLA_WIDTH), const),
            pl.BlockSpec((1, SWA_WIDTH), const),
            pl.BlockSpec((MLA_WIDTH + SWA_WIDTH, d), const, pipeline_mode=pl.Buffered(1)),
            pl.BlockSpec((1, d), const),
            pl.BlockSpec((1, d), const),
        ],
        out_specs=pl.BlockSpec((tm, d), lambda i: (i, 0)),
        out_shape=jax.ShapeDtypeStruct((s, d), F32),
        scratch_shapes=[pltpu.VMEM((MLA_WIDTH + SWA_WIDTH, d), BF16)],
        compiler_params=pltpu.CompilerParams(
            dimension_semantics=("arbitrary",),
            vmem_limit_bytes=_vmem_limit(est)),
        name="out_proj",
    )(a_mla, a_swa, h, mla_g, swa_g, w_out, g, b)


PLE_TM = 512


def _ple_kernel(h_ref, p_ref, wg_ref, wp_ref, o_ref, wgb_ref, wpb_ref):
    @pl.when(pl.program_id(0) == 0)
    def _():
        wgb_ref[...] = wg_ref[...].astype(BF16)
        wpb_ref[...] = wp_ref[...].astype(BF16)

    h = h_ref[...]
    gate = _dot(h.astype(BF16), wgb_ref[...])
    proj = _dot(p_ref[...].astype(BF16), wpb_ref[...])
    o_ref[...] = h + (1.0 / (1.0 + jnp.exp(-gate))) * proj


def _ple(h, p, w_gate, w_proj):
    s, d = h.shape
    tm = PLE_TM
    est = 4 * tm * d * 4 + 2 * tm * PLE_DIM * 4 + (d + PLE_DIM) * d * (4 + 2) + 4 * tm * d * 4
    const = lambda i: (0, 0)
    return pl.pallas_call(
        _ple_kernel,
        grid=(s // tm,),
        in_specs=[
            pl.BlockSpec((tm, d), lambda i: (i, 0)),
            pl.BlockSpec((tm, PLE_DIM), lambda i: (i, 0)),
            pl.BlockSpec((d, d), const, pipeline_mode=pl.Buffered(1)),
            pl.BlockSpec((PLE_DIM, d), const, pipeline_mode=pl.Buffered(1)),
        ],
        out_specs=pl.BlockSpec((tm, d), lambda i: (i, 0)),
        out_shape=jax.ShapeDtypeStruct((s, d), F32),
        scratch_shapes=[pltpu.VMEM((d, d), BF16), pltpu.VMEM((PLE_DIM, d), BF16)],
        compiler_params=pltpu.CompilerParams(
            dimension_semantics=("arbitrary",),
            vmem_limit_bytes=_vmem_limit(est)),
        name="ple",
    )(h, p, w_gate, w_proj)


def _split_w_in(w_in):
    d = w_in.shape[0]
    b_kr = Q_LORA + KV_LORA
    b_qs = b_kr + QK_ROPE
    b_ks = b_qs + SWA_WIDTH
    b_vs = b_ks + SWA_KV_WIDTH
    wz = jnp.concatenate([w_in[:, :b_kr], w_in[:, b_ks:b_vs]], axis=1)
    pad = jnp.zeros((d, V7X_LANES - QK_ROPE), w_in.dtype)
    wt = jnp.concatenate([w_in[:, b_qs:b_ks], w_in[:, b_vs:], w_in[:, b_kr:b_qs], pad], axis=1).T
    return wz, wt


def _pack_w_uq_t(w_uq):
    r = w_uq.shape[0]
    w = w_uq.reshape(r, MLA_HEADS, QK_NOPE + QK_ROPE)
    w = jnp.pad(w, ((0, 0), (0, 0), (0, MLA_QK_PAD - QK_NOPE - QK_ROPE)))
    return w.reshape(r, MLA_HEADS * MLA_QK_PAD).T


def _rope_freq_col():
    freqs = ROPE_THETA ** (-jnp.arange(0, QK_ROPE, 2, dtype=F32) / QK_ROPE)
    return freqs.reshape(HALF_ROPE, 1)


def kernel(x, p, positions, rel_bias, ln1_g, ln1_b, ffn1_w1, ffn1_w3, ffn1_w2, w_in, q_norm_g, w_uq,
           kv_norm_g, w_ukv, swa_sinks, mla_out_g, swa_out_g, w_out, ln2_g, ln2_b, ffn2_w1, ffn2_w3,
           ffn2_w2, ln3_g, ln3_b, ple_w_gate, ple_w_proj):
    assert x.shape == (1, SEQ, D_MODEL) and DEPTH == 1
    row = lambda a: a.reshape(1, -1)
    h = x[0]
    pos_row = positions[0].astype(F32).reshape(1, SEQ)
    freq_col = _rope_freq_col()
    bucket_tbl = jnp.asarray(_swa_bucket_table())

    i = 0
    h = _ffn_ln(h, ffn1_w1[i], ffn1_w3[i], ffn1_w2[i], row(ln1_g[i]), row(ln1_b[i]))

    w_ukv_h = w_ukv[i].reshape(KV_LORA, MLA_HEADS, QK_NOPE + V_HEAD)
    wkn = w_ukv_h[:, :, :QK_NOPE].reshape(KV_LORA, MLA_WIDTH).astype(BF16)
    wv_t = w_ukv_h[:, :, QK_NOPE:].reshape(KV_LORA, MLA_WIDTH).T.astype(BF16)
    wz, wt = _split_w_in(w_in[i])
    qt, k, vt, qst, ks, vst = _in_proj(
        h, wz.astype(BF16), wt.astype(BF16), row(q_norm_g[i]), row(kv_norm_g[i]),
        _pack_w_uq_t(w_uq[i]).astype(BF16), wkn, wv_t, pos_row, freq_col)

    a_mla = _mla_flash(qt, k, vt)
    sink_rows = jnp.repeat(swa_sinks[i], BLOCK).reshape(SWA_KV_HEADS, 1, SWA_COLS)
    a_swa = _swa_attn(rel_bias, bucket_tbl, qst, ks, vst, sink_rows)

    h = _out_proj(a_mla, a_swa, h, row(mla_out_g[i]), row(swa_out_g[i]), w_out[i],
                  row(ln2_g[i]), row(ln2_b[i]))

    h = _ffn_ln(h, ffn2_w1[i], ffn2_w3[i], ffn2_w2[i], row(ln3_g[i]), row(ln3_b[i]))
    h = _ple(h, p[i, 0], ple_w_gate[i], ple_w_proj[i])
    return h[None]
```

```python
import math

import numpy as np
import jax
import jax.numpy as jnp
from jax import lax
from jax.experimental import pallas as pl
from jax.experimental.pallas import tpu as pltpu

F32 = jnp.float32
BF16 = jnp.bfloat16

D_MODEL = 2048
SEQ = 8192
DEPTH = 1
PLE_DIM = 256
MLA_HEADS = 8
Q_LORA = 512
KV_LORA = 512
QK_NOPE = 128
QK_ROPE = 64
V_HEAD = 128
ROPE_THETA = 10000.0
SWA_HEADS = 16
SWA_KV_HEADS = 2
SWA_GROUP = SWA_HEADS // SWA_KV_HEADS
SWA_HEAD_DIM = 64
WINDOW = 128
BLOCK = 128
REL_BUCKETS = 32
REL_MAX_DIST = 128
D_FF = 5632
ALPHA = (2.0 * DEPTH) ** 0.25
EPS = 1e-5
MLA_WIDTH = MLA_HEADS * V_HEAD
SWA_WIDTH = SWA_HEADS * SWA_HEAD_DIM
SWA_KV_WIDTH = SWA_KV_HEADS * SWA_HEAD_DIM

V7X_LANES = 128
V7X_VMEM_BYTES = 64 * 1024 * 1024
MIB = 1024 * 1024

V7X_BF16_SUBLANES = 16

MLA_QK_PAD = 2 * V7X_LANES
MLA_V_ROWS = V_HEAD + V7X_BF16_SUBLANES
NEG_BIG = float(np.finfo(np.float32).min)
LOG2E = math.log2(math.e)


def _vmem_limit(estimate_bytes):
    return int(min(estimate_bytes + 8 * MIB, V7X_VMEM_BYTES - 4 * MIB))


def _layer_norm(y, g, b):
    mu = jnp.mean(y, axis=-1, keepdims=True)
    yc = y - mu
    var = jnp.mean(yc * yc, axis=-1, keepdims=True)
    return yc * lax.rsqrt(var + EPS) * g + b


def _rms_norm(x, g):
    ms = jnp.mean(x * x, axis=-1, keepdims=True)
    return x * lax.rsqrt(ms + EPS) * g


def _dot(a, b):
    return jnp.dot(a, b, preferred_element_type=F32)


def _dot_nt(a, b):
    return lax.dot_general(a, b, (((1,), (1,)), ((), ())), preferred_element_type=F32)


FFN_TM = 1024
FFN_TF = 256
FFN_ROW_CHUNK = 512
FFN_LN_ROWS = 256


def _ffn_ln_kernel(x_ref, w1_ref, w3_ref, w2_ref, g_ref, b_ref, o_ref, xb_ref, w1b_ref, w3b_ref, w2b_ref):
    f = pl.program_id(1)

    @pl.when(f == 0)
    def _():
        xb_ref[...] = x_ref[...].astype(BF16)
        o_ref[...] = jnp.zeros_like(o_ref)

    w1b_ref[...] = w1_ref[...].astype(BF16)
    w3b_ref[...] = w3_ref[...].astype(BF16)
    w2b_ref[...] = w2_ref[...].astype(BF16)

    for c in range(o_ref.shape[0] // FFN_ROW_CHUNK):
        rows = slice(c * FFN_ROW_CHUNK, (c + 1) * FFN_ROW_CHUNK)
        xb = xb_ref[rows, :]
        gate = _dot(xb, w1b_ref[...])
        up = _dot(xb, w3b_ref[...])
        hidden = gate * (1.0 / (1.0 + jnp.exp(-gate))) * up
        o_ref[rows, :] += _dot(hidden.astype(BF16), w2b_ref[...])

    @pl.when(f == pl.num_programs(1) - 1)
    def _():
        for c in range(o_ref.shape[0] // FFN_LN_ROWS):
            rows = slice(c * FFN_LN_ROWS, (c + 1) * FFN_LN_ROWS)
            y = ALPHA * x_ref[rows, :] + 0.5 * o_ref[rows, :]
            o_ref[rows, :] = _layer_norm(y, g_ref[...], b_ref[...])


def _ffn_ln(x, w1, w3, w2, g, b):
    s, d = x.shape
    dff = w1.shape[1]
    tm, tf = FFN_TM, FFN_TF
    est = (2 * tm * d * 4 + tm * d * 2 + 2 * 3 * d * tf * 4 + 3 * d * tf * 2 + 2 * tm * d * 4
           + 3 * FFN_ROW_CHUNK * tf * 4)
    return pl.pallas_call(
        _ffn_ln_kernel,
        grid=(s // tm, dff // tf),
        in_specs=[
            pl.BlockSpec((tm, d), lambda i, f: (i, 0)),
            pl.BlockSpec((d, tf), lambda i, f: (0, f)),
            pl.BlockSpec((d, tf), lambda i, f: (0, f)),
            pl.BlockSpec((tf, d), lambda i, f: (f, 0)),
            pl.BlockSpec((1, d), lambda i, f: (0, 0)),
            pl.BlockSpec((1, d), lambda i, f: (0, 0)),
        ],
        out_specs=pl.BlockSpec((tm, d), lambda i, f: (i, 0)),
        out_shape=jax.ShapeDtypeStruct((s, d), F32),
        scratch_shapes=[pltpu.VMEM((tm, d), BF16), pltpu.VMEM((d, tf), BF16), pltpu.VMEM((d, tf), BF16),
                        pltpu.VMEM((tf, d), BF16)],
        compiler_params=pltpu.CompilerParams(
            dimension_semantics=("parallel", "arbitrary"),
            vmem_limit_bytes=_vmem_limit(est)),
        name="ffn_ln",
    )(x, w1, w3, w2, g, b)


INPROJ_TM = 512
Z_CQ = 0
Z_CKV = Z_CQ + Q_LORA
Z_KS = Z_CKV + KV_LORA
Z_WIDTH = Z_KS + SWA_KV_WIDTH
ZT_QS = 0
ZT_VS = ZT_QS + SWA_WIDTH
ZT_KR = ZT_VS + SWA_KV_WIDTH
ZT_ROWS = ZT_KR + V7X_LANES
HALF_ROPE = QK_ROPE // 2


def _rope_rows(x1, x2, cos, sin):
    return x1 * cos - x2 * sin, x1 * sin + x2 * cos


def _in_proj_kernel(h_ref, wz_ref, wt_ref, qg_ref, kvg_ref, wuqt_ref, wkn_ref, wvt_ref, pos_ref,
                    freq_ref, qt_ref, k_ref, vt_ref, qst_ref, ks_ref, vst_ref):
    tm = h_ref.shape[0]
    hb = h_ref[...].astype(BF16)
    z = _dot(hb, wz_ref[...])
    zt = _dot_nt(wt_ref[...], hb)

    ang = freq_ref[...] * pos_ref[...]
    cos = jnp.cos(ang)
    sin = jnp.sin(ang)

    cq = _rms_norm(z[:, Z_CQ:Z_CQ + Q_LORA], qg_ref[...]).astype(BF16)
    ckv = _rms_norm(z[:, Z_CKV:Z_CKV + KV_LORA], kvg_ref[...]).astype(BF16)
    qt = _dot_nt(wuqt_ref[...], cq)
    kn = _dot(ckv, wkn_ref[...])
    vt = _dot_nt(wvt_ref[...], ckv)

    kr1, kr2 = _rope_rows(zt[ZT_KR:ZT_KR + HALF_ROPE], zt[ZT_KR + HALF_ROPE:ZT_KR + QK_ROPE], cos, sin)
    k_rope = jnp.concatenate([kr1, kr2, zt[ZT_KR + QK_ROPE:ZT_KR + V7X_LANES]], axis=0).T.astype(BF16)

    for h in range(MLA_HEADS):
        base = h * MLA_QK_PAD
        r0 = base + QK_NOPE
        qt_ref[base:r0, :] = qt[base:r0].astype(BF16)
        q1, q2 = _rope_rows(qt[r0:r0 + HALF_ROPE], qt[r0 + HALF_ROPE:r0 + QK_ROPE], cos, sin)
        qt_ref[r0:r0 + HALF_ROPE, :] = q1.astype(BF16)
        qt_ref[r0 + HALF_ROPE:r0 + QK_ROPE, :] = q2.astype(BF16)
        qt_ref[r0 + QK_ROPE:base + MLA_QK_PAD, :] = qt[r0 + QK_ROPE:base + MLA_QK_PAD].astype(BF16)
        k_ref[h, :, 0:QK_NOPE] = kn[:, h * QK_NOPE:(h + 1) * QK_NOPE].astype(BF16)
        k_ref[h, :, QK_NOPE:MLA_QK_PAD] = k_rope
        vt_ref[h, 0, 0:V_HEAD, :] = vt[h * V_HEAD:(h + 1) * V_HEAD].astype(BF16)
        vt_ref[h, 0, V_HEAD:MLA_V_ROWS, :] = jnp.ones((MLA_V_ROWS - V_HEAD, tm), BF16)

    qst_ref[...] = zt[ZT_QS:ZT_QS + SWA_WIDTH].reshape(SWA_HEADS, SWA_HEAD_DIM, tm).astype(BF16)
    for kv in range(SWA_KV_HEADS):
        lo = Z_KS + kv * SWA_HEAD_DIM
        ks_ref[kv] = z[:, lo:lo + SWA_HEAD_DIM].astype(BF16)
        lo = ZT_VS + kv * SWA_HEAD_DIM
        for c in range(tm // BLOCK):
            vst_ref[kv, c] = zt[lo:lo + SWA_HEAD_DIM, c * BLOCK:(c + 1) * BLOCK].astype(BF16)


def _in_proj(h, wz, wt, q_g, kv_g, wuq_t, wkn, wv_t, pos_row, freq_col):
    s, d = h.shape
    tm = INPROJ_TM
    assert tm == MLA_TK
    est = (2 * tm * d * 4 + 2 * d * (Z_WIDTH + ZT_ROWS) * 2
           + 2 * Q_LORA * (MLA_HEADS * MLA_QK_PAD + 2 * MLA_WIDTH) * 2
           + 2 * tm * (2 * MLA_HEADS * MLA_QK_PAD + MLA_WIDTH + SWA_WIDTH + 4 * V7X_LANES) * 2
           + 4 * tm * (Z_WIDTH + ZT_ROWS + 2 * MLA_HEADS * MLA_QK_PAD) * 4)
    const = lambda i: (0, 0)
    return pl.pallas_call(
        _in_proj_kernel,
        grid=(s // tm,),
        in_specs=[
            pl.BlockSpec((tm, d), lambda i: (i, 0)),
            pl.BlockSpec((d, Z_WIDTH), const),
            pl.BlockSpec((ZT_ROWS, d), const),
            pl.BlockSpec((1, Q_LORA), const),
            pl.BlockSpec((1, KV_LORA), const),
            pl.BlockSpec((MLA_HEADS * MLA_QK_PAD, Q_LORA), const),
            pl.BlockSpec((KV_LORA, MLA_WIDTH), const),
            pl.BlockSpec((MLA_WIDTH, KV_LORA), const),
            pl.BlockSpec((1, tm), lambda i: (0, i)),
            pl.BlockSpec((HALF_ROPE, 1), const),
        ],
        out_specs=[
            pl.BlockSpec((MLA_HEADS * MLA_QK_PAD, tm), lambda i: (0, i)),
            pl.BlockSpec((MLA_HEADS, tm, MLA_QK_PAD), lambda i: (0, i, 0)),
            pl.BlockSpec((MLA_HEADS, 1, MLA_V_ROWS, tm), lambda i: (0, i, 0, 0)),
            pl.BlockSpec((SWA_HEADS, SWA_HEAD_DIM, tm), lambda i: (0, 0, i)),
            pl.BlockSpec((SWA_KV_HEADS, tm, SWA_HEAD_DIM), lambda i: (0, i, 0)),
            pl.BlockSpec((SWA_KV_HEADS, tm // BLOCK, SWA_HEAD_DIM, BLOCK), lambda i: (0, i, 0, 0)),
        ],
        out_shape=[
            jax.ShapeDtypeStruct((MLA_HEADS * MLA_QK_PAD, s), BF16),
            jax.ShapeDtypeStruct((MLA_HEADS, s, MLA_QK_PAD), BF16),
            jax.ShapeDtypeStruct((MLA_HEADS, s // tm, MLA_V_ROWS, tm), BF16),
            jax.ShapeDtypeStruct((SWA_HEADS, SWA_HEAD_DIM, s), BF16),
            jax.ShapeDtypeStruct((SWA_KV_HEADS, s, SWA_HEAD_DIM), BF16),
            jax.ShapeDtypeStruct((SWA_KV_HEADS, s // BLOCK, SWA_HEAD_DIM, BLOCK), BF16),
        ],
        compiler_params=pltpu.CompilerParams(
            dimension_semantics=("parallel",),
            vmem_limit_bytes=_vmem_limit(est)),
        name="in_proj",
    )(h, wz, wt, q_g, kv_g, wuq_t, wkn, wv_t, pos_row, freq_col)


MLA_TQ = 512
MLA_TK = 512
MLA_HB = 2
MLA_SCALE = (QK_NOPE + QK_ROPE) ** -0.5
MLA_SCALE_LOG2E = MLA_SCALE * LOG2E


def _mla_flash_kernel(qt_ref, k_ref, vt_ref, o_ref, s_ref, smax_ref, p_ref, m_ref, corr_ref, acc_ref):
    i = pl.program_id(1)
    tk = MLA_TK
    heads = range(MLA_HB)

    def qk(t):
        start = pl.multiple_of(t * tk, tk)
        for h in heads:
            cols = slice(h * MLA_QK_PAD, (h + 1) * MLA_QK_PAD)
            s = _dot(k_ref[h, pl.ds(start, tk), :], qt_ref[cols, :])
            s_ref[h] = s
            smax_ref[h] = jnp.max(s, axis=0, keepdims=True)

    def softmax(scores, masked):
        for h in heads:
            s, smax = scores[h]
            s = s * MLA_SCALE_LOG2E
            if masked:
                key = lax.broadcasted_iota(jnp.int32, s.shape, 0)
                qry = lax.broadcasted_iota(jnp.int32, s.shape, 1)
                s = jnp.where(key <= qry, s, NEG_BIG)
                tile_max = jnp.max(s, axis=0, keepdims=True)
            else:
                tile_max = smax * MLA_SCALE_LOG2E
            m_prev = m_ref[h]
            m_new = jnp.maximum(m_prev, tile_max)
            p_ref[h] = jnp.exp2(s - m_new).astype(BF16)
            corr_ref[h] = jnp.exp2(m_prev - m_new)
            m_ref[h] = m_new

    def pv(t):
        for h in heads:
            acc_ref[h] = corr_ref[h] * acc_ref[h] + _dot(vt_ref[h, t], p_ref[h])

    def load_scores():
        return [(s_ref[h], smax_ref[h]) for h in heads]

    def init():
        m_ref[...] = jnp.full_like(m_ref, NEG_BIG)
        acc_ref[...] = jnp.zeros_like(acc_ref)

    def finish():
        softmax(load_scores(), masked=True)
        pv(i)
        for h in heads:
            o_ref[:, h * V_HEAD:(h + 1) * V_HEAD] = (
                acc_ref[h, 0:V_HEAD, :] / acc_ref[h, V_HEAD:V_HEAD + 1, :]).T

    @pl.when(i == 0)
    def _():
        init()
        qk(0)
        finish()

    @pl.when(i >= 1)
    def _():
        init()
        qk(0)
        scores = load_scores()
        qk(1)
        softmax(scores, masked=False)

        def body(t, carry):
            pv(t)
            scores = load_scores()
            qk(t + 2)
            softmax(scores, masked=False)
            return carry

        lax.fori_loop(0, i - 1, body, 0)
        pv(i - 1)
        finish()


def _mla_flash(qt, k, vt):
    s = k.shape[1]
    tq, tk, hb = MLA_TQ, MLA_TK, MLA_HB
    assert tq == tk
    est = (2 * hb * tq * MLA_QK_PAD * 2 + 2 * hb * s * MLA_QK_PAD * 2 + 2 * hb * s * MLA_V_ROWS * 2
           + 2 * hb * tq * V_HEAD * 4 + hb * tk * tq * (4 + 2) + hb * tq * MLA_V_ROWS * 4
           + 4 * hb * tq * tk * 4)
    return pl.pallas_call(
        _mla_flash_kernel,
        grid=(MLA_HEADS // hb, s // tq),
        in_specs=[
            pl.BlockSpec((hb * MLA_QK_PAD, tq), lambda g, i: (g, i)),
            pl.BlockSpec((hb, s, MLA_QK_PAD), lambda g, i: (g, 0, 0)),
            pl.BlockSpec((hb, s // tk, MLA_V_ROWS, tk), lambda g, i: (g, 0, 0, 0)),
        ],
        out_specs=pl.BlockSpec((tq, hb * V_HEAD), lambda g, i: (i, g)),
        out_shape=jax.ShapeDtypeStruct((s, MLA_WIDTH), F32),
        scratch_shapes=[pltpu.VMEM((hb, tk, tq), F32), pltpu.VMEM((hb, 1, tq), F32),
                        pltpu.VMEM((hb, tk, tq), BF16),
                        pltpu.VMEM((hb, 1, tq), F32), pltpu.VMEM((hb, 1, tq), F32),
                        pltpu.VMEM((hb, MLA_V_ROWS, tq), F32)],
        compiler_params=pltpu.CompilerParams(
            dimension_semantics=("parallel", "arbitrary"),
            vmem_limit_bytes=_vmem_limit(est)),
        name="mla_flash",
    )(qt, k, vt)


SWA_SCALE = SWA_HEAD_DIM ** -0.5
SWA_COLS = SWA_GROUP * BLOCK
SWA_NB = 4


def _t5_bucket_np(dist):
    n = np.maximum(dist, 0)
    max_exact = REL_BUCKETS // 2
    large = max_exact + (np.log(np.maximum(n, 1).astype(np.float32) / max_exact)
                         / math.log(REL_MAX_DIST / max_exact)
                         * (REL_BUCKETS - max_exact)).astype(np.int32)
    large = np.minimum(large, REL_BUCKETS - 1)
    return np.where(n < max_exact, n, large).astype(np.int32)


def _swa_bucket_table():
    j = np.arange(BLOCK)[:, None]
    i = np.arange(BLOCK)[None, :]
    dist = np.where(j <= i, i - j, BLOCK + i - j)
    return _t5_bucket_np(dist)


def _swa_kernel(rb_ref, bkt_ref, qt_ref, qt_next_ref, k_ref, vt_ref, sink_ref, o_ref, bias_ref, s_ref):
    kv = pl.program_id(0)
    n = pl.program_id(1)

    @pl.when((kv == 0) & (n == 0))
    def _():
        bkt = bkt_ref[...]
        for h in range(SWA_HEADS):
            acc = jnp.zeros((BLOCK, BLOCK), F32)
            for b in range(REL_BUCKETS):
                acc = jnp.where(bkt == b, rb_ref[b, h], acc)
            g = h % SWA_GROUP
            bias_ref[h // SWA_GROUP, :, g * BLOCK:(g + 1) * BLOCK] = acc * LOG2E

    key = lax.broadcasted_iota(jnp.int32, (BLOCK, SWA_COLS), 0)
    qry = lax.broadcasted_iota(jnp.int32, (BLOCK, SWA_COLS), 1) & (BLOCK - 1)
    lower = key <= qry
    lower_bf = jnp.where(lower, 1.0, 0.0).astype(BF16)
    bias = bias_ref[kv]
    sink = sink_ref[...] * LOG2E

    def band_scores(q_ref, step):
        for b in range(SWA_NB):
            blk = step * SWA_NB + b
            qt = jnp.concatenate([q_ref[g, :, b * BLOCK:(b + 1) * BLOCK] for g in range(SWA_GROUP)],
                                 axis=1)
            cur = pl.multiple_of(blk * BLOCK, BLOCK)
            prev = pl.multiple_of(jnp.maximum(blk - 1, 0) * BLOCK, BLOCK)
            k_band = jnp.concatenate([k_ref[pl.ds(prev, BLOCK), :], k_ref[pl.ds(cur, BLOCK), :]], axis=0)
            s_ref[b] = _dot(k_band, qt)

    @pl.when(n == 0)
    def _():
        band_scores(qt_ref, 0)

    bands = [s_ref[b] for b in range(SWA_NB)]
    band_scores(qt_next_ref, jnp.minimum(n + 1, pl.num_programs(1) - 1))

    for b in range(SWA_NB):
        blk = n * SWA_NB + b
        prev_blk = jnp.maximum(blk - 1, 0)
        s_band = bands[b]
        s = jnp.where(lower, s_band[BLOCK:], s_band[:BLOCK]) * (SWA_SCALE * LOG2E) + bias
        if b == 0:
            s = jnp.where(lower | (blk > 0), s, NEG_BIG)

        m = jnp.maximum(jnp.max(s, axis=0, keepdims=True), sink)
        e = jnp.exp2(s - m)
        denom = jnp.sum(e, axis=0, keepdims=True) + jnp.exp2(sink - m)
        p = (e * (1.0 / denom)).astype(BF16)
        p_cur = p * lower_bf
        p_prev = p - p_cur
        ot = _dot(vt_ref[blk], p_cur) + _dot(vt_ref[prev_blk], p_prev)
        o_ref[b * BLOCK:(b + 1) * BLOCK, :] = jnp.concatenate(
            [ot[:, g * BLOCK:(g + 1) * BLOCK] for g in range(SWA_GROUP)], axis=0).T


def _swa_attn(rel_bias, bucket_tbl, qst, ks, vst, sink_rows):
    s = ks.shape[1]
    rows = SWA_NB * BLOCK
    n_steps = s // rows
    est = (4 * SWA_GROUP * SWA_HEAD_DIM * rows * 2 + 2 * s * V7X_LANES * 2 + 2 * s * SWA_HEAD_DIM * 2
           + 2 * rows * SWA_GROUP * SWA_HEAD_DIM * 4 + SWA_KV_HEADS * BLOCK * SWA_COLS * 4
           + SWA_NB * 2 * BLOCK * SWA_COLS * 4 + 10 * SWA_NB * BLOCK * SWA_COLS * 4)
    return pl.pallas_call(
        _swa_kernel,
        grid=(SWA_KV_HEADS, n_steps),
        in_specs=[
            pl.BlockSpec(memory_space=pltpu.SMEM),
            pl.BlockSpec((BLOCK, BLOCK), lambda kv, n: (0, 0)),
            pl.BlockSpec((SWA_GROUP, SWA_HEAD_DIM, rows), lambda kv, n: (kv, 0, n)),
            pl.BlockSpec((SWA_GROUP, SWA_HEAD_DIM, rows),
                         lambda kv, n: (kv, 0, jnp.minimum(n + 1, n_steps - 1))),
            pl.BlockSpec((None, s, SWA_HEAD_DIM), lambda kv, n: (kv, 0, 0)),
            pl.BlockSpec((None, s // BLOCK, SWA_HEAD_DIM, BLOCK), lambda kv, n: (kv, 0, 0, 0)),
            pl.BlockSpec((None, 1, SWA_COLS), lambda kv, n: (kv, 0, 0)),
        ],
        out_specs=pl.BlockSpec((rows, SWA_GROUP * SWA_HEAD_DIM), lambda kv, n: (n, kv)),
        out_shape=jax.ShapeDtypeStruct((s, SWA_WIDTH), F32),
        scratch_shapes=[pltpu.VMEM((SWA_KV_HEADS, BLOCK, SWA_COLS), F32),
                        pltpu.VMEM((SWA_NB, 2 * BLOCK, SWA_COLS), F32)],
        compiler_params=pltpu.CompilerParams(
            dimension_semantics=("arbitrary", "arbitrary"),
            vmem_limit_bytes=_vmem_limit(est)),
        name="swa_attn",
    )(rel_bias, bucket_tbl, qst, qst, ks, vst, sink_rows)


OUTPROJ_TM = 512
OUTPROJ_CHUNKS = 2


def _out_proj_kernel(am_ref, as_ref, h_ref, mg_ref, sg_ref, w_ref, g_ref, b_ref, o_ref, wb_ref):
    @pl.when(pl.program_id(0) == 0)
    def _():
        wb_ref[...] = w_ref[...].astype(BF16)

    chunk = o_ref.shape[0] // OUTPROJ_CHUNKS
    for c in range(OUTPROJ_CHUNKS):
        rows = slice(c * chunk, (c + 1) * chunk)
        nm = _rms_norm(am_ref[rows, :], mg_ref[...]).astype(BF16)
        ns = _rms_norm(as_ref[rows, :], sg_ref[...]).astype(BF16)
        mixed = _dot(nm, wb_ref[0:MLA_WIDTH, :]) + _dot(ns, wb_ref[MLA_WIDTH:MLA_WIDTH + SWA_WIDTH, :])
        o_ref[rows, :] = _layer_norm(ALPHA * h_ref[rows, :] + mixed, g_ref[...], b_ref[...])


def _out_proj(a_mla, a_swa, h, mla_g, swa_g, w_out, g, b):
    s, d = h.shape
    tm = OUTPROJ_TM
    est = (2 * tm * (MLA_WIDTH + SWA_WIDTH) * 4 + 4 * tm * d * 4 + (MLA_WIDTH + SWA_WIDTH) * d * (4 + 2)
           + 4 * tm * d * 4)
    const = lambda i: (0, 0)
    return pl.pallas_call(
        _out_proj_kernel,
        grid=(s // tm,),
        in_specs=[
            pl.BlockSpec((tm, MLA_WIDTH), lambda i: (i, 0)),
            pl.BlockSpec((tm, SWA_WIDTH), lambda i: (i, 0)),
            pl.BlockSpec((tm, d), lambda i: (i, 0)),
            pl.BlockSpec((1, MLA_WIDTH), const),
            pl.BlockSpec((1, SWA_WIDTH), const),
            pl.BlockSpec((MLA_WIDTH + SWA_WIDTH, d), const, pipeline_mode=pl.Buffered(1)),
            pl.BlockSpec((1, d), const),
            pl.BlockSpec((1, d), const),
        ],
        out_specs=pl.BlockSpec((tm, d), lambda i: (i, 0)),
        out_shape=jax.ShapeDtypeStruct((s, d), F32),
        scratch_shapes=[pltpu.VMEM((MLA_WIDTH + SWA_WIDTH, d), BF16)],
        compiler_params=pltpu.CompilerParams(
            dimension_semantics=("arbitrary",),
            vmem_limit_bytes=_vmem_limit(est)),
        name="out_proj",
    )(a_mla, a_swa, h, mla_g, swa_g, w_out, g, b)


PLE_TM = 512


def _ple_kernel(h_ref, p_ref, wg_ref, wp_ref, o_ref, wgb_ref, wpb_ref):
    @pl.when(pl.program_id(0) == 0)
    def _():
        wgb_ref[...] = wg_ref[...].astype(BF16)
        wpb_ref[...] = wp_ref[...].astype(BF16)

    h = h_ref[...]
    gate = _dot(h.astype(BF16), wgb_ref[...])
    proj = _dot(p_ref[...].astype(BF16), wpb_ref[...])
    o_ref[...] = h + (1.0 / (1.0 + jnp.exp(-gate))) * proj


def _ple(h, p, w_gate, w_proj):
    s, d = h.shape
    tm = PLE_TM
    est = 4 * tm * d * 4 + 2 * tm * PLE_DIM * 4 + (d + PLE_DIM) * d * (4 + 2) + 4 * tm * d * 4
    const = lambda i: (0, 0)
    return pl.pallas_call(
        _ple_kernel,
        grid=(s // tm,),
        in_specs=[
            pl.BlockSpec((tm, d), lambda i: (i, 0)),
            pl.BlockSpec((tm, PLE_DIM), lambda i: (i, 0)),
            pl.BlockSpec((d, d), const, pipeline_mode=pl.Buffered(1)),
            pl.BlockSpec((PLE_DIM, d), const, pipeline_mode=pl.Buffered(1)),
        ],
        out_specs=pl.BlockSpec((tm, d), lambda i: (i, 0)),
        out_shape=jax.ShapeDtypeStruct((s, d), F32),
        scratch_shapes=[pltpu.VMEM((d, d), BF16), pltpu.VMEM((PLE_DIM, d), BF16)],
        compiler_params=pltpu.CompilerParams(
            dimension_semantics=("arbitrary",),
            vmem_limit_bytes=_vmem_limit(est)),
        name="ple",
    )(h, p, w_gate, w_proj)


def _split_w_in(w_in):
    d = w_in.shape[0]
    b_kr = Q_LORA + KV_LORA
    b_qs = b_kr + QK_ROPE
    b_ks = b_qs + SWA_WIDTH
    b_vs = b_ks + SWA_KV_WIDTH
    wz = jnp.concatenate([w_in[:, :b_kr], w_in[:, b_ks:b_vs]], axis=1)
    pad = jnp.zeros((d, V7X_LANES - QK_ROPE), w_in.dtype)
    wt = jnp.concatenate([w_in[:, b_qs:b_ks], w_in[:, b_vs:], w_in[:, b_kr:b_qs], pad], axis=1).T
    return wz, wt


def _pack_w_uq_t(w_uq):
    r = w_uq.shape[0]
    w = w_uq.reshape(r, MLA_HEADS, QK_NOPE + QK_ROPE)
    w = jnp.pad(w, ((0, 0), (0, 0), (0, MLA_QK_PAD - QK_NOPE - QK_ROPE)))
    return w.reshape(r, MLA_HEADS * MLA_QK_PAD).T


def _rope_freq_col():
    freqs = ROPE_THETA ** (-jnp.arange(0, QK_ROPE, 2, dtype=F32) / QK_ROPE)
    return freqs.reshape(HALF_ROPE, 1)


def kernel(x, p, positions, rel_bias, ln1_g, ln1_b, ffn1_w1, ffn1_w3, ffn1_w2, w_in, q_norm_g, w_uq,
           kv_norm_g, w_ukv, swa_sinks, mla_out_g, swa_out_g, w_out, ln2_g, ln2_b, ffn2_w1, ffn2_w3,
           ffn2_w2, ln3_g, ln3_b, ple_w_gate, ple_w_proj):
    assert x.shape == (1, SEQ, D_MODEL) and DEPTH == 1
    row = lambda a: a.reshape(1, -1)
    h = x[0]
    pos_row = positions[0].astype(F32).reshape(1, SEQ)
    freq_col = _rope_freq_col()
    bucket_tbl = jnp.asarray(_swa_bucket_table())

    i = 0
    h = _ffn_ln(h, ffn1_w1[i], ffn1_w3[i], ffn1_w2[i], row(ln1_g[i]), row(ln1_b[i]))

    w_ukv_h = w_ukv[i].reshape(KV_LORA, MLA_HEADS, QK_NOPE + V_HEAD)
    wkn = w_ukv_h[:, :, :QK_NOPE].reshape(KV_LORA, MLA_WIDTH).astype(BF16)
    wv_t = w_ukv_h[:, :, QK_NOPE:].reshape(KV_LORA, MLA_WIDTH).T.astype(BF16)
    wz, wt = _split_w_in(w_in[i])
    qt, k, vt, qst, ks, vst = _in_proj(
        h, wz.astype(BF16), wt.astype(BF16), row(q_norm_g[i]), row(kv_norm_g[i]),
        _pack_w_uq_t(w_uq[i]).astype(BF16), wkn, wv_t, pos_row, freq_col)

    a_mla = _mla_flash(qt, k, vt)
    sink_rows = jnp.repeat(swa_sinks[i], BLOCK).reshape(SWA_KV_HEADS, 1, SWA_COLS)
    a_swa = _swa_attn(rel_bias, bucket_tbl, qst, ks, vst, sink_rows)

    h = _out_proj(a_mla, a_swa, h, row(mla_out_g[i]), row(swa_out_g[i]), w_out[i],
                  row(ln2_g[i]), row(ln2_b[i]))

    h = _ffn_ln(h, ffn2_w1[i], ffn2_w3[i], ffn2_w2[i], row(ln3_g[i]), row(ln3_b[i]))
    h = _ple(h, p[i, 0], ple_w_gate[i], ple_w_proj[i])
    return h[None]
```

```python
import math

import numpy as np
import jax
import jax.numpy as jnp
from jax import lax
from jax.experimental import pallas as pl
from jax.experimental.pallas import tpu as pltpu

F32 = jnp.float32
BF16 = jnp.bfloat16

D_MODEL = 2048
SEQ = 8192
DEPTH = 1
PLE_DIM = 256
MLA_HEADS = 8
Q_LORA = 512
KV_LORA = 512
QK_NOPE = 128
QK_ROPE = 64
V_HEAD = 128
ROPE_THETA = 10000.0
SWA_HEADS = 16
SWA_KV_HEADS = 2
SWA_GROUP = SWA_HEADS // SWA_KV_HEADS
SWA_HEAD_DIM = 64
WINDOW = 128
BLOCK = 128
REL_BUCKETS = 32
REL_MAX_DIST = 128
D_FF = 5632
ALPHA = (2.0 * DEPTH) ** 0.25
EPS = 1e-5
MLA_WIDTH = MLA_HEADS * V_HEAD
SWA_WIDTH = SWA_HEADS * SWA_HEAD_DIM
SWA_KV_WIDTH = SWA_KV_HEADS * SWA_HEAD_DIM

V7X_LANES = 128
V7X_VMEM_BYTES = 64 * 1024 * 1024
MIB = 1024 * 1024

V7X_BF16_SUBLANES = 16

MLA_QK_PAD = 2 * V7X_LANES
MLA_V_ROWS = V_HEAD + V7X_BF16_SUBLANES
NEG_BIG = float(np.finfo(np.float32).min)
LOG2E = math.log2(math.e)


def _vmem_limit(estimate_bytes):
    return int(min(estimate_bytes + 8 * MIB, V7X_VMEM_BYTES - 4 * MIB))


def _layer_norm(y, g, b):
    mu = jnp.mean(y, axis=-1, keepdims=True)
    yc = y - mu
    var = jnp.mean(yc * yc, axis=-1, keepdims=True)
    return yc * lax.rsqrt(var + EPS) * g + b


def _rms_norm(x, g):
    ms = jnp.mean(x * x, axis=-1, keepdims=True)
    return x * lax.rsqrt(ms + EPS) * g


def _dot(a, b):
    return jnp.dot(a, b, preferred_element_type=F32)


def _dot_nt(a, b):
    return lax.dot_general(a, b, (((1,), (1,)), ((), ())), preferred_element_type=F32)


FFN_TM = 1024
FFN_TF = 256
FFN_ROW_CHUNK = 512
FFN_LN_ROWS = 256


def _ffn_ln_kernel(x_ref, w1_ref, w3_ref, w2_ref, g_ref, b_ref, o_ref, xb_ref, w1b_ref, w3b_ref, w2b_ref):
    f = pl.program_id(1)

    @pl.when(f == 0)
    def _():
        xb_ref[...] = x_ref[...].astype(BF16)
        o_ref[...] = jnp.zeros_like(o_ref)

    w1b_ref[...] = w1_ref[...].astype(BF16)
    w3b_ref[...] = w3_ref[...].astype(BF16)
    w2b_ref[...] = w2_ref[...].astype(BF16)

    for c in range(o_ref.shape[0] // FFN_ROW_CHUNK):
        rows = slice(c * FFN_ROW_CHUNK, (c + 1) * FFN_ROW_CHUNK)
        xb = xb_ref[rows, :]
        gate = _dot(xb, w1b_ref[...])
        up = _dot(xb, w3b_ref[...])
        hidden = gate * (1.0 / (1.0 + jnp.exp(-gate))) * up
        o_ref[rows, :] += _dot(hidden.astype(BF16), w2b_ref[...])

    @pl.when(f == pl.num_programs(1) - 1)
    def _():
        for c in range(o_ref.shape[0] // FFN_LN_ROWS):
            rows = slice(c * FFN_LN_ROWS, (c + 1) * FFN_LN_ROWS)
            y = ALPHA * x_ref[rows, :] + 0.5 * o_ref[rows, :]
            o_ref[rows, :] = _layer_norm(y, g_ref[...], b_ref[...])


def _ffn_ln(x, w1, w3, w2, g, b):
    s, d = x.shape
    dff = w1.shape[1]
    tm, tf = FFN_TM, FFN_TF
    est = (2 * tm * d * 4 + tm * d * 2 + 2 * 3 * d * tf * 4 + 3 * d * tf * 2 + 2 * tm * d * 4
           + 3 * FFN_ROW_CHUNK * tf * 4)
    return pl.pallas_call(
        _ffn_ln_kernel,
        grid=(s // tm, dff // tf),
        in_specs=[
            pl.BlockSpec((tm, d), lambda i, f: (i, 0)),
            pl.BlockSpec((d, tf), lambda i, f: (0, f)),
            pl.BlockSpec((d, tf), lambda i, f: (0, f)),
            pl.BlockSpec((tf, d), lambda i, f: (f, 0)),
            pl.BlockSpec((1, d), lambda i, f: (0, 0)),
            pl.BlockSpec((1, d), lambda i, f: (0, 0)),
        ],
        out_specs=pl.BlockSpec((tm, d), lambda i, f: (i, 0)),
        out_shape=jax.ShapeDtypeStruct((s, d), F32),
        scratch_shapes=[pltpu.VMEM((tm, d), BF16), pltpu.VMEM((d, tf), BF16), pltpu.VMEM((d, tf), BF16),
                        pltpu.VMEM((tf, d), BF16)],
        compiler_params=pltpu.CompilerParams(
            dimension_semantics=("parallel", "arbitrary"),
            vmem_limit_bytes=_vmem_limit(est)),
        name="ffn_ln",
    )(x, w1, w3, w2, g, b)


INPROJ_TM = 512
Z_CQ = 0
Z_CKV = Z_CQ + Q_LORA
Z_KS = Z_CKV + KV_LORA
Z_WIDTH = Z_KS + SWA_KV_WIDTH
ZT_QS = 0
ZT_VS = ZT_QS + SWA_WIDTH
ZT_KR = ZT_VS + SWA_KV_WIDTH
ZT_ROWS = ZT_KR + V7X_LANES
HALF_ROPE = QK_ROPE // 2


def _rope_rows(x1, x2, cos, sin):
    return x1 * cos - x2 * sin, x1 * sin + x2 * cos


def _in_proj_kernel(h_ref, wz_ref, wt_ref, qg_ref, kvg_ref, wuqt_ref, wkn_ref, wvt_ref, pos_ref,
                    freq_ref, qt_ref, k_ref, vt_ref, qst_ref, ks_ref, vst_ref):
    tm = h_ref.shape[0]
    hb = h_ref[...].astype(BF16)
    z = _dot(hb, wz_ref[...])
    zt = _dot_nt(wt_ref[...], hb)

    ang = freq_ref[...] * pos_ref[...]
    cos = jnp.cos(ang)
    sin = jnp.sin(ang)

    cq = _rms_norm(z[:, Z_CQ:Z_CQ + Q_LORA], qg_ref[...]).astype(BF16)
    ckv = _rms_norm(z[:, Z_CKV:Z_CKV + KV_LORA], kvg_ref[...]).astype(BF16)
    qt = _dot_nt(wuqt_ref[...], cq)
    kn = _dot(ckv, wkn_ref[...])
    vt = _dot_nt(wvt_ref[...], ckv)

    kr1, kr2 = _rope_rows(zt[ZT_KR:ZT_KR + HALF_ROPE], zt[ZT_KR + HALF_ROPE:ZT_KR + QK_ROPE], cos, sin)
    k_rope = jnp.concatenate([kr1, kr2, zt[ZT_KR + QK_ROPE:ZT_KR + V7X_LANES]], axis=0).T.astype(BF16)

    for h in range(MLA_HEADS):
        base = h * MLA_QK_PAD
        r0 = base + QK_NOPE
        qt_ref[base:r0, :] = qt[base:r0].astype(BF16)
        q1, q2 = _rope_rows(qt[r0:r0 + HALF_ROPE], qt[r0 + HALF_ROPE:r0 + QK_ROPE], cos, sin)
        qt_ref[r0:r0 + HALF_ROPE, :] = q1.astype(BF16)
        qt_ref[r0 + HALF_ROPE:r0 + QK_ROPE, :] = q2.astype(BF16)
        qt_ref[r0 + QK_ROPE:base + MLA_QK_PAD, :] = qt[r0 + QK_ROPE:base + MLA_QK_PAD].astype(BF16)
        k_ref[h, :, 0:QK_NOPE] = kn[:, h * QK_NOPE:(h + 1) * QK_NOPE].astype(BF16)
        k_ref[h, :, QK_NOPE:MLA_QK_PAD] = k_rope
        vt_ref[h, 0, 0:V_HEAD, :] = vt[h * V_HEAD:(h + 1) * V_HEAD].astype(BF16)
        vt_ref[h, 0, V_HEAD:MLA_V_ROWS, :] = jnp.ones((MLA_V_ROWS - V_HEAD, tm), BF16)

    qst_ref[...] = zt[ZT_QS:ZT_QS + SWA_WIDTH].reshape(SWA_HEADS, SWA_HEAD_DIM, tm).astype(BF16)
    for kv in range(SWA_KV_HEADS):
        lo = Z_KS + kv * SWA_HEAD_DIM
        ks_ref[kv] = z[:, lo:lo + SWA_HEAD_DIM].astype(BF16)
        lo = ZT_VS + kv * SWA_HEAD_DIM
        for c in range(tm // BLOCK):
            vst_ref[kv, c] = zt[lo:lo + SWA_HEAD_DIM, c * BLOCK:(c + 1) * BLOCK].astype(BF16)


def _in_proj(h, wz, wt, q_g, kv_g, wuq_t, wkn, wv_t, pos_row, freq_col):
    s, d = h.shape
    tm = INPROJ_TM
    assert tm == MLA_TK
    est = (2 * tm * d * 4 + 2 * d * (Z_WIDTH + ZT_ROWS) * 2
           + 2 * Q_LORA * (MLA_HEADS * MLA_QK_PAD + 2 * MLA_WIDTH) * 2
           + 2 * tm * (2 * MLA_HEADS * MLA_QK_PAD + MLA_WIDTH + SWA_WIDTH + 4 * V7X_LANES) * 2
           + 4 * tm * (Z_WIDTH + ZT_ROWS + 2 * MLA_HEADS * MLA_QK_PAD) * 4)
    const = lambda i: (0, 0)
    return pl.pallas_call(
        _in_proj_kernel,
        grid=(s // tm,),
        in_specs=[
            pl.BlockSpec((tm, d), lambda i: (i, 0)),
            pl.BlockSpec((d, Z_WIDTH), const),
            pl.BlockSpec((ZT_ROWS, d), const),
            pl.BlockSpec((1, Q_LORA), const),
            pl.BlockSpec((1, KV_LORA), const),
            pl.BlockSpec((MLA_HEADS * MLA_QK_PAD, Q_LORA), const),
            pl.BlockSpec((KV_LORA, MLA_WIDTH), const),
            pl.BlockSpec((MLA_WIDTH, KV_LORA), const),
            pl.BlockSpec((1, tm), lambda i: (0, i)),
            pl.BlockSpec((HALF_ROPE, 1), const),
        ],
        out_specs=[
            pl.BlockSpec((MLA_HEADS * MLA_QK_PAD, tm), lambda i: (0, i)),
            pl.BlockSpec((MLA_HEADS, tm, MLA_QK_PAD), lambda i: (0, i, 0)),
            pl.BlockSpec((MLA_HEADS, 1, MLA_V_ROWS, tm), lambda i: (0, i, 0, 0)),
            pl.BlockSpec((SWA_HEADS, SWA_HEAD_DIM, tm), lambda i: (0, 0, i)),
            pl.BlockSpec((SWA_KV_HEADS, tm, SWA_HEAD_DIM), lambda i: (0, i, 0)),
            pl.BlockSpec((SWA_KV_HEADS, tm // BLOCK, SWA_HEAD_DIM, BLOCK), lambda i: (0, i, 0, 0)),
        ],
        out_shape=[
            jax.ShapeDtypeStruct((MLA_HEADS * MLA_QK_PAD, s), BF16),
            jax.ShapeDtypeStruct((MLA_HEADS, s, MLA_QK_PAD), BF16),
            jax.ShapeDtypeStruct((MLA_HEADS, s // tm, MLA_V_ROWS, tm), BF16),
            jax.ShapeDtypeStruct((SWA_HEADS, SWA_HEAD_DIM, s), BF16),
            jax.ShapeDtypeStruct((SWA_KV_HEADS, s, SWA_HEAD_DIM), BF16),
            jax.ShapeDtypeStruct((SWA_KV_HEADS, s // BLOCK, SWA_HEAD_DIM, BLOCK), BF16),
        ],
        compiler_params=pltpu.CompilerParams(
            dimension_semantics=("parallel",),
            vmem_limit_bytes=_vmem_limit(est)),
        name="in_proj",
    )(h, wz, wt, q_g, kv_g, wuq_t, wkn, wv_t, pos_row, freq_col)


MLA_TQ = 512
MLA_TK = 512
MLA_ROW_PAD = V7X_LANES
MLA_HB = 2
MLA_SCALE = (QK_NOPE + QK_ROPE) ** -0.5
MLA_SCALE_LOG2E = MLA_SCALE * LOG2E


def _mla_flash_kernel(qt_ref, k_ref, vt_ref, o_ref, s_ref, smax_ref, p_ref, m_ref, corr_ref, acc_ref):
    i = pl.program_id(1)
    tk = MLA_TK
    heads = range(MLA_HB)

    def qk(t):
        start = pl.multiple_of(t * tk, tk)
        for h in heads:
            cols = slice(h * MLA_QK_PAD, (h + 1) * MLA_QK_PAD)
            s = _dot(k_ref[h, pl.ds(start, tk), :], qt_ref[cols, :])
            s_ref[h, :, 0:MLA_TQ] = s
            smax_ref[h] = jnp.max(s, axis=0, keepdims=True)

    def softmax(scores, masked):
        for h in heads:
            s, smax = scores[h]
            s = s * MLA_SCALE_LOG2E
            if masked:
                key = lax.broadcasted_iota(jnp.int32, s.shape, 0)
                qry = lax.broadcasted_iota(jnp.int32, s.shape, 1)
                s = jnp.where(key <= qry, s, NEG_BIG)
                tile_max = jnp.max(s, axis=0, keepdims=True)
            else:
                tile_max = smax * MLA_SCALE_LOG2E
            m_prev = m_ref[h]
            m_new = jnp.maximum(m_prev, tile_max)
            p_ref[h, :, 0:MLA_TQ] = jnp.exp2(s - m_new).astype(BF16)
            corr_ref[h] = jnp.exp2(m_prev - m_new)
            m_ref[h] = m_new

    def pv(t):
        for h in heads:
            acc_ref[h] = corr_ref[h] * acc_ref[h] + _dot(vt_ref[h, t], p_ref[h, :, 0:MLA_TQ])

    def load_scores():
        return [(s_ref[h, :, 0:MLA_TQ], smax_ref[h]) for h in heads]

    def init():
        m_ref[...] = jnp.full_like(m_ref, NEG_BIG)
        acc_ref[...] = jnp.zeros_like(acc_ref)

    def finish():
        softmax(load_scores(), masked=True)
        pv(i)
        for h in heads:
            o_ref[:, h * V_HEAD:(h + 1) * V_HEAD] = (
                acc_ref[h, 0:V_HEAD, :] / acc_ref[h, V_HEAD:V_HEAD + 1, :]).T

    @pl.when(i == 0)
    def _():
        init()
        qk(0)
        finish()

    @pl.when(i >= 1)
    def _():
        init()
        qk(0)
        scores = load_scores()
        qk(1)
        softmax(scores, masked=False)

        def body(t, carry):
            pv(t)
            scores = load_scores()
            qk(t + 2)
            softmax(scores, masked=False)
            return carry

        lax.fori_loop(0, i - 1, body, 0)
        pv(i - 1)
        finish()


def _mla_flash(qt, k, vt):
    s = k.shape[1]
    tq, tk, hb = MLA_TQ, MLA_TK, MLA_HB
    assert tq == tk
    est = (2 * hb * tq * MLA_QK_PAD * 2 + 2 * hb * s * MLA_QK_PAD * 2 + 2 * hb * s * MLA_V_ROWS * 2
           + 2 * hb * tq * V_HEAD * 4 + hb * tk * tq * (4 + 2) + hb * tq * MLA_V_ROWS * 4
           + 4 * hb * tq * tk * 4)
    return pl.pallas_call(
        _mla_flash_kernel,
        grid=(MLA_HEADS // hb, s // tq),
        in_specs=[
            pl.BlockSpec((hb * MLA_QK_PAD, tq), lambda g, i: (g, i)),
            pl.BlockSpec((hb, s, MLA_QK_PAD), lambda g, i: (g, 0, 0)),
            pl.BlockSpec((hb, s // tk, MLA_V_ROWS, tk), lambda g, i: (g, 0, 0, 0)),
        ],
        out_specs=pl.BlockSpec((tq, hb * V_HEAD), lambda g, i: (i, g)),
        out_shape=jax.ShapeDtypeStruct((s, MLA_WIDTH), F32),
        scratch_shapes=[pltpu.VMEM((hb, tk, tq + MLA_ROW_PAD), F32), pltpu.VMEM((hb, 1, tq), F32),
                        pltpu.VMEM((hb, tk, tq + MLA_ROW_PAD), BF16),
                        pltpu.VMEM((hb, 1, tq), F32), pltpu.VMEM((hb, 1, tq), F32),
                        pltpu.VMEM((hb, MLA_V_ROWS, tq), F32)],
        compiler_params=pltpu.CompilerParams(
            dimension_semantics=("parallel", "arbitrary"),
            vmem_limit_bytes=_vmem_limit(est)),
        name="mla_flash",
    )(qt, k, vt)


SWA_SCALE = SWA_HEAD_DIM ** -0.5
SWA_COLS = SWA_GROUP * BLOCK
SWA_NB = 4


def _t5_bucket_np(dist):
    n = np.maximum(dist, 0)
    max_exact = REL_BUCKETS // 2
    large = max_exact + (np.log(np.maximum(n, 1).astype(np.float32) / max_exact)
                         / math.log(REL_MAX_DIST / max_exact)
                         * (REL_BUCKETS - max_exact)).astype(np.int32)
    large = np.minimum(large, REL_BUCKETS - 1)
    return np.where(n < max_exact, n, large).astype(np.int32)


def _swa_bucket_table():
    j = np.arange(BLOCK)[:, None]
    i = np.arange(BLOCK)[None, :]
    dist = np.where(j <= i, i - j, BLOCK + i - j)
    return _t5_bucket_np(dist)


def _swa_kernel(rb_ref, bkt_ref, qt_ref, qt_next_ref, k_ref, vt_ref, sink_ref, o_ref, bias_ref, s_ref):
    kv = pl.program_id(0)
    n = pl.program_id(1)

    @pl.when((kv == 0) & (n == 0))
    def _():
        bkt = bkt_ref[...]
        for h in range(SWA_HEADS):
            acc = jnp.zeros((BLOCK, BLOCK), F32)
            for b in range(REL_BUCKETS):
                acc = jnp.where(bkt == b, rb_ref[b, h], acc)
            g = h % SWA_GROUP
            bias_ref[h // SWA_GROUP, :, g * BLOCK:(g + 1) * BLOCK] = acc * LOG2E

    key = lax.broadcasted_iota(jnp.int32, (BLOCK, SWA_COLS), 0)
    qry = lax.broadcasted_iota(jnp.int32, (BLOCK, SWA_COLS), 1) & (BLOCK - 1)
    lower = key <= qry
    lower_bf = jnp.where(lower, 1.0, 0.0).astype(BF16)
    bias = bias_ref[kv]
    sink = sink_ref[...] * LOG2E

    def band_scores(q_ref, step):
        for b in range(SWA_NB):
            blk = step * SWA_NB + b
            qt = jnp.concatenate([q_ref[g, :, b * BLOCK:(b + 1) * BLOCK] for g in range(SWA_GROUP)],
                                 axis=1)
            cur = pl.multiple_of(blk * BLOCK, BLOCK)
            prev = pl.multiple_of(jnp.maximum(blk - 1, 0) * BLOCK, BLOCK)
            k_band = jnp.concatenate([k_ref[pl.ds(prev, BLOCK), :], k_ref[pl.ds(cur, BLOCK), :]], axis=0)
            s_ref[b] = _dot(k_band, qt)

    @pl.when(n == 0)
    def _():
        band_scores(qt_ref, 0)

    bands = [s_ref[b] for b in range(SWA_NB)]
    band_scores(qt_next_ref, jnp.minimum(n + 1, pl.num_programs(1) - 1))

    for b in range(SWA_NB):
        blk = n * SWA_NB + b
        prev_blk = jnp.maximum(blk - 1, 0)
        s_band = bands[b]
        s = jnp.where(lower, s_band[BLOCK:], s_band[:BLOCK]) * (SWA_SCALE * LOG2E) + bias
        if b == 0:
            s = jnp.where(lower | (blk > 0), s, NEG_BIG)

        m = jnp.maximum(jnp.max(s, axis=0, keepdims=True), sink)
        e = jnp.exp2(s - m)
        denom = jnp.sum(e, axis=0, keepdims=True) + jnp.exp2(sink - m)
        p = (e * (1.0 / denom)).astype(BF16)
        p_cur = p * lower_bf
        p_prev = p - p_cur
        ot = _dot(vt_ref[blk], p_cur) + _dot(vt_ref[prev_blk], p_prev)
        o_ref[b * BLOCK:(b + 1) * BLOCK, :] = jnp.concatenate(
            [ot[:, g * BLOCK:(g + 1) * BLOCK] for g in range(SWA_GROUP)], axis=0).T


def _swa_attn(rel_bias, bucket_tbl, qst, ks, vst, sink_rows):
    s = ks.shape[1]
    rows = SWA_NB * BLOCK
    n_steps = s // rows
    est = (4 * SWA_GROUP * SWA_HEAD_DIM * rows * 2 + 2 * s * V7X_LANES * 2 + 2 * s * SWA_HEAD_DIM * 2
           + 2 * rows * SWA_GROUP * SWA_HEAD_DIM * 4 + SWA_KV_HEADS * BLOCK * SWA_COLS * 4
           + SWA_NB * 2 * BLOCK * SWA_COLS * 4 + 10 * SWA_NB * BLOCK * SWA_COLS * 4)
    return pl.pallas_call(
        _swa_kernel,
        grid=(SWA_KV_HEADS, n_steps),
        in_specs=[
            pl.BlockSpec(memory_space=pltpu.SMEM),
            pl.BlockSpec((BLOCK, BLOCK), lambda kv, n: (0, 0)),
            pl.BlockSpec((SWA_GROUP, SWA_HEAD_DIM, rows), lambda kv, n: (kv, 0, n)),
            pl.BlockSpec((SWA_GROUP, SWA_HEAD_DIM, rows),
                         lambda kv, n: (kv, 0, jnp.minimum(n + 1, n_steps - 1))),
            pl.BlockSpec((None, s, SWA_HEAD_DIM), lambda kv, n: (kv, 0, 0)),
            pl.BlockSpec((None, s // BLOCK, SWA_HEAD_DIM, BLOCK), lambda kv, n: (kv, 0, 0, 0)),
            pl.BlockSpec((None, 1, SWA_COLS), lambda kv, n: (kv, 0, 0)),
        ],
        out_specs=pl.BlockSpec((rows, SWA_GROUP * SWA_HEAD_DIM), lambda kv, n: (n, kv)),
        out_shape=jax.ShapeDtypeStruct((s, SWA_WIDTH), F32),
        scratch_shapes=[pltpu.VMEM((SWA_KV_HEADS, BLOCK, SWA_COLS), F32),
                        pltpu.VMEM((SWA_NB, 2 * BLOCK, SWA_COLS), F32)],
        compiler_params=pltpu.CompilerParams(
            dimension_semantics=("arbitrary", "arbitrary"),
            vmem_limit_bytes=_vmem_limit(est)),
        name="swa_attn",
    )(rel_bias, bucket_tbl, qst, qst, ks, vst, sink_rows)


OUTPROJ_TM = 512
OUTPROJ_CHUNKS = 2


def _out_proj_kernel(am_ref, as_ref, h_ref, mg_ref, sg_ref, w_ref, g_ref, b_ref, o_ref, wb_ref):
    @pl.when(pl.program_id(0) == 0)
    def _():
        wb_ref[...] = w_ref[...].astype(BF16)

    chunk = o_ref.shape[0] // OUTPROJ_CHUNKS
    for c in range(OUTPROJ_CHUNKS):
        rows = slice(c * chunk, (c + 1) * chunk)
        nm = _rms_norm(am_ref[rows, :], mg_ref[...]).astype(BF16)
        ns = _rms_norm(as_ref[rows, :], sg_ref[...]).astype(BF16)
        mixed = _dot(nm, wb_ref[0:MLA_WIDTH, :]) + _dot(ns, wb_ref[MLA_WIDTH:MLA_WIDTH + SWA_WIDTH, :])
        o_ref[rows, :] = _layer_norm(ALPHA * h_ref[rows, :] + mixed, g_ref[...], b_ref[...])


def _out_proj(a_mla, a_swa, h, mla_g, swa_g, w_out, g, b):
    s, d = h.shape
    tm = OUTPROJ_TM
    est = (2 * tm * (MLA_WIDTH + SWA_WIDTH) * 4 + 4 * tm * d * 4 + (MLA_WIDTH + SWA_WIDTH) * d * (4 + 2)
           + 4 * tm * d * 4)
    const = lambda i: (0, 0)
    return pl.pallas_call(
        _out_proj_kernel,
        grid=(s // tm,),
        in_specs=[
            pl.BlockSpec((tm, MLA_WIDTH), lambda i: (i, 0)),
            pl.BlockSpec((tm, SWA_WIDTH), lambda i: (i, 0)),
            pl.BlockSpec((tm, d), lambda i: (i, 0)),
            pl.BlockSpec((1, MLA_WIDTH), const),
            pl.BlockSpec((1, SWA_WIDTH), const),
            pl.BlockSpec((MLA_WIDTH + SWA_WIDTH, d), const, pipeline_mode=pl.Buffered(1)),
            pl.BlockSpec((1, d), const),
            pl.BlockSpec((1, d), const),
        ],
        out_specs=pl.BlockSpec((tm, d), lambda i: (i, 0)),
        out_shape=jax.ShapeDtypeStruct((s, d), F32),
        scratch_shapes=[pltpu.VMEM((MLA_WIDTH + SWA_WIDTH, d), BF16)],
        compiler_params=pltpu.CompilerParams(
            dimension_semantics=("arbitrary",),
            vmem_limit_bytes=_vmem_limit(est)),
        name="out_proj",
    )(a_mla, a_swa, h, mla_g, swa_g, w_out, g, b)


PLE_TM = 512


def _ple_kernel(h_ref, p_ref, wg_ref, wp_ref, o_ref, wgb_ref, wpb_ref):
    @pl.when(pl.program_id(0) == 0)
    def _():
        wgb_ref[...] = wg_ref[...].astype(BF16)
        wpb_ref[...] = wp_ref[...].astype(BF16)

    h = h_ref[...]
    gate = _dot(h.astype(BF16), wgb_ref[...])
    proj = _dot(p_ref[...].astype(BF16), wpb_ref[...])
    o_ref[...] = h + (1.0 / (1.0 + jnp.exp(-gate))) * proj


def _ple(h, p, w_gate, w_proj):
    s, d = h.shape
    tm = PLE_TM
    est = 4 * tm * d * 4 + 2 * tm * PLE_DIM * 4 + (d + PLE_DIM) * d * (4 + 2) + 4 * tm * d * 4
    const = lambda i: (0, 0)
    return pl.pallas_call(
        _ple_kernel,
        grid=(s // tm,),
        in_specs=[
            pl.BlockSpec((tm, d), lambda i: (i, 0)),
            pl.BlockSpec((tm, PLE_DIM), lambda i: (i, 0)),
            pl.BlockSpec((d, d), const, pipeline_mode=pl.Buffered(1)),
            pl.BlockSpec((PLE_DIM, d), const, pipeline_mode=pl.Buffered(1)),
        ],
        out_specs=pl.BlockSpec((tm, d), lambda i: (i, 0)),
        out_shape=jax.ShapeDtypeStruct((s, d), F32),
        scratch_shapes=[pltpu.VMEM((d, d), BF16), pltpu.VMEM((PLE_DIM, d), BF16)],
        compiler_params=pltpu.CompilerParams(
            dimension_semantics=("arbitrary",),
            vmem_limit_bytes=_vmem_limit(est)),
        name="ple",
    )(h, p, w_gate, w_proj)


def _split_w_in(w_in):
    d = w_in.shape[0]
    b_kr = Q_LORA + KV_LORA
    b_qs = b_kr + QK_ROPE
    b_ks = b_qs + SWA_WIDTH
    b_vs = b_ks + SWA_KV_WIDTH
    wz = jnp.concatenate([w_in[:, :b_kr], w_in[:, b_ks:b_vs]], axis=1)
    pad = jnp.zeros((d, V7X_LANES - QK_ROPE), w_in.dtype)
    wt = jnp.concatenate([w_in[:, b_qs:b_ks], w_in[:, b_vs:], w_in[:, b_kr:b_qs], pad], axis=1).T
    return wz, wt


def _pack_w_uq_t(w_uq):
    r = w_uq.shape[0]
    w = w_uq.reshape(r, MLA_HEADS, QK_NOPE + QK_ROPE)
    w = jnp.pad(w, ((0, 0), (0, 0), (0, MLA_QK_PAD - QK_NOPE - QK_ROPE)))
    return w.reshape(r, MLA_HEADS * MLA_QK_PAD).T


def _rope_freq_col():
    freqs = ROPE_THETA ** (-jnp.arange(0, QK_ROPE, 2, dtype=F32) / QK_ROPE)
    return freqs.reshape(HALF_ROPE, 1)


def kernel(x, p, positions, rel_bias, ln1_g, ln1_b, ffn1_w1, ffn1_w3, ffn1_w2, w_in, q_norm_g, w_uq,
           kv_norm_g, w_ukv, swa_sinks, mla_out_g, swa_out_g, w_out, ln2_g, ln2_b, ffn2_w1, ffn2_w3,
           ffn2_w2, ln3_g, ln3_b, ple_w_gate, ple_w_proj):
    assert x.shape == (1, SEQ, D_MODEL) and DEPTH == 1
    row = lambda a: a.reshape(1, -1)
    h = x[0]
    pos_row = positions[0].astype(F32).reshape(1, SEQ)
    freq_col = _rope_freq_col()
    bucket_tbl = jnp.asarray(_swa_bucket_table())

    i = 0
    h = _ffn_ln(h, ffn1_w1[i], ffn1_w3[i], ffn1_w2[i], row(ln1_g[i]), row(ln1_b[i]))

    w_ukv_h = w_ukv[i].reshape(KV_LORA, MLA_HEADS, QK_NOPE + V_HEAD)
    wkn = w_ukv_h[:, :, :QK_NOPE].reshape(KV_LORA, MLA_WIDTH).astype(BF16)
    wv_t = w_ukv_h[:, :, QK_NOPE:].reshape(KV_LORA, MLA_WIDTH).T.astype(BF16)
    wz, wt = _split_w_in(w_in[i])
    qt, k, vt, qst, ks, vst = _in_proj(
        h, wz.astype(BF16), wt.astype(BF16), row(q_norm_g[i]), row(kv_norm_g[i]),
        _pack_w_uq_t(w_uq[i]).astype(BF16), wkn, wv_t, pos_row, freq_col)

    a_mla = _mla_flash(qt, k, vt)
    sink_rows = jnp.repeat(swa_sinks[i], BLOCK).reshape(SWA_KV_HEADS, 1, SWA_COLS)
    a_swa = _swa_attn(rel_bias, bucket_tbl, qst, ks, vst, sink_rows)

    h = _out_proj(a_mla, a_swa, h, row(mla_out_g[i]), row(swa_out_g[i]), w_out[i],
                  row(ln2_g[i]), row(ln2_b[i]))

    h = _ffn_ln(h, ffn2_w1[i], ffn2_w3[i], ffn2_w2[i], row(ln3_g[i]), row(ln3_b[i]))
    h = _ple(h, p[i, 0], ple_w_gate[i], ple_w_proj[i])
    return h[None]
```

```python
import math

import numpy as np
import jax
import jax.numpy as jnp
from jax import lax
from jax.experimental import pallas as pl
from jax.experimental.pallas import tpu as pltpu

F32 = jnp.float32
BF16 = jnp.bfloat16

D_MODEL = 2048
SEQ = 8192
DEPTH = 1
PLE_DIM = 256
MLA_HEADS = 8
Q_LORA = 512
KV_LORA = 512
QK_NOPE = 128
QK_ROPE = 64
V_HEAD = 128
ROPE_THETA = 10000.0
SWA_HEADS = 16
SWA_KV_HEADS = 2
SWA_GROUP = SWA_HEADS // SWA_KV_HEADS
SWA_HEAD_DIM = 64
WINDOW = 128
BLOCK = 128
REL_BUCKETS = 32
REL_MAX_DIST = 128
D_FF = 5632
ALPHA = (2.0 * DEPTH) ** 0.25
EPS = 1e-5
MLA_WIDTH = MLA_HEADS * V_HEAD
SWA_WIDTH = SWA_HEADS * SWA_HEAD_DIM
SWA_KV_WIDTH = SWA_KV_HEADS * SWA_HEAD_DIM

V7X_LANES = 128
V7X_VMEM_BYTES = 64 * 1024 * 1024
MIB = 1024 * 1024

V7X_BF16_SUBLANES = 16
MXU_COLS = 256

MLA_QK_PAD = 2 * V7X_LANES
MLA_V_ROWS = V_HEAD + V7X_BF16_SUBLANES
NEG_BIG = float(np.finfo(np.float32).min)
LOG2E = math.log2(math.e)


def _vmem_limit(estimate_bytes):
    return int(min(estimate_bytes + 8 * MIB, V7X_VMEM_BYTES - 4 * MIB))


def _layer_norm(y, g, b):
    mu = jnp.mean(y, axis=-1, keepdims=True)
    yc = y - mu
    var = jnp.mean(yc * yc, axis=-1, keepdims=True)
    return yc * lax.rsqrt(var + EPS) * g + b


def _rms_norm(x, g):
    ms = jnp.mean(x * x, axis=-1, keepdims=True)
    return x * lax.rsqrt(ms + EPS) * g


def _dot(a, b):
    return jnp.dot(a, b, preferred_element_type=F32)


def _dot_nt(a, b):
    return lax.dot_general(a, b, (((1,), (1,)), ((), ())), preferred_element_type=F32)


FFN_TM = 1024
FFN_TF = 256
FFN_ROW_CHUNK = 512
FFN_LN_ROWS = 256


def _ffn_ln_kernel(x_ref, w1_ref, w3_ref, w2_ref, g_ref, b_ref, o_ref, xb_ref, w1b_ref, w3b_ref, w2b_ref):
    f = pl.program_id(1)

    @pl.when(f == 0)
    def _():
        xb_ref[...] = x_ref[...].astype(BF16)
        o_ref[...] = jnp.zeros_like(o_ref)

    w1b_ref[...] = w1_ref[...].astype(BF16)
    w3b_ref[...] = w3_ref[...].astype(BF16)
    w2b_ref[...] = w2_ref[...].astype(BF16)

    for c in range(o_ref.shape[0] // FFN_ROW_CHUNK):
        rows = slice(c * FFN_ROW_CHUNK, (c + 1) * FFN_ROW_CHUNK)
        xb = xb_ref[rows, :]
        gate = _dot(xb, w1b_ref[...])
        up = _dot(xb, w3b_ref[...])
        hidden = gate * (1.0 / (1.0 + jnp.exp(-gate))) * up
        o_ref[rows, :] += _dot(hidden.astype(BF16), w2b_ref[...])

    @pl.when(f == pl.num_programs(1) - 1)
    def _():
        for c in range(o_ref.shape[0] // FFN_LN_ROWS):
            rows = slice(c * FFN_LN_ROWS, (c + 1) * FFN_LN_ROWS)
            y = ALPHA * x_ref[rows, :] + 0.5 * o_ref[rows, :]
            o_ref[rows, :] = _layer_norm(y, g_ref[...], b_ref[...])


def _ffn_ln(x, w1, w3, w2, g, b):
    s, d = x.shape
    dff = w1.shape[1]
    tm, tf = FFN_TM, FFN_TF
    est = (2 * tm * d * 4 + tm * d * 2 + 2 * 3 * d * tf * 4 + 3 * d * tf * 2 + 2 * tm * d * 4
           + 3 * FFN_ROW_CHUNK * tf * 4)
    return pl.pallas_call(
        _ffn_ln_kernel,
        grid=(s // tm, dff // tf),
        in_specs=[
            pl.BlockSpec((tm, d), lambda i, f: (i, 0)),
            pl.BlockSpec((d, tf), lambda i, f: (0, f)),
            pl.BlockSpec((d, tf), lambda i, f: (0, f)),
            pl.BlockSpec((tf, d), lambda i, f: (f, 0)),
            pl.BlockSpec((1, d), lambda i, f: (0, 0)),
            pl.BlockSpec((1, d), lambda i, f: (0, 0)),
        ],
        out_specs=pl.BlockSpec((tm, d), lambda i, f: (i, 0)),
        out_shape=jax.ShapeDtypeStruct((s, d), F32),
        scratch_shapes=[pltpu.VMEM((tm, d), BF16), pltpu.VMEM((d, tf), BF16), pltpu.VMEM((d, tf), BF16),
                        pltpu.VMEM((tf, d), BF16)],
        compiler_params=pltpu.CompilerParams(
            dimension_semantics=("parallel", "arbitrary"),
            vmem_limit_bytes=_vmem_limit(est)),
        name="ffn_ln",
    )(x, w1, w3, w2, g, b)


INPROJ_TM = 512
Z_CQ = 0
Z_CKV = Z_CQ + Q_LORA
Z_KS = Z_CKV + KV_LORA
Z_WIDTH = Z_KS + SWA_KV_WIDTH
ZT_QS = 0
ZT_VS = ZT_QS + SWA_WIDTH
ZT_KR = ZT_VS + SWA_KV_WIDTH
ZT_ROWS = ZT_KR + V7X_LANES
HALF_ROPE = QK_ROPE // 2


def _rope_rows(x1, x2, cos, sin):
    return x1 * cos - x2 * sin, x1 * sin + x2 * cos


def _in_proj_kernel(h_ref, wz_ref, wt_ref, qg_ref, kvg_ref, wuqt_ref, wkn_ref, wvt_ref, pos_ref,
                    freq_ref, qt_ref, k_ref, vt_ref, qst_ref, ks_ref, vst_ref):
    tm = h_ref.shape[0]
    hb = h_ref[...].astype(BF16)
    z = _dot(hb, wz_ref[...])
    zt = _dot_nt(wt_ref[...], hb)

    ang = freq_ref[...] * pos_ref[...]
    cos = jnp.cos(ang)
    sin = jnp.sin(ang)

    cq = _rms_norm(z[:, Z_CQ:Z_CQ + Q_LORA], qg_ref[...]).astype(BF16)
    ckv = _rms_norm(z[:, Z_CKV:Z_CKV + KV_LORA], kvg_ref[...]).astype(BF16)
    qt = _dot_nt(wuqt_ref[...], cq)
    kn = _dot(ckv, wkn_ref[...])
    vt = _dot_nt(wvt_ref[...], ckv)

    kr1, kr2 = _rope_rows(zt[ZT_KR:ZT_KR + HALF_ROPE], zt[ZT_KR + HALF_ROPE:ZT_KR + QK_ROPE], cos, sin)
    k_rope = jnp.concatenate([kr1, kr2, zt[ZT_KR + QK_ROPE:ZT_KR + V7X_LANES]], axis=0).T.astype(BF16)

    for h in range(MLA_HEADS):
        base = h * MLA_QK_PAD
        r0 = base + QK_NOPE
        qt_ref[base:r0, :] = qt[base:r0].astype(BF16)
        q1, q2 = _rope_rows(qt[r0:r0 + HALF_ROPE], qt[r0 + HALF_ROPE:r0 + QK_ROPE], cos, sin)
        qt_ref[r0:r0 + HALF_ROPE, :] = q1.astype(BF16)
        qt_ref[r0 + HALF_ROPE:r0 + QK_ROPE, :] = q2.astype(BF16)
        qt_ref[r0 + QK_ROPE:base + MLA_QK_PAD, :] = qt[r0 + QK_ROPE:base + MLA_QK_PAD].astype(BF16)
        k_ref[h, :, 0:QK_NOPE] = kn[:, h * QK_NOPE:(h + 1) * QK_NOPE].astype(BF16)
        k_ref[h, :, QK_NOPE:MLA_QK_PAD] = k_rope
        vt_ref[h, 0, 0:V_HEAD, :] = vt[h * V_HEAD:(h + 1) * V_HEAD].astype(BF16)
        vt_ref[h, 0, V_HEAD:MLA_V_ROWS, :] = jnp.ones((MLA_V_ROWS - V_HEAD, tm), BF16)

    qst_ref[...] = zt[ZT_QS:ZT_QS + SWA_WIDTH].reshape(SWA_HEADS, SWA_HEAD_DIM, tm).astype(BF16)
    for kv in range(SWA_KV_HEADS):
        lo = Z_KS + kv * SWA_HEAD_DIM
        ks_ref[kv] = z[:, lo:lo + SWA_HEAD_DIM].astype(BF16)
        lo = ZT_VS + kv * SWA_HEAD_DIM
        for c in range(tm // BLOCK):
            vst_ref[kv, c] = zt[lo:lo + SWA_HEAD_DIM, c * BLOCK:(c + 1) * BLOCK].astype(BF16)


def _in_proj(h, wz, wt, q_g, kv_g, wuq_t, wkn, wv_t, pos_row, freq_col):
    s, d = h.shape
    tm = INPROJ_TM
    assert tm == MLA_TK
    est = (2 * tm * d * 4 + 2 * d * (Z_WIDTH + ZT_ROWS) * 2
           + 2 * Q_LORA * (MLA_HEADS * MLA_QK_PAD + 2 * MLA_WIDTH) * 2
           + 2 * tm * (2 * MLA_HEADS * MLA_QK_PAD + MLA_WIDTH + SWA_WIDTH + 4 * V7X_LANES) * 2
           + 4 * tm * (Z_WIDTH + ZT_ROWS + 2 * MLA_HEADS * MLA_QK_PAD) * 4)
    const = lambda i: (0, 0)
    return pl.pallas_call(
        _in_proj_kernel,
        grid=(s // tm,),
        in_specs=[
            pl.BlockSpec((tm, d), lambda i: (i, 0)),
            pl.BlockSpec((d, Z_WIDTH), const),
            pl.BlockSpec((ZT_ROWS, d), const),
            pl.BlockSpec((1, Q_LORA), const),
            pl.BlockSpec((1, KV_LORA), const),
            pl.BlockSpec((MLA_HEADS * MLA_QK_PAD, Q_LORA), const),
            pl.BlockSpec((KV_LORA, MLA_WIDTH), const),
            pl.BlockSpec((MLA_WIDTH, KV_LORA), const),
            pl.BlockSpec((1, tm), lambda i: (0, i)),
            pl.BlockSpec((HALF_ROPE, 1), const),
        ],
        out_specs=[
            pl.BlockSpec((MLA_HEADS * MLA_QK_PAD, tm), lambda i: (0, i)),
            pl.BlockSpec((MLA_HEADS, tm, MLA_QK_PAD), lambda i: (0, i, 0)),
            pl.BlockSpec((MLA_HEADS, 1, MLA_V_ROWS, tm), lambda i: (0, i, 0, 0)),
            pl.BlockSpec((SWA_HEADS, SWA_HEAD_DIM, tm), lambda i: (0, 0, i)),
            pl.BlockSpec((SWA_KV_HEADS, tm, SWA_HEAD_DIM), lambda i: (0, i, 0)),
            pl.BlockSpec((SWA_KV_HEADS, tm // BLOCK, SWA_HEAD_DIM, BLOCK), lambda i: (0, i, 0, 0)),
        ],
        out_shape=[
            jax.ShapeDtypeStruct((MLA_HEADS * MLA_QK_PAD, s), BF16),
            jax.ShapeDtypeStruct((MLA_HEADS, s, MLA_QK_PAD), BF16),
            jax.ShapeDtypeStruct((MLA_HEADS, s // tm, MLA_V_ROWS, tm), BF16),
            jax.ShapeDtypeStruct((SWA_HEADS, SWA_HEAD_DIM, s), BF16),
            jax.ShapeDtypeStruct((SWA_KV_HEADS, s, SWA_HEAD_DIM), BF16),
            jax.ShapeDtypeStruct((SWA_KV_HEADS, s // BLOCK, SWA_HEAD_DIM, BLOCK), BF16),
        ],
        compiler_params=pltpu.CompilerParams(
            dimension_semantics=("parallel",),
            vmem_limit_bytes=_vmem_limit(est)),
        name="in_proj",
    )(h, wz, wt, q_g, kv_g, wuq_t, wkn, wv_t, pos_row, freq_col)


MLA_TQ = 512
MLA_TK = 512
MLA_HB = 2
MLA_SCALE = (QK_NOPE + QK_ROPE) ** -0.5
MLA_SCALE_LOG2E = MLA_SCALE * LOG2E


def _mla_flash_kernel(qt_ref, k_ref, vt_ref, o_ref, s_ref, smax_ref, p_ref, m_ref, corr_ref, acc_ref):
    i = pl.program_id(1)
    tk = MLA_TK
    heads = range(MLA_HB)

    def qk(t):
        start = pl.multiple_of(t * tk, tk)
        for h in heads:
            cols = slice(h * MLA_QK_PAD, (h + 1) * MLA_QK_PAD)
            s = _dot(k_ref[h, pl.ds(start, tk), :], qt_ref[cols, :])
            s_ref[h] = s
            smax_ref[h] = jnp.max(s, axis=0, keepdims=True)

    def softmax(scores, masked):
        for h in heads:
            s, smax = scores[h]
            s = s * MLA_SCALE_LOG2E
            if masked:
                key = lax.broadcasted_iota(jnp.int32, s.shape, 0)
                qry = lax.broadcasted_iota(jnp.int32, s.shape, 1)
                s = jnp.where(key <= qry, s, NEG_BIG)
                tile_max = jnp.max(s, axis=0, keepdims=True)
            else:
                tile_max = smax * MLA_SCALE_LOG2E
            m_prev = m_ref[h]
            m_new = jnp.maximum(m_prev, tile_max)
            p_ref[h] = jnp.exp2(s - m_new).astype(BF16)
            corr_ref[h] = jnp.exp2(m_prev - m_new)
            m_ref[h] = m_new

    def pv(t):
        for h in heads:
            acc_ref[h] = corr_ref[h] * acc_ref[h] + _dot(vt_ref[h, t], p_ref[h])

    def load_scores():
        return [(s_ref[h], smax_ref[h]) for h in heads]

    def init():
        m_ref[...] = jnp.full_like(m_ref, NEG_BIG)
        acc_ref[...] = jnp.zeros_like(acc_ref)

    def finish():
        softmax(load_scores(), masked=True)
        pv(i)
        for h in heads:
            o_ref[:, h * V_HEAD:(h + 1) * V_HEAD] = (
                acc_ref[h, 0:V_HEAD, :] / acc_ref[h, V_HEAD:V_HEAD + 1, :]).T

    @pl.when(i == 0)
    def _():
        init()
        qk(0)
        finish()

    @pl.when(i >= 1)
    def _():
        init()
        qk(0)
        scores = load_scores()
        qk(1)
        softmax(scores, masked=False)

        def body(t, carry):
            pv(t)
            scores = load_scores()
            qk(t + 2)
            softmax(scores, masked=False)
            return carry

        lax.fori_loop(0, i - 1, body, 0)
        pv(i - 1)
        finish()


def _mla_flash(qt, k, vt):
    s = k.shape[1]
    tq, tk, hb = MLA_TQ, MLA_TK, MLA_HB
    assert tq == tk
    est = (2 * hb * tq * MLA_QK_PAD * 2 + 2 * hb * s * MLA_QK_PAD * 2 + 2 * hb * s * MLA_V_ROWS * 2
           + 2 * hb * tq * V_HEAD * 4 + hb * tk * tq * (4 + 2) + hb * tq * MLA_V_ROWS * 4
           + 4 * hb * tq * tk * 4)
    return pl.pallas_call(
        _mla_flash_kernel,
        grid=(MLA_HEADS // hb, s // tq),
        in_specs=[
            pl.BlockSpec((hb * MLA_QK_PAD, tq), lambda g, i: (g, i)),
            pl.BlockSpec((hb, s, MLA_QK_PAD), lambda g, i: (g, 0, 0)),
            pl.BlockSpec((hb, s // tk, MLA_V_ROWS, tk), lambda g, i: (g, 0, 0, 0)),
        ],
        out_specs=pl.BlockSpec((tq, hb * V_HEAD), lambda g, i: (i, g)),
        out_shape=jax.ShapeDtypeStruct((s, MLA_WIDTH), F32),
        scratch_shapes=[pltpu.VMEM((hb, tk, tq), F32), pltpu.VMEM((hb, 1, tq), F32),
                        pltpu.VMEM((hb, tk, tq), BF16),
                        pltpu.VMEM((hb, 1, tq), F32), pltpu.VMEM((hb, 1, tq), F32),
                        pltpu.VMEM((hb, MLA_V_ROWS, tq), F32)],
        compiler_params=pltpu.CompilerParams(
            dimension_semantics=("parallel", "arbitrary"),
            vmem_limit_bytes=_vmem_limit(est)),
        name="mla_flash",
    )(qt, k, vt)


SWA_SCALE = SWA_HEAD_DIM ** -0.5
SWA_COLS = SWA_GROUP * BLOCK
SWA_NB = 4


def _t5_bucket_np(dist):
    n = np.maximum(dist, 0)
    max_exact = REL_BUCKETS // 2
    large = max_exact + (np.log(np.maximum(n, 1).astype(np.float32) / max_exact)
                         / math.log(REL_MAX_DIST / max_exact)
                         * (REL_BUCKETS - max_exact)).astype(np.int32)
    large = np.minimum(large, REL_BUCKETS - 1)
    return np.where(n < max_exact, n, large).astype(np.int32)


def _swa_bucket_table():
    j = np.arange(BLOCK)[:, None]
    i = np.arange(BLOCK)[None, :]
    dist = np.where(j <= i, i - j, BLOCK + i - j)
    return _t5_bucket_np(dist)


def _swa_kernel(rb_ref, bkt_ref, qt_ref, qt_next_ref, k_ref, vt_ref, sink_ref, o_ref, bias_ref, s_ref):
    kv = pl.program_id(0)
    n = pl.program_id(1)

    @pl.when((kv == 0) & (n == 0))
    def _():
        bkt = bkt_ref[...]
        for h in range(SWA_HEADS):
            acc = jnp.zeros((BLOCK, BLOCK), F32)
            for b in range(REL_BUCKETS):
                acc = jnp.where(bkt == b, rb_ref[b, h], acc)
            g = h % SWA_GROUP
            bias_ref[h // SWA_GROUP, :, g * BLOCK:(g + 1) * BLOCK] = acc * LOG2E

    key = lax.broadcasted_iota(jnp.int32, (BLOCK, SWA_COLS), 0)
    qry = lax.broadcasted_iota(jnp.int32, (BLOCK, SWA_COLS), 1) & (BLOCK - 1)
    lower = key <= qry
    lower_bf = jnp.where(lower, 1.0, 0.0).astype(BF16)
    bias = bias_ref[kv]
    sink = sink_ref[...] * LOG2E

    def band_scores(q_ref, step):
        for b in range(SWA_NB):
            blk = step * SWA_NB + b
            qt = jnp.concatenate([q_ref[g, :, b * BLOCK:(b + 1) * BLOCK] for g in range(SWA_GROUP)],
                                 axis=1)
            cur = pl.multiple_of(blk * BLOCK, BLOCK)
            prev = pl.multiple_of(jnp.maximum(blk - 1, 0) * BLOCK, BLOCK)
            k_band = jnp.concatenate([k_ref[pl.ds(prev, BLOCK), :], k_ref[pl.ds(cur, BLOCK), :]], axis=0)
            s_ref[b] = _dot(k_band, qt)

    @pl.when(n == 0)
    def _():
        band_scores(qt_ref, 0)

    bands = [s_ref[b] for b in range(SWA_NB)]
    band_scores(qt_next_ref, jnp.minimum(n + 1, pl.num_programs(1) - 1))

    for b in range(SWA_NB):
        blk = n * SWA_NB + b
        prev_blk = jnp.maximum(blk - 1, 0)
        s_band = bands[b]
        s = jnp.where(lower, s_band[BLOCK:], s_band[:BLOCK]) * (SWA_SCALE * LOG2E) + bias
        if b == 0:
            s = jnp.where(lower | (blk > 0), s, NEG_BIG)

        m = jnp.maximum(jnp.max(s, axis=0, keepdims=True), sink)
        e = jnp.exp2(s - m)
        denom = jnp.sum(e, axis=0, keepdims=True) + jnp.exp2(sink - m)
        p = (e * (1.0 / denom)).astype(BF16)
        p_cur = p * lower_bf
        p_prev = p - p_cur
        ot = _dot(vt_ref[blk], p_cur) + _dot(vt_ref[prev_blk], p_prev)
        o_ref[b * BLOCK:(b + 1) * BLOCK, :] = jnp.concatenate(
            [ot[:, g * BLOCK:(g + 1) * BLOCK] for g in range(SWA_GROUP)], axis=0).T


def _swa_attn(rel_bias, bucket_tbl, qst, ks, vst, sink_rows):
    s = ks.shape[1]
    rows = SWA_NB * BLOCK
    n_steps = s // rows
    est = (4 * SWA_GROUP * SWA_HEAD_DIM * rows * 2 + 2 * s * V7X_LANES * 2 + 2 * s * SWA_HEAD_DIM * 2
           + 2 * rows * SWA_GROUP * SWA_HEAD_DIM * 4 + SWA_KV_HEADS * BLOCK * SWA_COLS * 4
           + SWA_NB * 2 * BLOCK * SWA_COLS * 4 + 10 * SWA_NB * BLOCK * SWA_COLS * 4)
    return pl.pallas_call(
        _swa_kernel,
        grid=(SWA_KV_HEADS, n_steps),
        in_specs=[
            pl.BlockSpec(memory_space=pltpu.SMEM),
            pl.BlockSpec((BLOCK, BLOCK), lambda kv, n: (0, 0)),
            pl.BlockSpec((SWA_GROUP, SWA_HEAD_DIM, rows), lambda kv, n: (kv, 0, n)),
            pl.BlockSpec((SWA_GROUP, SWA_HEAD_DIM, rows),
                         lambda kv, n: (kv, 0, jnp.minimum(n + 1, n_steps - 1))),
            pl.BlockSpec((None, s, SWA_HEAD_DIM), lambda kv, n: (kv, 0, 0)),
            pl.BlockSpec((None, s // BLOCK, SWA_HEAD_DIM, BLOCK), lambda kv, n: (kv, 0, 0, 0)),
            pl.BlockSpec((None, 1, SWA_COLS), lambda kv, n: (kv, 0, 0)),
        ],
        out_specs=pl.BlockSpec((rows, SWA_GROUP * SWA_HEAD_DIM), lambda kv, n: (n, kv)),
        out_shape=jax.ShapeDtypeStruct((s, SWA_WIDTH), F32),
        scratch_shapes=[pltpu.VMEM((SWA_KV_HEADS, BLOCK, SWA_COLS), F32),
                        pltpu.VMEM((SWA_NB, 2 * BLOCK, SWA_COLS), F32)],
        compiler_params=pltpu.CompilerParams(
            dimension_semantics=("arbitrary", "arbitrary"),
            vmem_limit_bytes=_vmem_limit(est)),
        name="swa_attn",
    )(rel_bias, bucket_tbl, qst, qst, ks, vst, sink_rows)


OUTPROJ_TM = 512
OUTPROJ_CHUNKS = 2


def _out_proj_kernel(am_ref, as_ref, h_ref, mg_ref, sg_ref, w_ref, g_ref, b_ref, o_ref, wb_ref):
    n_col = wb_ref.shape[0]

    @pl.when(pl.program_id(0) == 0)
    def _():
        for j in range(n_col):
            wb_ref[j] = w_ref[:, j * MXU_COLS:(j + 1) * MXU_COLS].astype(BF16)

    chunk = o_ref.shape[0] // OUTPROJ_CHUNKS
    for c in range(OUTPROJ_CHUNKS):
        rows = slice(c * chunk, (c + 1) * chunk)
        nm = _rms_norm(am_ref[rows, :], mg_ref[...]).astype(BF16)
        ns = _rms_norm(as_ref[rows, :], sg_ref[...]).astype(BF16)
        mixed = jnp.concatenate(
            [_dot(nm, wb_ref[j, 0:MLA_WIDTH, :]) + _dot(ns, wb_ref[j, MLA_WIDTH:MLA_WIDTH + SWA_WIDTH, :])
             for j in range(n_col)], axis=1)
        o_ref[rows, :] = _layer_norm(ALPHA * h_ref[rows, :] + mixed, g_ref[...], b_ref[...])


def _out_proj(a_mla, a_swa, h, mla_g, swa_g, w_out, g, b):
    s, d = h.shape
    tm = OUTPROJ_TM
    est = (2 * tm * (MLA_WIDTH + SWA_WIDTH) * 4 + 4 * tm * d * 4 + (MLA_WIDTH + SWA_WIDTH) * d * (4 + 2)
           + 4 * tm * d * 4)
    const = lambda i: (0, 0)
    return pl.pallas_call(
        _out_proj_kernel,
        grid=(s // tm,),
        in_specs=[
            pl.BlockSpec((tm, MLA_WIDTH), lambda i: (i, 0)),
            pl.BlockSpec((tm, SWA_WIDTH), lambda i: (i, 0)),
            pl.BlockSpec((tm, d), lambda i: (i, 0)),
            pl.BlockSpec((1, MLA_WIDTH), const),
            pl.BlockSpec((1, SWA_WIDTH), const),
            pl.BlockSpec((MLA_WIDTH + SWA_WIDTH, d), const, pipeline_mode=pl.Buffered(1)),
            pl.BlockSpec((1, d), const),
            pl.BlockSpec((1, d), const),
        ],
        out_specs=pl.BlockSpec((tm, d), lambda i: (i, 0)),
        out_shape=jax.ShapeDtypeStruct((s, d), F32),
        scratch_shapes=[pltpu.VMEM((d // MXU_COLS, MLA_WIDTH + SWA_WIDTH, MXU_COLS), BF16)],
        compiler_params=pltpu.CompilerParams(
            dimension_semantics=("arbitrary",),
            vmem_limit_bytes=_vmem_limit(est)),
        name="out_proj",
    )(a_mla, a_swa, h, mla_g, swa_g, w_out, g, b)


PLE_TM = 512


def _ple_kernel(h_ref, p_ref, wg_ref, wp_ref, o_ref, wgb_ref, wpb_ref):
    n_col = wgb_ref.shape[0]

    @pl.when(pl.program_id(0) == 0)
    def _():
        for c in range(n_col):
            cols = slice(c * MXU_COLS, (c + 1) * MXU_COLS)
            wgb_ref[c] = wg_ref[:, cols].astype(BF16)
            wpb_ref[c] = wp_ref[:, cols].astype(BF16)

    h = h_ref[...]
    hb = h.astype(BF16)
    pb = p_ref[...].astype(BF16)
    gate = jnp.concatenate([_dot(hb, wgb_ref[c]) for c in range(n_col)], axis=1)
    proj = jnp.concatenate([_dot(pb, wpb_ref[c]) for c in range(n_col)], axis=1)
    o_ref[...] = h + (1.0 / (1.0 + jnp.exp(-gate))) * proj


def _ple(h, p, w_gate, w_proj):
    s, d = h.shape
    tm = PLE_TM
    est = 4 * tm * d * 4 + 2 * tm * PLE_DIM * 4 + (d + PLE_DIM) * d * (4 + 2) + 4 * tm * d * 4
    const = lambda i: (0, 0)
    return pl.pallas_call(
        _ple_kernel,
        grid=(s // tm,),
        in_specs=[
            pl.BlockSpec((tm, d), lambda i: (i, 0)),
            pl.BlockSpec((tm, PLE_DIM), lambda i: (i, 0)),
            pl.BlockSpec((d, d), const, pipeline_mode=pl.Buffered(1)),
            pl.BlockSpec((PLE_DIM, d), const, pipeline_mode=pl.Buffered(1)),
        ],
        out_specs=pl.BlockSpec((tm, d), lambda i: (i, 0)),
        out_shape=jax.ShapeDtypeStruct((s, d), F32),
        scratch_shapes=[pltpu.VMEM((d // MXU_COLS, d, MXU_COLS), BF16),
                        pltpu.VMEM((d // MXU_COLS, PLE_DIM, MXU_COLS), BF16)],
        compiler_params=pltpu.CompilerParams(
            dimension_semantics=("arbitrary",),
            vmem_limit_bytes=_vmem_limit(est)),
        name="ple",
    )(h, p, w_gate, w_proj)


def _split_w_in(w_in):
    d = w_in.shape[0]
    b_kr = Q_LORA + KV_LORA
    b_qs = b_kr + QK_ROPE
    b_ks = b_qs + SWA_WIDTH
    b_vs = b_ks + SWA_KV_WIDTH
    wz = jnp.concatenate([w_in[:, :b_kr], w_in[:, b_ks:b_vs]], axis=1)
    pad = jnp.zeros((d, V7X_LANES - QK_ROPE), w_in.dtype)
    wt = jnp.concatenate([w_in[:, b_qs:b_ks], w_in[:, b_vs:], w_in[:, b_kr:b_qs], pad], axis=1).T
    return wz, wt


def _pack_w_uq_t(w_uq):
    r = w_uq.shape[0]
    w = w_uq.reshape(r, MLA_HEADS, QK_NOPE + QK_ROPE)
    w = jnp.pad(w, ((0, 0), (0, 0), (0, MLA_QK_PAD - QK_NOPE - QK_ROPE)))
    return w.reshape(r, MLA_HEADS * MLA_QK_PAD).T


def _rope_freq_col():
    freqs = ROPE_THETA ** (-jnp.arange(0, QK_ROPE, 2, dtype=F32) / QK_ROPE)
    return freqs.reshape(HALF_ROPE, 1)


def kernel(x, p, positions, rel_bias, ln1_g, ln1_b, ffn1_w1, ffn1_w3, ffn1_w2, w_in, q_norm_g, w_uq,
           kv_norm_g, w_ukv, swa_sinks, mla_out_g, swa_out_g, w_out, ln2_g, ln2_b, ffn2_w1, ffn2_w3,
           ffn2_w2, ln3_g, ln3_b, ple_w_gate, ple_w_proj):
    assert x.shape == (1, SEQ, D_MODEL) and DEPTH == 1
    row = lambda a: a.reshape(1, -1)
    h = x[0]
    pos_row = positions[0].astype(F32).reshape(1, SEQ)
    freq_col = _rope_freq_col()
    bucket_tbl = jnp.asarray(_swa_bucket_table())

    i = 0
    h = _ffn_ln(h, ffn1_w1[i], ffn1_w3[i], ffn1_w2[i], row(ln1_g[i]), row(ln1_b[i]))

    w_ukv_h = w_ukv[i].reshape(KV_LORA, MLA_HEADS, QK_NOPE + V_HEAD)
    wkn = w_ukv_h[:, :, :QK_NOPE].reshape(KV_LORA, MLA_WIDTH).astype(BF16)
    wv_t = w_ukv_h[:, :, QK_NOPE:].reshape(KV_LORA, MLA_WIDTH).T.astype(BF16)
    wz, wt = _split_w_in(w_in[i])
    qt, k, vt, qst, ks, vst = _in_proj(
        h, wz.astype(BF16), wt.astype(BF16), row(q_norm_g[i]), row(kv_norm_g[i]),
        _pack_w_uq_t(w_uq[i]).astype(BF16), wkn, wv_t, pos_row, freq_col)

    a_mla = _mla_flash(qt, k, vt)
    sink_rows = jnp.repeat(swa_sinks[i], BLOCK).reshape(SWA_KV_HEADS, 1, SWA_COLS)
    a_swa = _swa_attn(rel_bias, bucket_tbl, qst, ks, vst, sink_rows)

    h = _out_proj(a_mla, a_swa, h, row(mla_out_g[i]), row(swa_out_g[i]), w_out[i],
                  row(ln2_g[i]), row(ln2_b[i]))

    h = _ffn_ln(h, ffn2_w1[i], ffn2_w3[i], ffn2_w2[i], row(ln3_g[i]), row(ln3_b[i]))
    h = _ple(h, p[i, 0], ple_w_gate[i], ple_w_proj[i])
    return h[None]
```

```python
import math

import numpy as np
import jax
import jax.numpy as jnp
from jax import lax
from jax.experimental import pallas as pl
from jax.experimental.pallas import tpu as pltpu

F32 = jnp.float32
BF16 = jnp.bfloat16

D_MODEL = 2048
SEQ = 8192
DEPTH = 1
PLE_DIM = 256
MLA_HEADS = 8
Q_LORA = 512
KV_LORA = 512
QK_NOPE = 128
QK_ROPE = 64
V_HEAD = 128
ROPE_THETA = 10000.0
SWA_HEADS = 16
SWA_KV_HEADS = 2
SWA_GROUP = SWA_HEADS // SWA_KV_HEADS
SWA_HEAD_DIM = 64
WINDOW = 128
BLOCK = 128
REL_BUCKETS = 32
REL_MAX_DIST = 128
D_FF = 5632
ALPHA = (2.0 * DEPTH) ** 0.25
EPS = 1e-5
MLA_WIDTH = MLA_HEADS * V_HEAD
SWA_WIDTH = SWA_HEADS * SWA_HEAD_DIM
SWA_KV_WIDTH = SWA_KV_HEADS * SWA_HEAD_DIM

V7X_LANES = 128
V7X_VMEM_BYTES = 64 * 1024 * 1024
MIB = 1024 * 1024

V7X_BF16_SUBLANES = 16

MLA_QK_PAD = 2 * V7X_LANES
MLA_V_ROWS = V_HEAD + V7X_BF16_SUBLANES
NEG_BIG = float(np.finfo(np.float32).min)
LOG2E = math.log2(math.e)


def _vmem_limit(estimate_bytes):
    return int(min(estimate_bytes + 8 * MIB, V7X_VMEM_BYTES - 4 * MIB))


def _layer_norm(y, g, b):
    mu = jnp.mean(y, axis=-1, keepdims=True)
    yc = y - mu
    var = jnp.mean(yc * yc, axis=-1, keepdims=True)
    return yc * lax.rsqrt(var + EPS) * g + b


def _rms_norm(x, g):
    ms = jnp.mean(x * x, axis=-1, keepdims=True)
    return x * lax.rsqrt(ms + EPS) * g


def _dot(a, b):
    return jnp.dot(a, b, preferred_element_type=F32)


def _dot_nt(a, b):
    return lax.dot_general(a, b, (((1,), (1,)), ((), ())), preferred_element_type=F32)


FFN_TM = 1024
FFN_TF = 512
FFN_ROW_CHUNK = 512
FFN_LN_ROWS = 256


def _ffn_ln_kernel(x_hbm, w1_ref, w3_ref, w2_ref, g_ref, b_ref, o_hbm,
                   xs_ref, xb_ref, acc_ref, w1b_ref, w3b_ref, w2b_ref, x_sem, o_sem):
    i, f = pl.program_id(0), pl.program_id(1)
    n_tiles, n_f = pl.num_programs(0), pl.num_programs(1)
    tm = xs_ref.shape[0]
    n_out_chunks = tm // FFN_LN_ROWS

    def x_copy(tile):
        return pltpu.make_async_copy(x_hbm.at[pl.ds(tile * tm, tm), :], xs_ref, x_sem)

    def out_copy(tile, c):
        return pltpu.make_async_copy(
            acc_ref.at[pl.ds(c * FFN_LN_ROWS, FFN_LN_ROWS), :],
            o_hbm.at[pl.ds(tile * tm + c * FFN_LN_ROWS, FFN_LN_ROWS), :], o_sem.at[c])

    @pl.when(f == 0)
    def _():
        @pl.when(i == 0)
        def _():
            x_copy(0).start()

        x_copy(i).wait()

        @pl.when(i > 0)
        def _():
            for c in range(n_out_chunks):
                out_copy(i - 1, c).wait()

        x = xs_ref[...]
        xb_ref[...] = x.astype(BF16)
        acc_ref[...] = (2.0 * ALPHA) * x

        @pl.when(i + 1 < n_tiles)
        def _():
            x_copy(i + 1).start()

    w1b_ref[...] = w1_ref[...].astype(BF16)
    w3b_ref[...] = w3_ref[...].astype(BF16)
    w2b_ref[...] = w2_ref[...].astype(BF16)

    for c in range(tm // FFN_ROW_CHUNK):
        rows = slice(c * FFN_ROW_CHUNK, (c + 1) * FFN_ROW_CHUNK)
        xb = xb_ref[rows, :]
        gate = _dot(xb, w1b_ref[...])
        up = _dot(xb, w3b_ref[...])
        hidden = gate * (1.0 / (1.0 + jnp.exp(-gate))) * up
        acc_ref[rows, :] += _dot(hidden.astype(BF16), w2b_ref[...])

    @pl.when(f == n_f - 1)
    def _():
        for c in range(n_out_chunks):
            rows = slice(c * FFN_LN_ROWS, (c + 1) * FFN_LN_ROWS)
            acc_ref[rows, :] = _layer_norm(0.5 * acc_ref[rows, :], g_ref[...], b_ref[...])
            out_copy(i, c).start()

        @pl.when(i == n_tiles - 1)
        def _():
            for c in range(n_out_chunks):
                out_copy(i, c).wait()


def _ffn_ln(x, w1, w3, w2, g, b):
    s, d = x.shape
    dff = w1.shape[1]
    tm, tf = FFN_TM, FFN_TF
    est = (tm * d * 4 + tm * d * 2 + tm * d * 4 + 2 * 3 * d * tf * 4 + 3 * d * tf * 2
           + 3 * FFN_ROW_CHUNK * tf * 4)
    return pl.pallas_call(
        _ffn_ln_kernel,
        grid=(s // tm, dff // tf),
        in_specs=[
            pl.BlockSpec(memory_space=pl.ANY),
            pl.BlockSpec((d, tf), lambda i, f: (0, f)),
            pl.BlockSpec((d, tf), lambda i, f: (0, f)),
            pl.BlockSpec((tf, d), lambda i, f: (f, 0)),
            pl.BlockSpec((1, d), lambda i, f: (0, 0)),
            pl.BlockSpec((1, d), lambda i, f: (0, 0)),
        ],
        out_specs=pl.BlockSpec(memory_space=pl.ANY),
        out_shape=jax.ShapeDtypeStruct((s, d), F32),
        scratch_shapes=[pltpu.VMEM((tm, d), F32), pltpu.VMEM((tm, d), BF16), pltpu.VMEM((tm, d), F32),
                        pltpu.VMEM((d, tf), BF16), pltpu.VMEM((d, tf), BF16), pltpu.VMEM((tf, d), BF16),
                        pltpu.SemaphoreType.DMA, pltpu.SemaphoreType.DMA((tm // FFN_LN_ROWS,))],
        compiler_params=pltpu.CompilerParams(
            dimension_semantics=("arbitrary", "arbitrary"),
            vmem_limit_bytes=_vmem_limit(est)),
        name="ffn_ln",
    )(x, w1, w3, w2, g, b)


INPROJ_TM = 512
Z_CQ = 0
Z_CKV = Z_CQ + Q_LORA
Z_KS = Z_CKV + KV_LORA
Z_WIDTH = Z_KS + SWA_KV_WIDTH
ZT_QS = 0
ZT_VS = ZT_QS + SWA_WIDTH
ZT_KR = ZT_VS + SWA_KV_WIDTH
ZT_ROWS = ZT_KR + V7X_LANES
HALF_ROPE = QK_ROPE // 2


def _rope_rows(x1, x2, cos, sin):
    return x1 * cos - x2 * sin, x1 * sin + x2 * cos


def _in_proj_kernel(h_ref, wz_ref, wt_ref, qg_ref, kvg_ref, wuqt_ref, wkn_ref, wvt_ref, pos_ref,
                    freq_ref, qt_ref, k_ref, vt_ref, qst_ref, ks_ref, vst_ref):
    tm = h_ref.shape[0]
    hb = h_ref[...].astype(BF16)
    z = _dot(hb, wz_ref[...])
    zt = _dot_nt(wt_ref[...], hb)

    ang = freq_ref[...] * pos_ref[...]
    cos = jnp.cos(ang)
    sin = jnp.sin(ang)

    cq = _rms_norm(z[:, Z_CQ:Z_CQ + Q_LORA], qg_ref[...]).astype(BF16)
    ckv = _rms_norm(z[:, Z_CKV:Z_CKV + KV_LORA], kvg_ref[...]).astype(BF16)
    qt = _dot_nt(wuqt_ref[...], cq)
    kn = _dot(ckv, wkn_ref[...])
    vt = _dot_nt(wvt_ref[...], ckv)

    kr1, kr2 = _rope_rows(zt[ZT_KR:ZT_KR + HALF_ROPE], zt[ZT_KR + HALF_ROPE:ZT_KR + QK_ROPE], cos, sin)
    k_rope = jnp.concatenate([kr1, kr2, zt[ZT_KR + QK_ROPE:ZT_KR + V7X_LANES]], axis=0).T.astype(BF16)

    for h in range(MLA_HEADS):
        base = h * MLA_QK_PAD
        r0 = base + QK_NOPE
        qt_ref[base:r0, :] = qt[base:r0].astype(BF16)
        q1, q2 = _rope_rows(qt[r0:r0 + HALF_ROPE], qt[r0 + HALF_ROPE:r0 + QK_ROPE], cos, sin)
        qt_ref[r0:r0 + HALF_ROPE, :] = q1.astype(BF16)
        qt_ref[r0 + HALF_ROPE:r0 + QK_ROPE, :] = q2.astype(BF16)
        qt_ref[r0 + QK_ROPE:base + MLA_QK_PAD, :] = qt[r0 + QK_ROPE:base + MLA_QK_PAD].astype(BF16)
        k_ref[h, :, 0:QK_NOPE] = kn[:, h * QK_NOPE:(h + 1) * QK_NOPE].astype(BF16)
        k_ref[h, :, QK_NOPE:MLA_QK_PAD] = k_rope
        vt_ref[h, 0, 0:V_HEAD, :] = vt[h * V_HEAD:(h + 1) * V_HEAD].astype(BF16)
        vt_ref[h, 0, V_HEAD:MLA_V_ROWS, :] = jnp.ones((MLA_V_ROWS - V_HEAD, tm), BF16)

    qst_ref[...] = zt[ZT_QS:ZT_QS + SWA_WIDTH].reshape(SWA_HEADS, SWA_HEAD_DIM, tm).astype(BF16)
    for kv in range(SWA_KV_HEADS):
        lo = Z_KS + kv * SWA_HEAD_DIM
        ks_ref[kv] = z[:, lo:lo + SWA_HEAD_DIM].astype(BF16)
        lo = ZT_VS + kv * SWA_HEAD_DIM
        for c in range(tm // BLOCK):
            vst_ref[kv, c] = zt[lo:lo + SWA_HEAD_DIM, c * BLOCK:(c + 1) * BLOCK].astype(BF16)


def _in_proj(h, wz, wt, q_g, kv_g, wuq_t, wkn, wv_t, pos_row, freq_col):
    s, d = h.shape
    tm = INPROJ_TM
    assert tm == MLA_TK
    est = (2 * tm * d * 4 + 2 * d * (Z_WIDTH + ZT_ROWS) * 2
           + 2 * Q_LORA * (MLA_HEADS * MLA_QK_PAD + 2 * MLA_WIDTH) * 2
           + 2 * tm * (2 * MLA_HEADS * MLA_QK_PAD + MLA_WIDTH + SWA_WIDTH + 4 * V7X_LANES) * 2
           + 4 * tm * (Z_WIDTH + ZT_ROWS + 2 * MLA_HEADS * MLA_QK_PAD) * 4)
    const = lambda i: (0, 0)
    return pl.pallas_call(
        _in_proj_kernel,
        grid=(s // tm,),
        in_specs=[
            pl.BlockSpec((tm, d), lambda i: (i, 0)),
            pl.BlockSpec((d, Z_WIDTH), const),
            pl.BlockSpec((ZT_ROWS, d), const),
            pl.BlockSpec((1, Q_LORA), const),
            pl.BlockSpec((1, KV_LORA), const),
            pl.BlockSpec((MLA_HEADS * MLA_QK_PAD, Q_LORA), const),
            pl.BlockSpec((KV_LORA, MLA_WIDTH), const),
            pl.BlockSpec((MLA_WIDTH, KV_LORA), const),
            pl.BlockSpec((1, tm), lambda i: (0, i)),
            pl.BlockSpec((HALF_ROPE, 1), const),
        ],
        out_specs=[
            pl.BlockSpec((MLA_HEADS * MLA_QK_PAD, tm), lambda i: (0, i)),
            pl.BlockSpec((MLA_HEADS, tm, MLA_QK_PAD), lambda i: (0, i, 0)),
            pl.BlockSpec((MLA_HEADS, 1, MLA_V_ROWS, tm), lambda i: (0, i, 0, 0)),
            pl.BlockSpec((SWA_HEADS, SWA_HEAD_DIM, tm), lambda i: (0, 0, i)),
            pl.BlockSpec((SWA_KV_HEADS, tm, SWA_HEAD_DIM), lambda i: (0, i, 0)),
            pl.BlockSpec((SWA_KV_HEADS, tm // BLOCK, SWA_HEAD_DIM, BLOCK), lambda i: (0, i, 0, 0)),
        ],
        out_shape=[
            jax.ShapeDtypeStruct((MLA_HEADS * MLA_QK_PAD, s), BF16),
            jax.ShapeDtypeStruct((MLA_HEADS, s, MLA_QK_PAD), BF16),
            jax.ShapeDtypeStruct((MLA_HEADS, s // tm, MLA_V_ROWS, tm), BF16),
            jax.ShapeDtypeStruct((SWA_HEADS, SWA_HEAD_DIM, s), BF16),
            jax.ShapeDtypeStruct((SWA_KV_HEADS, s, SWA_HEAD_DIM), BF16),
            jax.ShapeDtypeStruct((SWA_KV_HEADS, s // BLOCK, SWA_HEAD_DIM, BLOCK), BF16),
        ],
        compiler_params=pltpu.CompilerParams(
            dimension_semantics=("parallel",),
            vmem_limit_bytes=_vmem_limit(est)),
        name="in_proj",
    )(h, wz, wt, q_g, kv_g, wuq_t, wkn, wv_t, pos_row, freq_col)


MLA_TQ = 512
MLA_TK = 512
MLA_HB = 2
MLA_SCALE = (QK_NOPE + QK_ROPE) ** -0.5
MLA_SCALE_LOG2E = MLA_SCALE * LOG2E


def _mla_flash_kernel(qt_ref, k_ref, vt_ref, o_ref, s_ref, smax_ref, p_ref, m_ref, corr_ref, acc_ref):
    i = pl.program_id(1)
    tk = MLA_TK
    heads = range(MLA_HB)

    def qk(t):
        start = pl.multiple_of(t * tk, tk)
        for h in heads:
            cols = slice(h * MLA_QK_PAD, (h + 1) * MLA_QK_PAD)
            s = _dot(k_ref[h, pl.ds(start, tk), :], qt_ref[cols, :])
            s_ref[h] = s
            smax_ref[h] = jnp.max(s, axis=0, keepdims=True)

    def softmax(scores, masked):
        for h in heads:
            s, smax = scores[h]
            s = s * MLA_SCALE_LOG2E
            if masked:
                key = lax.broadcasted_iota(jnp.int32, s.shape, 0)
                qry = lax.broadcasted_iota(jnp.int32, s.shape, 1)
                s = jnp.where(key <= qry, s, NEG_BIG)
                tile_max = jnp.max(s, axis=0, keepdims=True)
            else:
                tile_max = smax * MLA_SCALE_LOG2E
            m_prev = m_ref[h]
            m_new = jnp.maximum(m_prev, tile_max)
            p_ref[h] = jnp.exp2(s - m_new).astype(BF16)
            corr_ref[h] = jnp.exp2(m_prev - m_new)
            m_ref[h] = m_new

    def pv(t):
        for h in heads:
            acc_ref[h] = corr_ref[h] * acc_ref[h] + _dot(vt_ref[h, t], p_ref[h])

    def load_scores():
        return [(s_ref[h], smax_ref[h]) for h in heads]

    def init():
        m_ref[...] = jnp.full_like(m_ref, NEG_BIG)
        acc_ref[...] = jnp.zeros_like(acc_ref)

    def finish():
        softmax(load_scores(), masked=True)
        pv(i)
        for h in heads:
            o_ref[:, h * V_HEAD:(h + 1) * V_HEAD] = (
                acc_ref[h, 0:V_HEAD, :] / acc_ref[h, V_HEAD:V_HEAD + 1, :]).T

    @pl.when(i == 0)
    def _():
        init()
        qk(0)
        finish()

    @pl.when(i >= 1)
    def _():
        init()
        qk(0)
        scores = load_scores()
        qk(1)
        softmax(scores, masked=False)

        def body(t, carry):
            pv(t)
            scores = load_scores()
            qk(t + 2)
            softmax(scores, masked=False)
            return carry

        lax.fori_loop(0, i - 1, body, 0)
        pv(i - 1)
        finish()


def _mla_flash(qt, k, vt):
    s = k.shape[1]
    tq, tk, hb = MLA_TQ, MLA_TK, MLA_HB
    assert tq == tk
    est = (2 * hb * tq * MLA_QK_PAD * 2 + 2 * hb * s * MLA_QK_PAD * 2 + 2 * hb * s * MLA_V_ROWS * 2
           + 2 * hb * tq * V_HEAD * 4 + hb * tk * tq * (4 + 2) + hb * tq * MLA_V_ROWS * 4
           + 4 * hb * tq * tk * 4)
    return pl.pallas_call(
        _mla_flash_kernel,
        grid=(MLA_HEADS // hb, s // tq),
        in_specs=[
            pl.BlockSpec((hb * MLA_QK_PAD, tq), lambda g, i: (g, i)),
            pl.BlockSpec((hb, s, MLA_QK_PAD), lambda g, i: (g, 0, 0)),
            pl.BlockSpec((hb, s // tk, MLA_V_ROWS, tk), lambda g, i: (g, 0, 0, 0)),
        ],
        out_specs=pl.BlockSpec((tq, hb * V_HEAD), lambda g, i: (i, g)),
        out_shape=jax.ShapeDtypeStruct((s, MLA_WIDTH), F32),
        scratch_shapes=[pltpu.VMEM((hb, tk, tq), F32), pltpu.VMEM((hb, 1, tq), F32),
                        pltpu.VMEM((hb, tk, tq), BF16),
                        pltpu.VMEM((hb, 1, tq), F32), pltpu.VMEM((hb, 1, tq), F32),
                        pltpu.VMEM((hb, MLA_V_ROWS, tq), F32)],
        compiler_params=pltpu.CompilerParams(
            dimension_semantics=("parallel", "arbitrary"),
            vmem_limit_bytes=_vmem_limit(est)),
        name="mla_flash",
    )(qt, k, vt)


SWA_SCALE = SWA_HEAD_DIM ** -0.5
SWA_COLS = SWA_GROUP * BLOCK
SWA_NB = 4


def _t5_bucket_np(dist):
    n = np.maximum(dist, 0)
    max_exact = REL_BUCKETS // 2
    large = max_exact + (np.log(np.maximum(n, 1).astype(np.float32) / max_exact)
                         / math.log(REL_MAX_DIST / max_exact)
                         * (REL_BUCKETS - max_exact)).astype(np.int32)
    large = np.minimum(large, REL_BUCKETS - 1)
    return np.where(n < max_exact, n, large).astype(np.int32)


def _swa_bucket_table():
    j = np.arange(BLOCK)[:, None]
    i = np.arange(BLOCK)[None, :]
    dist = np.where(j <= i, i - j, BLOCK + i - j)
    return _t5_bucket_np(dist)


def _swa_kernel(rb_ref, bkt_ref, qt_ref, qt_next_ref, k_ref, vt_ref, sink_ref, o_ref, bias_ref, s_ref):
    kv = pl.program_id(0)
    n = pl.program_id(1)

    @pl.when((kv == 0) & (n == 0))
    def _():
        bkt = bkt_ref[...]
        for h in range(SWA_HEADS):
            acc = jnp.zeros((BLOCK, BLOCK), F32)
            for b in range(REL_BUCKETS):
                acc = jnp.where(bkt == b, rb_ref[b, h], acc)
            g = h % SWA_GROUP
            bias_ref[h // SWA_GROUP, :, g * BLOCK:(g + 1) * BLOCK] = acc * LOG2E

    key = lax.broadcasted_iota(jnp.int32, (BLOCK, SWA_COLS), 0)
    qry = lax.broadcasted_iota(jnp.int32, (BLOCK, SWA_COLS), 1) & (BLOCK - 1)
    lower = key <= qry
    lower_bf = jnp.where(lower, 1.0, 0.0).astype(BF16)
    bias = bias_ref[kv]
    sink = sink_ref[...] * LOG2E

    def band_scores(q_ref, step):
        for b in range(SWA_NB):
            blk = step * SWA_NB + b
            qt = jnp.concatenate([q_ref[g, :, b * BLOCK:(b + 1) * BLOCK] for g in range(SWA_GROUP)],
                                 axis=1)
            cur = pl.multiple_of(blk * BLOCK, BLOCK)
            prev = pl.multiple_of(jnp.maximum(blk - 1, 0) * BLOCK, BLOCK)
            k_band = jnp.concatenate([k_ref[pl.ds(prev, BLOCK), :], k_ref[pl.ds(cur, BLOCK), :]], axis=0)
            s_ref[b] = _dot(k_band, qt)

    @pl.when(n == 0)
    def _():
        band_scores(qt_ref, 0)

    bands = [s_ref[b] for b in range(SWA_NB)]
    band_scores(qt_next_ref, jnp.minimum(n + 1, pl.num_programs(1) - 1))

    for b in range(SWA_NB):
        blk = n * SWA_NB + b
        prev_blk = jnp.maximum(blk - 1, 0)
        s_band = bands[b]
        s = jnp.where(lower, s_band[BLOCK:], s_band[:BLOCK]) * (SWA_SCALE * LOG2E) + bias
        if b == 0:
            s = jnp.where(lower | (blk > 0), s, NEG_BIG)

        m = jnp.maximum(jnp.max(s, axis=0, keepdims=True), sink)
        e = jnp.exp2(s - m)
        denom = jnp.sum(e, axis=0, keepdims=True) + jnp.exp2(sink - m)
        p = (e * (1.0 / denom)).astype(BF16)
        p_cur = p * lower_bf
        p_prev = p - p_cur
        ot = _dot(vt_ref[blk], p_cur) + _dot(vt_ref[prev_blk], p_prev)
        o_ref[b * BLOCK:(b + 1) * BLOCK, :] = jnp.concatenate(
            [ot[:, g * BLOCK:(g + 1) * BLOCK] for g in range(SWA_GROUP)], axis=0).T


def _swa_attn(rel_bias, bucket_tbl, qst, ks, vst, sink_rows):
    s = ks.shape[1]
    rows = SWA_NB * BLOCK
    n_steps = s // rows
    est = (4 * SWA_GROUP * SWA_HEAD_DIM * rows * 2 + 2 * s * V7X_LANES * 2 + 2 * s * SWA_HEAD_DIM * 2
           + 2 * rows * SWA_GROUP * SWA_HEAD_DIM * 4 + SWA_KV_HEADS * BLOCK * SWA_COLS * 4
           + SWA_NB * 2 * BLOCK * SWA_COLS * 4 + 10 * SWA_NB * BLOCK * SWA_COLS * 4)
    return pl.pallas_call(
        _swa_kernel,
        grid=(SWA_KV_HEADS, n_steps),
        in_specs=[
            pl.BlockSpec(memory_space=pltpu.SMEM),
            pl.BlockSpec((BLOCK, BLOCK), lambda kv, n: (0, 0)),
            pl.BlockSpec((SWA_GROUP, SWA_HEAD_DIM, rows), lambda kv, n: (kv, 0, n)),
            pl.BlockSpec((SWA_GROUP, SWA_HEAD_DIM, rows),
                         lambda kv, n: (kv, 0, jnp.minimum(n + 1, n_steps - 1))),
            pl.BlockSpec((None, s, SWA_HEAD_DIM), lambda kv, n: (kv, 0, 0)),
            pl.BlockSpec((None, s // BLOCK, SWA_HEAD_DIM, BLOCK), lambda kv, n: (kv, 0, 0, 0)),
            pl.BlockSpec((None, 1, SWA_COLS), lambda kv, n: (kv, 0, 0)),
        ],
        out_specs=pl.BlockSpec((rows, SWA_GROUP * SWA_HEAD_DIM), lambda kv, n: (n, kv)),
        out_shape=jax.ShapeDtypeStruct((s, SWA_WIDTH), F32),
        scratch_shapes=[pltpu.VMEM((SWA_KV_HEADS, BLOCK, SWA_COLS), F32),
                        pltpu.VMEM((SWA_NB, 2 * BLOCK, SWA_COLS), F32)],
        compiler_params=pltpu.CompilerParams(
            dimension_semantics=("arbitrary", "arbitrary"),
            vmem_limit_bytes=_vmem_limit(est)),
        name="swa_attn",
    )(rel_bias, bucket_tbl, qst, qst, ks, vst, sink_rows)


OUTPROJ_TM = 512
OUTPROJ_CHUNKS = 2


def _out_proj_kernel(am_ref, as_ref, h_ref, mg_ref, sg_ref, w_ref, g_ref, b_ref, o_ref, wb_ref):
    @pl.when(pl.program_id(0) == 0)
    def _():
        wb_ref[...] = w_ref[...].astype(BF16)

    chunk = o_ref.shape[0] // OUTPROJ_CHUNKS
    for c in range(OUTPROJ_CHUNKS):
        rows = slice(c * chunk, (c + 1) * chunk)
        nm = _rms_norm(am_ref[rows, :], mg_ref[...]).astype(BF16)
        ns = _rms_norm(as_ref[rows, :], sg_ref[...]).astype(BF16)
        mixed = _dot(nm, wb_ref[0:MLA_WIDTH, :]) + _dot(ns, wb_ref[MLA_WIDTH:MLA_WIDTH + SWA_WIDTH, :])
        o_ref[rows, :] = _layer_norm(ALPHA * h_ref[rows, :] + mixed, g_ref[...], b_ref[...])


def _out_proj(a_mla, a_swa, h, mla_g, swa_g, w_out, g, b):
    s, d = h.shape
    tm = OUTPROJ_TM
    est = (2 * tm * (MLA_WIDTH + SWA_WIDTH) * 4 + 4 * tm * d * 4 + (MLA_WIDTH + SWA_WIDTH) * d * (4 + 2)
           + 4 * tm * d * 4)
    const = lambda i: (0, 0)
    return pl.pallas_call(
        _out_proj_kernel,
        grid=(s // tm,),
        in_specs=[
            pl.BlockSpec((tm, MLA_WIDTH), lambda i: (i, 0)),
            pl.BlockSpec((tm, SWA_WIDTH), lambda i: (i, 0)),
            pl.BlockSpec((tm, d), lambda i: (i, 0)),
            pl.BlockSpec((1, MLA_WIDTH), const),
            pl.BlockSpec((1, SWA_WIDTH), const),
            pl.BlockSpec((MLA_WIDTH + SWA_WIDTH, d), const, pipeline_mode=pl.Buffered(1)),
            pl.BlockSpec((1, d), const),
            pl.BlockSpec((1, d), const),
        ],
        out_specs=pl.BlockSpec((tm, d), lambda i: (i, 0)),
        out_shape=jax.ShapeDtypeStruct((s, d), F32),
        scratch_shapes=[pltpu.VMEM((MLA_WIDTH + SWA_WIDTH, d), BF16)],
        compiler_params=pltpu.CompilerParams(
            dimension_semantics=("arbitrary",),
            vmem_limit_bytes=_vmem_limit(est)),
        name="out_proj",
    )(a_mla, a_swa, h, mla_g, swa_g, w_out, g, b)


PLE_TM = 512


def _ple_kernel(h_ref, p_ref, wg_ref, wp_ref, o_ref, wgb_ref, wpb_ref):
    @pl.when(pl.program_id(0) == 0)
    def _():
        wgb_ref[...] = wg_ref[...].astype(BF16)
        wpb_ref[...] = wp_ref[...].astype(BF16)

    h = h_ref[...]
    gate = _dot(h.astype(BF16), wgb_ref[...])
    proj = _dot(p_ref[...].astype(BF16), wpb_ref[...])
    o_ref[...] = h + (1.0 / (1.0 + jnp.exp(-gate))) * proj


def _ple(h, p, w_gate, w_proj):
    s, d = h.shape
    tm = PLE_TM
    est = 4 * tm * d * 4 + 2 * tm * PLE_DIM * 4 + (d + PLE_DIM) * d * (4 + 2) + 4 * tm * d * 4
    const = lambda i: (0, 0)
    return pl.pallas_call(
        _ple_kernel,
        grid=(s // tm,),
        in_specs=[
            pl.BlockSpec((tm, d), lambda i: (i, 0)),
            pl.BlockSpec((tm, PLE_DIM), lambda i: (i, 0)),
            pl.BlockSpec((d, d), const, pipeline_mode=pl.Buffered(1)),
            pl.BlockSpec((PLE_DIM, d), const, pipeline_mode=pl.Buffered(1)),
        ],
        out_specs=pl.BlockSpec((tm, d), lambda i: (i, 0)),
        out_shape=jax.ShapeDtypeStruct((s, d), F32),
        scratch_shapes=[pltpu.VMEM((d, d), BF16), pltpu.VMEM((PLE_DIM, d), BF16)],
        compiler_params=pltpu.CompilerParams(
            dimension_semantics=("arbitrary",),
            vmem_limit_bytes=_vmem_limit(est)),
        name="ple",
    )(h, p, w_gate, w_proj)


def _split_w_in(w_in):
    d = w_in.shape[0]
    b_kr = Q_LORA + KV_LORA
    b_qs = b_kr + QK_ROPE
    b_ks = b_qs + SWA_WIDTH
    b_vs = b_ks + SWA_KV_WIDTH
    wz = jnp.concatenate([w_in[:, :b_kr], w_in[:, b_ks:b_vs]], axis=1)
    pad = jnp.zeros((d, V7X_LANES - QK_ROPE), w_in.dtype)
    wt = jnp.concatenate([w_in[:, b_qs:b_ks], w_in[:, b_vs:], w_in[:, b_kr:b_qs], pad], axis=1).T
    return wz, wt


def _pack_w_uq_t(w_uq):
    r = w_uq.shape[0]
    w = w_uq.reshape(r, MLA_HEADS, QK_NOPE + QK_ROPE)
    w = jnp.pad(w, ((0, 0), (0, 0), (0, MLA_QK_PAD - QK_NOPE - QK_ROPE)))
    return w.reshape(r, MLA_HEADS * MLA_QK_PAD).T


def _rope_freq_col():
    freqs = ROPE_THETA ** (-jnp.arange(0, QK_ROPE, 2, dtype=F32) / QK_ROPE)
    return freqs.reshape(HALF_ROPE, 1)


def kernel(x, p, positions, rel_bias, ln1_g, ln1_b, ffn1_w1, ffn1_w3, ffn1_w2, w_in, q_norm_g, w_uq,
           kv_norm_g, w_ukv, swa_sinks, mla_out_g, swa_out_g, w_out, ln2_g, ln2_b, ffn2_w1, ffn2_w3,
           ffn2_w2, ln3_g, ln3_b, ple_w_gate, ple_w_proj):
    assert x.shape == (1, SEQ, D_MODEL) and DEPTH == 1
    row = lambda a: a.reshape(1, -1)
    h = x[0]
    pos_row = positions[0].astype(F32).reshape(1, SEQ)
    freq_col = _rope_freq_col()
    bucket_tbl = jnp.asarray(_swa_bucket_table())

    i = 0
    h = _ffn_ln(h, ffn1_w1[i], ffn1_w3[i], ffn1_w2[i], row(ln1_g[i]), row(ln1_b[i]))

    w_ukv_h = w_ukv[i].reshape(KV_LORA, MLA_HEADS, QK_NOPE + V_HEAD)
    wkn = w_ukv_h[:, :, :QK_NOPE].reshape(KV_LORA, MLA_WIDTH).astype(BF16)
    wv_t = w_ukv_h[:, :, QK_NOPE:].reshape(KV_LORA, MLA_WIDTH).T.astype(BF16)
    wz, wt = _split_w_in(w_in[i])
    qt, k, vt, qst, ks, vst = _in_proj(
        h, wz.astype(BF16), wt.astype(BF16), row(q_norm_g[i]), row(kv_norm_g[i]),
        _pack_w_uq_t(w_uq[i]).astype(BF16), wkn, wv_t, pos_row, freq_col)

    a_mla = _mla_flash(qt, k, vt)
    sink_rows = jnp.repeat(swa_sinks[i], BLOCK).reshape(SWA_KV_HEADS, 1, SWA_COLS)
    a_swa = _swa_attn(rel_bias, bucket_tbl, qst, ks, vst, sink_rows)

    h = _out_proj(a_mla, a_swa, h, row(mla_out_g[i]), row(swa_out_g[i]), w_out[i],
                  row(ln2_g[i]), row(ln2_b[i]))

    h = _ffn_ln(h, ffn2_w1[i], ffn2_w3[i], ffn2_w2[i], row(ln3_g[i]), row(ln3_b[i]))
    h = _ple(h, p[i, 0], ple_w_gate[i], ple_w_proj[i])
    return h[None]
```

```python
import math

import numpy as np
import jax
import jax.numpy as jnp
from jax import lax
from jax.experimental import pallas as pl
from jax.experimental.pallas import tpu as pltpu

F32 = jnp.float32
BF16 = jnp.bfloat16

D_MODEL = 2048
SEQ = 8192
DEPTH = 1
PLE_DIM = 256
MLA_HEADS = 8
Q_LORA = 512
KV_LORA = 512
QK_NOPE = 128
QK_ROPE = 64
V_HEAD = 128
ROPE_THETA = 10000.0
SWA_HEADS = 16
SWA_KV_HEADS = 2
SWA_GROUP = SWA_HEADS // SWA_KV_HEADS
SWA_HEAD_DIM = 64
WINDOW = 128
BLOCK = 128
REL_BUCKETS = 32
REL_MAX_DIST = 128
D_FF = 5632
ALPHA = (2.0 * DEPTH) ** 0.25
EPS = 1e-5
MLA_WIDTH = MLA_HEADS * V_HEAD
SWA_WIDTH = SWA_HEADS * SWA_HEAD_DIM
SWA_KV_WIDTH = SWA_KV_HEADS * SWA_HEAD_DIM

V7X_LANES = 128
V7X_VMEM_BYTES = 64 * 1024 * 1024
MIB = 1024 * 1024

V7X_BF16_SUBLANES = 16

MLA_QK_PAD = 2 * V7X_LANES
MLA_V_ROWS = V_HEAD + V7X_BF16_SUBLANES
NEG_BIG = float(np.finfo(np.float32).min)
LOG2E = math.log2(math.e)


def _vmem_limit(estimate_bytes):
    return int(min(estimate_bytes + 8 * MIB, V7X_VMEM_BYTES - 4 * MIB))


def _layer_norm(y, g, b):
    mu = jnp.mean(y, axis=-1, keepdims=True)
    yc = y - mu
    var = jnp.mean(yc * yc, axis=-1, keepdims=True)
    return yc * lax.rsqrt(var + EPS) * g + b


def _rms_norm(x, g):
    ms = jnp.mean(x * x, axis=-1, keepdims=True)
    return x * lax.rsqrt(ms + EPS) * g


def _dot(a, b):
    return jnp.dot(a, b, preferred_element_type=F32)


def _dot_tt(w, x):
    return lax.dot_general(w, x, (((0,), (1,)), ((), ())), preferred_element_type=F32)


FFN_TM = 1024
FFN_TF = 256
FFN_ROW_CHUNK = 512
FFN_LN_ROWS = 256


def _ffn_ln_kernel(x_ref, w1_ref, w3_ref, w2_ref, g_ref, b_ref, o_ref, xb_ref, w1b_ref, w3b_ref, w2b_ref):
    f = pl.program_id(1)

    @pl.when(f == 0)
    def _():
        xb_ref[...] = x_ref[...].astype(BF16)
        o_ref[...] = jnp.zeros_like(o_ref)

    w1b_ref[...] = w1_ref[...].astype(BF16)
    w3b_ref[...] = w3_ref[...].astype(BF16)
    w2b_ref[...] = w2_ref[...].astype(BF16)

    for c in range(o_ref.shape[0] // FFN_ROW_CHUNK):
        rows = slice(c * FFN_ROW_CHUNK, (c + 1) * FFN_ROW_CHUNK)
        xb = xb_ref[rows, :]
        gate = _dot(xb, w1b_ref[...])
        up = _dot(xb, w3b_ref[...])
        hidden = gate * (1.0 / (1.0 + jnp.exp(-gate))) * up
        o_ref[rows, :] += _dot(hidden.astype(BF16), w2b_ref[...])

    @pl.when(f == pl.num_programs(1) - 1)
    def _():
        for c in range(o_ref.shape[0] // FFN_LN_ROWS):
            rows = slice(c * FFN_LN_ROWS, (c + 1) * FFN_LN_ROWS)
            y = ALPHA * x_ref[rows, :] + 0.5 * o_ref[rows, :]
            o_ref[rows, :] = _layer_norm(y, g_ref[...], b_ref[...])


def _ffn_ln(x, w1, w3, w2, g, b):
    s, d = x.shape
    dff = w1.shape[1]
    tm, tf = FFN_TM, FFN_TF
    est = (2 * tm * d * 4 + tm * d * 2 + 2 * 3 * d * tf * 4 + 3 * d * tf * 2 + 2 * tm * d * 4
           + 3 * FFN_ROW_CHUNK * tf * 4)
    return pl.pallas_call(
        _ffn_ln_kernel,
        grid=(s // tm, dff // tf),
        in_specs=[
            pl.BlockSpec((tm, d), lambda i, f: (i, 0)),
            pl.BlockSpec((d, tf), lambda i, f: (0, f)),
            pl.BlockSpec((d, tf), lambda i, f: (0, f)),
            pl.BlockSpec((tf, d), lambda i, f: (f, 0)),
            pl.BlockSpec((1, d), lambda i, f: (0, 0)),
            pl.BlockSpec((1, d), lambda i, f: (0, 0)),
        ],
        out_specs=pl.BlockSpec((tm, d), lambda i, f: (i, 0)),
        out_shape=jax.ShapeDtypeStruct((s, d), F32),
        scratch_shapes=[pltpu.VMEM((tm, d), BF16), pltpu.VMEM((d, tf), BF16), pltpu.VMEM((d, tf), BF16),
                        pltpu.VMEM((tf, d), BF16)],
        compiler_params=pltpu.CompilerParams(
            dimension_semantics=("parallel", "arbitrary"),
            vmem_limit_bytes=_vmem_limit(est)),
        name="ffn_ln",
    )(x, w1, w3, w2, g, b)


INPROJ_TM = 512
Z_CQ = 0
Z_CKV = Z_CQ + Q_LORA
Z_KS = Z_CKV + KV_LORA
Z_WIDTH = Z_KS + SWA_KV_WIDTH
ZT_QS = 0
ZT_VS = ZT_QS + SWA_WIDTH
ZT_KR = ZT_VS + SWA_KV_WIDTH
ZT_ROWS = ZT_KR + V7X_LANES
HALF_ROPE = QK_ROPE // 2


def _rope_rows(x1, x2, cos, sin):
    return x1 * cos - x2 * sin, x1 * sin + x2 * cos


def _in_proj_kernel(h_ref, wz_ref, wt_ref, qg_ref, kvg_ref, wuq_ref, wkn_ref, wv_ref, pos_ref,
                    freq_ref, qt_ref, k_ref, vt_ref, qst_ref, ks_ref, vst_ref):
    tm = h_ref.shape[0]
    hb = h_ref[...].astype(BF16)
    z = _dot(hb, wz_ref[...])
    zt = _dot_tt(wt_ref[...], hb)

    ang = freq_ref[...] * pos_ref[...]
    cos = jnp.cos(ang)
    sin = jnp.sin(ang)

    cq = _rms_norm(z[:, Z_CQ:Z_CQ + Q_LORA], qg_ref[...]).astype(BF16)
    ckv = _rms_norm(z[:, Z_CKV:Z_CKV + KV_LORA], kvg_ref[...]).astype(BF16)
    qt = _dot_tt(wuq_ref[...], cq)
    kn = _dot(ckv, wkn_ref[...])
    vt = _dot_tt(wv_ref[...], ckv)

    kr1, kr2 = _rope_rows(zt[ZT_KR:ZT_KR + HALF_ROPE], zt[ZT_KR + HALF_ROPE:ZT_KR + QK_ROPE], cos, sin)
    k_rope = jnp.concatenate([kr1, kr2, zt[ZT_KR + QK_ROPE:ZT_KR + V7X_LANES]], axis=0).T.astype(BF16)

    for h in range(MLA_HEADS):
        base = h * MLA_QK_PAD
        r0 = base + QK_NOPE
        qt_ref[base:r0, :] = qt[base:r0].astype(BF16)
        q1, q2 = _rope_rows(qt[r0:r0 + HALF_ROPE], qt[r0 + HALF_ROPE:r0 + QK_ROPE], cos, sin)
        qt_ref[r0:r0 + HALF_ROPE, :] = q1.astype(BF16)
        qt_ref[r0 + HALF_ROPE:r0 + QK_ROPE, :] = q2.astype(BF16)
        qt_ref[r0 + QK_ROPE:base + MLA_QK_PAD, :] = qt[r0 + QK_ROPE:base + MLA_QK_PAD].astype(BF16)
        k_ref[h, :, 0:QK_NOPE] = kn[:, h * QK_NOPE:(h + 1) * QK_NOPE].astype(BF16)
        k_ref[h, :, QK_NOPE:MLA_QK_PAD] = k_rope
        vt_ref[h, 0, 0:V_HEAD, :] = vt[h * V_HEAD:(h + 1) * V_HEAD].astype(BF16)
        vt_ref[h, 0, V_HEAD:MLA_V_ROWS, :] = jnp.ones((MLA_V_ROWS - V_HEAD, tm), BF16)

    qst_ref[...] = zt[ZT_QS:ZT_QS + SWA_WIDTH].reshape(SWA_HEADS, SWA_HEAD_DIM, tm).astype(BF16)
    for kv in range(SWA_KV_HEADS):
        lo = Z_KS + kv * SWA_HEAD_DIM
        ks_ref[kv] = z[:, lo:lo + SWA_HEAD_DIM].astype(BF16)
        lo = ZT_VS + kv * SWA_HEAD_DIM
        for c in range(tm // BLOCK):
            vst_ref[kv, c] = zt[lo:lo + SWA_HEAD_DIM, c * BLOCK:(c + 1) * BLOCK].astype(BF16)


def _in_proj(h, wz, wt, q_g, kv_g, wuq, wkn, wv, pos_row, freq_col):
    s, d = h.shape
    tm = INPROJ_TM
    assert tm == MLA_TK
    est = (2 * tm * d * 4 + 2 * d * (Z_WIDTH + ZT_ROWS) * 2
           + 2 * Q_LORA * (MLA_HEADS * MLA_QK_PAD + 2 * MLA_WIDTH) * 2
           + 2 * tm * (2 * MLA_HEADS * MLA_QK_PAD + MLA_WIDTH + SWA_WIDTH + 4 * V7X_LANES) * 2
           + 4 * tm * (Z_WIDTH + ZT_ROWS + 2 * MLA_HEADS * MLA_QK_PAD) * 4)
    const = lambda i: (0, 0)
    return pl.pallas_call(
        _in_proj_kernel,
        grid=(s // tm,),
        in_specs=[
            pl.BlockSpec((tm, d), lambda i: (i, 0)),
            pl.BlockSpec((d, Z_WIDTH), const),
            pl.BlockSpec((d, ZT_ROWS), const),
            pl.BlockSpec((1, Q_LORA), const),
            pl.BlockSpec((1, KV_LORA), const),
            pl.BlockSpec((Q_LORA, MLA_HEADS * MLA_QK_PAD), const),
            pl.BlockSpec((KV_LORA, MLA_WIDTH), const),
            pl.BlockSpec((KV_LORA, MLA_WIDTH), const),
            pl.BlockSpec((1, tm), lambda i: (0, i)),
            pl.BlockSpec((HALF_ROPE, 1), const),
        ],
        out_specs=[
            pl.BlockSpec((MLA_HEADS * MLA_QK_PAD, tm), lambda i: (0, i)),
            pl.BlockSpec((MLA_HEADS, tm, MLA_QK_PAD), lambda i: (0, i, 0)),
            pl.BlockSpec((MLA_HEADS, 1, MLA_V_ROWS, tm), lambda i: (0, i, 0, 0)),
            pl.BlockSpec((SWA_HEADS, SWA_HEAD_DIM, tm), lambda i: (0, 0, i)),
            pl.BlockSpec((SWA_KV_HEADS, tm, SWA_HEAD_DIM), lambda i: (0, i, 0)),
            pl.BlockSpec((SWA_KV_HEADS, tm // BLOCK, SWA_HEAD_DIM, BLOCK), lambda i: (0, i, 0, 0)),
        ],
        out_shape=[
            jax.ShapeDtypeStruct((MLA_HEADS * MLA_QK_PAD, s), BF16),
            jax.ShapeDtypeStruct((MLA_HEADS, s, MLA_QK_PAD), BF16),
            jax.ShapeDtypeStruct((MLA_HEADS, s // tm, MLA_V_ROWS, tm), BF16),
            jax.ShapeDtypeStruct((SWA_HEADS, SWA_HEAD_DIM, s), BF16),
            jax.ShapeDtypeStruct((SWA_KV_HEADS, s, SWA_HEAD_DIM), BF16),
            jax.ShapeDtypeStruct((SWA_KV_HEADS, s // BLOCK, SWA_HEAD_DIM, BLOCK), BF16),
        ],
        compiler_params=pltpu.CompilerParams(
            dimension_semantics=("parallel",),
            vmem_limit_bytes=_vmem_limit(est)),
        name="in_proj",
    )(h, wz, wt, q_g, kv_g, wuq, wkn, wv, pos_row, freq_col)


MLA_TQ = 512
MLA_TK = 512
MLA_HB = 2
MLA_SCALE = (QK_NOPE + QK_ROPE) ** -0.5
MLA_SCALE_LOG2E = MLA_SCALE * LOG2E


def _mla_flash_kernel(qt_ref, k_ref, vt_ref, o_ref, s_ref, smax_ref, p_ref, m_ref, corr_ref, acc_ref):
    i = pl.program_id(1)
    tk = MLA_TK
    heads = range(MLA_HB)

    def qk(t):
        start = pl.multiple_of(t * tk, tk)
        for h in heads:
            cols = slice(h * MLA_QK_PAD, (h + 1) * MLA_QK_PAD)
            s = _dot(k_ref[h, pl.ds(start, tk), :], qt_ref[cols, :])
            s_ref[h] = s
            smax_ref[h] = jnp.max(s, axis=0, keepdims=True)

    def softmax(scores, masked):
        for h in heads:
            s, smax = scores[h]
            s = s * MLA_SCALE_LOG2E
            if masked:
                key = lax.broadcasted_iota(jnp.int32, s.shape, 0)
                qry = lax.broadcasted_iota(jnp.int32, s.shape, 1)
                s = jnp.where(key <= qry, s, NEG_BIG)
                tile_max = jnp.max(s, axis=0, keepdims=True)
            else:
                tile_max = smax * MLA_SCALE_LOG2E
            m_prev = m_ref[h]
            m_new = jnp.maximum(m_prev, tile_max)
            p_ref[h] = jnp.exp2(s - m_new).astype(BF16)
            corr_ref[h] = jnp.exp2(m_prev - m_new)
            m_ref[h] = m_new

    def pv(t):
        for h in heads:
            acc_ref[h] = corr_ref[h] * acc_ref[h] + _dot(vt_ref[h, t], p_ref[h])

    def load_scores():
        return [(s_ref[h], smax_ref[h]) for h in heads]

    def init():
        m_ref[...] = jnp.full_like(m_ref, NEG_BIG)
        acc_ref[...] = jnp.zeros_like(acc_ref)

    def finish():
        softmax(load_scores(), masked=True)
        pv(i)
        for h in heads:
            o_ref[:, h * V_HEAD:(h + 1) * V_HEAD] = (
                acc_ref[h, 0:V_HEAD, :] / acc_ref[h, V_HEAD:V_HEAD + 1, :]).T

    @pl.when(i == 0)
    def _():
        init()
        qk(0)
        finish()

    @pl.when(i >= 1)
    def _():
        init()
        qk(0)
        scores = load_scores()
        qk(1)
        softmax(scores, masked=False)

        def body(t, carry):
            pv(t)
            scores = load_scores()
            qk(t + 2)
            softmax(scores, masked=False)
            return carry

        lax.fori_loop(0, i - 1, body, 0)
        pv(i - 1)
        finish()


def _mla_flash(qt, k, vt):
    s = k.shape[1]
    tq, tk, hb = MLA_TQ, MLA_TK, MLA_HB
    assert tq == tk
    est = (2 * hb * tq * MLA_QK_PAD * 2 + 2 * hb * s * MLA_QK_PAD * 2 + 2 * hb * s * MLA_V_ROWS * 2
           + 2 * hb * tq * V_HEAD * 4 + hb * tk * tq * (4 + 2) + hb * tq * MLA_V_ROWS * 4
           + 4 * hb * tq * tk * 4)
    return pl.pallas_call(
        _mla_flash_kernel,
        grid=(MLA_HEADS // hb, s // tq),
        in_specs=[
            pl.BlockSpec((hb * MLA_QK_PAD, tq), lambda g, i: (g, i)),
            pl.BlockSpec((hb, s, MLA_QK_PAD), lambda g, i: (g, 0, 0)),
            pl.BlockSpec((hb, s // tk, MLA_V_ROWS, tk), lambda g, i: (g, 0, 0, 0)),
        ],
        out_specs=pl.BlockSpec((tq, hb * V_HEAD), lambda g, i: (i, g)),
        out_shape=jax.ShapeDtypeStruct((s, MLA_WIDTH), F32),
        scratch_shapes=[pltpu.VMEM((hb, tk, tq), F32), pltpu.VMEM((hb, 1, tq), F32),
                        pltpu.VMEM((hb, tk, tq), BF16),
                        pltpu.VMEM((hb, 1, tq), F32), pltpu.VMEM((hb, 1, tq), F32),
                        pltpu.VMEM((hb, MLA_V_ROWS, tq), F32)],
        compiler_params=pltpu.CompilerParams(
            dimension_semantics=("parallel", "arbitrary"),
            vmem_limit_bytes=_vmem_limit(est)),
        name="mla_flash",
    )(qt, k, vt)


SWA_SCALE = SWA_HEAD_DIM ** -0.5
SWA_COLS = SWA_GROUP * BLOCK
SWA_NB = 4


def _t5_bucket_np(dist):
    n = np.maximum(dist, 0)
    max_exact = REL_BUCKETS // 2
    large = max_exact + (np.log(np.maximum(n, 1).astype(np.float32) / max_exact)
                         / math.log(REL_MAX_DIST / max_exact)
                         * (REL_BUCKETS - max_exact)).astype(np.int32)
    large = np.minimum(large, REL_BUCKETS - 1)
    return np.where(n < max_exact, n, large).astype(np.int32)


def _swa_bucket_table():
    j = np.arange(BLOCK)[:, None]
    i = np.arange(BLOCK)[None, :]
    dist = np.where(j <= i, i - j, BLOCK + i - j)
    return _t5_bucket_np(dist)


def _swa_kernel(rb_ref, bkt_ref, qt_ref, qt_next_ref, k_ref, vt_ref, sink_ref, o_ref, bias_ref, s_ref):
    kv = pl.program_id(0)
    n = pl.program_id(1)

    @pl.when((kv == 0) & (n == 0))
    def _():
        bkt = bkt_ref[...]
        for h in range(SWA_HEADS):
            acc = jnp.zeros((BLOCK, BLOCK), F32)
            for b in range(REL_BUCKETS):
                acc = jnp.where(bkt == b, rb_ref[b, h], acc)
            g = h % SWA_GROUP
            bias_ref[h // SWA_GROUP, :, g * BLOCK:(g + 1) * BLOCK] = acc * LOG2E

    key = lax.broadcasted_iota(jnp.int32, (BLOCK, SWA_COLS), 0)
    qry = lax.broadcasted_iota(jnp.int32, (BLOCK, SWA_COLS), 1) & (BLOCK - 1)
    lower = key <= qry
    lower_bf = jnp.where(lower, 1.0, 0.0).astype(BF16)
    bias = bias_ref[kv]
    sink = sink_ref[...] * LOG2E

    def band_scores(q_ref, step):
        for b in range(SWA_NB):
            blk = step * SWA_NB + b
            qt = jnp.concatenate([q_ref[g, :, b * BLOCK:(b + 1) * BLOCK] for g in range(SWA_GROUP)],
                                 axis=1)
            cur = pl.multiple_of(blk * BLOCK, BLOCK)
            prev = pl.multiple_of(jnp.maximum(blk - 1, 0) * BLOCK, BLOCK)
            k_band = jnp.concatenate([k_ref[pl.ds(prev, BLOCK), :], k_ref[pl.ds(cur, BLOCK), :]], axis=0)
            s_ref[b] = _dot(k_band, qt)

    @pl.when(n == 0)
    def _():
        band_scores(qt_ref, 0)

    bands = [s_ref[b] for b in range(SWA_NB)]
    band_scores(qt_next_ref, jnp.minimum(n + 1, pl.num_programs(1) - 1))

    for b in range(SWA_NB):
        blk = n * SWA_NB + b
        prev_blk = jnp.maximum(blk - 1, 0)
        s_band = bands[b]
        s = jnp.where(lower, s_band[BLOCK:], s_band[:BLOCK]) * (SWA_SCALE * LOG2E) + bias
        if b == 0:
            s = jnp.where(lower | (blk > 0), s, NEG_BIG)

        m = jnp.maximum(jnp.max(s, axis=0, keepdims=True), sink)
        e = jnp.exp2(s - m)
        denom = jnp.sum(e, axis=0, keepdims=True) + jnp.exp2(sink - m)
        p = (e * (1.0 / denom)).astype(BF16)
        p_cur = p * lower_bf
        p_prev = p - p_cur
        ot = _dot(vt_ref[blk], p_cur) + _dot(vt_ref[prev_blk], p_prev)
        o_ref[b * BLOCK:(b + 1) * BLOCK, :] = jnp.concatenate(
            [ot[:, g * BLOCK:(g + 1) * BLOCK] for g in range(SWA_GROUP)], axis=0).T


def _swa_attn(rel_bias, bucket_tbl, qst, ks, vst, sink_rows):
    s = ks.shape[1]
    rows = SWA_NB * BLOCK
    n_steps = s // rows
    est = (4 * SWA_GROUP * SWA_HEAD_DIM * rows * 2 + 2 * s * V7X_LANES * 2 + 2 * s * SWA_HEAD_DIM * 2
           + 2 * rows * SWA_GROUP * SWA_HEAD_DIM * 4 + SWA_KV_HEADS * BLOCK * SWA_COLS * 4
           + SWA_NB * 2 * BLOCK * SWA_COLS * 4 + 10 * SWA_NB * BLOCK * SWA_COLS * 4)
    return pl.pallas_call(
        _swa_kernel,
        grid=(SWA_KV_HEADS, n_steps),
        in_specs=[
            pl.BlockSpec(memory_space=pltpu.SMEM),
            pl.BlockSpec((BLOCK, BLOCK), lambda kv, n: (0, 0)),
            pl.BlockSpec((SWA_GROUP, SWA_HEAD_DIM, rows), lambda kv, n: (kv, 0, n)),
            pl.BlockSpec((SWA_GROUP, SWA_HEAD_DIM, rows),
                         lambda kv, n: (kv, 0, jnp.minimum(n + 1, n_steps - 1))),
            pl.BlockSpec((None, s, SWA_HEAD_DIM), lambda kv, n: (kv, 0, 0)),
            pl.BlockSpec((None, s // BLOCK, SWA_HEAD_DIM, BLOCK), lambda kv, n: (kv, 0, 0, 0)),
            pl.BlockSpec((None, 1, SWA_COLS), lambda kv, n: (kv, 0, 0)),
        ],
        out_specs=pl.BlockSpec((rows, SWA_GROUP * SWA_HEAD_DIM), lambda kv, n: (n, kv)),
        out_shape=jax.ShapeDtypeStruct((s, SWA_WIDTH), F32),
        scratch_shapes=[pltpu.VMEM((SWA_KV_HEADS, BLOCK, SWA_COLS), F32),
                        pltpu.VMEM((SWA_NB, 2 * BLOCK, SWA_COLS), F32)],
        compiler_params=pltpu.CompilerParams(
            dimension_semantics=("arbitrary", "arbitrary"),
            vmem_limit_bytes=_vmem_limit(est)),
        name="swa_attn",
    )(rel_bias, bucket_tbl, qst, qst, ks, vst, sink_rows)


OUTPROJ_TM = 512
OUTPROJ_CHUNKS = 2


def _out_proj_kernel(am_ref, as_ref, h_ref, mg_ref, sg_ref, w_ref, g_ref, b_ref, o_ref, wb_ref):
    @pl.when(pl.program_id(0) == 0)
    def _():
        wb_ref[...] = w_ref[...].astype(BF16)

    chunk = o_ref.shape[0] // OUTPROJ_CHUNKS
    for c in range(OUTPROJ_CHUNKS):
        rows = slice(c * chunk, (c + 1) * chunk)
        nm = _rms_norm(am_ref[rows, :], mg_ref[...]).astype(BF16)
        ns = _rms_norm(as_ref[rows, :], sg_ref[...]).astype(BF16)
        mixed = _dot(nm, wb_ref[0:MLA_WIDTH, :]) + _dot(ns, wb_ref[MLA_WIDTH:MLA_WIDTH + SWA_WIDTH, :])
        o_ref[rows, :] = _layer_norm(ALPHA * h_ref[rows, :] + mixed, g_ref[...], b_ref[...])


def _out_proj(a_mla, a_swa, h, mla_g, swa_g, w_out, g, b):
    s, d = h.shape
    tm = OUTPROJ_TM
    est = (2 * tm * (MLA_WIDTH + SWA_WIDTH) * 4 + 4 * tm * d * 4 + (MLA_WIDTH + SWA_WIDTH) * d * (4 + 2)
           + 4 * tm * d * 4)
    const = lambda i: (0, 0)
    return pl.pallas_call(
        _out_proj_kernel,
        grid=(s // tm,),
        in_specs=[
            pl.BlockSpec((tm, MLA_WIDTH), lambda i: (i, 0)),
            pl.BlockSpec((tm, SWA_WIDTH), lambda i: (i, 0)),
            pl.BlockSpec((tm, d), lambda i: (i, 0)),
            pl.BlockSpec((1, MLA_WIDTH), const),
            pl.BlockSpec((1, SWA_WIDTH), const),
            pl.BlockSpec((MLA_WIDTH + SWA_WIDTH, d), const, pipeline_mode=pl.Buffered(1)),
            pl.BlockSpec((1, d), const),
            pl.BlockSpec((1, d), const),
        ],
        out_specs=pl.BlockSpec((tm, d), lambda i: (i, 0)),
        out_shape=jax.ShapeDtypeStruct((s, d), F32),
        scratch_shapes=[pltpu.VMEM((MLA_WIDTH + SWA_WIDTH, d), BF16)],
        compiler_params=pltpu.CompilerParams(
            dimension_semantics=("arbitrary",),
            vmem_limit_bytes=_vmem_limit(est)),
        name="out_proj",
    )(a_mla, a_swa, h, mla_g, swa_g, w_out, g, b)


PLE_TM = 512


def _ple_kernel(h_ref, p_ref, wg_ref, wp_ref, o_ref, wgb_ref, wpb_ref):
    @pl.when(pl.program_id(0) == 0)
    def _():
        wgb_ref[...] = wg_ref[...].astype(BF16)
        wpb_ref[...] = wp_ref[...].astype(BF16)

    h = h_ref[...]
    gate = _dot(h.astype(BF16), wgb_ref[...])
    proj = _dot(p_ref[...].astype(BF16), wpb_ref[...])
    o_ref[...] = h + (1.0 / (1.0 + jnp.exp(-gate))) * proj


def _ple(h, p, w_gate, w_proj):
    s, d = h.shape
    tm = PLE_TM
    est = 4 * tm * d * 4 + 2 * tm * PLE_DIM * 4 + (d + PLE_DIM) * d * (4 + 2) + 4 * tm * d * 4
    const = lambda i: (0, 0)
    return pl.pallas_call(
        _ple_kernel,
        grid=(s // tm,),
        in_specs=[
            pl.BlockSpec((tm, d), lambda i: (i, 0)),
            pl.BlockSpec((tm, PLE_DIM), lambda i: (i, 0)),
            pl.BlockSpec((d, d), const, pipeline_mode=pl.Buffered(1)),
            pl.BlockSpec((PLE_DIM, d), const, pipeline_mode=pl.Buffered(1)),
        ],
        out_specs=pl.BlockSpec((tm, d), lambda i: (i, 0)),
        out_shape=jax.ShapeDtypeStruct((s, d), F32),
        scratch_shapes=[pltpu.VMEM((d, d), BF16), pltpu.VMEM((PLE_DIM, d), BF16)],
        compiler_params=pltpu.CompilerParams(
            dimension_semantics=("arbitrary",),
            vmem_limit_bytes=_vmem_limit(est)),
        name="ple",
    )(h, p, w_gate, w_proj)


def _split_w_in(w_in):
    d = w_in.shape[0]
    b_kr = Q_LORA + KV_LORA
    b_qs = b_kr + QK_ROPE
    b_ks = b_qs + SWA_WIDTH
    b_vs = b_ks + SWA_KV_WIDTH
    wz = jnp.concatenate([w_in[:, :b_kr], w_in[:, b_ks:b_vs]], axis=1)
    pad = jnp.zeros((d, V7X_LANES - QK_ROPE), w_in.dtype)
    wt = jnp.concatenate([w_in[:, b_qs:b_ks], w_in[:, b_vs:], w_in[:, b_kr:b_qs], pad], axis=1)
    return wz, wt


def _pack_w_uq(w_uq):
    r = w_uq.shape[0]
    w = w_uq.reshape(r, MLA_HEADS, QK_NOPE + QK_ROPE)
    w = jnp.pad(w, ((0, 0), (0, 0), (0, MLA_QK_PAD - QK_NOPE - QK_ROPE)))
    return w.reshape(r, MLA_HEADS * MLA_QK_PAD)


def _rope_freq_col():
    freqs = ROPE_THETA ** (-jnp.arange(0, QK_ROPE, 2, dtype=F32) / QK_ROPE)
    return freqs.reshape(HALF_ROPE, 1)


def kernel(x, p, positions, rel_bias, ln1_g, ln1_b, ffn1_w1, ffn1_w3, ffn1_w2, w_in, q_norm_g, w_uq,
           kv_norm_g, w_ukv, swa_sinks, mla_out_g, swa_out_g, w_out, ln2_g, ln2_b, ffn2_w1, ffn2_w3,
           ffn2_w2, ln3_g, ln3_b, ple_w_gate, ple_w_proj):
    assert x.shape == (1, SEQ, D_MODEL) and DEPTH == 1
    row = lambda a: a.reshape(1, -1)
    h = x[0]
    pos_row = positions[0].astype(F32).reshape(1, SEQ)
    freq_col = _rope_freq_col()
    bucket_tbl = jnp.asarray(_swa_bucket_table())

    i = 0
    h = _ffn_ln(h, ffn1_w1[i], ffn1_w3[i], ffn1_w2[i], row(ln1_g[i]), row(ln1_b[i]))

    w_ukv_h = w_ukv[i].reshape(KV_LORA, MLA_HEADS, QK_NOPE + V_HEAD)
    wkn = w_ukv_h[:, :, :QK_NOPE].reshape(KV_LORA, MLA_WIDTH).astype(BF16)
    wv = w_ukv_h[:, :, QK_NOPE:].reshape(KV_LORA, MLA_WIDTH).astype(BF16)
    wz, wt = _split_w_in(w_in[i])
    qt, k, vt, qst, ks, vst = _in_proj(
        h, wz.astype(BF16), wt.astype(BF16), row(q_norm_g[i]), row(kv_norm_g[i]),
        _pack_w_uq(w_uq[i]).astype(BF16), wkn, wv, pos_row, freq_col)

    a_mla = _mla_flash(qt, k, vt)
    sink_rows = jnp.repeat(swa_sinks[i], BLOCK).reshape(SWA_KV_HEADS, 1, SWA_COLS)
    a_swa = _swa_attn(rel_bias, bucket_tbl, qst, ks, vst, sink_rows)

    h = _out_proj(a_mla, a_swa, h, row(mla_out_g[i]), row(swa_out_g[i]), w_out[i],
                  row(ln2_g[i]), row(ln2_b[i]))

    h = _ffn_ln(h, ffn2_w1[i], ffn2_w3[i], ffn2_w2[i], row(ln3_g[i]), row(ln3_b[i]))
    h = _ple(h, p[i, 0], ple_w_gate[i], ple_w_proj[i])
    return h[None]
```

```python
import math

import numpy as np
import jax
import jax.numpy as jnp
from jax import lax
from jax.experimental import pallas as pl
from jax.experimental.pallas import tpu as pltpu

F32 = jnp.float32
BF16 = jnp.bfloat16

D_MODEL = 2048
SEQ = 8192
DEPTH = 1
PLE_DIM = 256
MLA_HEADS = 8
Q_LORA = 512
KV_LORA = 512
QK_NOPE = 128
QK_ROPE = 64
V_HEAD = 128
ROPE_THETA = 10000.0
SWA_HEADS = 16
SWA_KV_HEADS = 2
SWA_GROUP = SWA_HEADS // SWA_KV_HEADS
SWA_HEAD_DIM = 64
WINDOW = 128
BLOCK = 128
REL_BUCKETS = 32
REL_MAX_DIST = 128
D_FF = 5632
ALPHA = (2.0 * DEPTH) ** 0.25
EPS = 1e-5
MLA_WIDTH = MLA_HEADS * V_HEAD
SWA_WIDTH = SWA_HEADS * SWA_HEAD_DIM
SWA_KV_WIDTH = SWA_KV_HEADS * SWA_HEAD_DIM

V7X_LANES = 128
V7X_VMEM_BYTES = 64 * 1024 * 1024
MIB = 1024 * 1024

V7X_BF16_SUBLANES = 16

MLA_QK_PAD = 2 * V7X_LANES
MLA_V_ROWS = V_HEAD + V7X_BF16_SUBLANES
NEG_BIG = float(np.finfo(np.float32).min)
LOG2E = math.log2(math.e)


def _vmem_limit(estimate_bytes):
    return int(min(estimate_bytes + 8 * MIB, V7X_VMEM_BYTES - 4 * MIB))


def _layer_norm(y, g, b):
    mu = jnp.mean(y, axis=-1, keepdims=True)
    yc = y - mu
    var = jnp.mean(yc * yc, axis=-1, keepdims=True)
    return yc * lax.rsqrt(var + EPS) * g + b


def _rms_norm(x, g):
    ms = jnp.mean(x * x, axis=-1, keepdims=True)
    return x * lax.rsqrt(ms + EPS) * g


def _dot(a, b):
    return jnp.dot(a, b, preferred_element_type=F32)


def _dot_nt(a, b):
    return lax.dot_general(a, b, (((1,), (1,)), ((), ())), preferred_element_type=F32)


FFN_TM = 2048
FFN_TF = 256
FFN_ROW_CHUNK = 512
FFN_LN_ROWS = 128


def _ffn_ln_kernel(x_hbm, w1_ref, w3_ref, w2_ref, g_ref, b_ref, o_hbm,
                   xs_ref, xb_ref, acc_ref, w1b_ref, w3b_ref, w2b_ref, x_sem, o_sem):
    i, f = pl.program_id(0), pl.program_id(1)
    n_tiles, n_f = pl.num_programs(0), pl.num_programs(1)
    tm = xs_ref.shape[0]
    n_out_chunks = tm // FFN_LN_ROWS

    def x_copy(tile):
        return pltpu.make_async_copy(x_hbm.at[pl.ds(tile * tm, tm), :], xs_ref, x_sem)

    def out_copy(tile, c):
        return pltpu.make_async_copy(
            acc_ref.at[pl.ds(c * FFN_LN_ROWS, FFN_LN_ROWS), :],
            o_hbm.at[pl.ds(tile * tm + c * FFN_LN_ROWS, FFN_LN_ROWS), :], o_sem.at[c])

    @pl.when(f == 0)
    def _():
        @pl.when(i == 0)
        def _():
            x_copy(0).start()

        x_copy(i).wait()

        @pl.when(i > 0)
        def _():
            for c in range(n_out_chunks):
                out_copy(i - 1, c).wait()

        x = xs_ref[...]
        xb_ref[...] = x.astype(BF16)
        acc_ref[...] = (2.0 * ALPHA) * x

        @pl.when(i + 1 < n_tiles)
        def _():
            x_copy(i + 1).start()

    w1b_ref[...] = w1_ref[...].astype(BF16)
    w3b_ref[...] = w3_ref[...].astype(BF16)
    w2b_ref[...] = w2_ref[...].astype(BF16)

    for c in range(tm // FFN_ROW_CHUNK):
        rows = slice(c * FFN_ROW_CHUNK, (c + 1) * FFN_ROW_CHUNK)
        xb = xb_ref[rows, :]
        gate = _dot(xb, w1b_ref[...])
        up = _dot(xb, w3b_ref[...])
        hidden = gate * (1.0 / (1.0 + jnp.exp(-gate))) * up
        acc_ref[rows, :] += _dot(hidden.astype(BF16), w2b_ref[...])

    @pl.when(f == n_f - 1)
    def _():
        for c in range(n_out_chunks):
            rows = slice(c * FFN_LN_ROWS, (c + 1) * FFN_LN_ROWS)
            acc_ref[rows, :] = _layer_norm(0.5 * acc_ref[rows, :], g_ref[...], b_ref[...])
            out_copy(i, c).start()

        @pl.when(i == n_tiles - 1)
        def _():
            for c in range(n_out_chunks):
                out_copy(i, c).wait()


def _ffn_ln(x, w1, w3, w2, g, b):
    s, d = x.shape
    dff = w1.shape[1]
    tm, tf = FFN_TM, FFN_TF
    est = (tm * d * 4 + tm * d * 2 + tm * d * 4 + 2 * 3 * d * tf * 4 + 3 * d * tf * 2
           + 3 * FFN_ROW_CHUNK * tf * 4)
    return pl.pallas_call(
        _ffn_ln_kernel,
        grid=(s // tm, dff // tf),
        in_specs=[
            pl.BlockSpec(memory_space=pl.ANY),
            pl.BlockSpec((d, tf), lambda i, f: (0, f)),
            pl.BlockSpec((d, tf), lambda i, f: (0, f)),
            pl.BlockSpec((tf, d), lambda i, f: (f, 0)),
            pl.BlockSpec((1, d), lambda i, f: (0, 0)),
            pl.BlockSpec((1, d), lambda i, f: (0, 0)),
        ],
        out_specs=pl.BlockSpec(memory_space=pl.ANY),
        out_shape=jax.ShapeDtypeStruct((s, d), F32),
        scratch_shapes=[pltpu.VMEM((tm, d), F32), pltpu.VMEM((tm, d), BF16), pltpu.VMEM((tm, d), F32),
                        pltpu.VMEM((d, tf), BF16), pltpu.VMEM((d, tf), BF16), pltpu.VMEM((tf, d), BF16),
                        pltpu.SemaphoreType.DMA, pltpu.SemaphoreType.DMA((tm // FFN_LN_ROWS,))],
        compiler_params=pltpu.CompilerParams(
            dimension_semantics=("arbitrary", "arbitrary"),
            vmem_limit_bytes=_vmem_limit(est)),
        name="ffn_ln",
    )(x, w1, w3, w2, g, b)


INPROJ_TM = 512
Z_CQ = 0
Z_CKV = Z_CQ + Q_LORA
Z_KS = Z_CKV + KV_LORA
Z_WIDTH = Z_KS + SWA_KV_WIDTH
ZT_QS = 0
ZT_VS = ZT_QS + SWA_WIDTH
ZT_KR = ZT_VS + SWA_KV_WIDTH
ZT_ROWS = ZT_KR + V7X_LANES
HALF_ROPE = QK_ROPE // 2


def _rope_rows(x1, x2, cos, sin):
    return x1 * cos - x2 * sin, x1 * sin + x2 * cos


def _in_proj_kernel(h_ref, wz_ref, wt_ref, qg_ref, kvg_ref, wuqt_ref, wkn_ref, wvt_ref, pos_ref,
                    freq_ref, qt_ref, k_ref, vt_ref, qst_ref, ks_ref, vst_ref):
    tm = h_ref.shape[0]
    hb = h_ref[...].astype(BF16)
    z = _dot(hb, wz_ref[...])
    zt = _dot_nt(wt_ref[...], hb)

    ang = freq_ref[...] * pos_ref[...]
    cos = jnp.cos(ang)
    sin = jnp.sin(ang)

    cq = _rms_norm(z[:, Z_CQ:Z_CQ + Q_LORA], qg_ref[...]).astype(BF16)
    ckv = _rms_norm(z[:, Z_CKV:Z_CKV + KV_LORA], kvg_ref[...]).astype(BF16)
    qt = _dot_nt(wuqt_ref[...], cq)
    kn = _dot(ckv, wkn_ref[...])
    vt = _dot_nt(wvt_ref[...], ckv)

    kr1, kr2 = _rope_rows(zt[ZT_KR:ZT_KR + HALF_ROPE], zt[ZT_KR + HALF_ROPE:ZT_KR + QK_ROPE], cos, sin)
    k_rope = jnp.concatenate([kr1, kr2, zt[ZT_KR + QK_ROPE:ZT_KR + V7X_LANES]], axis=0).T.astype(BF16)

    for h in range(MLA_HEADS):
        base = h * MLA_QK_PAD
        r0 = base + QK_NOPE
        qt_ref[base:r0, :] = qt[base:r0].astype(BF16)
        q1, q2 = _rope_rows(qt[r0:r0 + HALF_ROPE], qt[r0 + HALF_ROPE:r0 + QK_ROPE], cos, sin)
        qt_ref[r0:r0 + HALF_ROPE, :] = q1.astype(BF16)
        qt_ref[r0 + HALF_ROPE:r0 + QK_ROPE, :] = q2.astype(BF16)
        qt_ref[r0 + QK_ROPE:base + MLA_QK_PAD, :] = qt[r0 + QK_ROPE:base + MLA_QK_PAD].astype(BF16)
        k_ref[h, :, 0:QK_NOPE] = kn[:, h * QK_NOPE:(h + 1) * QK_NOPE].astype(BF16)
        k_ref[h, :, QK_NOPE:MLA_QK_PAD] = k_rope
        vt_ref[h, 0, 0:V_HEAD, :] = vt[h * V_HEAD:(h + 1) * V_HEAD].astype(BF16)
        vt_ref[h, 0, V_HEAD:MLA_V_ROWS, :] = jnp.ones((MLA_V_ROWS - V_HEAD, tm), BF16)

    qst_ref[...] = zt[ZT_QS:ZT_QS + SWA_WIDTH].reshape(SWA_HEADS, SWA_HEAD_DIM, tm).astype(BF16)
    for kv in range(SWA_KV_HEADS):
        lo = Z_KS + kv * SWA_HEAD_DIM
        ks_ref[kv] = z[:, lo:lo + SWA_HEAD_DIM].astype(BF16)
        lo = ZT_VS + kv * SWA_HEAD_DIM
        for c in range(tm // BLOCK):
            vst_ref[kv, c] = zt[lo:lo + SWA_HEAD_DIM, c * BLOCK:(c + 1) * BLOCK].astype(BF16)


def _in_proj(h, wz, wt, q_g, kv_g, wuq_t, wkn, wv_t, pos_row, freq_col):
    s, d = h.shape
    tm = INPROJ_TM
    assert tm == MLA_TK
    est = (2 * tm * d * 4 + 2 * d * (Z_WIDTH + ZT_ROWS) * 2
           + 2 * Q_LORA * (MLA_HEADS * MLA_QK_PAD + 2 * MLA_WIDTH) * 2
           + 2 * tm * (2 * MLA_HEADS * MLA_QK_PAD + MLA_WIDTH + SWA_WIDTH + 4 * V7X_LANES) * 2
           + 4 * tm * (Z_WIDTH + ZT_ROWS + 2 * MLA_HEADS * MLA_QK_PAD) * 4)
    const = lambda i: (0, 0)
    return pl.pallas_call(
        _in_proj_kernel,
        grid=(s // tm,),
        in_specs=[
            pl.BlockSpec((tm, d), lambda i: (i, 0)),
            pl.BlockSpec((d, Z_WIDTH), const),
            pl.BlockSpec((ZT_ROWS, d), const),
            pl.BlockSpec((1, Q_LORA), const),
            pl.BlockSpec((1, KV_LORA), const),
            pl.BlockSpec((MLA_HEADS * MLA_QK_PAD, Q_LORA), const),
            pl.BlockSpec((KV_LORA, MLA_WIDTH), const),
            pl.BlockSpec((MLA_WIDTH, KV_LORA), const),
            pl.BlockSpec((1, tm), lambda i: (0, i)),
            pl.BlockSpec((HALF_ROPE, 1), const),
        ],
        out_specs=[
            pl.BlockSpec((MLA_HEADS * MLA_QK_PAD, tm), lambda i: (0, i)),
            pl.BlockSpec((MLA_HEADS, tm, MLA_QK_PAD), lambda i: (0, i, 0)),
            pl.BlockSpec((MLA_HEADS, 1, MLA_V_ROWS, tm), lambda i: (0, i, 0, 0)),
            pl.BlockSpec((SWA_HEADS, SWA_HEAD_DIM, tm), lambda i: (0, 0, i)),
            pl.BlockSpec((SWA_KV_HEADS, tm, SWA_HEAD_DIM), lambda i: (0, i, 0)),
            pl.BlockSpec((SWA_KV_HEADS, tm // BLOCK, SWA_HEAD_DIM, BLOCK), lambda i: (0, i, 0, 0)),
        ],
        out_shape=[
            jax.ShapeDtypeStruct((MLA_HEADS * MLA_QK_PAD, s), BF16),
            jax.ShapeDtypeStruct((MLA_HEADS, s, MLA_QK_PAD), BF16),
            jax.ShapeDtypeStruct((MLA_HEADS, s // tm, MLA_V_ROWS, tm), BF16),
            jax.ShapeDtypeStruct((SWA_HEADS, SWA_HEAD_DIM, s), BF16),
            jax.ShapeDtypeStruct((SWA_KV_HEADS, s, SWA_HEAD_DIM), BF16),
            jax.ShapeDtypeStruct((SWA_KV_HEADS, s // BLOCK, SWA_HEAD_DIM, BLOCK), BF16),
        ],
        compiler_params=pltpu.CompilerParams(
            dimension_semantics=("parallel",),
            vmem_limit_bytes=_vmem_limit(est)),
        name="in_proj",
    )(h, wz, wt, q_g, kv_g, wuq_t, wkn, wv_t, pos_row, freq_col)


MLA_TQ = 512
MLA_TK = 512
MLA_HB = 2
MLA_SCALE = (QK_NOPE + QK_ROPE) ** -0.5
MLA_SCALE_LOG2E = MLA_SCALE * LOG2E


def _mla_flash_kernel(qt_ref, k_ref, vt_ref, o_ref, s_ref, smax_ref, p_ref, m_ref, corr_ref, acc_ref):
    i = pl.program_id(1)
    tk = MLA_TK
    heads = range(MLA_HB)

    def qk(t):
        start = pl.multiple_of(t * tk, tk)
        for h in heads:
            cols = slice(h * MLA_QK_PAD, (h + 1) * MLA_QK_PAD)
            s = _dot(k_ref[h, pl.ds(start, tk), :], qt_ref[cols, :])
            s_ref[h] = s
            smax_ref[h] = jnp.max(s, axis=0, keepdims=True)

    def softmax(scores, masked):
        for h in heads:
            s, smax = scores[h]
            s = s * MLA_SCALE_LOG2E
            if masked:
                key = lax.broadcasted_iota(jnp.int32, s.shape, 0)
                qry = lax.broadcasted_iota(jnp.int32, s.shape, 1)
                s = jnp.where(key <= qry, s, NEG_BIG)
                tile_max = jnp.max(s, axis=0, keepdims=True)
            else:
                tile_max = smax * MLA_SCALE_LOG2E
            m_prev = m_ref[h]
            m_new = jnp.maximum(m_prev, tile_max)
            p_ref[h] = jnp.exp2(s - m_new).astype(BF16)
            corr_ref[h] = jnp.exp2(m_prev - m_new)
            m_ref[h] = m_new

    def pv(t):
        for h in heads:
            acc_ref[h] = corr_ref[h] * acc_ref[h] + _dot(vt_ref[h, t], p_ref[h])

    def load_scores():
        return [(s_ref[h], smax_ref[h]) for h in heads]

    def init():
        m_ref[...] = jnp.full_like(m_ref, NEG_BIG)
        acc_ref[...] = jnp.zeros_like(acc_ref)

    def finish():
        softmax(load_scores(), masked=True)
        pv(i)
        for h in heads:
            o_ref[:, h * V_HEAD:(h + 1) * V_HEAD] = (
                acc_ref[h, 0:V_HEAD, :] / acc_ref[h, V_HEAD:V_HEAD + 1, :]).T

    @pl.when(i == 0)
    def _():
        init()
        qk(0)
        finish()

    @pl.when(i >= 1)
    def _():
        init()
        qk(0)
        scores = load_scores()
        qk(1)
        softmax(scores, masked=False)

        def body(t, carry):
            pv(t)
            scores = load_scores()
            qk(t + 2)
            softmax(scores, masked=False)
            return carry

        lax.fori_loop(0, i - 1, body, 0)
        pv(i - 1)
        finish()


def _mla_flash(qt, k, vt):
    s = k.shape[1]
    tq, tk, hb = MLA_TQ, MLA_TK, MLA_HB
    assert tq == tk
    est = (2 * hb * tq * MLA_QK_PAD * 2 + 2 * hb * s * MLA_QK_PAD * 2 + 2 * hb * s * MLA_V_ROWS * 2
           + 2 * hb * tq * V_HEAD * 4 + hb * tk * tq * (4 + 2) + hb * tq * MLA_V_ROWS * 4
           + 4 * hb * tq * tk * 4)
    return pl.pallas_call(
        _mla_flash_kernel,
        grid=(MLA_HEADS // hb, s // tq),
        in_specs=[
            pl.BlockSpec((hb * MLA_QK_PAD, tq), lambda g, i: (g, i)),
            pl.BlockSpec((hb, s, MLA_QK_PAD), lambda g, i: (g, 0, 0)),
            pl.BlockSpec((hb, s // tk, MLA_V_ROWS, tk), lambda g, i: (g, 0, 0, 0)),
        ],
        out_specs=pl.BlockSpec((tq, hb * V_HEAD), lambda g, i: (i, g)),
        out_shape=jax.ShapeDtypeStruct((s, MLA_WIDTH), F32),
        scratch_shapes=[pltpu.VMEM((hb, tk, tq), F32), pltpu.VMEM((hb, 1, tq), F32),
                        pltpu.VMEM((hb, tk, tq), BF16),
                        pltpu.VMEM((hb, 1, tq), F32), pltpu.VMEM((hb, 1, tq), F32),
                        pltpu.VMEM((hb, MLA_V_ROWS, tq), F32)],
        compiler_params=pltpu.CompilerParams(
            dimension_semantics=("parallel", "arbitrary"),
            vmem_limit_bytes=_vmem_limit(est)),
        name="mla_flash",
    )(qt, k, vt)


SWA_SCALE = SWA_HEAD_DIM ** -0.5
SWA_COLS = SWA_GROUP * BLOCK
SWA_NB = 4


def _t5_bucket_np(dist):
    n = np.maximum(dist, 0)
    max_exact = REL_BUCKETS // 2
    large = max_exact + (np.log(np.maximum(n, 1).astype(np.float32) / max_exact)
                         / math.log(REL_MAX_DIST / max_exact)
                         * (REL_BUCKETS - max_exact)).astype(np.int32)
    large = np.minimum(large, REL_BUCKETS - 1)
    return np.where(n < max_exact, n, large).astype(np.int32)


def _swa_bucket_table():
    j = np.arange(BLOCK)[:, None]
    i = np.arange(BLOCK)[None, :]
    dist = np.where(j <= i, i - j, BLOCK + i - j)
    return _t5_bucket_np(dist)


def _swa_kernel(rb_ref, bkt_ref, qt_ref, qt_next_ref, k_ref, vt_ref, sink_ref, o_ref, bias_ref, s_ref):
    kv = pl.program_id(0)
    n = pl.program_id(1)

    @pl.when((kv == 0) & (n == 0))
    def _():
        bkt = bkt_ref[...]
        for h in range(SWA_HEADS):
            acc = jnp.zeros((BLOCK, BLOCK), F32)
            for b in range(REL_BUCKETS):
                acc = jnp.where(bkt == b, rb_ref[b, h], acc)
            g = h % SWA_GROUP
            bias_ref[h // SWA_GROUP, :, g * BLOCK:(g + 1) * BLOCK] = acc * LOG2E

    key = lax.broadcasted_iota(jnp.int32, (BLOCK, SWA_COLS), 0)
    qry = lax.broadcasted_iota(jnp.int32, (BLOCK, SWA_COLS), 1) & (BLOCK - 1)
    lower = key <= qry
    lower_bf = jnp.where(lower, 1.0, 0.0).astype(BF16)
    bias = bias_ref[kv]
    sink = sink_ref[...] * LOG2E

    def band_scores(q_ref, step):
        for b in range(SWA_NB):
            blk = step * SWA_NB + b
            qt = jnp.concatenate([q_ref[g, :, b * BLOCK:(b + 1) * BLOCK] for g in range(SWA_GROUP)],
                                 axis=1)
            cur = pl.multiple_of(blk * BLOCK, BLOCK)
            prev = pl.multiple_of(jnp.maximum(blk - 1, 0) * BLOCK, BLOCK)
            k_band = jnp.concatenate([k_ref[pl.ds(prev, BLOCK), :], k_ref[pl.ds(cur, BLOCK), :]], axis=0)
            s_ref[b] = _dot(k_band, qt)

    @pl.when(n == 0)
    def _():
        band_scores(qt_ref, 0)

    bands = [s_ref[b] for b in range(SWA_NB)]
    band_scores(qt_next_ref, jnp.minimum(n + 1, pl.num_programs(1) - 1))

    for b in range(SWA_NB):
        blk = n * SWA_NB + b
        prev_blk = jnp.maximum(blk - 1, 0)
        s_band = bands[b]
        s = jnp.where(lower, s_band[BLOCK:], s_band[:BLOCK]) * (SWA_SCALE * LOG2E) + bias
        if b == 0:
            s = jnp.where(lower | (blk > 0), s, NEG_BIG)

        m = jnp.maximum(jnp.max(s, axis=0, keepdims=True), sink)
        e = jnp.exp2(s - m)
        denom = jnp.sum(e, axis=0, keepdims=True) + jnp.exp2(sink - m)
        p = (e * (1.0 / denom)).astype(BF16)
        p_cur = p * lower_bf
        p_prev = p - p_cur
        ot = _dot(vt_ref[blk], p_cur) + _dot(vt_ref[prev_blk], p_prev)
        o_ref[b * BLOCK:(b + 1) * BLOCK, :] = jnp.concatenate(
            [ot[:, g * BLOCK:(g + 1) * BLOCK] for g in range(SWA_GROUP)], axis=0).T


def _swa_attn(rel_bias, bucket_tbl, qst, ks, vst, sink_rows):
    s = ks.shape[1]
    rows = SWA_NB * BLOCK
    n_steps = s // rows
    est = (4 * SWA_GROUP * SWA_HEAD_DIM * rows * 2 + 2 * s * V7X_LANES * 2 + 2 * s * SWA_HEAD_DIM * 2
           + 2 * rows * SWA_GROUP * SWA_HEAD_DIM * 4 + SWA_KV_HEADS * BLOCK * SWA_COLS * 4
           + SWA_NB * 2 * BLOCK * SWA_COLS * 4 + 10 * SWA_NB * BLOCK * SWA_COLS * 4)
    return pl.pallas_call(
        _swa_kernel,
        grid=(SWA_KV_HEADS, n_steps),
        in_specs=[
            pl.BlockSpec(memory_space=pltpu.SMEM),
            pl.BlockSpec((BLOCK, BLOCK), lambda kv, n: (0, 0)),
            pl.BlockSpec((SWA_GROUP, SWA_HEAD_DIM, rows), lambda kv, n: (kv, 0, n)),
            pl.BlockSpec((SWA_GROUP, SWA_HEAD_DIM, rows),
                         lambda kv, n: (kv, 0, jnp.minimum(n + 1, n_steps - 1))),
            pl.BlockSpec((None, s, SWA_HEAD_DIM), lambda kv, n: (kv, 0, 0)),
            pl.BlockSpec((None, s // BLOCK, SWA_HEAD_DIM, BLOCK), lambda kv, n: (kv, 0, 0, 0)),
            pl.BlockSpec((None, 1, SWA_COLS), lambda kv, n: (kv, 0, 0)),
        ],
        out_specs=pl.BlockSpec((rows, SWA_GROUP * SWA_HEAD_DIM), lambda kv, n: (n, kv)),
        out_shape=jax.ShapeDtypeStruct((s, SWA_WIDTH), F32),
        scratch_shapes=[pltpu.VMEM((SWA_KV_HEADS, BLOCK, SWA_COLS), F32),
                        pltpu.VMEM((SWA_NB, 2 * BLOCK, SWA_COLS), F32)],
        compiler_params=pltpu.CompilerParams(
            dimension_semantics=("arbitrary", "arbitrary"),
            vmem_limit_bytes=_vmem_limit(est)),
        name="swa_attn",
    )(rel_bias, bucket_tbl, qst, qst, ks, vst, sink_rows)


OUTPROJ_TM = 512
OUTPROJ_CHUNKS = 2


def _out_proj_kernel(am_ref, as_ref, h_ref, mg_ref, sg_ref, w_ref, g_ref, b_ref, o_ref, wb_ref):
    @pl.when(pl.program_id(0) == 0)
    def _():
        wb_ref[...] = w_ref[...].astype(BF16)

    chunk = o_ref.shape[0] // OUTPROJ_CHUNKS
    for c in range(OUTPROJ_CHUNKS):
        rows = slice(c * chunk, (c + 1) * chunk)
        nm = _rms_norm(am_ref[rows, :], mg_ref[...]).astype(BF16)
        ns = _rms_norm(as_ref[rows, :], sg_ref[...]).astype(BF16)
        mixed = _dot(nm, wb_ref[0:MLA_WIDTH, :]) + _dot(ns, wb_ref[MLA_WIDTH:MLA_WIDTH + SWA_WIDTH, :])
        o_ref[rows, :] = _layer_norm(ALPHA * h_ref[rows, :] + mixed, g_ref[...], b_ref[...])


def _out_proj(a_mla, a_swa, h, mla_g, swa_g, w_out, g, b):
    s, d = h.shape
    tm = OUTPROJ_TM
    est = (2 * tm * (MLA_WIDTH + SWA_WIDTH) * 4 + 4 * tm * d * 4 + (MLA_WIDTH + SWA_WIDTH) * d * (4 + 2)
           + 4 * tm * d * 4)
    const = lambda i: (0, 0)
    return pl.pallas_call(
        _out_proj_kernel,
        grid=(s // tm,),
        in_specs=[
            pl.BlockSpec((tm, MLA_WIDTH), lambda i: (i, 0)),
            pl.BlockSpec((tm, SWA_WIDTH), lambda i: (i, 0)),
            pl.BlockSpec((tm, d), lambda i: (i, 0)),
            pl.BlockSpec((1, MLA_WIDTH), const),
            pl.BlockSpec((1, SWA_WIDTH), const),
            pl.BlockSpec((MLA_WIDTH + SWA_WIDTH, d), const, pipeline_mode=pl.Buffered(1)),
            pl.BlockSpec((1, d), const),
            pl.BlockSpec((1, d), const),
        ],
        out_specs=pl.BlockSpec((tm, d), lambda i: (i, 0)),
        out_shape=jax.ShapeDtypeStruct((s, d), F32),
        scratch_shapes=[pltpu.VMEM((MLA_WIDTH + SWA_WIDTH, d), BF16)],
        compiler_params=pltpu.CompilerParams(
            dimension_semantics=("arbitrary",),
            vmem_limit_bytes=_vmem_limit(est)),
        name="out_proj",
    )(a_mla, a_swa, h, mla_g, swa_g, w_out, g, b)


PLE_TM = 512


def _ple_kernel(h_ref, p_ref, wg_ref, wp_ref, o_ref, wgb_ref, wpb_ref):
    @pl.when(pl.program_id(0) == 0)
    def _():
        wgb_ref[...] = wg_ref[...].astype(BF16)
        wpb_ref[...] = wp_ref[...].astype(BF16)

    h = h_ref[...]
    gate = _dot(h.astype(BF16), wgb_ref[...])
    proj = _dot(p_ref[...].astype(BF16), wpb_ref[...])
    o_ref[...] = h + (1.0 / (1.0 + jnp.exp(-gate))) * proj


def _ple(h, p, w_gate, w_proj):
    s, d = h.shape
    tm = PLE_TM
    est = 4 * tm * d * 4 + 2 * tm * PLE_DIM * 4 + (d + PLE_DIM) * d * (4 + 2) + 4 * tm * d * 4
    const = lambda i: (0, 0)
    return pl.pallas_call(
        _ple_kernel,
        grid=(s // tm,),
        in_specs=[
            pl.BlockSpec((tm, d), lambda i: (i, 0)),
            pl.BlockSpec((tm, PLE_DIM), lambda i: (i, 0)),
            pl.BlockSpec((d, d), const, pipeline_mode=pl.Buffered(1)),
            pl.BlockSpec((PLE_DIM, d), const, pipeline_mode=pl.Buffered(1)),
        ],
        out_specs=pl.BlockSpec((tm, d), lambda i: (i, 0)),
        out_shape=jax.ShapeDtypeStruct((s, d), F32),
        scratch_shapes=[pltpu.VMEM((d, d), BF16), pltpu.VMEM((PLE_DIM, d), BF16)],
        compiler_params=pltpu.CompilerParams(
            dimension_semantics=("arbitrary",),
            vmem_limit_bytes=_vmem_limit(est)),
        name="ple",
    )(h, p, w_gate, w_proj)


def _split_w_in(w_in):
    d = w_in.shape[0]
    b_kr = Q_LORA + KV_LORA
    b_qs = b_kr + QK_ROPE
    b_ks = b_qs + SWA_WIDTH
    b_vs = b_ks + SWA_KV_WIDTH
    wz = jnp.concatenate([w_in[:, :b_kr], w_in[:, b_ks:b_vs]], axis=1)
    pad = jnp.zeros((d, V7X_LANES - QK_ROPE), w_in.dtype)
    wt = jnp.concatenate([w_in[:, b_qs:b_ks], w_in[:, b_vs:], w_in[:, b_kr:b_qs], pad], axis=1).T
    return wz, wt


def _pack_w_uq_t(w_uq):
    r = w_uq.shape[0]
    w = w_uq.reshape(r, MLA_HEADS, QK_NOPE + QK_ROPE)
    w = jnp.pad(w, ((0, 0), (0, 0), (0, MLA_QK_PAD - QK_NOPE - QK_ROPE)))
    return w.reshape(r, MLA_HEADS * MLA_QK_PAD).T


def _rope_freq_col():
    freqs = ROPE_THETA ** (-jnp.arange(0, QK_ROPE, 2, dtype=F32) / QK_ROPE)
    return freqs.reshape(HALF_ROPE, 1)


def kernel(x, p, positions, rel_bias, ln1_g, ln1_b, ffn1_w1, ffn1_w3, ffn1_w2, w_in, q_norm_g, w_uq,
           kv_norm_g, w_ukv, swa_sinks, mla_out_g, swa_out_g, w_out, ln2_g, ln2_b, ffn2_w1, ffn2_w3,
           ffn2_w2, ln3_g, ln3_b, ple_w_gate, ple_w_proj):
    assert x.shape == (1, SEQ, D_MODEL) and DEPTH == 1
    row = lambda a: a.reshape(1, -1)
    h = x[0]
    pos_row = positions[0].astype(F32).reshape(1, SEQ)
    freq_col = _rope_freq_col()
    bucket_tbl = jnp.asarray(_swa_bucket_table())

    i = 0
    h = _ffn_ln(h, ffn1_w1[i], ffn1_w3[i], ffn1_w2[i], row(ln1_g[i]), row(ln1_b[i]))

    w_ukv_h = w_ukv[i].reshape(KV_LORA, MLA_HEADS, QK_NOPE + V_HEAD)
    wkn = w_ukv_h[:, :, :QK_NOPE].reshape(KV_LORA, MLA_WIDTH).astype(BF16)
    wv_t = w_ukv_h[:, :, QK_NOPE:].reshape(KV_LORA, MLA_WIDTH).T.astype(BF16)
    wz, wt = _split_w_in(w_in[i])
    qt, k, vt, qst, ks, vst = _in_proj(
        h, wz.astype(BF16), wt.astype(BF16), row(q_norm_g[i]), row(kv_norm_g[i]),
        _pack_w_uq_t(w_uq[i]).astype(BF16), wkn, wv_t, pos_row, freq_col)

    a_mla = _mla_flash(qt, k, vt)
    sink_rows = jnp.repeat(swa_sinks[i], BLOCK).reshape(SWA_KV_HEADS, 1, SWA_COLS)
    a_swa = _swa_attn(rel_bias, bucket_tbl, qst, ks, vst, sink_rows)

    h = _out_proj(a_mla, a_swa, h, row(mla_out_g[i]), row(swa_out_g[i]), w_out[i],
                  row(ln2_g[i]), row(ln2_b[i]))

    h = _ffn_ln(h, ffn2_w1[i], ffn2_w3[i], ffn2_w2[i], row(ln3_g[i]), row(ln3_b[i]))
    h = _ple(h, p[i, 0], ple_w_gate[i], ple_w_proj[i])
    return h[None]
```

```python
import math

import numpy as np
import jax
import jax.numpy as jnp
from jax import lax
from jax.experimental import pallas as pl
from jax.experimental.pallas import tpu as pltpu

F32 = jnp.float32
BF16 = jnp.bfloat16

D_MODEL = 2048
SEQ = 8192
DEPTH = 1
PLE_DIM = 256
MLA_HEADS = 8
Q_LORA = 512
KV_LORA = 512
QK_NOPE = 128
QK_ROPE = 64
V_HEAD = 128
ROPE_THETA = 10000.0
SWA_HEADS = 16
SWA_KV_HEADS = 2
SWA_GROUP = SWA_HEADS // SWA_KV_HEADS
SWA_HEAD_DIM = 64
WINDOW = 128
BLOCK = 128
REL_BUCKETS = 32
REL_MAX_DIST = 128
D_FF = 5632
ALPHA = (2.0 * DEPTH) ** 0.25
EPS = 1e-5
MLA_WIDTH = MLA_HEADS * V_HEAD
SWA_WIDTH = SWA_HEADS * SWA_HEAD_DIM
SWA_KV_WIDTH = SWA_KV_HEADS * SWA_HEAD_DIM

V7X_LANES = 128
V7X_VMEM_BYTES = 64 * 1024 * 1024
MIB = 1024 * 1024

V7X_BF16_SUBLANES = 16

MLA_QK_PAD = 2 * V7X_LANES
MLA_V_ROWS = V_HEAD + V7X_BF16_SUBLANES
NEG_BIG = float(np.finfo(np.float32).min)
LOG2E = math.log2(math.e)


def _vmem_limit(estimate_bytes):
    return int(min(estimate_bytes + 8 * MIB, V7X_VMEM_BYTES - 4 * MIB))


def _layer_norm(y, g, b):
    mu = jnp.mean(y, axis=-1, keepdims=True)
    yc = y - mu
    var = jnp.mean(yc * yc, axis=-1, keepdims=True)
    return yc * lax.rsqrt(var + EPS) * g + b


def _rms_norm(x, g):
    ms = jnp.mean(x * x, axis=-1, keepdims=True)
    return x * lax.rsqrt(ms + EPS) * g


def _dot(a, b):
    return jnp.dot(a, b, preferred_element_type=F32)


def _dot_nt(a, b):
    return lax.dot_general(a, b, (((1,), (1,)), ((), ())), preferred_element_type=F32)


FFN_TM = 2048
FFN_TF = 256
FFN_ROW_CHUNK = 512
FFN_LN_ROWS = 128


def _ffn_ln_kernel(x_hbm, w1_ref, w3_ref, w2_ref, g_ref, b_ref, o_hbm,
                   xs_ref, xb_ref, acc_ref, w1b_ref, w3b_ref, w2b_ref, x_sem, o_sem):
    i, f = pl.program_id(0), pl.program_id(1)
    n_tiles, n_f = pl.num_programs(0), pl.num_programs(1)
    tm = xs_ref.shape[0]
    n_out_chunks = tm // FFN_LN_ROWS

    def x_copy(tile):
        return pltpu.make_async_copy(x_hbm.at[pl.ds(tile * tm, tm), :], xs_ref, x_sem)

    def out_copy(tile, c):
        return pltpu.make_async_copy(
            acc_ref.at[pl.ds(c * FFN_LN_ROWS, FFN_LN_ROWS), :],
            o_hbm.at[pl.ds(tile * tm + c * FFN_LN_ROWS, FFN_LN_ROWS), :], o_sem.at[c])

    def chains(first, final):
        w1b_ref[...] = w1_ref[...].astype(BF16)
        w3b_ref[...] = w3_ref[...].astype(BF16)
        w2b_ref[...] = w2_ref[...].astype(BF16)
        for c in range(tm // FFN_ROW_CHUNK):
            rows = slice(c * FFN_ROW_CHUNK, (c + 1) * FFN_ROW_CHUNK)
            if first:
                x = xs_ref[rows, :]
                xb = x.astype(BF16)
                xb_ref[rows, :] = xb
                base = (2.0 * ALPHA) * x
            else:
                xb = xb_ref[rows, :]
                base = acc_ref[rows, :]
            gate = _dot(xb, w1b_ref[...])
            up = _dot(xb, w3b_ref[...])
            hidden = gate * (1.0 / (1.0 + jnp.exp(-gate))) * up
            acc_ref[rows, :] = base + _dot(hidden.astype(BF16), w2b_ref[...])
            if final:
                for k in range(c * FFN_ROW_CHUNK // FFN_LN_ROWS, (c + 1) * FFN_ROW_CHUNK // FFN_LN_ROWS):
                    sub = slice(k * FFN_LN_ROWS, (k + 1) * FFN_LN_ROWS)
                    acc_ref[sub, :] = _layer_norm(0.5 * acc_ref[sub, :], g_ref[...], b_ref[...])
                    out_copy(i, k).start()

    @pl.when(f == 0)
    def _():
        @pl.when(i == 0)
        def _():
            x_copy(0).start()

        x_copy(i).wait()

        @pl.when(i > 0)
        def _():
            for c in range(n_out_chunks):
                out_copy(i - 1, c).wait()

        chains(first=True, final=False)

        @pl.when(i + 1 < n_tiles)
        def _():
            x_copy(i + 1).start()

    @pl.when((f > 0) & (f < n_f - 1))
    def _():
        chains(first=False, final=False)

    @pl.when(f == n_f - 1)
    def _():
        chains(first=False, final=True)

        @pl.when(i == n_tiles - 1)
        def _():
            for c in range(n_out_chunks):
                out_copy(i, c).wait()


def _ffn_ln(x, w1, w3, w2, g, b):
    s, d = x.shape
    dff = w1.shape[1]
    tm, tf = FFN_TM, FFN_TF
    est = (tm * d * 4 + tm * d * 2 + tm * d * 4 + 2 * 3 * d * tf * 4 + 3 * d * tf * 2
           + 3 * FFN_ROW_CHUNK * tf * 4)
    return pl.pallas_call(
        _ffn_ln_kernel,
        grid=(s // tm, dff // tf),
        in_specs=[
            pl.BlockSpec(memory_space=pl.ANY),
            pl.BlockSpec((d, tf), lambda i, f: (0, f)),
            pl.BlockSpec((d, tf), lambda i, f: (0, f)),
            pl.BlockSpec((tf, d), lambda i, f: (f, 0)),
            pl.BlockSpec((1, d), lambda i, f: (0, 0)),
            pl.BlockSpec((1, d), lambda i, f: (0, 0)),
        ],
        out_specs=pl.BlockSpec(memory_space=pl.ANY),
        out_shape=jax.ShapeDtypeStruct((s, d), F32),
        scratch_shapes=[pltpu.VMEM((tm, d), F32), pltpu.VMEM((tm, d), BF16), pltpu.VMEM((tm, d), F32),
                        pltpu.VMEM((d, tf), BF16), pltpu.VMEM((d, tf), BF16), pltpu.VMEM((tf, d), BF16),
                        pltpu.SemaphoreType.DMA, pltpu.SemaphoreType.DMA((tm // FFN_LN_ROWS,))],
        compiler_params=pltpu.CompilerParams(
            dimension_semantics=("arbitrary", "arbitrary"),
            vmem_limit_bytes=_vmem_limit(est)),
        name="ffn_ln",
    )(x, w1, w3, w2, g, b)


INPROJ_TM = 512
Z_CQ = 0
Z_CKV = Z_CQ + Q_LORA
Z_KS = Z_CKV + KV_LORA
Z_WIDTH = Z_KS + SWA_KV_WIDTH
ZT_QS = 0
ZT_VS = ZT_QS + SWA_WIDTH
ZT_KR = ZT_VS + SWA_KV_WIDTH
ZT_ROWS = ZT_KR + V7X_LANES
HALF_ROPE = QK_ROPE // 2


def _rope_rows(x1, x2, cos, sin):
    return x1 * cos - x2 * sin, x1 * sin + x2 * cos


def _in_proj_kernel(h_ref, wz_ref, wt_ref, qg_ref, kvg_ref, wuqt_ref, wkn_ref, wvt_ref, pos_ref,
                    freq_ref, qt_ref, k_ref, vt_ref, qst_ref, ks_ref, vst_ref):
    tm = h_ref.shape[0]
    hb = h_ref[...].astype(BF16)
    z = _dot(hb, wz_ref[...])
    zt = _dot_nt(wt_ref[...], hb)

    ang = freq_ref[...] * pos_ref[...]
    cos = jnp.cos(ang)
    sin = jnp.sin(ang)

    cq = _rms_norm(z[:, Z_CQ:Z_CQ + Q_LORA], qg_ref[...]).astype(BF16)
    ckv = _rms_norm(z[:, Z_CKV:Z_CKV + KV_LORA], kvg_ref[...]).astype(BF16)
    qt = _dot_nt(wuqt_ref[...], cq)
    kn = _dot(ckv, wkn_ref[...])
    vt = _dot_nt(wvt_ref[...], ckv)

    kr1, kr2 = _rope_rows(zt[ZT_KR:ZT_KR + HALF_ROPE], zt[ZT_KR + HALF_ROPE:ZT_KR + QK_ROPE], cos, sin)
    k_rope = jnp.concatenate([kr1, kr2, zt[ZT_KR + QK_ROPE:ZT_KR + V7X_LANES]], axis=0).T.astype(BF16)

    for h in range(MLA_HEADS):
        base = h * MLA_QK_PAD
        r0 = base + QK_NOPE
        qt_ref[base:r0, :] = qt[base:r0].astype(BF16)
        q1, q2 = _rope_rows(qt[r0:r0 + HALF_ROPE], qt[r0 + HALF_ROPE:r0 + QK_ROPE], cos, sin)
        qt_ref[r0:r0 + HALF_ROPE, :] = q1.astype(BF16)
        qt_ref[r0 + HALF_ROPE:r0 + QK_ROPE, :] = q2.astype(BF16)
        qt_ref[r0 + QK_ROPE:base + MLA_QK_PAD, :] = qt[r0 + QK_ROPE:base + MLA_QK_PAD].astype(BF16)
        k_ref[h, :, 0:QK_NOPE] = kn[:, h * QK_NOPE:(h + 1) * QK_NOPE].astype(BF16)
        k_ref[h, :, QK_NOPE:MLA_QK_PAD] = k_rope
        vt_ref[h, 0, 0:V_HEAD, :] = vt[h * V_HEAD:(h + 1) * V_HEAD].astype(BF16)
        vt_ref[h, 0, V_HEAD:MLA_V_ROWS, :] = jnp.ones((MLA_V_ROWS - V_HEAD, tm), BF16)

    qst_ref[...] = zt[ZT_QS:ZT_QS + SWA_WIDTH].reshape(SWA_HEADS, SWA_HEAD_DIM, tm).astype(BF16)
    for kv in range(SWA_KV_HEADS):
        lo = Z_KS + kv * SWA_HEAD_DIM
        ks_ref[kv] = z[:, lo:lo + SWA_HEAD_DIM].astype(BF16)
        lo = ZT_VS + kv * SWA_HEAD_DIM
        for c in range(tm // BLOCK):
            vst_ref[kv, c] = zt[lo:lo + SWA_HEAD_DIM, c * BLOCK:(c + 1) * BLOCK].astype(BF16)


def _in_proj(h, wz, wt, q_g, kv_g, wuq_t, wkn, wv_t, pos_row, freq_col):
    s, d = h.shape
    tm = INPROJ_TM
    assert tm == MLA_TK
    est = (2 * tm * d * 4 + 2 * d * (Z_WIDTH + ZT_ROWS) * 2
           + 2 * Q_LORA * (MLA_HEADS * MLA_QK_PAD + 2 * MLA_WIDTH) * 2
           + 2 * tm * (2 * MLA_HEADS * MLA_QK_PAD + MLA_WIDTH + SWA_WIDTH + 4 * V7X_LANES) * 2
           + 4 * tm * (Z_WIDTH + ZT_ROWS + 2 * MLA_HEADS * MLA_QK_PAD) * 4)
    const = lambda i: (0, 0)
    return pl.pallas_call(
        _in_proj_kernel,
        grid=(s // tm,),
        in_specs=[
            pl.BlockSpec((tm, d), lambda i: (i, 0)),
            pl.BlockSpec((d, Z_WIDTH), const),
            pl.BlockSpec((ZT_ROWS, d), const),
            pl.BlockSpec((1, Q_LORA), const),
            pl.BlockSpec((1, KV_LORA), const),
            pl.BlockSpec((MLA_HEADS * MLA_QK_PAD, Q_LORA), const),
            pl.BlockSpec((KV_LORA, MLA_WIDTH), const),
            pl.BlockSpec((MLA_WIDTH, KV_LORA), const),
            pl.BlockSpec((1, tm), lambda i: (0, i)),
            pl.BlockSpec((HALF_ROPE, 1), const),
        ],
        out_specs=[
            pl.BlockSpec((MLA_HEADS * MLA_QK_PAD, tm), lambda i: (0, i)),
            pl.BlockSpec((MLA_HEADS, tm, MLA_QK_PAD), lambda i: (0, i, 0)),
            pl.BlockSpec((MLA_HEADS, 1, MLA_V_ROWS, tm), lambda i: (0, i, 0, 0)),
            pl.BlockSpec((SWA_HEADS, SWA_HEAD_DIM, tm), lambda i: (0, 0, i)),
            pl.BlockSpec((SWA_KV_HEADS, tm, SWA_HEAD_DIM), lambda i: (0, i, 0)),
            pl.BlockSpec((SWA_KV_HEADS, tm // BLOCK, SWA_HEAD_DIM, BLOCK), lambda i: (0, i, 0, 0)),
        ],
        out_shape=[
            jax.ShapeDtypeStruct((MLA_HEADS * MLA_QK_PAD, s), BF16),
            jax.ShapeDtypeStruct((MLA_HEADS, s, MLA_QK_PAD), BF16),
            jax.ShapeDtypeStruct((MLA_HEADS, s // tm, MLA_V_ROWS, tm), BF16),
            jax.ShapeDtypeStruct((SWA_HEADS, SWA_HEAD_DIM, s), BF16),
            jax.ShapeDtypeStruct((SWA_KV_HEADS, s, SWA_HEAD_DIM), BF16),
            jax.ShapeDtypeStruct((SWA_KV_HEADS, s // BLOCK, SWA_HEAD_DIM, BLOCK), BF16),
        ],
        compiler_params=pltpu.CompilerParams(
            dimension_semantics=("parallel",),
            vmem_limit_bytes=_vmem_limit(est)),
        name="in_proj",
    )(h, wz, wt, q_g, kv_g, wuq_t, wkn, wv_t, pos_row, freq_col)


MLA_TQ = 512
MLA_TK = 512
MLA_HB = 2
MLA_SCALE = (QK_NOPE + QK_ROPE) ** -0.5
MLA_SCALE_LOG2E = MLA_SCALE * LOG2E


def _mla_flash_kernel(qt_ref, k_ref, vt_ref, o_ref, s_ref, smax_ref, p_ref, m_ref, corr_ref, acc_ref):
    i = pl.program_id(1)
    tk = MLA_TK
    heads = range(MLA_HB)

    def qk(t):
        start = pl.multiple_of(t * tk, tk)
        for h in heads:
            cols = slice(h * MLA_QK_PAD, (h + 1) * MLA_QK_PAD)
            s = _dot(k_ref[h, pl.ds(start, tk), :], qt_ref[cols, :])
            s_ref[h] = s
            smax_ref[h] = jnp.max(s, axis=0, keepdims=True)

    def softmax(scores, masked):
        for h in heads:
            s, smax = scores[h]
            s = s * MLA_SCALE_LOG2E
            if masked:
                key = lax.broadcasted_iota(jnp.int32, s.shape, 0)
                qry = lax.broadcasted_iota(jnp.int32, s.shape, 1)
                s = jnp.where(key <= qry, s, NEG_BIG)
                tile_max = jnp.max(s, axis=0, keepdims=True)
            else:
                tile_max = smax * MLA_SCALE_LOG2E
            m_prev = m_ref[h]
            m_new = jnp.maximum(m_prev, tile_max)
            p_ref[h] = jnp.exp2(s - m_new).astype(BF16)
            corr_ref[h] = jnp.exp2(m_prev - m_new)
            m_ref[h] = m_new

    def pv(t):
        for h in heads:
            acc_ref[h] = corr_ref[h] * acc_ref[h] + _dot(vt_ref[h, t], p_ref[h])

    def load_scores():
        return [(s_ref[h], smax_ref[h]) for h in heads]

    def init():
        m_ref[...] = jnp.full_like(m_ref, NEG_BIG)
        acc_ref[...] = jnp.zeros_like(acc_ref)

    def finish():
        softmax(load_scores(), masked=True)
        pv(i)
        for h in heads:
            o_ref[:, h * V_HEAD:(h + 1) * V_HEAD] = (
                acc_ref[h, 0:V_HEAD, :] / acc_ref[h, V_HEAD:V_HEAD + 1, :]).T

    @pl.when(i == 0)
    def _():
        init()
        qk(0)
        finish()

    @pl.when(i >= 1)
    def _():
        init()
        qk(0)
        scores = load_scores()
        qk(1)
        softmax(scores, masked=False)

        def body(t, carry):
            pv(t)
            scores = load_scores()
            qk(t + 2)
            softmax(scores, masked=False)
            return carry

        lax.fori_loop(0, i - 1, body, 0)
        pv(i - 1)
        finish()


def _mla_flash(qt, k, vt):
    s = k.shape[1]
    tq, tk, hb = MLA_TQ, MLA_TK, MLA_HB
    assert tq == tk
    est = (2 * hb * tq * MLA_QK_PAD * 2 + 2 * hb * s * MLA_QK_PAD * 2 + 2 * hb * s * MLA_V_ROWS * 2
           + 2 * hb * tq * V_HEAD * 4 + hb * tk * tq * (4 + 2) + hb * tq * MLA_V_ROWS * 4
           + 4 * hb * tq * tk * 4)
    return pl.pallas_call(
        _mla_flash_kernel,
        grid=(MLA_HEADS // hb, s // tq),
        in_specs=[
            pl.BlockSpec((hb * MLA_QK_PAD, tq), lambda g, i: (g, i)),
            pl.BlockSpec((hb, s, MLA_QK_PAD), lambda g, i: (g, 0, 0)),
            pl.BlockSpec((hb, s // tk, MLA_V_ROWS, tk), lambda g, i: (g, 0, 0, 0)),
        ],
        out_specs=pl.BlockSpec((tq, hb * V_HEAD), lambda g, i: (i, g)),
        out_shape=jax.ShapeDtypeStruct((s, MLA_WIDTH), F32),
        scratch_shapes=[pltpu.VMEM((hb, tk, tq), F32), pltpu.VMEM((hb, 1, tq), F32),
                        pltpu.VMEM((hb, tk, tq), BF16),
                        pltpu.VMEM((hb, 1, tq), F32), pltpu.VMEM((hb, 1, tq), F32),
                        pltpu.VMEM((hb, MLA_V_ROWS, tq), F32)],
        compiler_params=pltpu.CompilerParams(
            dimension_semantics=("parallel", "arbitrary"),
            vmem_limit_bytes=_vmem_limit(est)),
        name="mla_flash",
    )(qt, k, vt)


SWA_SCALE = SWA_HEAD_DIM ** -0.5
SWA_COLS = SWA_GROUP * BLOCK
SWA_NB = 4


def _t5_bucket_np(dist):
    n = np.maximum(dist, 0)
    max_exact = REL_BUCKETS // 2
    large = max_exact + (np.log(np.maximum(n, 1).astype(np.float32) / max_exact)
                         / math.log(REL_MAX_DIST / max_exact)
                         * (REL_BUCKETS - max_exact)).astype(np.int32)
    large = np.minimum(large, REL_BUCKETS - 1)
    return np.where(n < max_exact, n, large).astype(np.int32)


def _swa_bucket_table():
    j = np.arange(BLOCK)[:, None]
    i = np.arange(BLOCK)[None, :]
    dist = np.where(j <= i, i - j, BLOCK + i - j)
    return _t5_bucket_np(dist)


def _swa_kernel(rb_ref, bkt_ref, qt_ref, qt_next_ref, k_ref, vt_ref, sink_ref, o_ref, bias_ref, s_ref):
    kv = pl.program_id(0)
    n = pl.program_id(1)

    @pl.when((kv == 0) & (n == 0))
    def _():
        bkt = bkt_ref[...]
        for h in range(SWA_HEADS):
            acc = jnp.zeros((BLOCK, BLOCK), F32)
            for b in range(REL_BUCKETS):
                acc = jnp.where(bkt == b, rb_ref[b, h], acc)
            g = h % SWA_GROUP
            bias_ref[h // SWA_GROUP, :, g * BLOCK:(g + 1) * BLOCK] = acc * LOG2E

    key = lax.broadcasted_iota(jnp.int32, (BLOCK, SWA_COLS), 0)
    qry = lax.broadcasted_iota(jnp.int32, (BLOCK, SWA_COLS), 1) & (BLOCK - 1)
    lower = key <= qry
    lower_bf = jnp.where(lower, 1.0, 0.0).astype(BF16)
    bias = bias_ref[kv]
    sink = sink_ref[...] * LOG2E

    def band_scores(q_ref, step):
        for b in range(SWA_NB):
            blk = step * SWA_NB + b
            qt = jnp.concatenate([q_ref[g, :, b * BLOCK:(b + 1) * BLOCK] for g in range(SWA_GROUP)],
                                 axis=1)
            cur = pl.multiple_of(blk * BLOCK, BLOCK)
            prev = pl.multiple_of(jnp.maximum(blk - 1, 0) * BLOCK, BLOCK)
            k_band = jnp.concatenate([k_ref[pl.ds(prev, BLOCK), :], k_ref[pl.ds(cur, BLOCK), :]], axis=0)
            s_ref[b] = _dot(k_band, qt)

    @pl.when(n == 0)
    def _():
        band_scores(qt_ref, 0)

    bands = [s_ref[b] for b in range(SWA_NB)]
    band_scores(qt_next_ref, jnp.minimum(n + 1, pl.num_programs(1) - 1))

    for b in range(SWA_NB):
        blk = n * SWA_NB + b
        prev_blk = jnp.maximum(blk - 1, 0)
        s_band = bands[b]
        s = jnp.where(lower, s_band[BLOCK:], s_band[:BLOCK]) * (SWA_SCALE * LOG2E) + bias
        if b == 0:
            s = jnp.where(lower | (blk > 0), s, NEG_BIG)

        m = jnp.maximum(jnp.max(s, axis=0, keepdims=True), sink)
        e = jnp.exp2(s - m)
        denom = jnp.sum(e, axis=0, keepdims=True) + jnp.exp2(sink - m)
        p = (e * (1.0 / denom)).astype(BF16)
        p_cur = p * lower_bf
        p_prev = p - p_cur
        ot = _dot(vt_ref[blk], p_cur) + _dot(vt_ref[prev_blk], p_prev)
        o_ref[b * BLOCK:(b + 1) * BLOCK, :] = jnp.concatenate(
            [ot[:, g * BLOCK:(g + 1) * BLOCK] for g in range(SWA_GROUP)], axis=0).T


def _swa_attn(rel_bias, bucket_tbl, qst, ks, vst, sink_rows):
    s = ks.shape[1]
    rows = SWA_NB * BLOCK
    n_steps = s // rows
    est = (4 * SWA_GROUP * SWA_HEAD_DIM * rows * 2 + 2 * s * V7X_LANES * 2 + 2 * s * SWA_HEAD_DIM * 2
           + 2 * rows * SWA_GROUP * SWA_HEAD_DIM * 4 + SWA_KV_HEADS * BLOCK * SWA_COLS * 4
           + SWA_NB * 2 * BLOCK * SWA_COLS * 4 + 10 * SWA_NB * BLOCK * SWA_COLS * 4)
    return pl.pallas_call(
        _swa_kernel,
        grid=(SWA_KV_HEADS, n_steps),
        in_specs=[
            pl.BlockSpec(memory_space=pltpu.SMEM),
            pl.BlockSpec((BLOCK, BLOCK), lambda kv, n: (0, 0)),
            pl.BlockSpec((SWA_GROUP, SWA_HEAD_DIM, rows), lambda kv, n: (kv, 0, n)),
            pl.BlockSpec((SWA_GROUP, SWA_HEAD_DIM, rows),
                         lambda kv, n: (kv, 0, jnp.minimum(n + 1, n_steps - 1))),
            pl.BlockSpec((None, s, SWA_HEAD_DIM), lambda kv, n: (kv, 0, 0)),
            pl.BlockSpec((None, s // BLOCK, SWA_HEAD_DIM, BLOCK), lambda kv, n: (kv, 0, 0, 0)),
            pl.BlockSpec((None, 1, SWA_COLS), lambda kv, n: (kv, 0, 0)),
        ],
        out_specs=pl.BlockSpec((rows, SWA_GROUP * SWA_HEAD_DIM), lambda kv, n: (n, kv)),
        out_shape=jax.ShapeDtypeStruct((s, SWA_WIDTH), F32),
        scratch_shapes=[pltpu.VMEM((SWA_KV_HEADS, BLOCK, SWA_COLS), F32),
                        pltpu.VMEM((SWA_NB, 2 * BLOCK, SWA_COLS), F32)],
        compiler_params=pltpu.CompilerParams(
            dimension_semantics=("arbitrary", "arbitrary"),
            vmem_limit_bytes=_vmem_limit(est)),
        name="swa_attn",
    )(rel_bias, bucket_tbl, qst, qst, ks, vst, sink_rows)


OUTPROJ_TM = 512
OUTPROJ_CHUNKS = 2


def _out_proj_kernel(am_ref, as_ref, h_ref, mg_ref, sg_ref, w_ref, g_ref, b_ref, o_ref, wb_ref):
    @pl.when(pl.program_id(0) == 0)
    def _():
        wb_ref[...] = w_ref[...].astype(BF16)

    chunk = o_ref.shape[0] // OUTPROJ_CHUNKS
    for c in range(OUTPROJ_CHUNKS):
        rows = slice(c * chunk, (c + 1) * chunk)
        nm = _rms_norm(am_ref[rows, :], mg_ref[...]).astype(BF16)
        ns = _rms_norm(as_ref[rows, :], sg_ref[...]).astype(BF16)
        mixed = _dot(nm, wb_ref[0:MLA_WIDTH, :]) + _dot(ns, wb_ref[MLA_WIDTH:MLA_WIDTH + SWA_WIDTH, :])
        o_ref[rows, :] = _layer_norm(ALPHA * h_ref[rows, :] + mixed, g_ref[...], b_ref[...])


def _out_proj(a_mla, a_swa, h, mla_g, swa_g, w_out, g, b):
    s, d = h.shape
    tm = OUTPROJ_TM
    est = (2 * tm * (MLA_WIDTH + SWA_WIDTH) * 4 + 4 * tm * d * 4 + (MLA_WIDTH + SWA_WIDTH) * d * (4 + 2)
           + 4 * tm * d * 4)
    const = lambda i: (0, 0)
    return pl.pallas_call(
        _out_proj_kernel,
        grid=(s // tm,),
        in_specs=[
            pl.BlockSpec((tm, MLA_WIDTH), lambda i: (i, 0)),
            pl.BlockSpec((tm, SWA_WIDTH), lambda i: (i, 0)),
            pl.BlockSpec((tm, d), lambda i: (i, 0)),
            pl.BlockSpec((1, MLA_WIDTH), const),
            pl.BlockSpec((1, SWA_WIDTH), const),
            pl.BlockSpec((MLA_WIDTH + SWA_WIDTH, d), const, pipeline_mode=pl.Buffered(1)),
            pl.BlockSpec((1, d), const),
            pl.BlockSpec((1, d), const),
        ],
        out_specs=pl.BlockSpec((tm, d), lambda i: (i, 0)),
        out_shape=jax.ShapeDtypeStruct((s, d), F32),
        scratch_shapes=[pltpu.VMEM((MLA_WIDTH + SWA_WIDTH, d), BF16)],
        compiler_params=pltpu.CompilerParams(
            dimension_semantics=("arbitrary",),
            vmem_limit_bytes=_vmem_limit(est)),
        name="out_proj",
    )(a_mla, a_swa, h, mla_g, swa_g, w_out, g, b)


PLE_TM = 512


def _ple_kernel(h_ref, p_ref, wg_ref, wp_ref, o_ref, wgb_ref, wpb_ref):
    @pl.when(pl.program_id(0) == 0)
    def _():
        wgb_ref[...] = wg_ref[...].astype(BF16)
        wpb_ref[...] = wp_ref[...].astype(BF16)

    h = h_ref[...]
    gate = _dot(h.astype(BF16), wgb_ref[...])
    proj = _dot(p_ref[...].astype(BF16), wpb_ref[...])
    o_ref[...] = h + (1.0 / (1.0 + jnp.exp(-gate))) * proj


def _ple(h, p, w_gate, w_proj):
    s, d = h.shape
    tm = PLE_TM
    est = 4 * tm * d * 4 + 2 * tm * PLE_DIM * 4 + (d + PLE_DIM) * d * (4 + 2) + 4 * tm * d * 4
    const = lambda i: (0, 0)
    return pl.pallas_call(
        _ple_kernel,
        grid=(s // tm,),
        in_specs=[
            pl.BlockSpec((tm, d), lambda i: (i, 0)),
            pl.BlockSpec((tm, PLE_DIM), lambda i: (i, 0)),
            pl.BlockSpec((d, d), const, pipeline_mode=pl.Buffered(1)),
            pl.BlockSpec((PLE_DIM, d), const, pipeline_mode=pl.Buffered(1)),
        ],
        out_specs=pl.BlockSpec((tm, d), lambda i: (i, 0)),
        out_shape=jax.ShapeDtypeStruct((s, d), F32),
        scratch_shapes=[pltpu.VMEM((d, d), BF16), pltpu.VMEM((PLE_DIM, d), BF16)],
        compiler_params=pltpu.CompilerParams(
            dimension_semantics=("arbitrary",),
            vmem_limit_bytes=_vmem_limit(est)),
        name="ple",
    )(h, p, w_gate, w_proj)


def _split_w_in(w_in):
    d = w_in.shape[0]
    b_kr = Q_LORA + KV_LORA
    b_qs = b_kr + QK_ROPE
    b_ks = b_qs + SWA_WIDTH
    b_vs = b_ks + SWA_KV_WIDTH
    wz = jnp.concatenate([w_in[:, :b_kr], w_in[:, b_ks:b_vs]], axis=1)
    pad = jnp.zeros((d, V7X_LANES - QK_ROPE), w_in.dtype)
    wt = jnp.concatenate([w_in[:, b_qs:b_ks], w_in[:, b_vs:], w_in[:, b_kr:b_qs], pad], axis=1).T
    return wz, wt


def _pack_w_uq_t(w_uq):
    r = w_uq.shape[0]
    w = w_uq.reshape(r, MLA_HEADS, QK_NOPE + QK_ROPE)
    w = jnp.pad(w, ((0, 0), (0, 0), (0, MLA_QK_PAD - QK_NOPE - QK_ROPE)))
    return w.reshape(r, MLA_HEADS * MLA_QK_PAD).T


def _rope_freq_col():
    freqs = ROPE_THETA ** (-jnp.arange(0, QK_ROPE, 2, dtype=F32) / QK_ROPE)
    return freqs.reshape(HALF_ROPE, 1)


def kernel(x, p, positions, rel_bias, ln1_g, ln1_b, ffn1_w1, ffn1_w3, ffn1_w2, w_in, q_norm_g, w_uq,
           kv_norm_g, w_ukv, swa_sinks, mla_out_g, swa_out_g, w_out, ln2_g, ln2_b, ffn2_w1, ffn2_w3,
           ffn2_w2, ln3_g, ln3_b, ple_w_gate, ple_w_proj):
    assert x.shape == (1, SEQ, D_MODEL) and DEPTH == 1
    row = lambda a: a.reshape(1, -1)
    h = x[0]
    pos_row = positions[0].astype(F32).reshape(1, SEQ)
    freq_col = _rope_freq_col()
    bucket_tbl = jnp.asarray(_swa_bucket_table())

    i = 0
    h = _ffn_ln(h, ffn1_w1[i], ffn1_w3[i], ffn1_w2[i], row(ln1_g[i]), row(ln1_b[i]))

    w_ukv_h = w_ukv[i].reshape(KV_LORA, MLA_HEADS, QK_NOPE + V_HEAD)
    wkn = w_ukv_h[:, :, :QK_NOPE].reshape(KV_LORA, MLA_WIDTH).astype(BF16)
    wv_t = w_ukv_h[:, :, QK_NOPE:].reshape(KV_LORA, MLA_WIDTH).T.astype(BF16)
    wz, wt = _split_w_in(w_in[i])
    qt, k, vt, qst, ks, vst = _in_proj(
        h, wz.astype(BF16), wt.astype(BF16), row(q_norm_g[i]), row(kv_norm_g[i]),
        _pack_w_uq_t(w_uq[i]).astype(BF16), wkn, wv_t, pos_row, freq_col)

    a_mla = _mla_flash(qt, k, vt)
    sink_rows = jnp.repeat(swa_sinks[i], BLOCK).reshape(SWA_KV_HEADS, 1, SWA_COLS)
    a_swa = _swa_attn(rel_bias, bucket_tbl, qst, ks, vst, sink_rows)

    h = _out_proj(a_mla, a_swa, h, row(mla_out_g[i]), row(swa_out_g[i]), w_out[i],
                  row(ln2_g[i]), row(ln2_b[i]))

    h = _ffn_ln(h, ffn2_w1[i], ffn2_w3[i], ffn2_w2[i], row(ln3_g[i]), row(ln3_b[i]))
    h = _ple(h, p[i, 0], ple_w_gate[i], ple_w_proj[i])
    return h[None]
```

```python
import math

import numpy as np
import jax
import jax.numpy as jnp
from jax import lax
from jax.experimental import pallas as pl
from jax.experimental.pallas import tpu as pltpu

F32 = jnp.float32
BF16 = jnp.bfloat16

D_MODEL = 2048
SEQ = 8192
DEPTH = 1
PLE_DIM = 256
MLA_HEADS = 8
Q_LORA = 512
KV_LORA = 512
QK_NOPE = 128
QK_ROPE = 64
V_HEAD = 128
ROPE_THETA = 10000.0
SWA_HEADS = 16
SWA_KV_HEADS = 2
SWA_GROUP = SWA_HEADS // SWA_KV_HEADS
SWA_HEAD_DIM = 64
WINDOW = 128
BLOCK = 128
REL_BUCKETS = 32
REL_MAX_DIST = 128
D_FF = 5632
ALPHA = (2.0 * DEPTH) ** 0.25
EPS = 1e-5
MLA_WIDTH = MLA_HEADS * V_HEAD
SWA_WIDTH = SWA_HEADS * SWA_HEAD_DIM
SWA_KV_WIDTH = SWA_KV_HEADS * SWA_HEAD_DIM

V7X_LANES = 128
V7X_VMEM_BYTES = 64 * 1024 * 1024
MIB = 1024 * 1024

V7X_BF16_SUBLANES = 16

MLA_QK_PAD = 2 * V7X_LANES
MLA_V_ROWS = V_HEAD + V7X_BF16_SUBLANES
NEG_BIG = float(np.finfo(np.float32).min)
LOG2E = math.log2(math.e)


def _vmem_limit(estimate_bytes):
    return int(min(estimate_bytes + 8 * MIB, V7X_VMEM_BYTES - 4 * MIB))


def _layer_norm(y, g, b):
    mu = jnp.mean(y, axis=-1, keepdims=True)
    yc = y - mu
    var = jnp.mean(yc * yc, axis=-1, keepdims=True)
    return yc * lax.rsqrt(var + EPS) * g + b


def _rms_norm(x, g):
    ms = jnp.mean(x * x, axis=-1, keepdims=True)
    return x * lax.rsqrt(ms + EPS) * g


def _dot(a, b):
    return jnp.dot(a, b, preferred_element_type=F32)


def _dot_nt(a, b):
    return lax.dot_general(a, b, (((1,), (1,)), ((), ())), preferred_element_type=F32)


FFN_TM = 2048
FFN_TF = 256
FFN_ROW_CHUNK = 512
FFN_LN_ROWS = 128


def _ffn_ln_kernel(x_hbm, w1_ref, w3_ref, w2_ref, g_ref, b_ref, o_hbm,
                   xs_ref, xb_ref, acc_ref, w1b_ref, w3b_ref, w2b_ref, x_sem, o_sem):
    i, f = pl.program_id(0), pl.program_id(1)
    n_tiles, n_f = pl.num_programs(0), pl.num_programs(1)
    tm = xs_ref.shape[0]
    n_out_chunks = tm // FFN_LN_ROWS

    def x_copy(tile):
        return pltpu.make_async_copy(x_hbm.at[pl.ds(tile * tm, tm), :], xs_ref, x_sem)

    def out_copy(tile, c):
        return pltpu.make_async_copy(
            acc_ref.at[pl.ds(c * FFN_LN_ROWS, FFN_LN_ROWS), :],
            o_hbm.at[pl.ds(tile * tm + c * FFN_LN_ROWS, FFN_LN_ROWS), :], o_sem.at[c])

    def chains(first, final):
        w1b_ref[...] = w1_ref[...].astype(BF16)
        w3b_ref[...] = w3_ref[...].astype(BF16)
        w2b_ref[...] = w2_ref[...].astype(BF16)
        for c in range(tm // FFN_ROW_CHUNK):
            rows = slice(c * FFN_ROW_CHUNK, (c + 1) * FFN_ROW_CHUNK)
            if first:
                x = xs_ref[rows, :]
                xb = x.astype(BF16)
                xb_ref[rows, :] = xb
                base = (2.0 * ALPHA) * x
            else:
                xb = xb_ref[rows, :]
                base = acc_ref[rows, :]
            gate = _dot(xb, w1b_ref[...])
            up = _dot(xb, w3b_ref[...])
            hidden = gate * (1.0 / (1.0 + jnp.exp(-gate))) * up
            acc_ref[rows, :] = base + _dot(hidden.astype(BF16), w2b_ref[...])
            if final:
                for k in range(c * FFN_ROW_CHUNK // FFN_LN_ROWS, (c + 1) * FFN_ROW_CHUNK // FFN_LN_ROWS):
                    sub = slice(k * FFN_LN_ROWS, (k + 1) * FFN_LN_ROWS)
                    acc_ref[sub, :] = _layer_norm(0.5 * acc_ref[sub, :], g_ref[...], b_ref[...])
                    out_copy(i, k).start()

    @pl.when(f == 0)
    def _():
        @pl.when(i == 0)
        def _():
            x_copy(0).start()

        x_copy(i).wait()

        @pl.when(i > 0)
        def _():
            for c in range(n_out_chunks):
                out_copy(i - 1, c).wait()

        chains(first=True, final=False)

        @pl.when(i + 1 < n_tiles)
        def _():
            x_copy(i + 1).start()

    @pl.when((f > 0) & (f < n_f - 1))
    def _():
        chains(first=False, final=False)

    @pl.when(f == n_f - 1)
    def _():
        chains(first=False, final=True)

        @pl.when(i == n_tiles - 1)
        def _():
            for c in range(n_out_chunks):
                out_copy(i, c).wait()


def _ffn_ln(x, w1, w3, w2, g, b):
    s, d = x.shape
    dff = w1.shape[1]
    tm, tf = FFN_TM, FFN_TF
    est = (tm * d * 4 + tm * d * 2 + tm * d * 4 + 2 * 3 * d * tf * 4 + 3 * d * tf * 2
           + 3 * FFN_ROW_CHUNK * tf * 4)
    return pl.pallas_call(
        _ffn_ln_kernel,
        grid=(s // tm, dff // tf),
        in_specs=[
            pl.BlockSpec(memory_space=pl.ANY),
            pl.BlockSpec((d, tf), lambda i, f: (0, f)),
            pl.BlockSpec((d, tf), lambda i, f: (0, f)),
            pl.BlockSpec((tf, d), lambda i, f: (f, 0)),
            pl.BlockSpec((1, d), lambda i, f: (0, 0)),
            pl.BlockSpec((1, d), lambda i, f: (0, 0)),
        ],
        out_specs=pl.BlockSpec(memory_space=pl.ANY),
        out_shape=jax.ShapeDtypeStruct((s, d), F32),
        scratch_shapes=[pltpu.VMEM((tm, d), F32), pltpu.VMEM((tm, d), BF16), pltpu.VMEM((tm, d), F32),
                        pltpu.VMEM((d, tf), BF16), pltpu.VMEM((d, tf), BF16), pltpu.VMEM((tf, d), BF16),
                        pltpu.SemaphoreType.DMA, pltpu.SemaphoreType.DMA((tm // FFN_LN_ROWS,))],
        compiler_params=pltpu.CompilerParams(
            dimension_semantics=("arbitrary", "arbitrary"),
            vmem_limit_bytes=_vmem_limit(est)),
        name="ffn_ln",
    )(x, w1, w3, w2, g, b)


INPROJ_TM = 512
Z_CQ = 0
Z_CKV = Z_CQ + Q_LORA
Z_KS = Z_CKV + KV_LORA
Z_WIDTH = Z_KS + SWA_KV_WIDTH
ZT_QS = 0
ZT_VS = ZT_QS + SWA_WIDTH
ZT_KR = ZT_VS + SWA_KV_WIDTH
ZT_ROWS = ZT_KR + V7X_LANES
HALF_ROPE = QK_ROPE // 2


def _rope_rows(x1, x2, cos, sin):
    return x1 * cos - x2 * sin, x1 * sin + x2 * cos


def _in_proj_kernel(h_ref, wz_ref, wt_ref, qg_ref, kvg_ref, wuqt_ref, wkn_ref, wvt_ref, pos_ref,
                    freq_ref, qt_ref, k_ref, vt_ref, qst_ref, ks_ref, vst_ref):
    tm = h_ref.shape[0]
    hb = h_ref[...].astype(BF16)
    z = _dot(hb, wz_ref[...])
    zt = _dot_nt(wt_ref[...], hb)

    ang = freq_ref[...] * pos_ref[...]
    cos = jnp.cos(ang)
    sin = jnp.sin(ang)

    cq = _rms_norm(z[:, Z_CQ:Z_CQ + Q_LORA], qg_ref[...]).astype(BF16)
    ckv = _rms_norm(z[:, Z_CKV:Z_CKV + KV_LORA], kvg_ref[...]).astype(BF16)
    qt = _dot_nt(wuqt_ref[...], cq)
    kn = _dot(ckv, wkn_ref[...])
    vt = _dot_nt(wvt_ref[...], ckv)

    kr1, kr2 = _rope_rows(zt[ZT_KR:ZT_KR + HALF_ROPE], zt[ZT_KR + HALF_ROPE:ZT_KR + QK_ROPE], cos, sin)
    k_rope = jnp.concatenate([kr1, kr2, zt[ZT_KR + QK_ROPE:ZT_KR + V7X_LANES]], axis=0).T.astype(BF16)

    for h in range(MLA_HEADS):
        base = h * MLA_QK_PAD
        r0 = base + QK_NOPE
        qt_ref[base:r0, :] = qt[base:r0].astype(BF16)
        q1, q2 = _rope_rows(qt[r0:r0 + HALF_ROPE], qt[r0 + HALF_ROPE:r0 + QK_ROPE], cos, sin)
        qt_ref[r0:r0 + HALF_ROPE, :] = q1.astype(BF16)
        qt_ref[r0 + HALF_ROPE:r0 + QK_ROPE, :] = q2.astype(BF16)
        qt_ref[r0 + QK_ROPE:base + MLA_QK_PAD, :] = qt[r0 + QK_ROPE:base + MLA_QK_PAD].astype(BF16)
        k_ref[h, :, 0:QK_NOPE] = kn[:, h * QK_NOPE:(h + 1) * QK_NOPE].astype(BF16)
        k_ref[h, :, QK_NOPE:MLA_QK_PAD] = k_rope
        vt_ref[h, 0, 0:V_HEAD, :] = vt[h * V_HEAD:(h + 1) * V_HEAD].astype(BF16)
        vt_ref[h, 0, V_HEAD:MLA_V_ROWS, :] = jnp.ones((MLA_V_ROWS - V_HEAD, tm), BF16)

    qst_ref[...] = zt[ZT_QS:ZT_QS + SWA_WIDTH].reshape(SWA_HEADS, SWA_HEAD_DIM, tm).astype(BF16)
    for kv in range(SWA_KV_HEADS):
        lo = Z_KS + kv * SWA_HEAD_DIM
        ks_ref[kv] = z[:, lo:lo + SWA_HEAD_DIM].astype(BF16)
        lo = ZT_VS + kv * SWA_HEAD_DIM
        for c in range(tm // BLOCK):
            vst_ref[kv, c] = zt[lo:lo + SWA_HEAD_DIM, c * BLOCK:(c + 1) * BLOCK].astype(BF16)


def _in_proj(h, wz, wt, q_g, kv_g, wuq_t, wkn, wv_t, pos_row, freq_col):
    s, d = h.shape
    tm = INPROJ_TM
    assert tm == MLA_TK
    est = (2 * tm * d * 4 + 2 * d * (Z_WIDTH + ZT_ROWS) * 2
           + 2 * Q_LORA * (MLA_HEADS * MLA_QK_PAD + 2 * MLA_WIDTH) * 2
           + 2 * tm * (2 * MLA_HEADS * MLA_QK_PAD + MLA_WIDTH + SWA_WIDTH + 4 * V7X_LANES) * 2
           + 4 * tm * (Z_WIDTH + ZT_ROWS + 2 * MLA_HEADS * MLA_QK_PAD) * 4)
    const = lambda i: (0, 0)
    return pl.pallas_call(
        _in_proj_kernel,
        grid=(s // tm,),
        in_specs=[
            pl.BlockSpec((tm, d), lambda i: (i, 0)),
            pl.BlockSpec((d, Z_WIDTH), const),
            pl.BlockSpec((ZT_ROWS, d), const),
            pl.BlockSpec((1, Q_LORA), const),
            pl.BlockSpec((1, KV_LORA), const),
            pl.BlockSpec((MLA_HEADS * MLA_QK_PAD, Q_LORA), const),
            pl.BlockSpec((KV_LORA, MLA_WIDTH), const),
            pl.BlockSpec((MLA_WIDTH, KV_LORA), const),
            pl.BlockSpec((1, tm), lambda i: (0, i)),
            pl.BlockSpec((HALF_ROPE, 1), const),
        ],
        out_specs=[
            pl.BlockSpec((MLA_HEADS * MLA_QK_PAD, tm), lambda i: (0, i)),
            pl.BlockSpec((MLA_HEADS, tm, MLA_QK_PAD), lambda i: (0, i, 0)),
            pl.BlockSpec((MLA_HEADS, 1, MLA_V_ROWS, tm), lambda i: (0, i, 0, 0)),
            pl.BlockSpec((SWA_HEADS, SWA_HEAD_DIM, tm), lambda i: (0, 0, i)),
            pl.BlockSpec((SWA_KV_HEADS, tm, SWA_HEAD_DIM), lambda i: (0, i, 0)),
            pl.BlockSpec((SWA_KV_HEADS, tm // BLOCK, SWA_HEAD_DIM, BLOCK), lambda i: (0, i, 0, 0)),
        ],
        out_shape=[
            jax.ShapeDtypeStruct((MLA_HEADS * MLA_QK_PAD, s), BF16),
            jax.ShapeDtypeStruct((MLA_HEADS, s, MLA_QK_PAD), BF16),
            jax.ShapeDtypeStruct((MLA_HEADS, s // tm, MLA_V_ROWS, tm), BF16),
            jax.ShapeDtypeStruct((SWA_HEADS, SWA_HEAD_DIM, s), BF16),
            jax.ShapeDtypeStruct((SWA_KV_HEADS, s, SWA_HEAD_DIM), BF16),
            jax.ShapeDtypeStruct((SWA_KV_HEADS, s // BLOCK, SWA_HEAD_DIM, BLOCK), BF16),
        ],
        compiler_params=pltpu.CompilerParams(
            dimension_semantics=("parallel",),
            vmem_limit_bytes=_vmem_limit(est)),
        name="in_proj",
    )(h, wz, wt, q_g, kv_g, wuq_t, wkn, wv_t, pos_row, freq_col)


MLA_TQ = 512
MLA_TK = 512
MLA_HB = 2
MLA_SCALE = (QK_NOPE + QK_ROPE) ** -0.5
MLA_SCALE_LOG2E = MLA_SCALE * LOG2E


def _mla_flash_kernel(qt_ref, k_ref, vt_ref, o_ref, s_ref, smax_ref, p_ref, m_ref, corr_ref, acc_ref):
    i = pl.program_id(1)
    tk = MLA_TK
    heads = range(MLA_HB)

    def qk(t):
        start = pl.multiple_of(t * tk, tk)
        for h in heads:
            cols = slice(h * MLA_QK_PAD, (h + 1) * MLA_QK_PAD)
            s = _dot(k_ref[h, pl.ds(start, tk), :], qt_ref[cols, :])
            s_ref[h] = s
            smax_ref[h] = jnp.max(s, axis=0, keepdims=True)

    def softmax(scores, masked):
        for h in heads:
            s, smax = scores[h]
            s = s * MLA_SCALE_LOG2E
            if masked:
                key = lax.broadcasted_iota(jnp.int32, s.shape, 0)
                qry = lax.broadcasted_iota(jnp.int32, s.shape, 1)
                s = jnp.where(key <= qry, s, NEG_BIG)
                tile_max = jnp.max(s, axis=0, keepdims=True)
            else:
                tile_max = smax * MLA_SCALE_LOG2E
            m_prev = m_ref[h]
            m_new = jnp.maximum(m_prev, tile_max)
            p_ref[h] = jnp.exp2(s - m_new).astype(BF16)
            corr_ref[h] = jnp.exp2(m_prev - m_new)
            m_ref[h] = m_new

    def pv(t):
        for h in heads:
            acc_ref[h] = corr_ref[h] * acc_ref[h] + _dot(vt_ref[h, t], p_ref[h])

    def load_scores():
        return [(s_ref[h], smax_ref[h]) for h in heads]

    def init():
        m_ref[...] = jnp.full_like(m_ref, NEG_BIG)
        acc_ref[...] = jnp.zeros_like(acc_ref)

    def finish():
        softmax(load_scores(), masked=True)
        pv(i)
        for h in heads:
            o_ref[:, h * V_HEAD:(h + 1) * V_HEAD] = (
                acc_ref[h, 0:V_HEAD, :] / acc_ref[h, V_HEAD:V_HEAD + 1, :]).T

    @pl.when(i == 0)
    def _():
        init()
        qk(0)
        finish()

    @pl.when(i >= 1)
    def _():
        init()
        qk(0)
        scores = load_scores()
        qk(1)
        softmax(scores, masked=False)

        def body(t, carry):
            pv(t)
            scores = load_scores()
            qk(t + 2)
            softmax(scores, masked=False)
            return carry

        lax.fori_loop(0, i - 1, body, 0)
        pv(i - 1)
        finish()


def _mla_flash(qt, k, vt):
    s = k.shape[1]
    tq, tk, hb = MLA_TQ, MLA_TK, MLA_HB
    assert tq == tk
    est = (2 * hb * tq * MLA_QK_PAD * 2 + 2 * hb * s * MLA_QK_PAD * 2 + 2 * hb * s * MLA_V_ROWS * 2
           + 2 * hb * tq * V_HEAD * 4 + hb * tk * tq * (4 + 2) + hb * tq * MLA_V_ROWS * 4
           + 4 * hb * tq * tk * 4)
    return pl.pallas_call(
        _mla_flash_kernel,
        grid=(MLA_HEADS // hb, s // tq),
        in_specs=[
            pl.BlockSpec((hb * MLA_QK_PAD, tq), lambda g, i: (g, i)),
            pl.BlockSpec((hb, s, MLA_QK_PAD), lambda g, i: (g, 0, 0)),
            pl.BlockSpec((hb, s // tk, MLA_V_ROWS, tk), lambda g, i: (g, 0, 0, 0)),
        ],
        out_specs=pl.BlockSpec((tq, hb * V_HEAD), lambda g, i: (i, g)),
        out_shape=jax.ShapeDtypeStruct((s, MLA_WIDTH), F32),
        scratch_shapes=[pltpu.VMEM((hb, tk, tq), F32), pltpu.VMEM((hb, 1, tq), F32),
                        pltpu.VMEM((hb, tk, tq), BF16),
                        pltpu.VMEM((hb, 1, tq), F32), pltpu.VMEM((hb, 1, tq), F32),
                        pltpu.VMEM((hb, MLA_V_ROWS, tq), F32)],
        compiler_params=pltpu.CompilerParams(
            dimension_semantics=("parallel", "arbitrary"),
            vmem_limit_bytes=_vmem_limit(est)),
        name="mla_flash",
    )(qt, k, vt)


SWA_SCALE = SWA_HEAD_DIM ** -0.5
SWA_COLS = SWA_GROUP * BLOCK


def _t5_bucket_np(dist):
    n = np.maximum(dist, 0)
    max_exact = REL_BUCKETS // 2
    large = max_exact + (np.log(np.maximum(n, 1).astype(np.float32) / max_exact)
                         / math.log(REL_MAX_DIST / max_exact)
                         * (REL_BUCKETS - max_exact)).astype(np.int32)
    large = np.minimum(large, REL_BUCKETS - 1)
    return np.where(n < max_exact, n, large).astype(np.int32)


def _swa_bucket_table():
    j = np.arange(BLOCK)[:, None]
    i = np.arange(BLOCK)[None, :]
    dist = np.where(j <= i, i - j, BLOCK + i - j)
    return _t5_bucket_np(dist)


def _swa_bias_init(rb_ref, bkt_ref, bias_ref):
    bkt = bkt_ref[...]
    for h in range(SWA_HEADS):
        acc = jnp.zeros((BLOCK, BLOCK), F32)
        for b in range(REL_BUCKETS):
            acc = jnp.where(bkt == b, rb_ref[b, h], acc)
        g = h % SWA_GROUP
        bias_ref[h // SWA_GROUP, :, g * BLOCK:(g + 1) * BLOCK] = acc * LOG2E


def _swa_block_fn(first_tile, q_ref, kc_ref, kp_ref, vc_ref, vp_ref, sink_ref, bias_ref, o_ref):
    key = lax.broadcasted_iota(jnp.int32, (BLOCK, SWA_COLS), 0)
    qry = lax.broadcasted_iota(jnp.int32, (BLOCK, SWA_COLS), 1) & (BLOCK - 1)
    lower = key <= qry
    lower_bf = jnp.where(lower, 1.0, 0.0).astype(BF16)

    def block(kv, b):
        bias = bias_ref[kv]
        sink = sink_ref[kv] * LOG2E
        blk = slice(b * BLOCK, (b + 1) * BLOCK)
        qt = jnp.concatenate([q_ref[kv * SWA_GROUP + g, :, blk] for g in range(SWA_GROUP)],
                             axis=1)
        if b == 0:
            k_prev, v_prev = kp_ref[kv], vp_ref[kv, 0]
        else:
            k_prev, v_prev = kc_ref[kv, (b - 1) * BLOCK:b * BLOCK, :], vc_ref[kv, b - 1]
        s_band = _dot(jnp.concatenate([k_prev, kc_ref[kv, blk, :]], axis=0), qt)
        s = jnp.where(lower, s_band[BLOCK:], s_band[:BLOCK]) * (SWA_SCALE * LOG2E) + bias
        if b == 0:
            s = jnp.where(lower | jnp.logical_not(first_tile), s, NEG_BIG)

        m = jnp.maximum(jnp.max(s, axis=0, keepdims=True), sink)
        e = jnp.exp2(s - m)
        denom = jnp.sum(e, axis=0, keepdims=True) + jnp.exp2(sink - m)
        p = (e * (1.0 / denom)).astype(BF16)
        p_cur = p * lower_bf
        p_prev = p - p_cur
        ot = _dot(jnp.concatenate([vc_ref[kv, b], v_prev], axis=1),
                  jnp.concatenate([p_cur, p_prev], axis=0))
        o_ref[blk, kv * SWA_GROUP * SWA_HEAD_DIM:(kv + 1) * SWA_GROUP * SWA_HEAD_DIM] = jnp.concatenate(
            [ot[:, g * BLOCK:(g + 1) * BLOCK] for g in range(SWA_GROUP)], axis=0).T

    return block


OUTPROJ_TM = 512
OUTPROJ_CHUNKS = 2


def _out_proj_kernel(rb_ref, bkt_ref, am_ref, q_ref, kc_ref, kp_ref, vc_ref, vp_ref, sink_ref, h_ref,
                     mg_ref, sg_ref, w_ref, g_ref, b_ref, o_ref, wb_ref, bias_ref, as_ref, nm_ref):
    step = pl.program_id(0)

    @pl.when(step == 0)
    def _():
        wb_ref[...] = w_ref[...].astype(BF16)
        _swa_bias_init(rb_ref, bkt_ref, bias_ref)

    tm, d = o_ref.shape
    nb = tm // BLOCK
    swa_block = _swa_block_fn(step == 0, q_ref, kc_ref, kp_ref, vc_ref, vp_ref, sink_ref, bias_ref, as_ref)
    nm_ref[...] = _rms_norm(am_ref[...], mg_ref[...]).astype(BF16)
    panel = d // (SWA_KV_HEADS * nb)
    for j in range(SWA_KV_HEADS * nb):
        swa_block(j // nb, j % nb)
        cols = slice(j * panel, (j + 1) * panel)
        o_ref[:, cols] = _dot(nm_ref[...], wb_ref[0:MLA_WIDTH, cols])

    chunk = tm // OUTPROJ_CHUNKS
    for c in range(OUTPROJ_CHUNKS):
        rows = slice(c * chunk, (c + 1) * chunk)
        ns = _rms_norm(as_ref[rows, :], sg_ref[...]).astype(BF16)
        mixed = o_ref[rows, :] + _dot(ns, wb_ref[MLA_WIDTH:MLA_WIDTH + SWA_WIDTH, :])
        o_ref[rows, :] = _layer_norm(ALPHA * h_ref[rows, :] + mixed, g_ref[...], b_ref[...])


def _out_proj(rel_bias, bucket_tbl, a_mla, qst, ks, vst, sink_rows, h, mla_g, swa_g, w_out, g, b):
    s, d = h.shape
    tm = OUTPROJ_TM
    nb = tm // BLOCK
    est = (2 * tm * MLA_WIDTH * 4 + 4 * tm * d * 4 + (MLA_WIDTH + SWA_WIDTH) * d * (4 + 2)
           + 2 * tm * (SWA_WIDTH + 2 * SWA_KV_HEADS * V7X_LANES) * 2 + tm * SWA_WIDTH * 4 + tm * MLA_WIDTH * 2
           + SWA_KV_HEADS * BLOCK * SWA_COLS * 4 + 4 * tm * d * 4)
    const = lambda i: (0, 0)
    before = lambda i: jnp.maximum(i * nb - 1, 0)
    return pl.pallas_call(
        _out_proj_kernel,
        grid=(s // tm,),
        in_specs=[
            pl.BlockSpec(memory_space=pltpu.SMEM),
            pl.BlockSpec((BLOCK, BLOCK), const),
            pl.BlockSpec((tm, MLA_WIDTH), lambda i: (i, 0)),
            pl.BlockSpec((SWA_HEADS, SWA_HEAD_DIM, tm), lambda i: (0, 0, i)),
            pl.BlockSpec((SWA_KV_HEADS, tm, SWA_HEAD_DIM), lambda i: (0, i, 0)),
            pl.BlockSpec((SWA_KV_HEADS, BLOCK, SWA_HEAD_DIM), lambda i: (0, before(i), 0)),
            pl.BlockSpec((SWA_KV_HEADS, nb, SWA_HEAD_DIM, BLOCK), lambda i: (0, i, 0, 0)),
            pl.BlockSpec((SWA_KV_HEADS, 1, SWA_HEAD_DIM, BLOCK), lambda i: (0, before(i), 0, 0)),
            pl.BlockSpec((SWA_KV_HEADS, 1, SWA_COLS), lambda i: (0, 0, 0)),
            pl.BlockSpec((tm, d), lambda i: (i, 0)),
            pl.BlockSpec((1, MLA_WIDTH), const),
            pl.BlockSpec((1, SWA_WIDTH), const),
            pl.BlockSpec((MLA_WIDTH + SWA_WIDTH, d), const, pipeline_mode=pl.Buffered(1)),
            pl.BlockSpec((1, d), const),
            pl.BlockSpec((1, d), const),
        ],
        out_specs=pl.BlockSpec((tm, d), lambda i: (i, 0)),
        out_shape=jax.ShapeDtypeStruct((s, d), F32),
        scratch_shapes=[pltpu.VMEM((MLA_WIDTH + SWA_WIDTH, d), BF16),
                        pltpu.VMEM((SWA_KV_HEADS, BLOCK, SWA_COLS), F32),
                        pltpu.VMEM((tm, SWA_WIDTH), F32),
                        pltpu.VMEM((tm, MLA_WIDTH), BF16)],
        compiler_params=pltpu.CompilerParams(
            dimension_semantics=("arbitrary",),
            vmem_limit_bytes=_vmem_limit(est)),
        name="out_proj",
    )(rel_bias, bucket_tbl, a_mla, qst, ks, ks, vst, vst, sink_rows, h, mla_g, swa_g, w_out, g, b)


PLE_TM = 512


def _ple_kernel(h_ref, p_ref, wg_ref, wp_ref, o_ref, wgb_ref, wpb_ref):
    @pl.when(pl.program_id(0) == 0)
    def _():
        wgb_ref[...] = wg_ref[...].astype(BF16)
        wpb_ref[...] = wp_ref[...].astype(BF16)

    h = h_ref[...]
    gate = _dot(h.astype(BF16), wgb_ref[...])
    proj = _dot(p_ref[...].astype(BF16), wpb_ref[...])
    o_ref[...] = h + (1.0 / (1.0 + jnp.exp(-gate))) * proj


def _ple(h, p, w_gate, w_proj):
    s, d = h.shape
    tm = PLE_TM
    est = 4 * tm * d * 4 + 2 * tm * PLE_DIM * 4 + (d + PLE_DIM) * d * (4 + 2) + 4 * tm * d * 4
    const = lambda i: (0, 0)
    return pl.pallas_call(
        _ple_kernel,
        grid=(s // tm,),
        in_specs=[
            pl.BlockSpec((tm, d), lambda i: (i, 0)),
            pl.BlockSpec((tm, PLE_DIM), lambda i: (i, 0)),
            pl.BlockSpec((d, d), const, pipeline_mode=pl.Buffered(1)),
            pl.BlockSpec((PLE_DIM, d), const, pipeline_mode=pl.Buffered(1)),
        ],
        out_specs=pl.BlockSpec((tm, d), lambda i: (i, 0)),
        out_shape=jax.ShapeDtypeStruct((s, d), F32),
        scratch_shapes=[pltpu.VMEM((d, d), BF16), pltpu.VMEM((PLE_DIM, d), BF16)],
        compiler_params=pltpu.CompilerParams(
            dimension_semantics=("arbitrary",),
            vmem_limit_bytes=_vmem_limit(est)),
        name="ple",
    )(h, p, w_gate, w_proj)


def _split_w_in(w_in):
    d = w_in.shape[0]
    b_kr = Q_LORA + KV_LORA
    b_qs = b_kr + QK_ROPE
    b_ks = b_qs + SWA_WIDTH
    b_vs = b_ks + SWA_KV_WIDTH
    wz = jnp.concatenate([w_in[:, :b_kr], w_in[:, b_ks:b_vs]], axis=1)
    pad = jnp.zeros((d, V7X_LANES - QK_ROPE), w_in.dtype)
    wt = jnp.concatenate([w_in[:, b_qs:b_ks], w_in[:, b_vs:], w_in[:, b_kr:b_qs], pad], axis=1).T
    return wz, wt


def _pack_w_uq_t(w_uq):
    r = w_uq.shape[0]
    w = w_uq.reshape(r, MLA_HEADS, QK_NOPE + QK_ROPE)
    w = jnp.pad(w, ((0, 0), (0, 0), (0, MLA_QK_PAD - QK_NOPE - QK_ROPE)))
    return w.reshape(r, MLA_HEADS * MLA_QK_PAD).T


def _rope_freq_col():
    freqs = ROPE_THETA ** (-jnp.arange(0, QK_ROPE, 2, dtype=F32) / QK_ROPE)
    return freqs.reshape(HALF_ROPE, 1)


def kernel(x, p, positions, rel_bias, ln1_g, ln1_b, ffn1_w1, ffn1_w3, ffn1_w2, w_in, q_norm_g, w_uq,
           kv_norm_g, w_ukv, swa_sinks, mla_out_g, swa_out_g, w_out, ln2_g, ln2_b, ffn2_w1, ffn2_w3,
           ffn2_w2, ln3_g, ln3_b, ple_w_gate, ple_w_proj):
    assert x.shape == (1, SEQ, D_MODEL) and DEPTH == 1
    row = lambda a: a.reshape(1, -1)
    h = x[0]
    pos_row = positions[0].astype(F32).reshape(1, SEQ)
    freq_col = _rope_freq_col()
    bucket_tbl = jnp.asarray(_swa_bucket_table())

    i = 0
    h = _ffn_ln(h, ffn1_w1[i], ffn1_w3[i], ffn1_w2[i], row(ln1_g[i]), row(ln1_b[i]))

    w_ukv_h = w_ukv[i].reshape(KV_LORA, MLA_HEADS, QK_NOPE + V_HEAD)
    wkn = w_ukv_h[:, :, :QK_NOPE].reshape(KV_LORA, MLA_WIDTH).astype(BF16)
    wv_t = w_ukv_h[:, :, QK_NOPE:].reshape(KV_LORA, MLA_WIDTH).T.astype(BF16)
    wz, wt = _split_w_in(w_in[i])
    qt, k, vt, qst, ks, vst = _in_proj(
        h, wz.astype(BF16), wt.astype(BF16), row(q_norm_g[i]), row(kv_norm_g[i]),
        _pack_w_uq_t(w_uq[i]).astype(BF16), wkn, wv_t, pos_row, freq_col)

    a_mla = _mla_flash(qt, k, vt)
    sink_rows = jnp.repeat(swa_sinks[i], BLOCK).reshape(SWA_KV_HEADS, 1, SWA_COLS)

    h = _out_proj(rel_bias, bucket_tbl, a_mla, qst, ks, vst, sink_rows, h,
                  row(mla_out_g[i]), row(swa_out_g[i]), w_out[i],
                  row(ln2_g[i]), row(ln2_b[i]))

    h = _ffn_ln(h, ffn2_w1[i], ffn2_w3[i], ffn2_w2[i], row(ln3_g[i]), row(ln3_b[i]))
    h = _ple(h, p[i, 0], ple_w_gate[i], ple_w_proj[i])
    return h[None]
```

```python
import math

import numpy as np
import jax
import jax.numpy as jnp
from jax import lax
from jax.experimental import pallas as pl
from jax.experimental.pallas import tpu as pltpu

F32 = jnp.float32
BF16 = jnp.bfloat16

D_MODEL = 2048
SEQ = 8192
DEPTH = 1
PLE_DIM = 256
MLA_HEADS = 8
Q_LORA = 512
KV_LORA = 512
QK_NOPE = 128
QK_ROPE = 64
V_HEAD = 128
ROPE_THETA = 10000.0
SWA_HEADS = 16
SWA_KV_HEADS = 2
SWA_GROUP = SWA_HEADS // SWA_KV_HEADS
SWA_HEAD_DIM = 64
WINDOW = 128
BLOCK = 128
REL_BUCKETS = 32
REL_MAX_DIST = 128
D_FF = 5632
ALPHA = (2.0 * DEPTH) ** 0.25
EPS = 1e-5
MLA_WIDTH = MLA_HEADS * V_HEAD
SWA_WIDTH = SWA_HEADS * SWA_HEAD_DIM
SWA_KV_WIDTH = SWA_KV_HEADS * SWA_HEAD_DIM

V7X_LANES = 128
V7X_VMEM_BYTES = 64 * 1024 * 1024
MIB = 1024 * 1024

V7X_BF16_SUBLANES = 16

MLA_QK_PAD = 2 * V7X_LANES
MLA_V_ROWS = V_HEAD + V7X_BF16_SUBLANES
NEG_BIG = float(np.finfo(np.float32).min)
LOG2E = math.log2(math.e)


def _vmem_limit(estimate_bytes):
    return int(min(estimate_bytes + 8 * MIB, V7X_VMEM_BYTES - 4 * MIB))


def _layer_norm(y, g, b):
    mu = jnp.mean(y, axis=-1, keepdims=True)
    yc = y - mu
    var = jnp.mean(yc * yc, axis=-1, keepdims=True)
    return yc * lax.rsqrt(var + EPS) * g + b


def _rms_norm(x, g):
    ms = jnp.mean(x * x, axis=-1, keepdims=True)
    return x * lax.rsqrt(ms + EPS) * g


def _dot(a, b):
    return jnp.dot(a, b, preferred_element_type=F32)


def _dot_nt(a, b):
    return lax.dot_general(a, b, (((1,), (1,)), ((), ())), preferred_element_type=F32)


FFN_TM = 2048
FFN_TF = 256
FFN_ROW_CHUNK = 512
FFN_LN_ROWS = 128


def _ffn_ln_kernel(x_hbm, w1_ref, w3_ref, w2_ref, g_ref, b_ref, o_hbm,
                   xs_ref, xb_ref, acc_ref, w1b_ref, w3b_ref, w2b_ref, x_sem, o_sem):
    i, f = pl.program_id(0), pl.program_id(1)
    n_tiles, n_f = pl.num_programs(0), pl.num_programs(1)
    tm = xs_ref.shape[0]
    n_out_chunks = tm // FFN_LN_ROWS

    def x_copy(tile):
        return pltpu.make_async_copy(x_hbm.at[pl.ds(tile * tm, tm), :], xs_ref, x_sem)

    def out_copy(tile, c):
        return pltpu.make_async_copy(
            acc_ref.at[pl.ds(c * FFN_LN_ROWS, FFN_LN_ROWS), :],
            o_hbm.at[pl.ds(tile * tm + c * FFN_LN_ROWS, FFN_LN_ROWS), :], o_sem.at[c])

    def chains(first, final):
        w1b_ref[...] = w1_ref[...].astype(BF16)
        w3b_ref[...] = w3_ref[...].astype(BF16)
        w2b_ref[...] = w2_ref[...].astype(BF16)
        for c in range(tm // FFN_ROW_CHUNK):
            rows = slice(c * FFN_ROW_CHUNK, (c + 1) * FFN_ROW_CHUNK)
            if first:
                x = xs_ref[rows, :]
                xb = x.astype(BF16)
                xb_ref[rows, :] = xb
                base = (2.0 * ALPHA) * x
            else:
                xb = xb_ref[rows, :]
                base = acc_ref[rows, :]
            gate = _dot(xb, w1b_ref[...])
            up = _dot(xb, w3b_ref[...])
            hidden = gate * (1.0 / (1.0 + jnp.exp(-gate))) * up
            acc_ref[rows, :] = base + _dot(hidden.astype(BF16), w2b_ref[...])
            if final:
                for k in range(c * FFN_ROW_CHUNK // FFN_LN_ROWS, (c + 1) * FFN_ROW_CHUNK // FFN_LN_ROWS):
                    sub = slice(k * FFN_LN_ROWS, (k + 1) * FFN_LN_ROWS)
                    acc_ref[sub, :] = _layer_norm(0.5 * acc_ref[sub, :], g_ref[...], b_ref[...])
                    out_copy(i, k).start()

    @pl.when(f == 0)
    def _():
        @pl.when(i == 0)
        def _():
            x_copy(0).start()

        x_copy(i).wait()

        @pl.when(i > 0)
        def _():
            for c in range(n_out_chunks):
                out_copy(i - 1, c).wait()

        chains(first=True, final=False)

        @pl.when(i + 1 < n_tiles)
        def _():
            x_copy(i + 1).start()

    @pl.when((f > 0) & (f < n_f - 1))
    def _():
        chains(first=False, final=False)

    @pl.when(f == n_f - 1)
    def _():
        chains(first=False, final=True)

        @pl.when(i == n_tiles - 1)
        def _():
            for c in range(n_out_chunks):
                out_copy(i, c).wait()


def _ffn_ln(x, w1, w3, w2, g, b):
    s, d = x.shape
    dff = w1.shape[1]
    tm, tf = FFN_TM, FFN_TF
    est = (tm * d * 4 + tm * d * 2 + tm * d * 4 + 2 * 3 * d * tf * 4 + 3 * d * tf * 2
           + 3 * FFN_ROW_CHUNK * tf * 4)
    return pl.pallas_call(
        _ffn_ln_kernel,
        grid=(s // tm, dff // tf),
        in_specs=[
            pl.BlockSpec(memory_space=pl.ANY),
            pl.BlockSpec((d, tf), lambda i, f: (0, f)),
            pl.BlockSpec((d, tf), lambda i, f: (0, f)),
            pl.BlockSpec((tf, d), lambda i, f: (f, 0)),
            pl.BlockSpec((1, d), lambda i, f: (0, 0)),
            pl.BlockSpec((1, d), lambda i, f: (0, 0)),
        ],
        out_specs=pl.BlockSpec(memory_space=pl.ANY),
        out_shape=jax.ShapeDtypeStruct((s, d), F32),
        scratch_shapes=[pltpu.VMEM((tm, d), F32), pltpu.VMEM((tm, d), BF16), pltpu.VMEM((tm, d), F32),
                        pltpu.VMEM((d, tf), BF16), pltpu.VMEM((d, tf), BF16), pltpu.VMEM((tf, d), BF16),
                        pltpu.SemaphoreType.DMA, pltpu.SemaphoreType.DMA((tm // FFN_LN_ROWS,))],
        compiler_params=pltpu.CompilerParams(
            dimension_semantics=("arbitrary", "arbitrary"),
            vmem_limit_bytes=_vmem_limit(est)),
        name="ffn_ln",
    )(x, w1, w3, w2, g, b)


INPROJ_TM = 512
Z_CQ = 0
Z_CKV = Z_CQ + Q_LORA
Z_KS = Z_CKV + KV_LORA
Z_WIDTH = Z_KS + SWA_KV_WIDTH
ZT_QS = 0
ZT_VS = ZT_QS + SWA_WIDTH
ZT_KR = ZT_VS + SWA_KV_WIDTH
ZT_ROWS = ZT_KR + V7X_LANES
HALF_ROPE = QK_ROPE // 2


def _rope_rows(x1, x2, cos, sin):
    return x1 * cos - x2 * sin, x1 * sin + x2 * cos


def _in_proj_kernel(h_ref, wz_ref, wt_ref, qg_ref, kvg_ref, wuqt_ref, wkn_ref, wvt_ref, pos_ref,
                    freq_ref, qt_ref, k_ref, vt_ref, qst_ref, ks_ref, vst_ref):
    tm = h_ref.shape[0]
    hb = h_ref[...].astype(BF16)
    z = _dot(hb, wz_ref[...])
    zt = _dot_nt(wt_ref[...], hb)

    ang = freq_ref[...] * pos_ref[...]
    cos = jnp.cos(ang)
    sin = jnp.sin(ang)

    cq = _rms_norm(z[:, Z_CQ:Z_CQ + Q_LORA], qg_ref[...]).astype(BF16)
    ckv = _rms_norm(z[:, Z_CKV:Z_CKV + KV_LORA], kvg_ref[...]).astype(BF16)
    qt = _dot_nt(wuqt_ref[...], cq)
    kn = _dot(ckv, wkn_ref[...])
    vt = _dot_nt(wvt_ref[...], ckv)

    kr1, kr2 = _rope_rows(zt[ZT_KR:ZT_KR + HALF_ROPE], zt[ZT_KR + HALF_ROPE:ZT_KR + QK_ROPE], cos, sin)
    k_rope = jnp.concatenate([kr1, kr2, zt[ZT_KR + QK_ROPE:ZT_KR + V7X_LANES]], axis=0).T.astype(BF16)

    for h in range(MLA_HEADS):
        base = h * MLA_QK_PAD
        r0 = base + QK_NOPE
        qt_ref[base:r0, :] = qt[base:r0].astype(BF16)
        q1, q2 = _rope_rows(qt[r0:r0 + HALF_ROPE], qt[r0 + HALF_ROPE:r0 + QK_ROPE], cos, sin)
        qt_ref[r0:r0 + HALF_ROPE, :] = q1.astype(BF16)
        qt_ref[r0 + HALF_ROPE:r0 + QK_ROPE, :] = q2.astype(BF16)
        qt_ref[r0 + QK_ROPE:base + MLA_QK_PAD, :] = qt[r0 + QK_ROPE:base + MLA_QK_PAD].astype(BF16)
        k_ref[h, :, 0:QK_NOPE] = kn[:, h * QK_NOPE:(h + 1) * QK_NOPE].astype(BF16)
        k_ref[h, :, QK_NOPE:MLA_QK_PAD] = k_rope
        vt_ref[h, 0, 0:V_HEAD, :] = vt[h * V_HEAD:(h + 1) * V_HEAD].astype(BF16)
        vt_ref[h, 0, V_HEAD:MLA_V_ROWS, :] = jnp.ones((MLA_V_ROWS - V_HEAD, tm), BF16)

    qst_ref[...] = zt[ZT_QS:ZT_QS + SWA_WIDTH].reshape(SWA_HEADS, SWA_HEAD_DIM, tm).astype(BF16)
    for kv in range(SWA_KV_HEADS):
        lo = Z_KS + kv * SWA_HEAD_DIM
        ks_ref[kv] = z[:, lo:lo + SWA_HEAD_DIM].astype(BF16)
        lo = ZT_VS + kv * SWA_HEAD_DIM
        for c in range(tm // BLOCK):
            vst_ref[kv, c] = zt[lo:lo + SWA_HEAD_DIM, c * BLOCK:(c + 1) * BLOCK].astype(BF16)


def _in_proj(h, wz, wt, q_g, kv_g, wuq_t, wkn, wv_t, pos_row, freq_col):
    s, d = h.shape
    tm = INPROJ_TM
    assert tm == MLA_TK
    est = (2 * tm * d * 4 + 2 * d * (Z_WIDTH + ZT_ROWS) * 2
           + 2 * Q_LORA * (MLA_HEADS * MLA_QK_PAD + 2 * MLA_WIDTH) * 2
           + 2 * tm * (2 * MLA_HEADS * MLA_QK_PAD + MLA_WIDTH + SWA_WIDTH + 4 * V7X_LANES) * 2
           + 4 * tm * (Z_WIDTH + ZT_ROWS + 2 * MLA_HEADS * MLA_QK_PAD) * 4)
    const = lambda i: (0, 0)
    return pl.pallas_call(
        _in_proj_kernel,
        grid=(s // tm,),
        in_specs=[
            pl.BlockSpec((tm, d), lambda i: (i, 0)),
            pl.BlockSpec((d, Z_WIDTH), const),
            pl.BlockSpec((ZT_ROWS, d), const),
            pl.BlockSpec((1, Q_LORA), const),
            pl.BlockSpec((1, KV_LORA), const),
            pl.BlockSpec((MLA_HEADS * MLA_QK_PAD, Q_LORA), const),
            pl.BlockSpec((KV_LORA, MLA_WIDTH), const),
            pl.BlockSpec((MLA_WIDTH, KV_LORA), const),
            pl.BlockSpec((1, tm), lambda i: (0, i)),
            pl.BlockSpec((HALF_ROPE, 1), const),
        ],
        out_specs=[
            pl.BlockSpec((MLA_HEADS * MLA_QK_PAD, tm), lambda i: (0, i)),
            pl.BlockSpec((MLA_HEADS, tm, MLA_QK_PAD), lambda i: (0, i, 0)),
            pl.BlockSpec((MLA_HEADS, 1, MLA_V_ROWS, tm), lambda i: (0, i, 0, 0)),
            pl.BlockSpec((SWA_HEADS, SWA_HEAD_DIM, tm), lambda i: (0, 0, i)),
            pl.BlockSpec((SWA_KV_HEADS, tm, SWA_HEAD_DIM), lambda i: (0, i, 0)),
            pl.BlockSpec((SWA_KV_HEADS, tm // BLOCK, SWA_HEAD_DIM, BLOCK), lambda i: (0, i, 0, 0)),
        ],
        out_shape=[
            jax.ShapeDtypeStruct((MLA_HEADS * MLA_QK_PAD, s), BF16),
            jax.ShapeDtypeStruct((MLA_HEADS, s, MLA_QK_PAD), BF16),
            jax.ShapeDtypeStruct((MLA_HEADS, s // tm, MLA_V_ROWS, tm), BF16),
            jax.ShapeDtypeStruct((SWA_HEADS, SWA_HEAD_DIM, s), BF16),
            jax.ShapeDtypeStruct((SWA_KV_HEADS, s, SWA_HEAD_DIM), BF16),
            jax.ShapeDtypeStruct((SWA_KV_HEADS, s // BLOCK, SWA_HEAD_DIM, BLOCK), BF16),
        ],
        compiler_params=pltpu.CompilerParams(
            dimension_semantics=("parallel",),
            vmem_limit_bytes=_vmem_limit(est)),
        name="in_proj",
    )(h, wz, wt, q_g, kv_g, wuq_t, wkn, wv_t, pos_row, freq_col)


MLA_TQ = 512
MLA_TK = 512
MLA_HB = 2
MLA_SCALE = (QK_NOPE + QK_ROPE) ** -0.5
MLA_SCALE_LOG2E = MLA_SCALE * LOG2E


def _mla_flash_kernel(qt_ref, qn_ref, k_ref, vt_ref, o_ref,
                      s_ref, smax_ref, sn_ref, snmax_ref, p_ref, m_ref, corr_ref, acc_ref):
    i = pl.program_id(1)
    tk = MLA_TK
    heads = range(MLA_HB)

    def qk(q_ref, t, s_out, smax_out):
        start = pl.multiple_of(t * tk, tk)
        for h in heads:
            cols = slice(h * MLA_QK_PAD, (h + 1) * MLA_QK_PAD)
            s = _dot(k_ref[h, pl.ds(start, tk), :], q_ref[cols, :])
            s_out[h] = s
            smax_out[h] = jnp.max(s, axis=0, keepdims=True)

    def softmax(scores, masked):
        for h in heads:
            s, smax = scores[h]
            s = s * MLA_SCALE_LOG2E
            if masked:
                key = lax.broadcasted_iota(jnp.int32, s.shape, 0)
                qry = lax.broadcasted_iota(jnp.int32, s.shape, 1)
                s = jnp.where(key <= qry, s, NEG_BIG)
                tile_max = jnp.max(s, axis=0, keepdims=True)
            else:
                tile_max = smax * MLA_SCALE_LOG2E
            m_prev = m_ref[h]
            m_new = jnp.maximum(m_prev, tile_max)
            p_ref[h] = jnp.exp2(s - m_new).astype(BF16)
            corr_ref[h] = jnp.exp2(m_prev - m_new)
            m_ref[h] = m_new

    def pv(t):
        for h in heads:
            acc_ref[h] = corr_ref[h] * acc_ref[h] + _dot(vt_ref[h, t], p_ref[h])

    def load_scores(s_in, smax_in):
        return [(s_in[h], smax_in[h]) for h in heads]

    def drain_and_prime():
        diag = load_scores(s_ref, smax_ref)
        qk(qn_ref, 0, sn_ref, snmax_ref)
        softmax(diag, masked=True)
        pv(i)
        qk(qn_ref, 1, s_ref, smax_ref)
        for h in heads:
            o_ref[:, h * V_HEAD:(h + 1) * V_HEAD] = (
                acc_ref[h, 0:V_HEAD, :] / acc_ref[h, V_HEAD:V_HEAD + 1, :]).T
        m_ref[...] = jnp.full_like(m_ref, NEG_BIG)
        acc_ref[...] = jnp.zeros_like(acc_ref)
        softmax(load_scores(sn_ref, snmax_ref), masked=False)

    @pl.when(i == 0)
    def _():
        m_ref[...] = jnp.full_like(m_ref, NEG_BIG)
        acc_ref[...] = jnp.zeros_like(acc_ref)
        qk(qt_ref, 0, s_ref, smax_ref)
        drain_and_prime()

    @pl.when(i >= 1)
    def _():
        def body(t, carry):
            pv(t)
            scores = load_scores(s_ref, smax_ref)
            qk(qt_ref, t + 2, s_ref, smax_ref)
            softmax(scores, masked=False)
            return carry

        lax.fori_loop(0, i - 1, body, 0)
        pv(i - 1)
        drain_and_prime()


def _mla_flash(qt, k, vt):
    s = k.shape[1]
    tq, tk, hb = MLA_TQ, MLA_TK, MLA_HB
    assert tq == tk and s // tq >= 2
    n_q = s // tq
    est = (4 * hb * tq * MLA_QK_PAD * 2 + 2 * hb * s * MLA_QK_PAD * 2 + 2 * hb * s * MLA_V_ROWS * 2
           + 2 * hb * tq * V_HEAD * 4 + hb * tk * tq * (4 + 4 + 2) + hb * tq * MLA_V_ROWS * 4
           + 4 * hb * tq * tk * 4)
    return pl.pallas_call(
        _mla_flash_kernel,
        grid=(MLA_HEADS // hb, s // tq),
        in_specs=[
            pl.BlockSpec((hb * MLA_QK_PAD, tq), lambda g, i: (g, i)),
            pl.BlockSpec((hb * MLA_QK_PAD, tq), lambda g, i: (g, jnp.minimum(i + 1, n_q - 1))),
            pl.BlockSpec((hb, s, MLA_QK_PAD), lambda g, i: (g, 0, 0)),
            pl.BlockSpec((hb, s // tk, MLA_V_ROWS, tk), lambda g, i: (g, 0, 0, 0)),
        ],
        out_specs=pl.BlockSpec((tq, hb * V_HEAD), lambda g, i: (i, g)),
        out_shape=jax.ShapeDtypeStruct((s, MLA_WIDTH), F32),
        scratch_shapes=[pltpu.VMEM((hb, tk, tq), F32), pltpu.VMEM((hb, 1, tq), F32),
                        pltpu.VMEM((hb, tk, tq), F32), pltpu.VMEM((hb, 1, tq), F32),
                        pltpu.VMEM((hb, tk, tq), BF16),
                        pltpu.VMEM((hb, 1, tq), F32), pltpu.VMEM((hb, 1, tq), F32),
                        pltpu.VMEM((hb, MLA_V_ROWS, tq), F32)],
        compiler_params=pltpu.CompilerParams(
            dimension_semantics=("parallel", "arbitrary"),
            vmem_limit_bytes=_vmem_limit(est)),
        name="mla_flash",
    )(qt, qt, k, vt)


SWA_SCALE = SWA_HEAD_DIM ** -0.5
SWA_COLS = SWA_GROUP * BLOCK


def _t5_bucket_np(dist):
    n = np.maximum(dist, 0)
    max_exact = REL_BUCKETS // 2
    large = max_exact + (np.log(np.maximum(n, 1).astype(np.float32) / max_exact)
                         / math.log(REL_MAX_DIST / max_exact)
                         * (REL_BUCKETS - max_exact)).astype(np.int32)
    large = np.minimum(large, REL_BUCKETS - 1)
    return np.where(n < max_exact, n, large).astype(np.int32)


def _swa_bucket_table():
    j = np.arange(BLOCK)[:, None]
    i = np.arange(BLOCK)[None, :]
    dist = np.where(j <= i, i - j, BLOCK + i - j)
    return _t5_bucket_np(dist)


def _swa_bias_init(rb_ref, bkt_ref, bias_ref):
    bkt = bkt_ref[...]
    for h in range(SWA_HEADS):
        acc = jnp.zeros((BLOCK, BLOCK), F32)
        for b in range(REL_BUCKETS):
            acc = jnp.where(bkt == b, rb_ref[b, h], acc)
        g = h % SWA_GROUP
        bias_ref[h // SWA_GROUP, :, g * BLOCK:(g + 1) * BLOCK] = acc * LOG2E


def _swa_block_fn(first_tile, q_ref, kc_ref, kp_ref, vc_ref, vp_ref, sink_ref, bias_ref, o_ref):
    key = lax.broadcasted_iota(jnp.int32, (BLOCK, SWA_COLS), 0)
    qry = lax.broadcasted_iota(jnp.int32, (BLOCK, SWA_COLS), 1) & (BLOCK - 1)
    lower = key <= qry
    lower_bf = jnp.where(lower, 1.0, 0.0).astype(BF16)

    def block(kv, b):
        bias = bias_ref[kv]
        sink = sink_ref[kv] * LOG2E
        blk = slice(b * BLOCK, (b + 1) * BLOCK)
        qt = jnp.concatenate([q_ref[kv * SWA_GROUP + g, :, blk] for g in range(SWA_GROUP)],
                             axis=1)
        if b == 0:
            k_prev, v_prev = kp_ref[kv], vp_ref[kv, 0]
        else:
            k_prev, v_prev = kc_ref[kv, (b - 1) * BLOCK:b * BLOCK, :], vc_ref[kv, b - 1]
        s_band = _dot(jnp.concatenate([k_prev, kc_ref[kv, blk, :]], axis=0), qt)
        s = jnp.where(lower, s_band[BLOCK:], s_band[:BLOCK]) * (SWA_SCALE * LOG2E) + bias
        if b == 0:
            s = jnp.where(lower | jnp.logical_not(first_tile), s, NEG_BIG)

        m = jnp.maximum(jnp.max(s, axis=0, keepdims=True), sink)
        e = jnp.exp2(s - m)
        denom = jnp.sum(e, axis=0, keepdims=True) + jnp.exp2(sink - m)
        p = (e * (1.0 / denom)).astype(BF16)
        p_cur = p * lower_bf
        p_prev = p - p_cur
        ot = _dot(jnp.concatenate([vc_ref[kv, b], v_prev], axis=1),
                  jnp.concatenate([p_cur, p_prev], axis=0))
        o_ref[blk, kv * SWA_GROUP * SWA_HEAD_DIM:(kv + 1) * SWA_GROUP * SWA_HEAD_DIM] = jnp.concatenate(
            [ot[:, g * BLOCK:(g + 1) * BLOCK] for g in range(SWA_GROUP)], axis=0).T

    return block


OUTPROJ_TM = 512
OUTPROJ_CHUNKS = 2


def _out_proj_kernel(rb_ref, bkt_ref, am_ref, q_ref, kc_ref, kp_ref, vc_ref, vp_ref, sink_ref, h_ref,
                     mg_ref, sg_ref, w_ref, g_ref, b_ref, o_ref, wb_ref, bias_ref, as_ref, nm_ref):
    step = pl.program_id(0)

    @pl.when(step == 0)
    def _():
        wb_ref[...] = w_ref[...].astype(BF16)
        _swa_bias_init(rb_ref, bkt_ref, bias_ref)

    tm, d = o_ref.shape
    nb = tm // BLOCK
    swa_block = _swa_block_fn(step == 0, q_ref, kc_ref, kp_ref, vc_ref, vp_ref, sink_ref, bias_ref, as_ref)
    nm_ref[...] = _rms_norm(am_ref[...], mg_ref[...]).astype(BF16)
    panel = d // (SWA_KV_HEADS * nb)
    for j in range(SWA_KV_HEADS * nb):
        swa_block(j // nb, j % nb)
        cols = slice(j * panel, (j + 1) * panel)
        o_ref[:, cols] = _dot(nm_ref[...], wb_ref[0:MLA_WIDTH, cols])

    chunk = tm // OUTPROJ_CHUNKS
    for c in range(OUTPROJ_CHUNKS):
        rows = slice(c * chunk, (c + 1) * chunk)
        ns = _rms_norm(as_ref[rows, :], sg_ref[...]).astype(BF16)
        mixed = o_ref[rows, :] + _dot(ns, wb_ref[MLA_WIDTH:MLA_WIDTH + SWA_WIDTH, :])
        o_ref[rows, :] = _layer_norm(ALPHA * h_ref[rows, :] + mixed, g_ref[...], b_ref[...])


def _out_proj(rel_bias, bucket_tbl, a_mla, qst, ks, vst, sink_rows, h, mla_g, swa_g, w_out, g, b):
    s, d = h.shape
    tm = OUTPROJ_TM
    nb = tm // BLOCK
    est = (2 * tm * MLA_WIDTH * 4 + 4 * tm * d * 4 + (MLA_WIDTH + SWA_WIDTH) * d * (4 + 2)
           + 2 * tm * (SWA_WIDTH + 2 * SWA_KV_HEADS * V7X_LANES) * 2 + tm * SWA_WIDTH * 4 + tm * MLA_WIDTH * 2
           + SWA_KV_HEADS * BLOCK * SWA_COLS * 4 + 4 * tm * d * 4)
    const = lambda i: (0, 0)
    before = lambda i: jnp.maximum(i * nb - 1, 0)
    return pl.pallas_call(
        _out_proj_kernel,
        grid=(s // tm,),
        in_specs=[
            pl.BlockSpec(memory_space=pltpu.SMEM),
            pl.BlockSpec((BLOCK, BLOCK), const),
            pl.BlockSpec((tm, MLA_WIDTH), lambda i: (i, 0)),
            pl.BlockSpec((SWA_HEADS, SWA_HEAD_DIM, tm), lambda i: (0, 0, i)),
            pl.BlockSpec((SWA_KV_HEADS, tm, SWA_HEAD_DIM), lambda i: (0, i, 0)),
            pl.BlockSpec((SWA_KV_HEADS, BLOCK, SWA_HEAD_DIM), lambda i: (0, before(i), 0)),
            pl.BlockSpec((SWA_KV_HEADS, nb, SWA_HEAD_DIM, BLOCK), lambda i: (0, i, 0, 0)),
            pl.BlockSpec((SWA_KV_HEADS, 1, SWA_HEAD_DIM, BLOCK), lambda i: (0, before(i), 0, 0)),
            pl.BlockSpec((SWA_KV_HEADS, 1, SWA_COLS), lambda i: (0, 0, 0)),
            pl.BlockSpec((tm, d), lambda i: (i, 0)),
            pl.BlockSpec((1, MLA_WIDTH), const),
            pl.BlockSpec((1, SWA_WIDTH), const),
            pl.BlockSpec((MLA_WIDTH + SWA_WIDTH, d), const, pipeline_mode=pl.Buffered(1)),
            pl.BlockSpec((1, d), const),
            pl.BlockSpec((1, d), const),
        ],
        out_specs=pl.BlockSpec((tm, d), lambda i: (i, 0)),
        out_shape=jax.ShapeDtypeStruct((s, d), F32),
        scratch_shapes=[pltpu.VMEM((MLA_WIDTH + SWA_WIDTH, d), BF16),
                        pltpu.VMEM((SWA_KV_HEADS, BLOCK, SWA_COLS), F32),
                        pltpu.VMEM((tm, SWA_WIDTH), F32),
                        pltpu.VMEM((tm, MLA_WIDTH), BF16)],
        compiler_params=pltpu.CompilerParams(
            dimension_semantics=("arbitrary",),
            vmem_limit_bytes=_vmem_limit(est)),
        name="out_proj",
    )(rel_bias, bucket_tbl, a_mla, qst, ks, ks, vst, vst, sink_rows, h, mla_g, swa_g, w_out, g, b)


PLE_TM = 512


def _ple_kernel(h_ref, p_ref, wg_ref, wp_ref, o_ref, wgb_ref, wpb_ref):
    @pl.when(pl.program_id(0) == 0)
    def _():
        wgb_ref[...] = wg_ref[...].astype(BF16)
        wpb_ref[...] = wp_ref[...].astype(BF16)

    h = h_ref[...]
    gate = _dot(h.astype(BF16), wgb_ref[...])
    proj = _dot(p_ref[...].astype(BF16), wpb_ref[...])
    o_ref[...] = h + (1.0 / (1.0 + jnp.exp(-gate))) * proj


def _ple(h, p, w_gate, w_proj):
    s, d = h.shape
    tm = PLE_TM
    est = 4 * tm * d * 4 + 2 * tm * PLE_DIM * 4 + (d + PLE_DIM) * d * (4 + 2) + 4 * tm * d * 4
    const = lambda i: (0, 0)
    return pl.pallas_call(
        _ple_kernel,
        grid=(s // tm,),
        in_specs=[
            pl.BlockSpec((tm, d), lambda i: (i, 0)),
            pl.BlockSpec((tm, PLE_DIM), lambda i: (i, 0)),
            pl.BlockSpec((d, d), const, pipeline_mode=pl.Buffered(1)),
            pl.BlockSpec((PLE_DIM, d), const, pipeline_mode=pl.Buffered(1)),
        ],
        out_specs=pl.BlockSpec((tm, d), lambda i: (i, 0)),
        out_shape=jax.ShapeDtypeStruct((s, d), F32),
        scratch_shapes=[pltpu.VMEM((d, d), BF16), pltpu.VMEM((PLE_DIM, d), BF16)],
        compiler_params=pltpu.CompilerParams(
            dimension_semantics=("arbitrary",),
            vmem_limit_bytes=_vmem_limit(est)),
        name="ple",
    )(h, p, w_gate, w_proj)


def _split_w_in(w_in):
    d = w_in.shape[0]
    b_kr = Q_LORA + KV_LORA
    b_qs = b_kr + QK_ROPE
    b_ks = b_qs + SWA_WIDTH
    b_vs = b_ks + SWA_KV_WIDTH
    wz = jnp.concatenate([w_in[:, :b_kr], w_in[:, b_ks:b_vs]], axis=1)
    pad = jnp.zeros((d, V7X_LANES - QK_ROPE), w_in.dtype)
    wt = jnp.concatenate([w_in[:, b_qs:b_ks], w_in[:, b_vs:], w_in[:, b_kr:b_qs], pad], axis=1).T
    return wz, wt


def _pack_w_uq_t(w_uq):
    r = w_uq.shape[0]
    w = w_uq.reshape(r, MLA_HEADS, QK_NOPE + QK_ROPE)
    w = jnp.pad(w, ((0, 0), (0, 0), (0, MLA_QK_PAD - QK_NOPE - QK_ROPE)))
    return w.reshape(r, MLA_HEADS * MLA_QK_PAD).T


def _rope_freq_col():
    freqs = ROPE_THETA ** (-jnp.arange(0, QK_ROPE, 2, dtype=F32) / QK_ROPE)
    return freqs.reshape(HALF_ROPE, 1)


def kernel(x, p, positions, rel_bias, ln1_g, ln1_b, ffn1_w1, ffn1_w3, ffn1_w2, w_in, q_norm_g, w_uq,
           kv_norm_g, w_ukv, swa_sinks, mla_out_g, swa_out_g, w_out, ln2_g, ln2_b, ffn2_w1, ffn2_w3,
           ffn2_w2, ln3_g, ln3_b, ple_w_gate, ple_w_proj):
    assert x.shape == (1, SEQ, D_MODEL) and DEPTH == 1
    row = lambda a: a.reshape(1, -1)
    h = x[0]
    pos_row = positions[0].astype(F32).reshape(1, SEQ)
    freq_col = _rope_freq_col()
    bucket_tbl = jnp.asarray(_swa_bucket_table())

    i = 0
    h = _ffn_ln(h, ffn1_w1[i], ffn1_w3[i], ffn1_w2[i], row(ln1_g[i]), row(ln1_b[i]))

    w_ukv_h = w_ukv[i].reshape(KV_LORA, MLA_HEADS, QK_NOPE + V_HEAD)
    wkn = w_ukv_h[:, :, :QK_NOPE].reshape(KV_LORA, MLA_WIDTH).astype(BF16)
    wv_t = w_ukv_h[:, :, QK_NOPE:].reshape(KV_LORA, MLA_WIDTH).T.astype(BF16)
    wz, wt = _split_w_in(w_in[i])
    qt, k, vt, qst, ks, vst = _in_proj(
        h, wz.astype(BF16), wt.astype(BF16), row(q_norm_g[i]), row(kv_norm_g[i]),
        _pack_w_uq_t(w_uq[i]).astype(BF16), wkn, wv_t, pos_row, freq_col)

    a_mla = _mla_flash(qt, k, vt)
    sink_rows = jnp.repeat(swa_sinks[i], BLOCK).reshape(SWA_KV_HEADS, 1, SWA_COLS)

    h = _out_proj(rel_bias, bucket_tbl, a_mla, qst, ks, vst, sink_rows, h,
                  row(mla_out_g[i]), row(swa_out_g[i]), w_out[i],
                  row(ln2_g[i]), row(ln2_b[i]))

    h = _ffn_ln(h, ffn2_w1[i], ffn2_w3[i], ffn2_w2[i], row(ln3_g[i]), row(ln3_b[i]))
    h = _ple(h, p[i, 0], ple_w_gate[i], ple_w_proj[i])
    return h[None]
```

```python
import math

import numpy as np
import jax
import jax.numpy as jnp
from jax import lax
from jax.experimental import pallas as pl
from jax.experimental.pallas import tpu as pltpu

F32 = jnp.float32
BF16 = jnp.bfloat16

D_MODEL = 2048
SEQ = 8192
DEPTH = 1
PLE_DIM = 256
MLA_HEADS = 8
Q_LORA = 512
KV_LORA = 512
QK_NOPE = 128
QK_ROPE = 64
V_HEAD = 128
ROPE_THETA = 10000.0
SWA_HEADS = 16
SWA_KV_HEADS = 2
SWA_GROUP = SWA_HEADS // SWA_KV_HEADS
SWA_HEAD_DIM = 64
WINDOW = 128
BLOCK = 128
REL_BUCKETS = 32
REL_MAX_DIST = 128
D_FF = 5632
ALPHA = (2.0 * DEPTH) ** 0.25
EPS = 1e-5
MLA_WIDTH = MLA_HEADS * V_HEAD
SWA_WIDTH = SWA_HEADS * SWA_HEAD_DIM
SWA_KV_WIDTH = SWA_KV_HEADS * SWA_HEAD_DIM

V7X_LANES = 128
V7X_VMEM_BYTES = 64 * 1024 * 1024
MIB = 1024 * 1024

V7X_BF16_SUBLANES = 16

MLA_QK_PAD = 2 * V7X_LANES
MLA_V_ROWS = V_HEAD + V7X_BF16_SUBLANES
NEG_BIG = float(np.finfo(np.float32).min)
LOG2E = math.log2(math.e)


def _vmem_limit(estimate_bytes):
    return int(min(estimate_bytes + 8 * MIB, V7X_VMEM_BYTES - 4 * MIB))


def _layer_norm(y, g, b):
    mu = jnp.mean(y, axis=-1, keepdims=True)
    yc = y - mu
    var = jnp.mean(yc * yc, axis=-1, keepdims=True)
    return yc * lax.rsqrt(var + EPS) * g + b


def _rms_norm(x, g):
    ms = jnp.mean(x * x, axis=-1, keepdims=True)
    return x * lax.rsqrt(ms + EPS) * g


def _dot(a, b):
    return jnp.dot(a, b, preferred_element_type=F32)


def _dot_nt(a, b):
    return lax.dot_general(a, b, (((1,), (1,)), ((), ())), preferred_element_type=F32)


FFN_TM = 2048
FFN_TF = 256
FFN_ROW_CHUNK = 512
FFN_LN_ROWS = 128


def _ffn_ln_kernel(x_hbm, w1_ref, w3_ref, w2_ref, g_ref, b_ref, o_hbm,
                   xs_ref, xb_ref, acc_ref, w1b_ref, w3b_ref, w2b_ref, x_sem, o_sem):
    i, f = pl.program_id(0), pl.program_id(1)
    n_tiles, n_f = pl.num_programs(0), pl.num_programs(1)
    tm = xs_ref.shape[0]
    n_out_chunks = tm // FFN_LN_ROWS

    def x_copy(tile):
        return pltpu.make_async_copy(x_hbm.at[pl.ds(tile * tm, tm), :], xs_ref, x_sem)

    def out_copy(tile, c):
        return pltpu.make_async_copy(
            acc_ref.at[pl.ds(c * FFN_LN_ROWS, FFN_LN_ROWS), :],
            o_hbm.at[pl.ds(tile * tm + c * FFN_LN_ROWS, FFN_LN_ROWS), :], o_sem.at[c])

    def chains(first, final):
        w1b_ref[...] = w1_ref[...].astype(BF16)
        w3b_ref[...] = w3_ref[...].astype(BF16)
        w2b_ref[...] = w2_ref[...].astype(BF16)
        for c in range(tm // FFN_ROW_CHUNK):
            rows = slice(c * FFN_ROW_CHUNK, (c + 1) * FFN_ROW_CHUNK)
            if first:
                x = xs_ref[rows, :]
                xb = x.astype(BF16)
                xb_ref[rows, :] = xb
                base = (2.0 * ALPHA) * x
            else:
                xb = xb_ref[rows, :]
                base = acc_ref[rows, :]
            gate = _dot(xb, w1b_ref[...])
            up = _dot(xb, w3b_ref[...])
            hidden = gate * (1.0 / (1.0 + jnp.exp(-gate))) * up
            acc_ref[rows, :] = base + _dot(hidden.astype(BF16), w2b_ref[...])
            if final:
                for k in range(c * FFN_ROW_CHUNK // FFN_LN_ROWS, (c + 1) * FFN_ROW_CHUNK // FFN_LN_ROWS):
                    sub = slice(k * FFN_LN_ROWS, (k + 1) * FFN_LN_ROWS)
                    acc_ref[sub, :] = _layer_norm(0.5 * acc_ref[sub, :], g_ref[...], b_ref[...])
                    out_copy(i, k).start()

    @pl.when(f == 0)
    def _():
        @pl.when(i == 0)
        def _():
            x_copy(0).start()

        x_copy(i).wait()

        @pl.when(i > 0)
        def _():
            for c in range(n_out_chunks):
                out_copy(i - 1, c).wait()

        chains(first=True, final=False)

        @pl.when(i + 1 < n_tiles)
        def _():
            x_copy(i + 1).start()

    @pl.when((f > 0) & (f < n_f - 1))
    def _():
        chains(first=False, final=False)

    @pl.when(f == n_f - 1)
    def _():
        chains(first=False, final=True)

        @pl.when(i == n_tiles - 1)
        def _():
            for c in range(n_out_chunks):
                out_copy(i, c).wait()


def _ffn_ln(x, w1, w3, w2, g, b):
    s, d = x.shape
    dff = w1.shape[1]
    tm, tf = FFN_TM, FFN_TF
    est = (tm * d * 4 + tm * d * 2 + tm * d * 4 + 2 * 3 * d * tf * 4 + 3 * d * tf * 2
           + 3 * FFN_ROW_CHUNK * tf * 4)
    return pl.pallas_call(
        _ffn_ln_kernel,
        grid=(s // tm, dff // tf),
        in_specs=[
            pl.BlockSpec(memory_space=pl.ANY),
            pl.BlockSpec((d, tf), lambda i, f: (0, f)),
            pl.BlockSpec((d, tf), lambda i, f: (0, f)),
            pl.BlockSpec((tf, d), lambda i, f: (f, 0)),
            pl.BlockSpec((1, d), lambda i, f: (0, 0)),
            pl.BlockSpec((1, d), lambda i, f: (0, 0)),
        ],
        out_specs=pl.BlockSpec(memory_space=pl.ANY),
        out_shape=jax.ShapeDtypeStruct((s, d), F32),
        scratch_shapes=[pltpu.VMEM((tm, d), F32), pltpu.VMEM((tm, d), BF16), pltpu.VMEM((tm, d), F32),
                        pltpu.VMEM((d, tf), BF16), pltpu.VMEM((d, tf), BF16), pltpu.VMEM((tf, d), BF16),
                        pltpu.SemaphoreType.DMA, pltpu.SemaphoreType.DMA((tm // FFN_LN_ROWS,))],
        compiler_params=pltpu.CompilerParams(
            dimension_semantics=("arbitrary", "arbitrary"),
            vmem_limit_bytes=_vmem_limit(est)),
        name="ffn_ln",
    )(x, w1, w3, w2, g, b)


INPROJ_TM = 512
Z_CQ = 0
Z_CKV = Z_CQ + Q_LORA
Z_KS = Z_CKV + KV_LORA
Z_WIDTH = Z_KS + SWA_KV_WIDTH
ZT_QS = 0
ZT_VS = ZT_QS + SWA_WIDTH
ZT_KR = ZT_VS + SWA_KV_WIDTH
ZT_ROWS = ZT_KR + V7X_LANES
HALF_ROPE = QK_ROPE // 2


def _rope_rows(x1, x2, cos, sin):
    return x1 * cos - x2 * sin, x1 * sin + x2 * cos


def _in_proj_kernel(h_ref, wz_ref, wt_ref, qg_ref, kvg_ref, wuqt_ref, wkn_ref, wvt_ref, pos_ref,
                    freq_ref, qt_ref, k_ref, vt_ref, qst_ref, ks_ref, vst_ref):
    tm = h_ref.shape[0]
    hb = h_ref[...].astype(BF16)
    z = _dot(hb, wz_ref[...])
    zt = _dot_nt(wt_ref[...], hb)

    ang = freq_ref[...] * pos_ref[...]
    cos = jnp.cos(ang)
    sin = jnp.sin(ang)

    cq = _rms_norm(z[:, Z_CQ:Z_CQ + Q_LORA], qg_ref[...]).astype(BF16)
    ckv = _rms_norm(z[:, Z_CKV:Z_CKV + KV_LORA], kvg_ref[...]).astype(BF16)
    qt = _dot_nt(wuqt_ref[...], cq)
    kn = _dot(ckv, wkn_ref[...])
    vt = _dot_nt(wvt_ref[...], ckv)

    kr1, kr2 = _rope_rows(zt[ZT_KR:ZT_KR + HALF_ROPE], zt[ZT_KR + HALF_ROPE:ZT_KR + QK_ROPE], cos, sin)
    k_rope = jnp.concatenate([kr1, kr2, zt[ZT_KR + QK_ROPE:ZT_KR + V7X_LANES]], axis=0).T.astype(BF16)

    for h in range(MLA_HEADS):
        base = h * MLA_QK_PAD
        r0 = base + QK_NOPE
        qt_ref[base:r0, :] = qt[base:r0].astype(BF16)
        q1, q2 = _rope_rows(qt[r0:r0 + HALF_ROPE], qt[r0 + HALF_ROPE:r0 + QK_ROPE], cos, sin)
        qt_ref[r0:r0 + HALF_ROPE, :] = q1.astype(BF16)
        qt_ref[r0 + HALF_ROPE:r0 + QK_ROPE, :] = q2.astype(BF16)
        qt_ref[r0 + QK_ROPE:base + MLA_QK_PAD, :] = qt[r0 + QK_ROPE:base + MLA_QK_PAD].astype(BF16)
        k_ref[h, :, 0:QK_NOPE] = kn[:, h * QK_NOPE:(h + 1) * QK_NOPE].astype(BF16)
        k_ref[h, :, QK_NOPE:MLA_QK_PAD] = k_rope
        vt_ref[h, 0, 0:V_HEAD, :] = vt[h * V_HEAD:(h + 1) * V_HEAD].astype(BF16)
        vt_ref[h, 0, V_HEAD:MLA_V_ROWS, :] = jnp.ones((MLA_V_ROWS - V_HEAD, tm), BF16)

    qst_ref[...] = zt[ZT_QS:ZT_QS + SWA_WIDTH].reshape(SWA_HEADS, SWA_HEAD_DIM, tm).astype(BF16)
    for kv in range(SWA_KV_HEADS):
        lo = Z_KS + kv * SWA_HEAD_DIM
        ks_ref[kv] = z[:, lo:lo + SWA_HEAD_DIM].astype(BF16)
        lo = ZT_VS + kv * SWA_HEAD_DIM
        for c in range(tm // BLOCK):
            vst_ref[kv, c] = zt[lo:lo + SWA_HEAD_DIM, c * BLOCK:(c + 1) * BLOCK].astype(BF16)


def _in_proj(h, wz, wt, q_g, kv_g, wuq_t, wkn, wv_t, pos_row, freq_col):
    s, d = h.shape
    tm = INPROJ_TM
    assert tm == MLA_TK
    est = (2 * tm * d * 4 + 2 * d * (Z_WIDTH + ZT_ROWS) * 2
           + 2 * Q_LORA * (MLA_HEADS * MLA_QK_PAD + 2 * MLA_WIDTH) * 2
           + 2 * tm * (2 * MLA_HEADS * MLA_QK_PAD + MLA_WIDTH + SWA_WIDTH + 4 * V7X_LANES) * 2
           + 4 * tm * (Z_WIDTH + ZT_ROWS + 2 * MLA_HEADS * MLA_QK_PAD) * 4)
    const = lambda i: (0, 0)
    return pl.pallas_call(
        _in_proj_kernel,
        grid=(s // tm,),
        in_specs=[
            pl.BlockSpec((tm, d), lambda i: (i, 0)),
            pl.BlockSpec((d, Z_WIDTH), const),
            pl.BlockSpec((ZT_ROWS, d), const),
            pl.BlockSpec((1, Q_LORA), const),
            pl.BlockSpec((1, KV_LORA), const),
            pl.BlockSpec((MLA_HEADS * MLA_QK_PAD, Q_LORA), const),
            pl.BlockSpec((KV_LORA, MLA_WIDTH), const),
            pl.BlockSpec((MLA_WIDTH, KV_LORA), const),
            pl.BlockSpec((1, tm), lambda i: (0, i)),
            pl.BlockSpec((HALF_ROPE, 1), const),
        ],
        out_specs=[
            pl.BlockSpec((MLA_HEADS * MLA_QK_PAD, tm), lambda i: (0, i)),
            pl.BlockSpec((MLA_HEADS, tm, MLA_QK_PAD), lambda i: (0, i, 0)),
            pl.BlockSpec((MLA_HEADS, 1, MLA_V_ROWS, tm), lambda i: (0, i, 0, 0)),
            pl.BlockSpec((SWA_HEADS, SWA_HEAD_DIM, tm), lambda i: (0, 0, i)),
            pl.BlockSpec((SWA_KV_HEADS, tm, SWA_HEAD_DIM), lambda i: (0, i, 0)),
            pl.BlockSpec((SWA_KV_HEADS, tm // BLOCK, SWA_HEAD_DIM, BLOCK), lambda i: (0, i, 0, 0)),
        ],
        out_shape=[
            jax.ShapeDtypeStruct((MLA_HEADS * MLA_QK_PAD, s), BF16),
            jax.ShapeDtypeStruct((MLA_HEADS, s, MLA_QK_PAD), BF16),
            jax.ShapeDtypeStruct((MLA_HEADS, s // tm, MLA_V_ROWS, tm), BF16),
            jax.ShapeDtypeStruct((SWA_HEADS, SWA_HEAD_DIM, s), BF16),
            jax.ShapeDtypeStruct((SWA_KV_HEADS, s, SWA_HEAD_DIM), BF16),
            jax.ShapeDtypeStruct((SWA_KV_HEADS, s // BLOCK, SWA_HEAD_DIM, BLOCK), BF16),
        ],
        compiler_params=pltpu.CompilerParams(
            dimension_semantics=("parallel",),
            vmem_limit_bytes=_vmem_limit(est)),
        name="in_proj",
    )(h, wz, wt, q_g, kv_g, wuq_t, wkn, wv_t, pos_row, freq_col)


MLA_TQ = 512
MLA_TK = 512
MLA_HB = 2
MLA_SCALE = (QK_NOPE + QK_ROPE) ** -0.5
MLA_SCALE_LOG2E = MLA_SCALE * LOG2E


def _mla_flash_kernel(qt_ref, qn_ref, k_ref, vt_ref, o_ref,
                      s_ref, smax_ref, sn_ref, snmax_ref, p_ref, m_ref, corr_ref, acc_ref):
    i = pl.program_id(1)
    tk = MLA_TK
    heads = range(MLA_HB)

    def qk(q_ref, t, s_out, smax_out):
        start = pl.multiple_of(t * tk, tk)
        for h in heads:
            cols = slice(h * MLA_QK_PAD, (h + 1) * MLA_QK_PAD)
            s = _dot(k_ref[h, pl.ds(start, tk), :], q_ref[cols, :]) * MLA_SCALE_LOG2E
            s_out[h] = s
            smax_out[h] = jnp.max(s, axis=0, keepdims=True)

    def softmax(scores, masked):
        for h in heads:
            s, tile_max = scores[h]
            if masked:
                key = lax.broadcasted_iota(jnp.int32, s.shape, 0)
                qry = lax.broadcasted_iota(jnp.int32, s.shape, 1)
                s = jnp.where(key <= qry, s, NEG_BIG)
                tile_max = jnp.max(s, axis=0, keepdims=True)
            m_prev = m_ref[h]
            m_new = jnp.maximum(m_prev, tile_max)
            p_ref[h] = jnp.exp2(s - m_new).astype(BF16)
            corr_ref[h] = jnp.exp2(m_prev - m_new)
            m_ref[h] = m_new

    def pv(t):
        for h in heads:
            acc_ref[h] = corr_ref[h] * acc_ref[h] + _dot(vt_ref[h, t], p_ref[h])

    def load_scores(s_in, smax_in):
        return [(s_in[h], smax_in[h]) for h in heads]

    def drain_and_prime():
        diag = load_scores(s_ref, smax_ref)
        qk(qn_ref, 0, sn_ref, snmax_ref)
        softmax(diag, masked=True)
        pv(i)
        qk(qn_ref, 1, s_ref, smax_ref)
        for h in heads:
            o_ref[:, h * V_HEAD:(h + 1) * V_HEAD] = (
                acc_ref[h, 0:V_HEAD, :] / acc_ref[h, V_HEAD:V_HEAD + 1, :]).T
        m_ref[...] = jnp.full_like(m_ref, NEG_BIG)
        acc_ref[...] = jnp.zeros_like(acc_ref)
        softmax(load_scores(sn_ref, snmax_ref), masked=False)

    @pl.when(i == 0)
    def _():
        m_ref[...] = jnp.full_like(m_ref, NEG_BIG)
        acc_ref[...] = jnp.zeros_like(acc_ref)
        qk(qt_ref, 0, s_ref, smax_ref)
        drain_and_prime()

    @pl.when(i >= 1)
    def _():
        def body(t, carry):
            pv(t)
            scores = load_scores(s_ref, smax_ref)
            qk(qt_ref, t + 2, s_ref, smax_ref)
            softmax(scores, masked=False)
            return carry

        lax.fori_loop(0, i - 1, body, 0)
        pv(i - 1)
        drain_and_prime()


def _mla_flash(qt, k, vt):
    s = k.shape[1]
    tq, tk, hb = MLA_TQ, MLA_TK, MLA_HB
    assert tq == tk and s // tq >= 2
    n_q = s // tq
    est = (4 * hb * tq * MLA_QK_PAD * 2 + 2 * hb * s * MLA_QK_PAD * 2 + 2 * hb * s * MLA_V_ROWS * 2
           + 2 * hb * tq * V_HEAD * 4 + hb * tk * tq * (4 + 4 + 2) + hb * tq * MLA_V_ROWS * 4
           + 4 * hb * tq * tk * 4)
    return pl.pallas_call(
        _mla_flash_kernel,
        grid=(MLA_HEADS // hb, s // tq),
        in_specs=[
            pl.BlockSpec((hb * MLA_QK_PAD, tq), lambda g, i: (g, i)),
            pl.BlockSpec((hb * MLA_QK_PAD, tq), lambda g, i: (g, jnp.minimum(i + 1, n_q - 1))),
            pl.BlockSpec((hb, s, MLA_QK_PAD), lambda g, i: (g, 0, 0)),
            pl.BlockSpec((hb, s // tk, MLA_V_ROWS, tk), lambda g, i: (g, 0, 0, 0)),
        ],
        out_specs=pl.BlockSpec((tq, hb * V_HEAD), lambda g, i: (i, g)),
        out_shape=jax.ShapeDtypeStruct((s, MLA_WIDTH), F32),
        scratch_shapes=[pltpu.VMEM((hb, tk, tq), F32), pltpu.VMEM((hb, 1, tq), F32),
                        pltpu.VMEM((hb, tk, tq), F32), pltpu.VMEM((hb, 1, tq), F32),
                        pltpu.VMEM((hb, tk, tq), BF16),
                        pltpu.VMEM((hb, 1, tq), F32), pltpu.VMEM((hb, 1, tq), F32),
                        pltpu.VMEM((hb, MLA_V_ROWS, tq), F32)],
        compiler_params=pltpu.CompilerParams(
            dimension_semantics=("parallel", "arbitrary"),
            vmem_limit_bytes=_vmem_limit(est)),
        name="mla_flash",
    )(qt, qt, k, vt)


SWA_SCALE = SWA_HEAD_DIM ** -0.5
SWA_COLS = SWA_GROUP * BLOCK


def _t5_bucket_np(dist):
    n = np.maximum(dist, 0)
    max_exact = REL_BUCKETS // 2
    large = max_exact + (np.log(np.maximum(n, 1).astype(np.float32) / max_exact)
                         / math.log(REL_MAX_DIST / max_exact)
                         * (REL_BUCKETS - max_exact)).astype(np.int32)
    large = np.minimum(large, REL_BUCKETS - 1)
    return np.where(n < max_exact, n, large).astype(np.int32)


def _swa_bucket_table():
    j = np.arange(BLOCK)[:, None]
    i = np.arange(BLOCK)[None, :]
    dist = np.where(j <= i, i - j, BLOCK + i - j)
    return _t5_bucket_np(dist)


def _swa_bias_init(rb_ref, bkt_ref, bias_ref):
    bkt = bkt_ref[...]
    for h in range(SWA_HEADS):
        acc = jnp.zeros((BLOCK, BLOCK), F32)
        for b in range(REL_BUCKETS):
            acc = jnp.where(bkt == b, rb_ref[b, h], acc)
        g = h % SWA_GROUP
        bias_ref[h // SWA_GROUP, :, g * BLOCK:(g + 1) * BLOCK] = acc * LOG2E


def _swa_block_fn(first_tile, q_ref, kc_ref, kp_ref, vc_ref, vp_ref, sink_ref, bias_ref, o_ref):
    key = lax.broadcasted_iota(jnp.int32, (BLOCK, SWA_COLS), 0)
    qry = lax.broadcasted_iota(jnp.int32, (BLOCK, SWA_COLS), 1) & (BLOCK - 1)
    lower = key <= qry
    lower_bf = jnp.where(lower, 1.0, 0.0).astype(BF16)

    def block(kv, b):
        bias = bias_ref[kv]
        sink = sink_ref[kv] * LOG2E
        blk = slice(b * BLOCK, (b + 1) * BLOCK)
        qt = jnp.concatenate([q_ref[kv * SWA_GROUP + g, :, blk] for g in range(SWA_GROUP)],
                             axis=1)
        if b == 0:
            k_prev, v_prev = kp_ref[kv], vp_ref[kv, 0]
        else:
            k_prev, v_prev = kc_ref[kv, (b - 1) * BLOCK:b * BLOCK, :], vc_ref[kv, b - 1]
        s_band = _dot(jnp.concatenate([k_prev, kc_ref[kv, blk, :]], axis=0), qt)
        s = jnp.where(lower, s_band[BLOCK:], s_band[:BLOCK]) * (SWA_SCALE * LOG2E) + bias
        if b == 0:
            s = jnp.where(lower | jnp.logical_not(first_tile), s, NEG_BIG)

        m = jnp.maximum(jnp.max(s, axis=0, keepdims=True), sink)
        e = jnp.exp2(s - m)
        denom = jnp.sum(e, axis=0, keepdims=True) + jnp.exp2(sink - m)
        p = (e * (1.0 / denom)).astype(BF16)
        p_cur = p * lower_bf
        p_prev = p - p_cur
        ot = _dot(jnp.concatenate([vc_ref[kv, b], v_prev], axis=1),
                  jnp.concatenate([p_cur, p_prev], axis=0))
        o_ref[blk, kv * SWA_GROUP * SWA_HEAD_DIM:(kv + 1) * SWA_GROUP * SWA_HEAD_DIM] = jnp.concatenate(
            [ot[:, g * BLOCK:(g + 1) * BLOCK] for g in range(SWA_GROUP)], axis=0).T

    return block


OUTPROJ_TM = 512
OUTPROJ_CHUNKS = 2


def _out_proj_kernel(rb_ref, bkt_ref, am_ref, q_ref, kc_ref, kp_ref, vc_ref, vp_ref, sink_ref, h_ref,
                     mg_ref, sg_ref, w_ref, g_ref, b_ref, o_ref, wb_ref, bias_ref, as_ref, nm_ref):
    step = pl.program_id(0)

    @pl.when(step == 0)
    def _():
        wb_ref[...] = w_ref[...].astype(BF16)
        _swa_bias_init(rb_ref, bkt_ref, bias_ref)

    tm, d = o_ref.shape
    nb = tm // BLOCK
    swa_block = _swa_block_fn(step == 0, q_ref, kc_ref, kp_ref, vc_ref, vp_ref, sink_ref, bias_ref, as_ref)
    nm_ref[...] = _rms_norm(am_ref[...], mg_ref[...]).astype(BF16)
    panel = d // (SWA_KV_HEADS * nb)
    for j in range(SWA_KV_HEADS * nb):
        swa_block(j // nb, j % nb)
        cols = slice(j * panel, (j + 1) * panel)
        o_ref[:, cols] = _dot(nm_ref[...], wb_ref[0:MLA_WIDTH, cols])

    chunk = tm // OUTPROJ_CHUNKS
    for c in range(OUTPROJ_CHUNKS):
        rows = slice(c * chunk, (c + 1) * chunk)
        ns = _rms_norm(as_ref[rows, :], sg_ref[...]).astype(BF16)
        mixed = o_ref[rows, :] + _dot(ns, wb_ref[MLA_WIDTH:MLA_WIDTH + SWA_WIDTH, :])
        o_ref[rows, :] = _layer_norm(ALPHA * h_ref[rows, :] + mixed, g_ref[...], b_ref[...])


def _out_proj(rel_bias, bucket_tbl, a_mla, qst, ks, vst, sink_rows, h, mla_g, swa_g, w_out, g, b):
    s, d = h.shape
    tm = OUTPROJ_TM
    nb = tm // BLOCK
    est = (2 * tm * MLA_WIDTH * 4 + 4 * tm * d * 4 + (MLA_WIDTH + SWA_WIDTH) * d * (4 + 2)
           + 2 * tm * (SWA_WIDTH + 2 * SWA_KV_HEADS * V7X_LANES) * 2 + tm * SWA_WIDTH * 4 + tm * MLA_WIDTH * 2
           + SWA_KV_HEADS * BLOCK * SWA_COLS * 4 + 4 * tm * d * 4)
    const = lambda i: (0, 0)
    before = lambda i: jnp.maximum(i * nb - 1, 0)
    return pl.pallas_call(
        _out_proj_kernel,
        grid=(s // tm,),
        in_specs=[
            pl.BlockSpec(memory_space=pltpu.SMEM),
            pl.BlockSpec((BLOCK, BLOCK), const),
            pl.BlockSpec((tm, MLA_WIDTH), lambda i: (i, 0)),
            pl.BlockSpec((SWA_HEADS, SWA_HEAD_DIM, tm), lambda i: (0, 0, i)),
            pl.BlockSpec((SWA_KV_HEADS, tm, SWA_HEAD_DIM), lambda i: (0, i, 0)),
            pl.BlockSpec((SWA_KV_HEADS, BLOCK, SWA_HEAD_DIM), lambda i: (0, before(i), 0)),
            pl.BlockSpec((SWA_KV_HEADS, nb, SWA_HEAD_DIM, BLOCK), lambda i: (0, i, 0, 0)),
            pl.BlockSpec((SWA_KV_HEADS, 1, SWA_HEAD_DIM, BLOCK), lambda i: (0, before(i), 0, 0)),
            pl.BlockSpec((SWA_KV_HEADS, 1, SWA_COLS), lambda i: (0, 0, 0)),
            pl.BlockSpec((tm, d), lambda i: (i, 0)),
            pl.BlockSpec((1, MLA_WIDTH), const),
            pl.BlockSpec((1, SWA_WIDTH), const),
            pl.BlockSpec((MLA_WIDTH + SWA_WIDTH, d), const, pipeline_mode=pl.Buffered(1)),
            pl.BlockSpec((1, d), const),
            pl.BlockSpec((1, d), const),
        ],
        out_specs=pl.BlockSpec((tm, d), lambda i: (i, 0)),
        out_shape=jax.ShapeDtypeStruct((s, d), F32),
        scratch_shapes=[pltpu.VMEM((MLA_WIDTH + SWA_WIDTH, d), BF16),
                        pltpu.VMEM((SWA_KV_HEADS, BLOCK, SWA_COLS), F32),
                        pltpu.VMEM((tm, SWA_WIDTH), F32),
                        pltpu.VMEM((tm, MLA_WIDTH), BF16)],
        compiler_params=pltpu.CompilerParams(
            dimension_semantics=("arbitrary",),
            vmem_limit_bytes=_vmem_limit(est)),
        name="out_proj",
    )(rel_bias, bucket_tbl, a_mla, qst, ks, ks, vst, vst, sink_rows, h, mla_g, swa_g, w_out, g, b)


PLE_TM = 512


def _ple_kernel(h_ref, p_ref, wg_ref, wp_ref, o_ref, wgb_ref, wpb_ref):
    @pl.when(pl.program_id(0) == 0)
    def _():
        wgb_ref[...] = wg_ref[...].astype(BF16)
        wpb_ref[...] = wp_ref[...].astype(BF16)

    h = h_ref[...]
    gate = _dot(h.astype(BF16), wgb_ref[...])
    proj = _dot(p_ref[...].astype(BF16), wpb_ref[...])
    o_ref[...] = h + (1.0 / (1.0 + jnp.exp(-gate))) * proj


def _ple(h, p, w_gate, w_proj):
    s, d = h.shape
    tm = PLE_TM
    est = 4 * tm * d * 4 + 2 * tm * PLE_DIM * 4 + (d + PLE_DIM) * d * (4 + 2) + 4 * tm * d * 4
    const = lambda i: (0, 0)
    return pl.pallas_call(
        _ple_kernel,
        grid=(s // tm,),
        in_specs=[
            pl.BlockSpec((tm, d), lambda i: (i, 0)),
            pl.BlockSpec((tm, PLE_DIM), lambda i: (i, 0)),
            pl.BlockSpec((d, d), const, pipeline_mode=pl.Buffered(1)),
            pl.BlockSpec((PLE_DIM, d), const, pipeline_mode=pl.Buffered(1)),
        ],
        out_specs=pl.BlockSpec((tm, d), lambda i: (i, 0)),
        out_shape=jax.ShapeDtypeStruct((s, d), F32),
        scratch_shapes=[pltpu.VMEM((d, d), BF16), pltpu.VMEM((PLE_DIM, d), BF16)],
        compiler_params=pltpu.CompilerParams(
            dimension_semantics=("arbitrary",),
            vmem_limit_bytes=_vmem_limit(est)),
        name="ple",
    )(h, p, w_gate, w_proj)


def _split_w_in(w_in):
    d = w_in.shape[0]
    b_kr = Q_LORA + KV_LORA
    b_qs = b_kr + QK_ROPE
    b_ks = b_qs + SWA_WIDTH
    b_vs = b_ks + SWA_KV_WIDTH
    wz = jnp.concatenate([w_in[:, :b_kr], w_in[:, b_ks:b_vs]], axis=1)
    pad = jnp.zeros((d, V7X_LANES - QK_ROPE), w_in.dtype)
    wt = jnp.concatenate([w_in[:, b_qs:b_ks], w_in[:, b_vs:], w_in[:, b_kr:b_qs], pad], axis=1).T
    return wz, wt


def _pack_w_uq_t(w_uq):
    r = w_uq.shape[0]
    w = w_uq.reshape(r, MLA_HEADS, QK_NOPE + QK_ROPE)
    w = jnp.pad(w, ((0, 0), (0, 0), (0, MLA_QK_PAD - QK_NOPE - QK_ROPE)))
    return w.reshape(r, MLA_HEADS * MLA_QK_PAD).T


def _rope_freq_col():
    freqs = ROPE_THETA ** (-jnp.arange(0, QK_ROPE, 2, dtype=F32) / QK_ROPE)
    return freqs.reshape(HALF_ROPE, 1)


def kernel(x, p, positions, rel_bias, ln1_g, ln1_b, ffn1_w1, ffn1_w3, ffn1_w2, w_in, q_norm_g, w_uq,
           kv_norm_g, w_ukv, swa_sinks, mla_out_g, swa_out_g, w_out, ln2_g, ln2_b, ffn2_w1, ffn2_w3,
           ffn2_w2, ln3_g, ln3_b, ple_w_gate, ple_w_proj):
    assert x.shape == (1, SEQ, D_MODEL) and DEPTH == 1
    row = lambda a: a.reshape(1, -1)
    h = x[0]
    pos_row = positions[0].astype(F32).reshape(1, SEQ)
    freq_col = _rope_freq_col()
    bucket_tbl = jnp.asarray(_swa_bucket_table())

    i = 0
    h = _ffn_ln(h, ffn1_w1[i], ffn1_w3[i], ffn1_w2[i], row(ln1_g[i]), row(ln1_b[i]))

    w_ukv_h = w_ukv[i].reshape(KV_LORA, MLA_HEADS, QK_NOPE + V_HEAD)
    wkn = w_ukv_h[:, :, :QK_NOPE].reshape(KV_LORA, MLA_WIDTH).astype(BF16)
    wv_t = w_ukv_h[:, :, QK_NOPE:].reshape(KV_LORA, MLA_WIDTH).T.astype(BF16)
    wz, wt = _split_w_in(w_in[i])
    qt, k, vt, qst, ks, vst = _in_proj(
        h, wz.astype(BF16), wt.astype(BF16), row(q_norm_g[i]), row(kv_norm_g[i]),
        _pack_w_uq_t(w_uq[i]).astype(BF16), wkn, wv_t, pos_row, freq_col)

    a_mla = _mla_flash(qt, k, vt)
    sink_rows = jnp.repeat(swa_sinks[i], BLOCK).reshape(SWA_KV_HEADS, 1, SWA_COLS)

    h = _out_proj(rel_bias, bucket_tbl, a_mla, qst, ks, vst, sink_rows, h,
                  row(mla_out_g[i]), row(swa_out_g[i]), w_out[i],
                  row(ln2_g[i]), row(ln2_b[i]))

    h = _ffn_ln(h, ffn2_w1[i], ffn2_w3[i], ffn2_w2[i], row(ln3_g[i]), row(ln3_b[i]))
    h = _ple(h, p[i, 0], ple_w_gate[i], ple_w_proj[i])
    return h[None]
```

```python
import math

import numpy as np
import jax
import jax.numpy as jnp
from jax import lax
from jax.experimental import pallas as pl
from jax.experimental.pallas import tpu as pltpu

F32 = jnp.float32
BF16 = jnp.bfloat16

D_MODEL = 2048
SEQ = 8192
DEPTH = 1
PLE_DIM = 256
MLA_HEADS = 8
Q_LORA = 512
KV_LORA = 512
QK_NOPE = 128
QK_ROPE = 64
V_HEAD = 128
ROPE_THETA = 10000.0
SWA_HEADS = 16
SWA_KV_HEADS = 2
SWA_GROUP = SWA_HEADS // SWA_KV_HEADS
SWA_HEAD_DIM = 64
WINDOW = 128
BLOCK = 128
REL_BUCKETS = 32
REL_MAX_DIST = 128
D_FF = 5632
ALPHA = (2.0 * DEPTH) ** 0.25
EPS = 1e-5
MLA_WIDTH = MLA_HEADS * V_HEAD
SWA_WIDTH = SWA_HEADS * SWA_HEAD_DIM
SWA_KV_WIDTH = SWA_KV_HEADS * SWA_HEAD_DIM

V7X_LANES = 128
V7X_VMEM_BYTES = 64 * 1024 * 1024
MIB = 1024 * 1024

V7X_BF16_SUBLANES = 16

MLA_QK_PAD = 2 * V7X_LANES
MLA_V_ROWS = V_HEAD + V7X_BF16_SUBLANES
NEG_BIG = float(np.finfo(np.float32).min)
LOG2E = math.log2(math.e)


def _vmem_limit(estimate_bytes):
    return int(min(estimate_bytes + 8 * MIB, V7X_VMEM_BYTES - 4 * MIB))


def _layer_norm(y, g, b):
    mu = jnp.mean(y, axis=-1, keepdims=True)
    yc = y - mu
    var = jnp.mean(yc * yc, axis=-1, keepdims=True)
    return yc * lax.rsqrt(var + EPS) * g + b


def _rms_norm(x, g):
    ms = jnp.mean(x * x, axis=-1, keepdims=True)
    return x * lax.rsqrt(ms + EPS) * g


def _dot(a, b):
    return jnp.dot(a, b, preferred_element_type=F32)


def _dot_nt(a, b):
    return lax.dot_general(a, b, (((1,), (1,)), ((), ())), preferred_element_type=F32)


FFN_TM = 2048
FFN_TF = 256
FFN_ROW_CHUNK = 512
FFN_LN_ROWS = 128


def _ffn_ln_kernel(x_hbm, w1_ref, w3_ref, w2_ref, g_ref, b_ref, o_hbm,
                   xs_ref, xb_ref, acc_ref, w1b_ref, w3b_ref, w2b_ref, x_sem, o_sem):
    i, f = pl.program_id(0), pl.program_id(1)
    n_tiles, n_f = pl.num_programs(0), pl.num_programs(1)
    tm = xs_ref.shape[0]
    n_out_chunks = tm // FFN_LN_ROWS

    def x_copy(tile):
        return pltpu.make_async_copy(x_hbm.at[pl.ds(tile * tm, tm), :], xs_ref, x_sem)

    def out_copy(tile, c):
        return pltpu.make_async_copy(
            acc_ref.at[pl.ds(c * FFN_LN_ROWS, FFN_LN_ROWS), :],
            o_hbm.at[pl.ds(tile * tm + c * FFN_LN_ROWS, FFN_LN_ROWS), :], o_sem.at[c])

    def chains(first, final):
        w1b_ref[...] = w1_ref[...].astype(BF16)
        w3b_ref[...] = w3_ref[...].astype(BF16)
        w2b_ref[...] = w2_ref[...].astype(BF16)
        for c in range(tm // FFN_ROW_CHUNK):
            rows = slice(c * FFN_ROW_CHUNK, (c + 1) * FFN_ROW_CHUNK)
            if first:
                x = xs_ref[rows, :]
                xb = x.astype(BF16)
                xb_ref[rows, :] = xb
                base = (2.0 * ALPHA) * x
            else:
                xb = xb_ref[rows, :]
                base = acc_ref[rows, :]
            gate = _dot(xb, w1b_ref[...])
            up = _dot(xb, w3b_ref[...])
            hidden = gate * (1.0 / (1.0 + jnp.exp(-gate))) * up
            acc_ref[rows, :] = base + _dot(hidden.astype(BF16), w2b_ref[...])
            if final:
                for k in range(c * FFN_ROW_CHUNK // FFN_LN_ROWS, (c + 1) * FFN_ROW_CHUNK // FFN_LN_ROWS):
                    sub = slice(k * FFN_LN_ROWS, (k + 1) * FFN_LN_ROWS)
                    acc_ref[sub, :] = _layer_norm(0.5 * acc_ref[sub, :], g_ref[...], b_ref[...])
                    out_copy(i, k).start()

    @pl.when(f == 0)
    def _():
        @pl.when(i == 0)
        def _():
            x_copy(0).start()

        x_copy(i).wait()

        @pl.when(i > 0)
        def _():
            for c in range(n_out_chunks):
                out_copy(i - 1, c).wait()

        chains(first=True, final=False)

        @pl.when(i + 1 < n_tiles)
        def _():
            x_copy(i + 1).start()

    @pl.when((f > 0) & (f < n_f - 1))
    def _():
        chains(first=False, final=False)

    @pl.when(f == n_f - 1)
    def _():
        chains(first=False, final=True)

        @pl.when(i == n_tiles - 1)
        def _():
            for c in range(n_out_chunks):
                out_copy(i, c).wait()


def _ffn_ln(x, w1, w3, w2, g, b):
    s, d = x.shape
    dff = w1.shape[1]
    tm, tf = FFN_TM, FFN_TF
    est = (tm * d * 4 + tm * d * 2 + tm * d * 4 + 2 * 3 * d * tf * 4 + 3 * d * tf * 2
           + 3 * FFN_ROW_CHUNK * tf * 4)
    return pl.pallas_call(
        _ffn_ln_kernel,
        grid=(s // tm, dff // tf),
        in_specs=[
            pl.BlockSpec(memory_space=pl.ANY),
            pl.BlockSpec((d, tf), lambda i, f: (0, f)),
            pl.BlockSpec((d, tf), lambda i, f: (0, f)),
            pl.BlockSpec((tf, d), lambda i, f: (f, 0)),
            pl.BlockSpec((1, d), lambda i, f: (0, 0)),
            pl.BlockSpec((1, d), lambda i, f: (0, 0)),
        ],
        out_specs=pl.BlockSpec(memory_space=pl.ANY),
        out_shape=jax.ShapeDtypeStruct((s, d), F32),
        scratch_shapes=[pltpu.VMEM((tm, d), F32), pltpu.VMEM((tm, d), BF16), pltpu.VMEM((tm, d), F32),
                        pltpu.VMEM((d, tf), BF16), pltpu.VMEM((d, tf), BF16), pltpu.VMEM((tf, d), BF16),
                        pltpu.SemaphoreType.DMA, pltpu.SemaphoreType.DMA((tm // FFN_LN_ROWS,))],
        compiler_params=pltpu.CompilerParams(
            dimension_semantics=("arbitrary", "arbitrary"),
            vmem_limit_bytes=_vmem_limit(est)),
        name="ffn_ln",
    )(x, w1, w3, w2, g, b)


INPROJ_TM = 512
Z_CQ = 0
Z_CKV = Z_CQ + Q_LORA
Z_KS = Z_CKV + KV_LORA
Z_WIDTH = Z_KS + SWA_KV_WIDTH
ZT_QS = 0
ZT_VS = ZT_QS + SWA_WIDTH
ZT_KR = ZT_VS + SWA_KV_WIDTH
ZT_ROWS = ZT_KR + V7X_LANES
HALF_ROPE = QK_ROPE // 2


def _rope_rows(x1, x2, cos, sin):
    return x1 * cos - x2 * sin, x1 * sin + x2 * cos


def _in_proj_kernel(h_ref, wz_ref, wt_ref, qg_ref, kvg_ref, wuqt_ref, wkn_ref, wvt_ref, pos_ref,
                    freq_ref, qt_ref, k_ref, vt_ref, qst_ref, ks_ref, vst_ref):
    tm = h_ref.shape[0]
    hb = h_ref[...].astype(BF16)
    z = _dot(hb, wz_ref[...])
    zt = _dot_nt(wt_ref[...], hb)

    ang = freq_ref[...] * pos_ref[...]
    cos = jnp.cos(ang)
    sin = jnp.sin(ang)

    cq = _rms_norm(z[:, Z_CQ:Z_CQ + Q_LORA], qg_ref[...]).astype(BF16)
    ckv = _rms_norm(z[:, Z_CKV:Z_CKV + KV_LORA], kvg_ref[...]).astype(BF16)
    qt = _dot_nt(wuqt_ref[...], cq)
    kn = _dot(ckv, wkn_ref[...])
    vt = _dot_nt(wvt_ref[...], ckv)

    kr1, kr2 = _rope_rows(zt[ZT_KR:ZT_KR + HALF_ROPE], zt[ZT_KR + HALF_ROPE:ZT_KR + QK_ROPE], cos, sin)
    k_rope = jnp.concatenate([kr1, kr2, zt[ZT_KR + QK_ROPE:ZT_KR + V7X_LANES]], axis=0).T.astype(BF16)

    for h in range(MLA_HEADS):
        base = h * MLA_QK_PAD
        r0 = base + QK_NOPE
        qt_ref[base:r0, :] = qt[base:r0].astype(BF16)
        q1, q2 = _rope_rows(qt[r0:r0 + HALF_ROPE], qt[r0 + HALF_ROPE:r0 + QK_ROPE], cos, sin)
        qt_ref[r0:r0 + HALF_ROPE, :] = q1.astype(BF16)
        qt_ref[r0 + HALF_ROPE:r0 + QK_ROPE, :] = q2.astype(BF16)
        qt_ref[r0 + QK_ROPE:base + MLA_QK_PAD, :] = qt[r0 + QK_ROPE:base + MLA_QK_PAD].astype(BF16)
        k_ref[h, :, 0:QK_NOPE] = kn[:, h * QK_NOPE:(h + 1) * QK_NOPE].astype(BF16)
        k_ref[h, :, QK_NOPE:MLA_QK_PAD] = k_rope
        vt_ref[h, 0, 0:V_HEAD, :] = vt[h * V_HEAD:(h + 1) * V_HEAD].astype(BF16)
        vt_ref[h, 0, V_HEAD:MLA_V_ROWS, :] = jnp.ones((MLA_V_ROWS - V_HEAD, tm), BF16)

    qst_ref[...] = zt[ZT_QS:ZT_QS + SWA_WIDTH].reshape(SWA_HEADS, SWA_HEAD_DIM, tm).astype(BF16)
    for kv in range(SWA_KV_HEADS):
        lo = Z_KS + kv * SWA_HEAD_DIM
        ks_ref[kv] = z[:, lo:lo + SWA_HEAD_DIM].astype(BF16)
        lo = ZT_VS + kv * SWA_HEAD_DIM
        for c in range(tm // BLOCK):
            vst_ref[kv, c] = zt[lo:lo + SWA_HEAD_DIM, c * BLOCK:(c + 1) * BLOCK].astype(BF16)


def _in_proj(h, wz, wt, q_g, kv_g, wuq_t, wkn, wv_t, pos_row, freq_col):
    s, d = h.shape
    tm = INPROJ_TM
    assert tm == MLA_TK
    est = (2 * tm * d * 4 + 2 * d * (Z_WIDTH + ZT_ROWS) * 2
           + 2 * Q_LORA * (MLA_HEADS * MLA_QK_PAD + 2 * MLA_WIDTH) * 2
           + 2 * tm * (2 * MLA_HEADS * MLA_QK_PAD + MLA_WIDTH + SWA_WIDTH + 4 * V7X_LANES) * 2
           + 4 * tm * (Z_WIDTH + ZT_ROWS + 2 * MLA_HEADS * MLA_QK_PAD) * 4)
    const = lambda i: (0, 0)
    return pl.pallas_call(
        _in_proj_kernel,
        grid=(s // tm,),
        in_specs=[
            pl.BlockSpec((tm, d), lambda i: (i, 0)),
            pl.BlockSpec((d, Z_WIDTH), const),
            pl.BlockSpec((ZT_ROWS, d), const),
            pl.BlockSpec((1, Q_LORA), const),
            pl.BlockSpec((1, KV_LORA), const),
            pl.BlockSpec((MLA_HEADS * MLA_QK_PAD, Q_LORA), const),
            pl.BlockSpec((KV_LORA, MLA_WIDTH), const),
            pl.BlockSpec((MLA_WIDTH, KV_LORA), const),
            pl.BlockSpec((1, tm), lambda i: (0, i)),
            pl.BlockSpec((HALF_ROPE, 1), const),
        ],
        out_specs=[
            pl.BlockSpec((MLA_HEADS * MLA_QK_PAD, tm), lambda i: (0, i)),
            pl.BlockSpec((MLA_HEADS, tm, MLA_QK_PAD), lambda i: (0, i, 0)),
            pl.BlockSpec((MLA_HEADS, 1, MLA_V_ROWS, tm), lambda i: (0, i, 0, 0)),
            pl.BlockSpec((SWA_HEADS, SWA_HEAD_DIM, tm), lambda i: (0, 0, i)),
            pl.BlockSpec((SWA_KV_HEADS, tm, SWA_HEAD_DIM), lambda i: (0, i, 0)),
            pl.BlockSpec((SWA_KV_HEADS, tm // BLOCK, SWA_HEAD_DIM, BLOCK), lambda i: (0, i, 0, 0)),
        ],
        out_shape=[
            jax.ShapeDtypeStruct((MLA_HEADS * MLA_QK_PAD, s), BF16),
            jax.ShapeDtypeStruct((MLA_HEADS, s, MLA_QK_PAD), BF16),
            jax.ShapeDtypeStruct((MLA_HEADS, s // tm, MLA_V_ROWS, tm), BF16),
            jax.ShapeDtypeStruct((SWA_HEADS, SWA_HEAD_DIM, s), BF16),
            jax.ShapeDtypeStruct((SWA_KV_HEADS, s, SWA_HEAD_DIM), BF16),
            jax.ShapeDtypeStruct((SWA_KV_HEADS, s // BLOCK, SWA_HEAD_DIM, BLOCK), BF16),
        ],
        compiler_params=pltpu.CompilerParams(
            dimension_semantics=("parallel",),
            vmem_limit_bytes=_vmem_limit(est)),
        name="in_proj",
    )(h, wz, wt, q_g, kv_g, wuq_t, wkn, wv_t, pos_row, freq_col)


MLA_TQ = 512
MLA_TK = 512
MLA_HB = 2
MLA_SCALE = (QK_NOPE + QK_ROPE) ** -0.5
MLA_SCALE_LOG2E = MLA_SCALE * LOG2E


def _mla_flash_kernel(qt_ref, qn_ref, k_ref, vt_ref, o_ref,
                      s_ref, smax_ref, sn_ref, snmax_ref, p_ref, m_ref, corr_ref, acc_ref):
    i = pl.program_id(1)
    tk = MLA_TK
    heads = range(MLA_HB)

    def qk(q_ref, t, s_out, smax_out):
        start = pl.multiple_of(t * tk, tk)
        for h in heads:
            cols = slice(h * MLA_QK_PAD, (h + 1) * MLA_QK_PAD)
            s = _dot(k_ref[h, pl.ds(start, tk), :], q_ref[cols, :])
            s_out[h] = s
            smax_out[h] = jnp.max(s, axis=0, keepdims=True)

    def softmax(scores, masked):
        for h in heads:
            s, smax = scores[h]
            s = s * MLA_SCALE_LOG2E
            if masked:
                key = lax.broadcasted_iota(jnp.int32, s.shape, 0)
                qry = lax.broadcasted_iota(jnp.int32, s.shape, 1)
                s = jnp.where(key <= qry, s, NEG_BIG)
                tile_max = jnp.max(s, axis=0, keepdims=True)
            else:
                tile_max = smax * MLA_SCALE_LOG2E
            m_prev = m_ref[h]
            m_new = jnp.maximum(m_prev, tile_max)
            p_ref[h] = jnp.exp2(s - m_new).astype(BF16)
            corr_ref[h] = jnp.exp2(m_prev - m_new)
            m_ref[h] = m_new

    def pv(t):
        for h in heads:
            acc_ref[h] = corr_ref[h] * acc_ref[h] + _dot(vt_ref[h, t], p_ref[h])

    def load_scores(s_in, smax_in):
        return [(s_in[h], smax_in[h]) for h in heads]

    def drain_and_prime():
        diag = load_scores(s_ref, smax_ref)
        qk(qn_ref, 0, sn_ref, snmax_ref)
        softmax(diag, masked=True)
        pv(i)
        qk(qn_ref, 1, s_ref, smax_ref)
        for h in heads:
            o_ref[:, h * V_HEAD:(h + 1) * V_HEAD] = (
                acc_ref[h, 0:V_HEAD, :] / acc_ref[h, V_HEAD:V_HEAD + 1, :]).T
        m_ref[...] = jnp.full_like(m_ref, NEG_BIG)
        acc_ref[...] = jnp.zeros_like(acc_ref)
        softmax(load_scores(sn_ref, snmax_ref), masked=False)

    @pl.when(i == 0)
    def _():
        m_ref[...] = jnp.full_like(m_ref, NEG_BIG)
        acc_ref[...] = jnp.zeros_like(acc_ref)
        qk(qt_ref, 0, s_ref, smax_ref)
        drain_and_prime()

    @pl.when(i >= 1)
    def _():
        def body(t, carry):
            pv(t)
            scores = load_scores(s_ref, smax_ref)
            qk(qt_ref, t + 2, s_ref, smax_ref)
            softmax(scores, masked=False)
            return carry

        lax.fori_loop(0, i - 1, body, 0)
        pv(i - 1)
        drain_and_prime()


def _mla_flash(qt, k, vt):
    s = k.shape[1]
    tq, tk, hb = MLA_TQ, MLA_TK, MLA_HB
    assert tq == tk and s // tq >= 2
    n_q = s // tq
    est = (4 * hb * tq * MLA_QK_PAD * 2 + 2 * hb * s * MLA_QK_PAD * 2 + 2 * hb * s * MLA_V_ROWS * 2
           + 2 * hb * tq * V_HEAD * 4 + hb * tk * tq * (4 + 4 + 2) + hb * tq * MLA_V_ROWS * 4
           + 4 * hb * tq * tk * 4)
    return pl.pallas_call(
        _mla_flash_kernel,
        grid=(MLA_HEADS // hb, s // tq),
        in_specs=[
            pl.BlockSpec((hb * MLA_QK_PAD, tq), lambda g, i: (g, i)),
            pl.BlockSpec((hb * MLA_QK_PAD, tq), lambda g, i: (g, jnp.minimum(i + 1, n_q - 1))),
            pl.BlockSpec((hb, s, MLA_QK_PAD), lambda g, i: (g, 0, 0)),
            pl.BlockSpec((hb, s // tk, MLA_V_ROWS, tk), lambda g, i: (g, 0, 0, 0)),
        ],
        out_specs=pl.BlockSpec((tq, hb * V_HEAD), lambda g, i: (i, g)),
        out_shape=jax.ShapeDtypeStruct((s, MLA_WIDTH), F32),
        scratch_shapes=[pltpu.VMEM((hb, tk, tq), F32), pltpu.VMEM((hb, 1, tq), F32),
                        pltpu.VMEM((hb, tk, tq), F32), pltpu.VMEM((hb, 1, tq), F32),
                        pltpu.VMEM((hb, tk, tq), BF16),
                        pltpu.VMEM((hb, 1, tq), F32), pltpu.VMEM((hb, 1, tq), F32),
                        pltpu.VMEM((hb, MLA_V_ROWS, tq), F32)],
        compiler_params=pltpu.CompilerParams(
            dimension_semantics=("parallel", "arbitrary"),
            vmem_limit_bytes=_vmem_limit(est)),
        name="mla_flash",
    )(qt, qt, k, vt)


SWA_SCALE = SWA_HEAD_DIM ** -0.5
SWA_COLS = SWA_GROUP * BLOCK


def _t5_bucket_np(dist):
    n = np.maximum(dist, 0)
    max_exact = REL_BUCKETS // 2
    large = max_exact + (np.log(np.maximum(n, 1).astype(np.float32) / max_exact)
                         / math.log(REL_MAX_DIST / max_exact)
                         * (REL_BUCKETS - max_exact)).astype(np.int32)
    large = np.minimum(large, REL_BUCKETS - 1)
    return np.where(n < max_exact, n, large).astype(np.int32)


def _swa_bucket_table():
    j = np.arange(BLOCK)[:, None]
    i = np.arange(BLOCK)[None, :]
    dist = np.where(j <= i, i - j, BLOCK + i - j)
    return _t5_bucket_np(dist)


def _swa_bias_init(rb_ref, bkt_ref, bias_ref):
    bkt = bkt_ref[...]
    for h in range(SWA_HEADS):
        acc = jnp.zeros((BLOCK, BLOCK), F32)
        for b in range(REL_BUCKETS):
            acc = jnp.where(bkt == b, rb_ref[b, h], acc)
        g = h % SWA_GROUP
        bias_ref[h // SWA_GROUP, :, g * BLOCK:(g + 1) * BLOCK] = acc * LOG2E


def _swa_block_fn(first_tile, q_ref, kc_ref, kp_ref, vc_ref, vp_ref, sink_ref, bias_ref, o_ref):
    key = lax.broadcasted_iota(jnp.int32, (BLOCK, SWA_COLS), 0)
    qry = lax.broadcasted_iota(jnp.int32, (BLOCK, SWA_COLS), 1) & (BLOCK - 1)
    lower = key <= qry
    lower_bf = jnp.where(lower, 1.0, 0.0).astype(BF16)

    def block(kv, b):
        bias = bias_ref[kv]
        sink = sink_ref[kv] * LOG2E
        blk = slice(b * BLOCK, (b + 1) * BLOCK)
        qt = jnp.concatenate([q_ref[kv * SWA_GROUP + g, :, blk] for g in range(SWA_GROUP)],
                             axis=1)
        if b == 0:
            k_prev, v_prev = kp_ref[kv], vp_ref[kv, 0]
        else:
            k_prev, v_prev = kc_ref[kv, (b - 1) * BLOCK:b * BLOCK, :], vc_ref[kv, b - 1]
        s_band = _dot(jnp.concatenate([k_prev, kc_ref[kv, blk, :]], axis=0), qt)
        s = jnp.where(lower, s_band[BLOCK:], s_band[:BLOCK]) * (SWA_SCALE * LOG2E) + bias
        if b == 0:
            s = jnp.where(lower | jnp.logical_not(first_tile), s, NEG_BIG)

        m = jnp.maximum(jnp.max(s, axis=0, keepdims=True), sink)
        e = jnp.exp2(s - m)
        denom = jnp.sum(e, axis=0, keepdims=True) + jnp.exp2(sink - m)
        p = (e * (1.0 / denom)).astype(BF16)
        p_cur = p * lower_bf
        p_prev = p - p_cur
        ot = _dot(jnp.concatenate([vc_ref[kv, b], v_prev], axis=1),
                  jnp.concatenate([p_cur, p_prev], axis=0))
        o_ref[blk, kv * SWA_GROUP * SWA_HEAD_DIM:(kv + 1) * SWA_GROUP * SWA_HEAD_DIM] = jnp.concatenate(
            [ot[:, g * BLOCK:(g + 1) * BLOCK] for g in range(SWA_GROUP)], axis=0).T

    return block


OUTPROJ_TM = 512


def _out_proj_kernel(rb_ref, bkt_ref, am_ref, q_ref, kc_ref, kp_ref, vc_ref, vp_ref, sink_ref, h_ref,
                     mg_ref, sg_ref, w_ref, g_ref, b_ref, o_ref, wb_ref, bias_ref, as_ref, nm_ref, ns_ref):
    step = pl.program_id(0)

    @pl.when(step == 0)
    def _():
        wb_ref[...] = w_ref[...].astype(BF16)
        _swa_bias_init(rb_ref, bkt_ref, bias_ref)

    tm, d = o_ref.shape
    upper, lower = slice(0, tm // 2), slice(tm // 2, tm)
    swa_block = _swa_block_fn(step == 0, q_ref, kc_ref, kp_ref, vc_ref, vp_ref, sink_ref, bias_ref, as_ref)
    units = [(kv, b) for b in range(tm // BLOCK) for kv in range(SWA_KV_HEADS)]
    half = len(units) // 2
    panel, wide = d // len(units), d // half
    nm_ref[...] = _rms_norm(am_ref[...], mg_ref[...]).astype(BF16)
    for j, (kv, b) in enumerate(units):
        swa_block(kv, b)
        cols = slice(j * panel, (j + 1) * panel)
        o_ref[:, cols] = _dot(nm_ref[...], wb_ref[0:MLA_WIDTH, cols])
        if j == half - 1:
            ns_ref[...] = _rms_norm(as_ref[upper, :], sg_ref[...]).astype(BF16)
        if j >= half:
            cols = slice((j - half) * wide, (j - half + 1) * wide)
            o_ref[upper, cols] += _dot(ns_ref[...], wb_ref[MLA_WIDTH:MLA_WIDTH + SWA_WIDTH, cols])

    ns = _rms_norm(as_ref[lower, :], sg_ref[...]).astype(BF16)
    mixed = o_ref[lower, :] + _dot(ns, wb_ref[MLA_WIDTH:MLA_WIDTH + SWA_WIDTH, :])
    o_ref[upper, :] = _layer_norm(ALPHA * h_ref[upper, :] + o_ref[upper, :], g_ref[...], b_ref[...])
    o_ref[lower, :] = _layer_norm(ALPHA * h_ref[lower, :] + mixed, g_ref[...], b_ref[...])


def _out_proj(rel_bias, bucket_tbl, a_mla, qst, ks, vst, sink_rows, h, mla_g, swa_g, w_out, g, b):
    s, d = h.shape
    tm = OUTPROJ_TM
    nb = tm // BLOCK
    est = (2 * tm * MLA_WIDTH * 4 + 4 * tm * d * 4 + (MLA_WIDTH + SWA_WIDTH) * d * (4 + 2)
           + 2 * tm * (SWA_WIDTH + 2 * SWA_KV_HEADS * V7X_LANES) * 2 + tm * SWA_WIDTH * 4 + tm * MLA_WIDTH * 2
           + tm // 2 * SWA_WIDTH * 2
           + SWA_KV_HEADS * BLOCK * SWA_COLS * 4 + 4 * tm * d * 4)
    const = lambda i: (0, 0)
    before = lambda i: jnp.maximum(i * nb - 1, 0)
    return pl.pallas_call(
        _out_proj_kernel,
        grid=(s // tm,),
        in_specs=[
            pl.BlockSpec(memory_space=pltpu.SMEM),
            pl.BlockSpec((BLOCK, BLOCK), const),
            pl.BlockSpec((tm, MLA_WIDTH), lambda i: (i, 0)),
            pl.BlockSpec((SWA_HEADS, SWA_HEAD_DIM, tm), lambda i: (0, 0, i)),
            pl.BlockSpec((SWA_KV_HEADS, tm, SWA_HEAD_DIM), lambda i: (0, i, 0)),
            pl.BlockSpec((SWA_KV_HEADS, BLOCK, SWA_HEAD_DIM), lambda i: (0, before(i), 0)),
            pl.BlockSpec((SWA_KV_HEADS, nb, SWA_HEAD_DIM, BLOCK), lambda i: (0, i, 0, 0)),
            pl.BlockSpec((SWA_KV_HEADS, 1, SWA_HEAD_DIM, BLOCK), lambda i: (0, before(i), 0, 0)),
            pl.BlockSpec((SWA_KV_HEADS, 1, SWA_COLS), lambda i: (0, 0, 0)),
            pl.BlockSpec((tm, d), lambda i: (i, 0)),
            pl.BlockSpec((1, MLA_WIDTH), const),
            pl.BlockSpec((1, SWA_WIDTH), const),
            pl.BlockSpec((MLA_WIDTH + SWA_WIDTH, d), const, pipeline_mode=pl.Buffered(1)),
            pl.BlockSpec((1, d), const),
            pl.BlockSpec((1, d), const),
        ],
        out_specs=pl.BlockSpec((tm, d), lambda i: (i, 0)),
        out_shape=jax.ShapeDtypeStruct((s, d), F32),
        scratch_shapes=[pltpu.VMEM((MLA_WIDTH + SWA_WIDTH, d), BF16),
                        pltpu.VMEM((SWA_KV_HEADS, BLOCK, SWA_COLS), F32),
                        pltpu.VMEM((tm, SWA_WIDTH), F32),
                        pltpu.VMEM((tm, MLA_WIDTH), BF16),
                        pltpu.VMEM((tm // 2, SWA_WIDTH), BF16)],
        compiler_params=pltpu.CompilerParams(
            dimension_semantics=("arbitrary",),
            vmem_limit_bytes=_vmem_limit(est)),
        name="out_proj",
    )(rel_bias, bucket_tbl, a_mla, qst, ks, ks, vst, vst, sink_rows, h, mla_g, swa_g, w_out, g, b)


PLE_TM = 512


def _ple_kernel(h_ref, p_ref, wg_ref, wp_ref, o_ref, wgb_ref, wpb_ref):
    @pl.when(pl.program_id(0) == 0)
    def _():
        wgb_ref[...] = wg_ref[...].astype(BF16)
        wpb_ref[...] = wp_ref[...].astype(BF16)

    h = h_ref[...]
    gate = _dot(h.astype(BF16), wgb_ref[...])
    proj = _dot(p_ref[...].astype(BF16), wpb_ref[...])
    o_ref[...] = h + (1.0 / (1.0 + jnp.exp(-gate))) * proj


def _ple(h, p, w_gate, w_proj):
    s, d = h.shape
    tm = PLE_TM
    est = 4 * tm * d * 4 + 2 * tm * PLE_DIM * 4 + (d + PLE_DIM) * d * (4 + 2) + 4 * tm * d * 4
    const = lambda i: (0, 0)
    return pl.pallas_call(
        _ple_kernel,
        grid=(s // tm,),
        in_specs=[
            pl.BlockSpec((tm, d), lambda i: (i, 0)),
            pl.BlockSpec((tm, PLE_DIM), lambda i: (i, 0)),
            pl.BlockSpec((d, d), const, pipeline_mode=pl.Buffered(1)),
            pl.BlockSpec((PLE_DIM, d), const, pipeline_mode=pl.Buffered(1)),
        ],
        out_specs=pl.BlockSpec((tm, d), lambda i: (i, 0)),
        out_shape=jax.ShapeDtypeStruct((s, d), F32),
        scratch_shapes=[pltpu.VMEM((d, d), BF16), pltpu.VMEM((PLE_DIM, d), BF16)],
        compiler_params=pltpu.CompilerParams(
            dimension_semantics=("arbitrary",),
            vmem_limit_bytes=_vmem_limit(est)),
        name="ple",
    )(h, p, w_gate, w_proj)


def _split_w_in(w_in):
    d = w_in.shape[0]
    b_kr = Q_LORA + KV_LORA
    b_qs = b_kr + QK_ROPE
    b_ks = b_qs + SWA_WIDTH
    b_vs = b_ks + SWA_KV_WIDTH
    wz = jnp.concatenate([w_in[:, :b_kr], w_in[:, b_ks:b_vs]], axis=1)
    pad = jnp.zeros((d, V7X_LANES - QK_ROPE), w_in.dtype)
    wt = jnp.concatenate([w_in[:, b_qs:b_ks], w_in[:, b_vs:], w_in[:, b_kr:b_qs], pad], axis=1).T
    return wz, wt


def _pack_w_uq_t(w_uq):
    r = w_uq.shape[0]
    w = w_uq.reshape(r, MLA_HEADS, QK_NOPE + QK_ROPE)
    w = jnp.pad(w, ((0, 0), (0, 0), (0, MLA_QK_PAD - QK_NOPE - QK_ROPE)))
    return w.reshape(r, MLA_HEADS * MLA_QK_PAD).T


def _rope_freq_col():
    freqs = ROPE_THETA ** (-jnp.arange(0, QK_ROPE, 2, dtype=F32) / QK_ROPE)
    return freqs.reshape(HALF_ROPE, 1)


def kernel(x, p, positions, rel_bias, ln1_g, ln1_b, ffn1_w1, ffn1_w3, ffn1_w2, w_in, q_norm_g, w_uq,
           kv_norm_g, w_ukv, swa_sinks, mla_out_g, swa_out_g, w_out, ln2_g, ln2_b, ffn2_w1, ffn2_w3,
           ffn2_w2, ln3_g, ln3_b, ple_w_gate, ple_w_proj):
    assert x.shape == (1, SEQ, D_MODEL) and DEPTH == 1
    row = lambda a: a.reshape(1, -1)
    h = x[0]
    pos_row = positions[0].astype(F32).reshape(1, SEQ)
    freq_col = _rope_freq_col()
    bucket_tbl = jnp.asarray(_swa_bucket_table())

    i = 0
    h = _ffn_ln(h, ffn1_w1[i], ffn1_w3[i], ffn1_w2[i], row(ln1_g[i]), row(ln1_b[i]))

    w_ukv_h = w_ukv[i].reshape(KV_LORA, MLA_HEADS, QK_NOPE + V_HEAD)
    wkn = w_ukv_h[:, :, :QK_NOPE].reshape(KV_LORA, MLA_WIDTH).astype(BF16)
    wv_t = w_ukv_h[:, :, QK_NOPE:].reshape(KV_LORA, MLA_WIDTH).T.astype(BF16)
    wz, wt = _split_w_in(w_in[i])
    qt, k, vt, qst, ks, vst = _in_proj(
        h, wz.astype(BF16), wt.astype(BF16), row(q_norm_g[i]), row(kv_norm_g[i]),
        _pack_w_uq_t(w_uq[i]).astype(BF16), wkn, wv_t, pos_row, freq_col)

    a_mla = _mla_flash(qt, k, vt)
    sink_rows = jnp.repeat(swa_sinks[i], BLOCK).reshape(SWA_KV_HEADS, 1, SWA_COLS)

    h = _out_proj(rel_bias, bucket_tbl, a_mla, qst, ks, vst, sink_rows, h,
                  row(mla_out_g[i]), row(swa_out_g[i]), w_out[i],
                  row(ln2_g[i]), row(ln2_b[i]))

    h = _ffn_ln(h, ffn2_w1[i], ffn2_w3[i], ffn2_w2[i], row(ln3_g[i]), row(ln3_b[i]))
    h = _ple(h, p[i, 0], ple_w_gate[i], ple_w_proj[i])
    return h[None]
```
